```python
import math
import jax, jax.numpy as jnp
from jax import lax
import numpy as np

D_MODEL = 1024
BATCH = 8
SEQ = 16384
DEPTH = 1

GRID_W = 64
CTX_LEN = 256
SSD_HEAD_DIM = 64
D_SSD = D_MODEL
SSD_HEADS = D_SSD // SSD_HEAD_DIM
D_STATE = 128
D_CONV = 5
CHUNK = 128
D_POOL = D_MODEL
POOL_WINDOWS = (2, 4, 8, 16)
N_POOL_GROUPS = len(POOL_WINDOWS)
POOL_GROUP_DIM = D_POOL // N_POOL_GROUPS
D_MIX = D_SSD + D_POOL
D_XBC = D_SSD + 2 * D_STATE
D_IN_PROJ = D_SSD + D_XBC + 2 * SSD_HEADS + D_POOL
D_FF = ((8 * D_MODEL // 3 + 255) // 256) * 256
DEEPNORM_ALPHA = (2 * DEPTH) ** 0.25
DEEPNORM_BETA = (8 * DEPTH) ** -0.25
LN_EPS = 1e-5

kernel_name = 'hybrid_ssd_pool_deepnorm_dit_block'


def layer_norm(x, g, b):
    xf = x.astype(jnp.float32)
    mu = jnp.mean(xf, axis=-1, keepdims=True)
    var = jnp.mean(jnp.square(xf - mu), axis=-1, keepdims=True)
    return ((xf - mu) * lax.rsqrt(var + LN_EPS) * g.astype(jnp.float32) + b.astype(jnp.float32)).astype(x.dtype)


def rms_norm(x, g):
    xf = x.astype(jnp.float32)
    return (xf * lax.rsqrt(jnp.mean(xf * xf, axis=-1, keepdims=True) + LN_EPS) * g.astype(jnp.float32)).astype(x.dtype)


def modulate(x, shift, scale):
    return x * (1 + scale) + shift


def split_projection(h, w_in):
    proj = h @ w_in
    o1 = D_SSD
    o2 = o1 + D_XBC
    o3 = o2 + 2 * SSD_HEADS
    return proj[..., :o1], proj[..., o1:o2], proj[..., o2:o3], proj[..., o3:]


def depthwise_conv_centred(u, w, b):
    out = lax.conv_general_dilated(
        u, w[:, None, :].astype(u.dtype), window_strides=(1,),
        padding=[(D_CONV // 2, D_CONV // 2)],
        dimension_numbers=('NWC', 'WIO', 'NWC'),
        feature_group_count=u.shape[-1])
    return out + b


def ssd_chunked_scan(x, dt, a_neg, B, C, h0):
    f32 = jnp.float32
    b, L, H, P = x.shape
    N = B.shape[-1]
    nc = L // CHUNK
    xc = x.astype(f32).reshape(b, nc, CHUNK, H, P)
    dtc = dt.astype(f32).reshape(b, nc, CHUNK, H)
    Bc = B.astype(f32).reshape(b, nc, CHUNK, N)
    Cc = C.astype(f32).reshape(b, nc, CHUNK, N)
    a_cum = jnp.cumsum(dtc * a_neg.astype(f32), axis=2)
    a_cum_h = jnp.moveaxis(a_cum, -1, 2)
    seg = a_cum_h[..., :, None] - a_cum_h[..., None, :]
    lower = jnp.tril(jnp.ones((CHUNK, CHUNK), dtype=bool))
    decay = jnp.exp(jnp.where(lower, seg, -jnp.inf))
    cb = jnp.einsum('bcin,bcjn->bcij', Cc, Bc)
    scores = cb[:, :, None] * decay * jnp.moveaxis(dtc, -1, 2)[..., None, :]
    y_diag = jnp.einsum('bchij,bcjhp->bcihp', scores, xc)
    w_end = jnp.exp(a_cum[:, :, -1:, :] - a_cum) * dtc
    states = jnp.einsum('bcjn,bcjhp->bchpn', Bc, xc * w_end[..., None])
    chunk_decay = jnp.exp(a_cum[:, :, -1, :])

    def step(h, inp):
        dec, st = inp
        return h * dec[:, :, None, None] + st, h

    h_final, h_prev = lax.scan(step, h0.astype(f32),
                               (jnp.moveaxis(chunk_decay, 1, 0), jnp.moveaxis(states, 1, 0)))
    h_prev = jnp.moveaxis(h_prev, 0, 1)
    y_off = jnp.einsum('bcin,bchpn->bcihp', Cc, h_prev) * jnp.exp(a_cum)[..., None]
    y = (y_diag + y_off).reshape(b, L, H, P)
    return y.astype(x.dtype), h_final


def ssd_bidirectional(xbc, dt_raw, dt_bias, a_log, h0_fwd, h0_bwd):
    b, L, _ = xbc.shape
    xs = xbc[..., :D_SSD].reshape(b, L, SSD_HEADS, SSD_HEAD_DIM)
    Bm = xbc[..., D_SSD:D_SSD + D_STATE]
    Cm = xbc[..., D_SSD + D_STATE:]
    dt = jax.nn.softplus(dt_raw.reshape(b, L, 2, SSD_HEADS) + dt_bias)
    a_neg = -jnp.exp(a_log.astype(jnp.float32))
    y_f, h_f = ssd_chunked_scan(xs, dt[:, :, 0], a_neg[0], Bm, Cm, h0_fwd)
    flip = lambda t: jnp.flip(t, axis=1)
    y_b, h_b = ssd_chunked_scan(flip(xs), flip(dt[:, :, 1]), a_neg[1], flip(Bm), flip(Cm), h0_bwd)
    return y_f + flip(y_b), xs, h_f, h_b


def box_mean(u, w, axis):
    n = u.shape[axis]
    pad = [(0, 0)] * u.ndim
    pad[axis] = (1, 0)
    cs = jnp.pad(jnp.cumsum(u, axis=axis), pad)
    pos = jnp.arange(n)
    lo = jnp.clip(pos - w // 2, 0, n)
    hi = jnp.clip(pos + (w - w // 2), 0, n)
    total = jnp.take(cs, hi, axis=axis) - jnp.take(cs, lo, axis=axis)
    shape = [1] * u.ndim
    shape[axis] = n
    return total / (hi - lo).astype(u.dtype).reshape(shape)


def pool_mixer(u, rows, pool_w, pool_scale):
    b, L, _ = u.shape
    uf = u.astype(jnp.float32).reshape(b, L, N_POOL_GROUPS, POOL_GROUP_DIM)
    outs = []
    for g, w in enumerate(POOL_WINDOWS):
        ug = uf[:, :, g]
        if rows is None:
            m = box_mean(ug, w, 1)
        else:
            grid = ug.reshape(b, rows, GRID_W, POOL_GROUP_DIM)
            m = box_mean(box_mean(grid, w, 1), w, 2).reshape(b, L, POOL_GROUP_DIM)
        outs.append(m - ug)
    d = jnp.stack(outs, axis=2)
    y = jnp.einsum('blgi,gio->blgo', d, pool_w.astype(jnp.float32)).reshape(b, L, D_POOL)
    return (y * pool_scale.astype(jnp.float32)).astype(u.dtype)


def merge_head_groups(y_ssd, xs, z, u_pool, rows, d_skip, ssd_norm_g, pool_w, pool_scale, w_out):
    b, L = z.shape[:2]
    y = (y_ssd + d_skip[:, None] * xs).reshape(b, L, D_SSD)
    y = rms_norm(y * jax.nn.silu(z), ssd_norm_g)
    p = pool_mixer(u_pool, rows, pool_w, pool_scale)
    return jnp.concatenate([y, p], axis=-1) @ w_out


def swiglu(h, w_gate, w_up, w_down):
    return (jax.nn.silu(h @ w_gate) * (h @ w_up)) @ w_down


def _fwd_setup_inputs(seed: int = 0) -> dict:
    key = jax.random.key(seed)
    ks = jax.random.split(key, 26)
    f32 = jnp.float32
    nrm = lambda k, shape, s: jax.random.normal(k, shape, f32) * s
    dt0 = jnp.exp(jax.random.uniform(ks[10], (DEPTH, 2, SSD_HEADS), f32,
                                     minval=math.log(1e-3), maxval=math.log(1e-1)))
    return {
        'x': nrm(ks[0], (BATCH, SEQ, D_MODEL), 1.0),
        'c': nrm(ks[1], (BATCH, D_MODEL), 1.0),
        'ctx': nrm(ks[2], (BATCH, CTX_LEN, D_MODEL), 1.0),
        'c_ctx': nrm(ks[3], (D_MODEL,), 1.0),
        'emb_ln_g': 1.0 + nrm(ks[4], (D_MODEL,), 0.02),
        'emb_ln_b': nrm(ks[5], (D_MODEL,), 0.02),
        'w_ada': nrm(ks[6], (DEPTH, D_MODEL, 6 * D_MODEL), 0.5 * D_MODEL ** -0.5),
        'b_ada': nrm(ks[7], (DEPTH, 6 * D_MODEL), 0.01),
        'in_proj': nrm(ks[8], (DEPTH, D_MODEL, D_IN_PROJ), D_MODEL ** -0.5),
        'conv_w': nrm(ks[9], (DEPTH, D_CONV, D_XBC), D_CONV ** -0.5),
        'conv_b': nrm(ks[11], (DEPTH, D_XBC), 0.01),
        'dt_bias': dt0 + jnp.log(-jnp.expm1(-dt0)),
        'a_log': jnp.log(jax.random.uniform(ks[12], (DEPTH, 2, SSD_HEADS), f32, minval=1.0, maxval=16.0)),
        'd_skip': 1.0 + nrm(ks[13], (DEPTH, SSD_HEADS), 0.1),
        'ssd_norm_g': 1.0 + nrm(ks[14], (DEPTH, D_SSD), 0.02),
        'pool_w': nrm(ks[15], (DEPTH, N_POOL_GROUPS, POOL_GROUP_DIM, POOL_GROUP_DIM), POOL_GROUP_DIM ** -0.5),
        'pool_scale': 1.0 + nrm(ks[16], (DEPTH, D_POOL), 0.02),
        'w_out': nrm(ks[17], (DEPTH, D_MIX, D_MODEL), DEEPNORM_BETA * D_MIX ** -0.5),
        'ln1_g': 1.0 + nrm(ks[18], (DEPTH, D_MODEL), 0.02),
        'ln1_b': nrm(ks[19], (DEPTH, D_MODEL), 0.02),
        'w_gate': nrm(ks[20], (DEPTH, D_MODEL, D_FF), D_MODEL ** -0.5),
        'w_up': nrm(ks[21], (DEPTH, D_MODEL, D_FF), D_MODEL ** -0.5),
        'w_down': nrm(ks[22], (DEPTH, D_FF, D_MODEL), DEEPNORM_BETA * D_FF ** -0.5),
        'ln2_g': 1.0 + nrm(ks[23], (DEPTH, D_MODEL), 0.02),
        'ln2_b': nrm(ks[24], (DEPTH, D_MODEL), 0.02),
    }


def _fwd_reference(x, c, ctx, c_ctx, emb_ln_g, emb_ln_b, w_ada, b_ada, in_proj, conv_w, conv_b,
              dt_bias, a_log, d_skip, ssd_norm_g, pool_w, pool_scale, w_out, ln1_g, ln1_b,
              w_gate, w_up, w_down, ln2_g, ln2_b):
    b = x.shape[0]
    rows = x.shape[1] // GRID_W
    x = layer_norm(x, emb_ln_g, emb_ln_b)
    xc = layer_norm(ctx, emb_ln_g, emb_ln_b)
    silu_c = jax.nn.silu(c)
    silu_cc = jax.nn.silu(c_ctx)
    h_zero = jnp.zeros((b, SSD_HEADS, SSD_HEAD_DIM, D_STATE), jnp.float32)
    for l in range(DEPTH):
        mod = (silu_c @ w_ada[l] + b_ada[l])[:, None, :]
        sh1, sc1, g1, sh2, sc2, g2 = jnp.split(mod, 6, axis=-1)
        modc = silu_cc @ w_ada[l] + b_ada[l]
        sh1c, sc1c, g1c, sh2c, sc2c, g2c = jnp.split(modc, 6, axis=-1)

        zc, xbcc, dtc, upc = split_projection(modulate(xc, sh1c, sc1c), in_proj[l])
        xbcc = jax.nn.silu(depthwise_conv_centred(xbcc, conv_w[l], conv_b[l]))
        yc, xsc, hf_ctx, hb_ctx = ssd_bidirectional(xbcc, dtc, dt_bias[l], a_log[l], h_zero, h_zero)

        z, xbc, dt_raw, up = split_projection(modulate(x, sh1, sc1), in_proj[l])
        xbc = jax.nn.silu(depthwise_conv_centred(xbc, conv_w[l], conv_b[l]))
        y, xs, _, _ = ssd_bidirectional(xbc, dt_raw, dt_bias[l], a_log[l], hf_ctx, hb_ctx)
        mix = merge_head_groups(y, xs, z, up, rows, d_skip[l], ssd_norm_g[l], pool_w[l],
                                pool_scale[l], w_out[l])
        x = layer_norm(DEEPNORM_ALPHA * x + g1 * mix, ln1_g[l], ln1_b[l])
        ffn = swiglu(modulate(x, sh2, sc2), w_gate[l], w_up[l], w_down[l])
        x = layer_norm(DEEPNORM_ALPHA * x + g2 * ffn, ln2_g[l], ln2_b[l])

        if l + 1 < DEPTH:
            mixc = merge_head_groups(yc, xsc, zc, upc, None, d_skip[l], ssd_norm_g[l], pool_w[l],
                                     pool_scale[l], w_out[l])
            xc = layer_norm(DEEPNORM_ALPHA * xc + g1c * mixc, ln1_g[l], ln1_b[l])
            ffnc = swiglu(modulate(xc, sh2c, sc2c), w_gate[l], w_up[l], w_down[l])
            xc = layer_norm(DEEPNORM_ALPHA * xc + g2c * ffnc, ln2_g[l], ln2_b[l])
    return x


import jax as _jax
import jax.numpy as _jnp

TWIN_FORMAT = 'train_step'
FWD_PARAMS = ['x', 'c', 'ctx', 'c_ctx', 'emb_ln_g', 'emb_ln_b', 'w_ada', 'b_ada', 'in_proj', 'conv_w', 'conv_b', 'dt_bias', 'a_log', 'd_skip', 'ssd_norm_g', 'pool_w', 'pool_scale', 'w_out', 'ln1_g', 'ln1_b', 'w_gate', 'w_up', 'w_down', 'ln2_g', 'ln2_b']
TWIN_WEIGHTS = ['c_ctx', 'emb_ln_g', 'emb_ln_b', 'w_ada', 'b_ada', 'in_proj', 'conv_w', 'conv_b', 'dt_bias', 'a_log', 'd_skip', 'ssd_norm_g', 'pool_w', 'pool_scale', 'w_out', 'ln1_g', 'ln1_b', 'w_gate', 'w_up', 'w_down', 'ln2_g', 'ln2_b']
TWIN_DIFF_INPUT = 'x'
TWIN_INPUTS = ['x', 'c', 'ctx', 'c_ctx', 'emb_ln_g', 'emb_ln_b', 'w_ada', 'b_ada', 'in_proj', 'conv_w', 'conv_b', 'dt_bias', 'a_log', 'd_skip', 'ssd_norm_g', 'pool_w', 'pool_scale', 'w_out', 'ln1_g', 'ln1_b', 'w_gate', 'w_up', 'w_down', 'ln2_g', 'ln2_b', 'loss_target', 'm_c_ctx', 'm_emb_ln_g', 'm_emb_ln_b', 'm_w_ada', 'm_b_ada', 'm_in_proj', 'm_conv_w', 'm_conv_b', 'm_dt_bias', 'm_a_log', 'm_d_skip', 'm_ssd_norm_g', 'm_pool_w', 'm_pool_scale', 'm_w_out', 'm_ln1_g', 'm_ln1_b', 'm_w_gate', 'm_w_up', 'm_w_down', 'm_ln2_g', 'm_ln2_b', 'v_c_ctx', 'v_emb_ln_g', 'v_emb_ln_b', 'v_w_ada', 'v_b_ada', 'v_in_proj', 'v_conv_w', 'v_conv_b', 'v_dt_bias', 'v_a_log', 'v_d_skip', 'v_ssd_norm_g', 'v_pool_w', 'v_pool_scale', 'v_w_out', 'v_ln1_g', 'v_ln1_b', 'v_w_gate', 'v_w_up', 'v_w_down', 'v_ln2_g', 'v_ln2_b']
TWIN_OUTPUTS = ['loss', 'grad_x', 'grad_c_ctx', 'grad_emb_ln_g', 'grad_emb_ln_b', 'grad_w_ada', 'grad_b_ada', 'grad_in_proj', 'grad_conv_w', 'grad_conv_b', 'grad_dt_bias', 'grad_a_log', 'grad_d_skip', 'grad_ssd_norm_g', 'grad_pool_w', 'grad_pool_scale', 'grad_w_out', 'grad_ln1_g', 'grad_ln1_b', 'grad_w_gate', 'grad_w_up', 'grad_w_down', 'grad_ln2_g', 'grad_ln2_b', 'delta_c_ctx', 'delta_emb_ln_g', 'delta_emb_ln_b', 'delta_w_ada', 'delta_b_ada', 'delta_in_proj', 'delta_conv_w', 'delta_conv_b', 'delta_dt_bias', 'delta_a_log', 'delta_d_skip', 'delta_ssd_norm_g', 'delta_pool_w', 'delta_pool_scale', 'delta_w_out', 'delta_ln1_g', 'delta_ln1_b', 'delta_w_gate', 'delta_w_up', 'delta_w_down', 'delta_ln2_g', 'delta_ln2_b', 'new_m_c_ctx', 'new_m_emb_ln_g', 'new_m_emb_ln_b', 'new_m_w_ada', 'new_m_b_ada', 'new_m_in_proj', 'new_m_conv_w', 'new_m_conv_b', 'new_m_dt_bias', 'new_m_a_log', 'new_m_d_skip', 'new_m_ssd_norm_g', 'new_m_pool_w', 'new_m_pool_scale', 'new_m_w_out', 'new_m_ln1_g', 'new_m_ln1_b', 'new_m_w_gate', 'new_m_w_up', 'new_m_w_down', 'new_m_ln2_g', 'new_m_ln2_b', 'new_v_c_ctx', 'new_v_emb_ln_g', 'new_v_emb_ln_b', 'new_v_w_ada', 'new_v_b_ada', 'new_v_in_proj', 'new_v_conv_w', 'new_v_conv_b', 'new_v_dt_bias', 'new_v_a_log', 'new_v_d_skip', 'new_v_ssd_norm_g', 'new_v_pool_w', 'new_v_pool_scale', 'new_v_w_out', 'new_v_ln1_g', 'new_v_ln1_b', 'new_v_w_gate', 'new_v_w_up', 'new_v_w_down', 'new_v_ln2_g', 'new_v_ln2_b']
TWIN_LEAF_KINDS = {'loss': 'loss', 'grad_x': 'grad_x', 'grad_c_ctx': 'grad_w', 'grad_emb_ln_g': 'grad_w', 'grad_emb_ln_b': 'grad_w', 'grad_w_ada': 'grad_w', 'grad_b_ada': 'grad_w', 'grad_in_proj': 'grad_w', 'grad_conv_w': 'grad_w', 'grad_conv_b': 'grad_w', 'grad_dt_bias': 'grad_w', 'grad_a_log': 'grad_w', 'grad_d_skip': 'grad_w', 'grad_ssd_norm_g': 'grad_w', 'grad_pool_w': 'grad_w', 'grad_pool_scale': 'grad_w', 'grad_w_out': 'grad_w', 'grad_ln1_g': 'grad_w', 'grad_ln1_b': 'grad_w', 'grad_w_gate': 'grad_w', 'grad_w_up': 'grad_w', 'grad_w_down': 'grad_w', 'grad_ln2_g': 'grad_w', 'grad_ln2_b': 'grad_w', 'delta_c_ctx': 'delta_w', 'delta_emb_ln_g': 'delta_w', 'delta_emb_ln_b': 'delta_w', 'delta_w_ada': 'delta_w', 'delta_b_ada': 'delta_w', 'delta_in_proj': 'delta_w', 'delta_conv_w': 'delta_w', 'delta_conv_b': 'delta_w', 'delta_dt_bias': 'delta_w', 'delta_a_log': 'delta_w', 'delta_d_skip': 'delta_w', 'delta_ssd_norm_g': 'delta_w', 'delta_pool_w': 'delta_w', 'delta_pool_scale': 'delta_w', 'delta_w_out': 'delta_w', 'delta_ln1_g': 'delta_w', 'delta_ln1_b': 'delta_w', 'delta_w_gate': 'delta_w', 'delta_w_up': 'delta_w', 'delta_w_down': 'delta_w', 'delta_ln2_g': 'delta_w', 'delta_ln2_b': 'delta_w', 'new_m_c_ctx': 'new_m', 'new_m_emb_ln_g': 'new_m', 'new_m_emb_ln_b': 'new_m', 'new_m_w_ada': 'new_m', 'new_m_b_ada': 'new_m', 'new_m_in_proj': 'new_m', 'new_m_conv_w': 'new_m', 'new_m_conv_b': 'new_m', 'new_m_dt_bias': 'new_m', 'new_m_a_log': 'new_m', 'new_m_d_skip': 'new_m', 'new_m_ssd_norm_g': 'new_m', 'new_m_pool_w': 'new_m', 'new_m_pool_scale': 'new_m', 'new_m_w_out': 'new_m', 'new_m_ln1_g': 'new_m', 'new_m_ln1_b': 'new_m', 'new_m_w_gate': 'new_m', 'new_m_w_up': 'new_m', 'new_m_w_down': 'new_m', 'new_m_ln2_g': 'new_m', 'new_m_ln2_b': 'new_m', 'new_v_c_ctx': 'new_v', 'new_v_emb_ln_g': 'new_v', 'new_v_emb_ln_b': 'new_v', 'new_v_w_ada': 'new_v', 'new_v_b_ada': 'new_v', 'new_v_in_proj': 'new_v', 'new_v_conv_w': 'new_v', 'new_v_conv_b': 'new_v', 'new_v_dt_bias': 'new_v', 'new_v_a_log': 'new_v', 'new_v_d_skip': 'new_v', 'new_v_ssd_norm_g': 'new_v', 'new_v_pool_w': 'new_v', 'new_v_pool_scale': 'new_v', 'new_v_w_out': 'new_v', 'new_v_ln1_g': 'new_v', 'new_v_ln1_b': 'new_v', 'new_v_w_gate': 'new_v', 'new_v_w_up': 'new_v', 'new_v_w_down': 'new_v', 'new_v_ln2_g': 'new_v', 'new_v_ln2_b': 'new_v'}


def _forward(args):
    return _fwd_reference(*[args[k] for k in FWD_PARAMS])


def _output_shape():
    def fwd():
        inp = _fwd_setup_inputs(0)
        return _fwd_reference(*[inp[k] for k in FWD_PARAMS])
    out = _jax.eval_shape(fwd)
    return out.shape, out.dtype

N_MICROBATCH = 1
ADAM_LR = 0.001
ADAM_B1 = 0.9
ADAM_B2 = 0.999
ADAM_EPS = 1e-08
ADAM_WD = 0.01
ADAM_STEP = 10
PER_EXAMPLE_BATCH_AXIS = {'x': 0, 'c': 0, 'ctx': 0, 'loss_target': 0}
SHARED_INPUTS = []
_WEIGHT_DTYPES = {'c_ctx': _jnp.float32, 'emb_ln_g': _jnp.float32, 'emb_ln_b': _jnp.float32, 'w_ada': _jnp.float32, 'b_ada': _jnp.float32, 'in_proj': _jnp.float32, 'conv_w': _jnp.float32, 'conv_b': _jnp.float32, 'dt_bias': _jnp.float32, 'a_log': _jnp.float32, 'd_skip': _jnp.float32, 'ssd_norm_g': _jnp.float32, 'pool_w': _jnp.float32, 'pool_scale': _jnp.float32, 'w_out': _jnp.float32, 'ln1_g': _jnp.float32, 'ln1_b': _jnp.float32, 'w_gate': _jnp.float32, 'w_up': _jnp.float32, 'w_down': _jnp.float32, 'ln2_g': _jnp.float32, 'ln2_b': _jnp.float32}
MOMENT_SCALE = {'c_ctx': 2.241190e-03, 'emb_ln_g': 4.211422e+00, 'emb_ln_b': 1.759947e+00, 'w_ada': 6.380817e-02, 'b_ada': 1.125508e-01, 'in_proj': 3.753584e-02, 'conv_w': 3.744447e-02, 'conv_b': 5.279138e-02, 'dt_bias': 7.041554e-02, 'a_log': 1.516791e-01, 'd_skip': 1.962036e-01, 'ssd_norm_g': 4.060518e-02, 'pool_w': 3.838641e-02, 'pool_scale': 3.782824e-02, 'w_out': 9.408754e-02, 'ln1_g': 4.322344e+00, 'ln1_b': 1.820647e+00, 'w_gate': 2.419022e-02, 'w_up': 2.358296e-02, 'w_down': 6.551951e-02, 'ln2_g': 1.280317e+02, 'ln2_b': 3.716005e+00}


def _to_microbatches(a, axis):
    t = _jnp.moveaxis(a, axis, 0)
    t = t.reshape((N_MICROBATCH, t.shape[0] // N_MICROBATCH) + t.shape[1:])
    return _jnp.moveaxis(t, 1, axis + 1)


def setup_inputs(seed: int = 0) -> dict:
    inp = _fwd_setup_inputs(seed)
    key = _jax.random.fold_in(_jax.random.key(seed), 7919)
    shape, _ = _output_shape()
    out = dict(inp)
    out["loss_target"] = _jax.random.normal(_jax.random.fold_in(key, 0), shape, _jnp.float32)
    for i, name in enumerate(TWIN_WEIGHTS):
        w = inp[name].astype(_jnp.float32)
        if MOMENT_SCALE is None:
            s = _jnp.sqrt(_jnp.mean(_jnp.square(w)) + 1e-30)
        else:
            s = MOMENT_SCALE[name]
        km, kv = _jax.random.split(_jax.random.fold_in(key, i + 1))
        out[name] = w
        out["m_" + name] = s * _jax.random.normal(km, w.shape, _jnp.float32)
        out["v_" + name] = (s * s) * _jax.random.uniform(kv, w.shape, _jnp.float32, 0.5, 1.5)
    if N_MICROBATCH > 1:
        for name, axis in PER_EXAMPLE_BATCH_AXIS.items():
            out[name] = _to_microbatches(out[name], axis)
    return {'x': out['x'], 'c': out['c'], 'ctx': out['ctx'], 'c_ctx': out['c_ctx'], 'emb_ln_g': out['emb_ln_g'], 'emb_ln_b': out['emb_ln_b'], 'w_ada': out['w_ada'], 'b_ada': out['b_ada'], 'in_proj': out['in_proj'], 'conv_w': out['conv_w'], 'conv_b': out['conv_b'], 'dt_bias': out['dt_bias'], 'a_log': out['a_log'], 'd_skip': out['d_skip'], 'ssd_norm_g': out['ssd_norm_g'], 'pool_w': out['pool_w'], 'pool_scale': out['pool_scale'], 'w_out': out['w_out'], 'ln1_g': out['ln1_g'], 'ln1_b': out['ln1_b'], 'w_gate': out['w_gate'], 'w_up': out['w_up'], 'w_down': out['w_down'], 'ln2_g': out['ln2_g'], 'ln2_b': out['ln2_b'], 'loss_target': out['loss_target'], 'm_c_ctx': out['m_c_ctx'], 'm_emb_ln_g': out['m_emb_ln_g'], 'm_emb_ln_b': out['m_emb_ln_b'], 'm_w_ada': out['m_w_ada'], 'm_b_ada': out['m_b_ada'], 'm_in_proj': out['m_in_proj'], 'm_conv_w': out['m_conv_w'], 'm_conv_b': out['m_conv_b'], 'm_dt_bias': out['m_dt_bias'], 'm_a_log': out['m_a_log'], 'm_d_skip': out['m_d_skip'], 'm_ssd_norm_g': out['m_ssd_norm_g'], 'm_pool_w': out['m_pool_w'], 'm_pool_scale': out['m_pool_scale'], 'm_w_out': out['m_w_out'], 'm_ln1_g': out['m_ln1_g'], 'm_ln1_b': out['m_ln1_b'], 'm_w_gate': out['m_w_gate'], 'm_w_up': out['m_w_up'], 'm_w_down': out['m_w_down'], 'm_ln2_g': out['m_ln2_g'], 'm_ln2_b': out['m_ln2_b'], 'v_c_ctx': out['v_c_ctx'], 'v_emb_ln_g': out['v_emb_ln_g'], 'v_emb_ln_b': out['v_emb_ln_b'], 'v_w_ada': out['v_w_ada'], 'v_b_ada': out['v_b_ada'], 'v_in_proj': out['v_in_proj'], 'v_conv_w': out['v_conv_w'], 'v_conv_b': out['v_conv_b'], 'v_dt_bias': out['v_dt_bias'], 'v_a_log': out['v_a_log'], 'v_d_skip': out['v_d_skip'], 'v_ssd_norm_g': out['v_ssd_norm_g'], 'v_pool_w': out['v_pool_w'], 'v_pool_scale': out['v_pool_scale'], 'v_w_out': out['v_w_out'], 'v_ln1_g': out['v_ln1_g'], 'v_ln1_b': out['v_ln1_b'], 'v_w_gate': out['v_w_gate'], 'v_w_up': out['v_w_up'], 'v_w_down': out['v_w_down'], 'v_ln2_g': out['v_ln2_g'], 'v_ln2_b': out['v_ln2_b']}


def _loss(weights, diff, rest, loss_target):
    with _jax.named_scope("forward"):
        args = {**rest, TWIN_DIFF_INPUT: diff, **{k: w.astype(_WEIGHT_DTYPES[k]) for k, w in weights.items()}}
        y = _forward(args)
    with _jax.named_scope("loss_head"):
        err = _jnp.square(y.astype(_jnp.float32) - loss_target)
        return 0.5 * _jnp.sum(_jnp.mean(err, axis=-1)) if err.ndim else 0.5 * err


def _adamw(w, g, m, v):
    m = ADAM_B1 * m + (1.0 - ADAM_B1) * g
    v = ADAM_B2 * v + (1.0 - ADAM_B2) * _jnp.square(g)
    m_hat = m / (1.0 - ADAM_B1 ** ADAM_STEP)
    v_hat = v / (1.0 - ADAM_B2 ** ADAM_STEP)
    delta = -ADAM_LR * (m_hat / (_jnp.sqrt(v_hat) + ADAM_EPS) + ADAM_WD * w)
    return delta, m, v


def reference(x, c, ctx, c_ctx, emb_ln_g, emb_ln_b, w_ada, b_ada, in_proj, conv_w, conv_b, dt_bias, a_log, d_skip, ssd_norm_g, pool_w, pool_scale, w_out, ln1_g, ln1_b, w_gate, w_up, w_down, ln2_g, ln2_b, loss_target, m_c_ctx, m_emb_ln_g, m_emb_ln_b, m_w_ada, m_b_ada, m_in_proj, m_conv_w, m_conv_b, m_dt_bias, m_a_log, m_d_skip, m_ssd_norm_g, m_pool_w, m_pool_scale, m_w_out, m_ln1_g, m_ln1_b, m_w_gate, m_w_up, m_w_down, m_ln2_g, m_ln2_b, v_c_ctx, v_emb_ln_g, v_emb_ln_b, v_w_ada, v_b_ada, v_in_proj, v_conv_w, v_conv_b, v_dt_bias, v_a_log, v_d_skip, v_ssd_norm_g, v_pool_w, v_pool_scale, v_w_out, v_ln1_g, v_ln1_b, v_w_gate, v_w_up, v_w_down, v_ln2_g, v_ln2_b):
    given = dict(x=x, c=c, ctx=ctx, c_ctx=c_ctx, emb_ln_g=emb_ln_g, emb_ln_b=emb_ln_b, w_ada=w_ada, b_ada=b_ada, in_proj=in_proj, conv_w=conv_w, conv_b=conv_b, dt_bias=dt_bias, a_log=a_log, d_skip=d_skip, ssd_norm_g=ssd_norm_g, pool_w=pool_w, pool_scale=pool_scale, w_out=w_out, ln1_g=ln1_g, ln1_b=ln1_b, w_gate=w_gate, w_up=w_up, w_down=w_down, ln2_g=ln2_g, ln2_b=ln2_b, loss_target=loss_target, m_c_ctx=m_c_ctx, m_emb_ln_g=m_emb_ln_g, m_emb_ln_b=m_emb_ln_b, m_w_ada=m_w_ada, m_b_ada=m_b_ada, m_in_proj=m_in_proj, m_conv_w=m_conv_w, m_conv_b=m_conv_b, m_dt_bias=m_dt_bias, m_a_log=m_a_log, m_d_skip=m_d_skip, m_ssd_norm_g=m_ssd_norm_g, m_pool_w=m_pool_w, m_pool_scale=m_pool_scale, m_w_out=m_w_out, m_ln1_g=m_ln1_g, m_ln1_b=m_ln1_b, m_w_gate=m_w_gate, m_w_up=m_w_up, m_w_down=m_w_down, m_ln2_g=m_ln2_g, m_ln2_b=m_ln2_b, v_c_ctx=v_c_ctx, v_emb_ln_g=v_emb_ln_g, v_emb_ln_b=v_emb_ln_b, v_w_ada=v_w_ada, v_b_ada=v_b_ada, v_in_proj=v_in_proj, v_conv_w=v_conv_w, v_conv_b=v_conv_b, v_dt_bias=v_dt_bias, v_a_log=v_a_log, v_d_skip=v_d_skip, v_ssd_norm_g=v_ssd_norm_g, v_pool_w=v_pool_w, v_pool_scale=v_pool_scale, v_w_out=v_w_out, v_ln1_g=v_ln1_g, v_ln1_b=v_ln1_b, v_w_gate=v_w_gate, v_w_up=v_w_up, v_w_down=v_w_down, v_ln2_g=v_ln2_g, v_ln2_b=v_ln2_b)
    weights = {n: given[n] for n in TWIN_WEIGHTS}
    shared = {n: given[n] for n in SHARED_INPUTS}
    per_example = {n: given[n] for n in ['x', 'c', 'ctx']}
    grad_fn = _jax.value_and_grad(_loss, argnums=(0, 1))

    def one_microbatch(ex, loss_target):
        ex = dict(ex)
        diff = ex.pop(TWIN_DIFF_INPUT)
        return grad_fn(weights, diff, {**shared, **ex}, loss_target)

    if N_MICROBATCH == 1:
        loss, (grad_w, grad_x) = one_microbatch(per_example, given["loss_target"])
    else:
        def body(carry, xs):
            loss_sum, grad_sum = carry
            l_k, (gw_k, gx_k) = one_microbatch(xs[0], xs[1])
            with _jax.named_scope("update"):
                return (loss_sum + l_k, _jax.tree.map(_jnp.add, grad_sum, gw_k)), gx_k

        init = (_jnp.zeros((), _jnp.float32), _jax.tree.map(_jnp.zeros_like, weights))
        (loss, grad_w), grad_x = _jax.lax.scan(body, init, (per_example, given["loss_target"]))
    with _jax.named_scope("update"):
        delta_w, new_m, new_v = {}, {}, {}
        for n in TWIN_WEIGHTS:
            delta_w[n], new_m[n], new_v[n] = _adamw(weights[n], grad_w[n], given["m_" + n], given["v_" + n])
    return (loss, grad_x, *[grad_w[n] for n in TWIN_WEIGHTS], *[delta_w[n] for n in TWIN_WEIGHTS],
            *[new_m[n] for n in TWIN_WEIGHTS], *[new_v[n] for n in TWIN_WEIGHTS])
```

```python
import functools
import math

import jax
import jax.numpy as jnp
from jax import lax
from jax.experimental import pallas as pl
from jax.experimental.pallas import tpu as pltpu

F32 = jnp.float32
BF16 = jnp.bfloat16
HI = lax.Precision.HIGHEST

NDEV = 8
D = 1024
NH = 16
HP = 64
NS = 128
Q = 128
DXBC = 1280
DFF = 2816
FFC = 1408
GW = 64
PR = 8
PT = PR * GW
WINDOWS = (2, 4, 8, 16)
PG = 256
DIN = 3360
WIN = 3584
ALPHA = 2.0 ** 0.25
LN_EPS = 1e-5
TM = 256

ADAM_LR = 0.001
ADAM_B1 = 0.9
ADAM_B2 = 0.999
ADAM_EPS = 1e-08
ADAM_WD = 0.01
ADAM_STEP = 10

V_SH1, V_SC1, V_G1, V_SH2, V_SC2, V_G2 = 0, 1, 2, 3, 4, 5
V_EMBG, V_EMBB, V_LN1G, V_LN1B, V_LN2G, V_LN2B = 6, 7, 8, 9, 10, 11
V_SSDG, V_PSC, V_DSK, V_LOSS = 12, 13, 14, 15
NV = 16

VMEM_LIMIT = 60 * 1024 * 1024


def _cp(ndim=1):
    return pltpu.CompilerParams(dimension_semantics=("arbitrary",) * ndim, vmem_limit_bytes=VMEM_LIMIT)


def _dot(a, b, precision=None):
    return jnp.dot(a, b, preferred_element_type=F32, precision=precision)


def _dot_nt(a, b):
    return lax.dot_general(a, b, (((1,), (1,)), ((), ())), preferred_element_type=F32)


def _dot_tn(a, b, precision=None):
    return lax.dot_general(a, b, (((0,), (0,)), ((), ())), preferred_element_type=F32, precision=precision)


def _sigmoid(x):
    return 1.0 / (1.0 + jnp.exp(-x))


def _softplus(x):
    return jnp.maximum(x, 0.0) + jnp.log(1.0 + jnp.exp(-jnp.abs(x)))


def _ln(x, g, b):
    mu = jnp.mean(x, axis=-1, keepdims=True)
    xc = x - mu
    var = jnp.mean(xc * xc, axis=-1, keepdims=True)
    rstd = lax.rsqrt(var + LN_EPS)
    n = xc * rstd
    return n * g + b, n, rstd


def _ln_bwd(dy, n, rstd, g):
    dn = dy * g
    return rstd * (dn - jnp.mean(dn, axis=-1, keepdims=True) - n * jnp.mean(dn * n, axis=-1, keepdims=True))


def _rowsum(x):
    return jnp.sum(x, axis=0, keepdims=True)


def _resident(shape):
    nd = len(shape)
    return pl.BlockSpec(shape, lambda *_: (0,) * nd, pipeline_mode=pl.Buffered(1))


def _const_out(shape):
    nd = len(shape)
    return pl.BlockSpec(shape, lambda *_: (0,) * nd)


def _tiles(tm, width):
    return pl.BlockSpec((tm, width), lambda i: (i, 0))


def _halo_specs(tm, width, n_rows):
    r = tm // 8
    last = n_rows // 8 - 1
    prev = pl.BlockSpec((8, width), lambda i: (jnp.maximum(i * r - 1, 0), 0))
    nxt = pl.BlockSpec((8, width), lambda i: (jnp.minimum((i + 1) * r, last), 0))
    return prev, nxt


def _acc_tn(acc_ref, a, b, chunk=512):
    n = b.shape[1]
    for c0 in range(0, n, chunk):
        c1 = min(c0 + chunk, n)
        acc_ref[:, c0:c1] += _dot_tn(a, b[:, c0:c1])


def _my_coords():
    return lax.axis_index("x"), lax.axis_index("y"), lax.axis_index("c")


def _peer(k, mx, my, mc):
    kx, ky, kc = (k >> 2) & 1, (k >> 1) & 1, k & 1
    px = 1 - mx if kx else mx
    py = 1 - my if ky else my
    pc = 1 - mc if kc else mc
    return px, py, pc


def _exchange(x, name, gather):
    shape = x.shape if not gather else (NDEV,) + x.shape

    def body(x_ref, out_ref, send_sems, recv_sems, local_sem):
        mx, my, mc = _my_coords()
        me = 4 * mx + 2 * my + mc
        local = pltpu.make_async_copy(x_ref if gather else x_ref.at[me], out_ref.at[me], local_sem)
        local.start()
        sends = []
        for k in range(1, NDEV):
            px, py, pc = _peer(k, mx, my, mc)
            pid = 4 * px + 2 * py + pc
            cp = pltpu.make_async_remote_copy(
                src_ref=x_ref if gather else x_ref.at[pid],
                dst_ref=out_ref.at[me],
                send_sem=send_sems.at[k - 1],
                recv_sem=recv_sems.at[k - 1],
                device_id=(px, py, pc),
                device_id_type=pl.DeviceIdType.MESH,
            )
            cp.start()
            sends.append(cp)
        for k in range(1, NDEV):
            px, py, pc = _peer(k, mx, my, mc)
            pid = 4 * px + 2 * py + pc
            pltpu.make_async_remote_copy(
                src_ref=x_ref if gather else x_ref.at[pid],
                dst_ref=out_ref.at[pid],
                send_sem=send_sems.at[k - 1],
                recv_sem=recv_sems.at[k - 1],
                device_id=(px, py, pc),
                device_id_type=pl.DeviceIdType.MESH,
            ).wait_recv()
        for cp in sends:
            cp.wait_send()
        local.wait()

    return pl.pallas_call(
        body,
        name=name,
        out_shape=jax.ShapeDtypeStruct(shape, x.dtype),
        in_specs=[pl.BlockSpec(memory_space=pl.ANY)],
        out_specs=pl.BlockSpec(memory_space=pl.ANY),
        scratch_shapes=[
            pltpu.SemaphoreType.DMA((NDEV - 1,)),
            pltpu.SemaphoreType.DMA((NDEV - 1,)),
            pltpu.SemaphoreType.DMA,
        ],
    )(x)


def _mod_call(c_all, w_ada, b_ada):
    ncol = w_ada.shape[1]

    def body(c_ref, w_ref, b_ref, silu_ref, mod_ref):
        cv = c_ref[...]
        s = cv * _sigmoid(cv)
        silu_ref[...] = s
        mod_ref[...] = _dot(s.astype(BF16), w_ref[...].astype(BF16)) + b_ref[...]

    return pl.pallas_call(
        body,
        name="mod_fwd",
        out_shape=(jax.ShapeDtypeStruct((16, D), F32), jax.ShapeDtypeStruct((16, ncol), F32)),
    )(c_all, w_ada, b_ada)


def _f1_call(x0, vec, w_in, name):
    L = x0.shape[0]

    def body(x_ref, vec_ref, w_ref, xe_ref, h1_ref, z_ref, xbc_ref, dt_ref, up_ref):
        xe, _, _ = _ln(x_ref[...], vec_ref[V_EMBG:V_EMBG + 1, :], vec_ref[V_EMBB:V_EMBB + 1, :])
        h1 = (xe * (1.0 + vec_ref[V_SC1:V_SC1 + 1, :]) + vec_ref[V_SH1:V_SH1 + 1, :]).astype(BF16)
        proj = _dot(h1, w_ref[...])
        xe_ref[...] = xe
        h1_ref[...] = h1
        z_ref[...] = proj[:, 0:1024]
        xbc_ref[...] = proj[:, 1024:2304]
        dt_ref[...] = proj[:, 2304:2560]
        up_ref[...] = proj[:, 2560:3584]

    return pl.pallas_call(
        body,
        name=name,
        grid=(L // TM,),
        in_specs=[_tiles(TM, D), _resident((NV, D)), _resident((D, WIN))],
        out_specs=[_tiles(TM, D), _tiles(TM, D), _tiles(TM, D), _tiles(TM, DXBC), _tiles(TM, 256), _tiles(TM, D)],
        out_shape=(
            jax.ShapeDtypeStruct((L, D), F32),
            jax.ShapeDtypeStruct((L, D), BF16),
            jax.ShapeDtypeStruct((L, D), F32),
            jax.ShapeDtypeStruct((L, DXBC), F32),
            jax.ShapeDtypeStruct((L, 256), F32),
            jax.ShapeDtypeStruct((L, D), F32),
        ),
        compiler_params=_cp(),
    )(x0, vec, w_in)


def _extended(ext, cur_ref, prev_ref, next_ref):
    i = pl.program_id(0)
    n = pl.num_programs(0)
    tm = cur_ref.shape[0]
    ext[0:8, :] = jnp.where(i > 0, prev_ref[...], 0.0)
    ext[8:8 + tm, :] = cur_ref[...]
    ext[8 + tm:16 + tm, :] = jnp.where(i < n - 1, next_ref[...], 0.0)
    return ext


def _shifted(ext, offset, tm):
    return ext[8 + offset:8 + offset + tm, :]


def _conv_pre(ext, cw_ref, tm):
    acc = cw_ref[5:6, :] + cw_ref[0:1, :] * _shifted(ext, -2, tm)
    for k in range(1, 5):
        acc = acc + cw_ref[k:k + 1, :] * _shifted(ext, k - 2, tm)
    return acc


def _f2_call(xbc_raw, cw, name):
    L = xbc_raw.shape[0]
    prev, nxt = _halo_specs(TM, DXBC, L)

    def body(cur_ref, prev_ref, next_ref, cw_ref, out_ref, ext):
        pre = _conv_pre(_extended(ext, cur_ref, prev_ref, next_ref), cw_ref, TM)
        out_ref[...] = pre * _sigmoid(pre)

    return pl.pallas_call(
        body,
        name=name,
        grid=(L // TM,),
        in_specs=[_tiles(TM, DXBC), prev, nxt, _resident((8, DXBC))],
        out_specs=_tiles(TM, DXBC),
        out_shape=jax.ShapeDtypeStruct((L, DXBC), F32),
        scratch_shapes=[pltpu.VMEM((TM + 16, DXBC), F32)],
        compiler_params=_cp(),
    )(xbc_raw, xbc_raw, xbc_raw, cw)


def _ssd_common(d, dtr, par_ref):
    lane = lax.broadcasted_iota(jnp.int32, (1, 128), 1)
    hmask = lane < NH
    bias = jnp.where(d == 0, par_ref[0:1, :], par_ref[1:2, :])
    alog = jnp.where(d == 0, par_ref[2:3, :], par_ref[3:4, :])
    aneg = jnp.where(hmask, -jnp.exp(alog), 0.0)
    pre = dtr + bias
    dt = jnp.where(hmask, _softplus(pre), 0.0)
    a = dt * aneg
    row = lax.broadcasted_iota(jnp.int32, (Q, Q), 0)
    col = lax.broadcasted_iota(jnp.int32, (Q, Q), 1)
    maskf = jnp.where(d == 0, (row >= col).astype(F32), (row <= col).astype(F32))
    A = _dot(maskf, a, HI)
    atot = _rowsum(a)
    atot_col = jnp.sum(a.T, axis=1, keepdims=True)
    return dict(hmask=hmask, aneg=aneg, pre=pre, dt=dt, a=a, maskf=maskf, A=A, AT=A.T, dtT=dt.T,
                atot=atot, atot_col=atot_col, lane=lane)


def _column(v, lane, h):
    return jnp.sum(jnp.where(lane == h, v, 0.0), axis=1, keepdims=True)


def _head_expand(e_ref, v):
    return _dot(v, e_ref[...], HI)


def _state_decay(et_ref, atot_col):
    return _dot(et_ref[...], jnp.broadcast_to(jnp.exp(atot_col), (128, 128)), HI)


def _ssd_fwd_call(xbc, dtr, h0, par, e_mat, et_mat, name):
    L = xbc.shape[0]
    nc = L // Q

    def chunk(d, s):
        return jnp.where(d == 0, s, nc - 1 - s)

    def body(xbc_ref, dtr_ref, h0_ref, par_ref, e_ref, et_ref, y_ref, hp_ref, hf_ref, hs, AT, dtT):
        d = pl.program_id(0)
        s = pl.program_id(1)

        @pl.when(s == 0)
        def _():
            hs[...] = h0_ref[0]

        q = _ssd_common(d, dtr_ref[...], par_ref)
        A, maskf, lane = q["A"], q["maskf"], q["lane"]
        AT[...] = q["AT"]
        dtT[...] = q["dtT"]
        hprev = hs[...]
        hp_ref[0, 0] = hprev
        bb = xbc_ref[:, 1024:1152].astype(BF16)
        cb = xbc_ref[:, 1152:1280].astype(BF16)
        g = _dot_nt(cb, bb)
        yoff = _dot_nt(cb, hprev.astype(BF16)) * _head_expand(e_ref, jnp.exp(A))
        for k in range(NH // 2):
            ks = slice(128 * k, 128 * k + 128)
            xp = xbc_ref[:, ks]
            acc = yoff[:, ks]
            for half in range(2):
                h = 2 * k + half
                seg = _column(A, lane, h) - AT[h:h + 1, :]
                lm = jnp.exp(jnp.minimum(seg, 0.0)) * maskf
                sc = (g * lm * dtT[h:h + 1, :]).astype(BF16)
                inhead = (lane >= HP) if half else (lane < HP)
                acc = acc + _dot(sc, jnp.where(inhead, xp, 0.0).astype(BF16))
            y_ref[0, :, ks] = acc
        wend = jnp.exp(q["atot"] - A) * q["dt"]
        xw = (xbc_ref[:, 0:1024] * _head_expand(e_ref, wend)).astype(BF16)
        hnew = hprev * _state_decay(et_ref, q["atot_col"]) + _dot_tn(xw, bb)
        hs[...] = hnew
        hf_ref[0] = hnew

    return pl.pallas_call(
        body,
        name=name,
        grid=(2, nc),
        in_specs=[
            pl.BlockSpec((Q, DXBC), lambda d, s: (chunk(d, s), 0)),
            pl.BlockSpec((Q, 128), lambda d, s: (chunk(d, s), d)),
            pl.BlockSpec((1, D, NS), lambda d, s: (d, 0, 0)),
            _resident((8, 128)),
            _resident((128, D)),
            _resident((D, 128)),
        ],
        out_specs=[
            pl.BlockSpec((1, Q, D), lambda d, s: (d, chunk(d, s), 0)),
            pl.BlockSpec((1, 1, D, NS), lambda d, s: (d, chunk(d, s), 0, 0)),
            pl.BlockSpec((1, D, NS), lambda d, s: (d, 0, 0)),
        ],
        out_shape=(
            jax.ShapeDtypeStruct((2, L, D), F32),
            jax.ShapeDtypeStruct((2, nc, D, NS), F32),
            jax.ShapeDtypeStruct((2, D, NS), F32),
        ),
        scratch_shapes=[pltpu.VMEM((D, NS), F32), pltpu.VMEM((128, Q), F32), pltpu.VMEM((128, Q), F32)],
        compiler_params=_cp(2),
    )(xbc, dtr, h0, par, e_mat, et_mat)


def _pool_specs(n_tiles):
    cur = pl.BlockSpec((PT, D), lambda i: (i, 0))
    prev = pl.BlockSpec((PT, D), lambda i: (jnp.maximum(i - 1, 0), 0))
    nxt = pl.BlockSpec((PT, D), lambda i: (jnp.minimum(i + 1, n_tiles - 1), 0))
    return cur, prev, nxt


def _split_dot(m, x):
    hi = x.astype(BF16)
    lo = (x - hi.astype(F32)).astype(BF16)
    return _dot(m, hi) + _dot(m, lo)


def _pool_fwd_call(up, pmat, icnt, pool_w):
    L = up.shape[0]
    nt = L // PT
    cur, prev, nxt = _pool_specs(nt)

    def body(cur_ref, prev_ref, next_ref, m_ref, ic_ref, pw_ref, d_ref, pun_ref):
        i = pl.program_id(0)
        n = pl.num_programs(0)
        lane = lax.broadcasted_iota(jnp.int32, (1, 128), 1)
        icv = ic_ref[...]
        for g in range(4):
            gs = slice(PG * g, PG * g + PG)
            top = jnp.where(i > 0, prev_ref[:, gs], 0.0)
            bot = jnp.where(i < n - 1, next_ref[:, gs], 0.0)
            mid = cur_ref[:, gs]
            box = _split_dot(m_ref[g], jnp.concatenate([top, mid, bot], axis=0))
            dg = (box * _column(icv, lane, g) - mid).astype(BF16)
            d_ref[:, gs] = dg
            pun_ref[:, gs] = _dot(dg, pw_ref[g])

    return pl.pallas_call(
        body,
        name="pool_fwd",
        grid=(nt,),
        in_specs=[cur, prev, nxt, _resident((4, PT, 3 * PT)), _tiles(PT, 128), _resident((4, PG, PG))],
        out_specs=[_tiles(PT, D), _tiles(PT, D)],
        out_shape=(jax.ShapeDtypeStruct((L, D), BF16), jax.ShapeDtypeStruct((L, D), F32)),
        compiler_params=_cp(),
    )(up, up, up, pmat, icnt, pool_w)


def _gated(y2_ref, xs, z, vec_ref):
    ym = y2_ref[0] + y2_ref[1] + vec_ref[V_DSK:V_DSK + 1, :] * xs
    sz = _sigmoid(z)
    gated = ym * (z * sz)
    r = lax.rsqrt(jnp.mean(gated * gated, axis=-1, keepdims=True) + LN_EPS)
    return ym, sz, gated, r


def _merge_call(y2, xbc, z, pun, xe, vec, w_out):
    L = z.shape[0]

    def body(y2_ref, xs_ref, z_ref, pun_ref, xe_ref, vec_ref, w_ref, x1_ref, mix_ref, cat_ref):
        _, _, gated, r = _gated(y2_ref, xs_ref[...], z_ref[...], vec_ref)
        yn = gated * r * vec_ref[V_SSDG:V_SSDG + 1, :]
        p = pun_ref[...] * vec_ref[V_PSC:V_PSC + 1, :]
        cat = jnp.concatenate([yn, p], axis=1).astype(BF16)
        mix = _dot(cat, w_ref[...])
        pre1 = ALPHA * xe_ref[...] + vec_ref[V_G1:V_G1 + 1, :] * mix
        x1, _, _ = _ln(pre1, vec_ref[V_LN1G:V_LN1G + 1, :], vec_ref[V_LN1B:V_LN1B + 1, :])
        x1_ref[...] = x1
        mix_ref[...] = mix
        cat_ref[...] = cat

    return pl.pallas_call(
        body,
        name="merge_fwd",
        grid=(L // TM,),
        in_specs=[
            pl.BlockSpec((2, TM, D), lambda i: (0, i, 0)),
            _tiles(TM, D), _tiles(TM, D), _tiles(TM, D), _tiles(TM, D),
            _resident((NV, D)), _resident((2 * D, D)),
        ],
        out_specs=[_tiles(TM, D), _tiles(TM, D), _tiles(TM, 2 * D)],
        out_shape=(
            jax.ShapeDtypeStruct((L, D), F32),
            jax.ShapeDtypeStruct((L, D), F32),
            jax.ShapeDtypeStruct((L, 2 * D), BF16),
        ),
        compiler_params=_cp(),
    )(y2, xbc, z, pun, xe, vec, w_out)


def _ffn_fwd_call(x1, tgt, vec, w_gate, w_up, w_down):
    L = x1.shape[0]

    def body(x1_ref, tgt_ref, vec_ref, wg_ref, wu_ref, wd_ref, dpre_ref, gacc_ref):
        @pl.when(pl.program_id(0) == 0)
        def _():
            gacc_ref[...] = jnp.zeros_like(gacc_ref)

        x1 = x1_ref[...]
        h2 = (x1 * (1.0 + vec_ref[V_SC2:V_SC2 + 1, :]) + vec_ref[V_SH2:V_SH2 + 1, :]).astype(BF16)
        gt = _dot(h2, wg_ref[...])
        f = (gt * _sigmoid(gt) * _dot(h2, wu_ref[...])).astype(BF16)
        ffn = _dot(f, wd_ref[...])
        g2 = vec_ref[V_G2:V_G2 + 1, :]
        lng = vec_ref[V_LN2G:V_LN2G + 1, :]
        x2, n2, rstd2 = _ln(ALPHA * x1 + g2 * ffn, lng, vec_ref[V_LN2B:V_LN2B + 1, :])
        diff = x2 - tgt_ref[...]
        dx2 = diff * (1.0 / D)
        dpre2 = _ln_bwd(dx2, n2, rstd2, lng)
        dpre_ref[...] = dpre2
        gacc_ref[V_LN2G:V_LN2G + 1, :] += _rowsum(dx2 * n2)
        gacc_ref[V_LN2B:V_LN2B + 1, :] += _rowsum(dx2)
        gacc_ref[V_G2:V_G2 + 1, :] += _rowsum(dpre2 * ffn)
        gacc_ref[V_LOSS:V_LOSS + 1, :] += jnp.sum(diff * diff) * (0.5 / D)

    return pl.pallas_call(
        body,
        name="ffn_fwd",
        grid=(L // TM,),
        in_specs=[_tiles(TM, D), _tiles(TM, D), _resident((NV, D)),
                  _resident((D, DFF)), _resident((D, DFF)), _resident((DFF, D))],
        out_specs=[_tiles(TM, D), _const_out((NV, D))],
        out_shape=(jax.ShapeDtypeStruct((L, D), F32), jax.ShapeDtypeStruct((NV, D), F32)),
        compiler_params=_cp(),
    )(x1, tgt, vec, w_gate, w_up, w_down)


def _ffn_bwd_call(x1, dpre2, vec, w_gate, w_up, w_down):
    L = x1.shape[0]
    nt = L // TM
    nj = DFF // FFC

    def body(x1_ref, dpre_ref, vec_ref, wg_ref, wu_ref, wd_ref, dh2_ref, dwg_ref, dwu_ref, dwd_ref, ag, au, ad):
        j = pl.program_id(0)
        i = pl.program_id(1)

        @pl.when(i == 0)
        def _():
            ag[...] = jnp.zeros_like(ag)
            au[...] = jnp.zeros_like(au)
            ad[...] = jnp.zeros_like(ad)

        h2 = (x1_ref[...] * (1.0 + vec_ref[V_SC2:V_SC2 + 1, :]) + vec_ref[V_SH2:V_SH2 + 1, :]).astype(BF16)
        gt = _dot(h2, wg_ref[...])
        up = _dot(h2, wu_ref[...])
        sg = _sigmoid(gt)
        sl = gt * sg
        f = (sl * up).astype(BF16)
        dffn = (vec_ref[V_G2:V_G2 + 1, :] * dpre_ref[...]).astype(BF16)
        df = _dot_nt(dffn, wd_ref[...])
        dgt = (df * up * (sg * (1.0 + gt * (1.0 - sg)))).astype(BF16)
        dup = (df * sl).astype(BF16)
        dh2_ref[0] = _dot_nt(dgt, wg_ref[...]) + _dot_nt(dup, wu_ref[...])
        _acc_tn(ag, h2, dgt)
        _acc_tn(au, h2, dup)
        _acc_tn(ad, f, dffn)

        @pl.when(i == nt - 1)
        def _():
            pltpu.sync_copy(ag, dwg_ref.at[j])
            pltpu.sync_copy(au, dwu_ref.at[j])
            pltpu.sync_copy(ad, dwd_ref.at[j])

    any_spec = pl.BlockSpec(memory_space=pl.ANY)
    return pl.pallas_call(
        body,
        name="ffn_bwd",
        grid=(nj, nt),
        in_specs=[
            pl.BlockSpec((TM, D), lambda j, i: (i, 0)),
            pl.BlockSpec((TM, D), lambda j, i: (i, 0)),
            _resident((NV, D)),
            pl.BlockSpec((D, FFC), lambda j, i: (0, j)),
            pl.BlockSpec((D, FFC), lambda j, i: (0, j)),
            pl.BlockSpec((FFC, D), lambda j, i: (j, 0)),
        ],
        out_specs=[pl.BlockSpec((1, TM, D), lambda j, i: (j, i, 0)), any_spec, any_spec, any_spec],
        out_shape=(
            jax.ShapeDtypeStruct((nj, L, D), F32),
            jax.ShapeDtypeStruct((nj, D, FFC), F32),
            jax.ShapeDtypeStruct((nj, D, FFC), F32),
            jax.ShapeDtypeStruct((nj, FFC, D), F32),
        ),
        scratch_shapes=[pltpu.VMEM((D, FFC), F32), pltpu.VMEM((D, FFC), F32), pltpu.VMEM((FFC, D), F32)],
        compiler_params=_cp(2),
    )(x1, dpre2, vec, w_gate, w_up, w_down)


def _merge_bwd_call(dh2p, dpre2, x1, xe, mix, cat, y2, xbc, z, dpool, pun, vec, w_out, pool_w):
    L = z.shape[0]
    nt = L // TM

    def body(dh2_ref, dpre2_ref, x1_ref, xe_ref, mix_ref, cat_ref, y2_ref, xs_ref, z_ref, dpool_ref, pun_ref,
             vec_ref, w_ref, pw_ref,
             dxe_ref, dy_ref, dz_ref, dd_ref, dpw_ref, gacc_ref, dwo_ref, dwo_s):
        i = pl.program_id(0)

        @pl.when(i == 0)
        def _():
            gacc_ref[...] = jnp.zeros_like(gacc_ref)
            dpw_ref[...] = jnp.zeros_like(dpw_ref)
            dwo_s[...] = jnp.zeros_like(dwo_s)

        def vrow(r):
            return vec_ref[r:r + 1, :]

        def gadd(r, val):
            gacc_ref[r:r + 1, :] += _rowsum(val)

        x1 = x1_ref[...]
        dh2 = dh2_ref[0] + dh2_ref[1]
        dx1 = ALPHA * dpre2_ref[...] + dh2 * (1.0 + vrow(V_SC2))
        gadd(V_SC2, dh2 * x1)
        gadd(V_SH2, dh2)
        mix = mix_ref[...]
        _, n1, rstd1 = _ln(ALPHA * xe_ref[...] + vrow(V_G1) * mix, vrow(V_LN1G), vrow(V_LN1B))
        gadd(V_LN1G, dx1 * n1)
        gadd(V_LN1B, dx1)
        dpre1 = _ln_bwd(dx1, n1, rstd1, vrow(V_LN1G))
        dxe_ref[...] = ALPHA * dpre1
        gadd(V_G1, dpre1 * mix)
        dmix = (vrow(V_G1) * dpre1).astype(BF16)
        dcat = _dot_nt(dmix, w_ref[...])
        cat = cat_ref[...]
        for c0 in range(0, 2 * D, 512):
            dwo_s[c0:c0 + 512, :] += _dot_tn(cat[:, c0:c0 + 512], dmix)
        dyn = dcat[:, 0:D]
        dp = dcat[:, D:2 * D]
        xs = xs_ref[...]
        z = z_ref[...]
        ym, sz, gated, r = _gated(y2_ref, xs, z, vec_ref)
        gadd(V_SSDG, dyn * gated * r)
        a = dyn * vrow(V_SSDG)
        dgated = r * a - gated * (r * r * r * jnp.mean(a * gated, axis=-1, keepdims=True))
        dym = dgated * (z * sz)
        dy_ref[...] = dym
        dz_ref[...] = dgated * ym * (sz * (1.0 + z * (1.0 - sz)))
        gadd(V_DSK, dym * xs)
        gadd(V_PSC, dp * pun_ref[...])
        dps = (dp * vrow(V_PSC)).astype(BF16)
        dpool = dpool_ref[...]
        for g in range(4):
            gs = slice(PG * g, PG * g + PG)
            dd_ref[:, gs] = _dot_nt(dps[:, gs], pw_ref[g])
            dpw_ref[g] += _dot_tn(dpool[:, gs], dps[:, gs])

        @pl.when(i == nt - 1)
        def _():
            pltpu.sync_copy(dwo_s, dwo_ref)

    return pl.pallas_call(
        body,
        name="merge_bwd",
        grid=(nt,),
        in_specs=[
            pl.BlockSpec((2, TM, D), lambda i: (0, i, 0)),
            _tiles(TM, D), _tiles(TM, D), _tiles(TM, D), _tiles(TM, D), _tiles(TM, 2 * D),
            pl.BlockSpec((2, TM, D), lambda i: (0, i, 0)),
            _tiles(TM, D), _tiles(TM, D), _tiles(TM, D), _tiles(TM, D),
            _resident((NV, D)), _resident((2 * D, D)), _resident((4, PG, PG)),
        ],
        out_specs=[_tiles(TM, D), _tiles(TM, D), _tiles(TM, D), _tiles(TM, D),
                   _const_out((4, PG, PG)), _const_out((NV, D)), pl.BlockSpec(memory_space=pl.ANY)],
        out_shape=(
            jax.ShapeDtypeStruct((L, D), F32),
            jax.ShapeDtypeStruct((L, D), F32),
            jax.ShapeDtypeStruct((L, D), F32),
            jax.ShapeDtypeStruct((L, D), F32),
            jax.ShapeDtypeStruct((4, PG, PG), F32),
            jax.ShapeDtypeStruct((NV, D), F32),
            jax.ShapeDtypeStruct((2 * D, D), F32),
        ),
        scratch_shapes=[pltpu.VMEM((2 * D, D), F32)],
        compiler_params=_cp(),
    )(dh2p, dpre2, x1, xe, mix, cat, y2, xbc, z, dpool, pun, vec, w_out, pool_w)


def _pool_bwd_call(dd, pmat_t, icnt):
    L = dd.shape[0]
    nt = L // PT
    cur, prev, nxt = _pool_specs(nt)
    icur = pl.BlockSpec((PT, 128), lambda i: (i, 0))
    iprev = pl.BlockSpec((PT, 128), lambda i: (jnp.maximum(i - 1, 0), 0))
    inxt = pl.BlockSpec((PT, 128), lambda i: (jnp.minimum(i + 1, nt - 1), 0))

    def body(cur_ref, prev_ref, next_ref, ic_ref, icp_ref, icn_ref, m_ref, du_ref):
        i = pl.program_id(0)
        n = pl.num_programs(0)
        lane = lax.broadcasted_iota(jnp.int32, (1, 128), 1)
        icv, icp, icn = ic_ref[...], icp_ref[...], icn_ref[...]
        for g in range(4):
            gs = slice(PG * g, PG * g + PG)
            top = jnp.where(i > 0, prev_ref[:, gs] * _column(icp, lane, g), 0.0)
            bot = jnp.where(i < n - 1, next_ref[:, gs] * _column(icn, lane, g), 0.0)
            mid = cur_ref[:, gs]
            ext = jnp.concatenate([top, mid * _column(icv, lane, g), bot], axis=0)
            du_ref[:, gs] = _split_dot(m_ref[g], ext) - mid

    return pl.pallas_call(
        body,
        name="pool_bwd",
        grid=(nt,),
        in_specs=[cur, prev, nxt, icur, iprev, inxt, _resident((4, PT, 3 * PT))],
        out_specs=_tiles(PT, D),
        out_shape=jax.ShapeDtypeStruct((L, D), F32),
        compiler_params=_cp(),
    )(dd, dd, dd, icnt, icnt, icnt, pmat_t)


def _ssd_bwd_call(dy, xbc, dtr, hprev_all, dh_init, par, e_mat, et_mat, dskip, name):
    L = xbc.shape[0]
    nc = L // Q

    def chunk(d, s):
        return jnp.where(d == 0, nc - 1 - s, s)

    def body(dy_ref, xbc_ref, dtr_ref, hp_ref, dhi_ref, par_ref, e_ref, et_ref, dsk_ref,
             dxbc_ref, ddtr_ref, acc_ref, dh0_ref, dh_s, AT, dtT):
        d = pl.program_id(0)
        s = pl.program_id(1)

        @pl.when(s == 0)
        def _():
            dh_s[...] = dhi_ref[0]
            acc_ref[...] = jnp.zeros_like(acc_ref)

        q = _ssd_common(d, dtr_ref[...], par_ref)
        A, maskf, lane, dt, atot = q["A"], q["maskf"], q["lane"], q["dt"], q["atot"]
        AT[...] = q["AT"]
        dtT[...] = q["dtT"]
        hprev = hp_ref[0, 0]
        hpb = hprev.astype(BF16)
        dh = dh_s[...]
        dhb = dh.astype(BF16)
        xs = xbc_ref[:, 0:1024]
        bb = xbc_ref[:, 1024:1152].astype(BF16)
        cb = xbc_ref[:, 1152:1280].astype(BF16)
        dy = dy_ref[...]
        ea_f = _head_expand(e_ref, jnp.exp(A))
        ch = _dot_nt(cb, hpb)
        dch = (dy * ea_f).astype(BF16)
        dC = _dot(dch, hpb)
        dhprev = _dot_tn(dch, cb)
        dA = _dot(dy * ch * ea_f, et_ref[...], HI)
        dec = _state_decay(et_ref, q["atot_col"])
        dhprev = dhprev + dh * dec
        rs = jnp.sum(dh * hprev * dec, axis=1, keepdims=True)
        datot = jnp.max(_dot(e_ref[...], jnp.broadcast_to(rs, (D, 128)), HI).T, axis=0, keepdims=True)
        ear = jnp.exp(atot - A)
        wend = ear * dt
        wf = _head_expand(e_ref, wend)
        xw = (xs * wf).astype(BF16)
        dxw = _dot_nt(bb, dhb)
        dB = _dot(xw, dhb)
        dxs = dxw * wf
        dwend = _dot(dxw * xs, et_ref[...], HI)
        ddt = dwend * ear
        de = dwend * wend
        datot = datot + _rowsum(de)
        dA = dA - de
        g = _dot_nt(cb, bb)
        dG = jnp.zeros((Q, Q), F32)
        ddtT = jnp.zeros((128, Q), F32)
        dAT = jnp.zeros((128, Q), F32)
        sub = lax.broadcasted_iota(jnp.int32, (128, 1), 0)
        dskip = jnp.where(d == 0, dsk_ref[...], 0.0)
        for k in range(NH // 2):
            ks = slice(128 * k, 128 * k + 128)
            xp = xs[:, ks]
            dyp = dy[:, ks]
            accdx = dxs[:, ks] + dyp * dskip[:, ks]
            for half in range(2):
                h = 2 * k + half
                inhead = (lane >= HP) if half else (lane < HP)
                seg = _column(A, lane, h) - AT[h:h + 1, :]
                lm = jnp.exp(jnp.minimum(seg, 0.0)) * maskf
                dtrow = dtT[h:h + 1, :]
                gl = g * lm
                sc = gl * dtrow
                dyh = jnp.where(inhead, dyp, 0.0).astype(BF16)
                xh = jnp.where(inhead, xp, 0.0).astype(BF16)
                dS = _dot_nt(dyh, xh)
                accdx = accdx + _dot(sc.T.astype(BF16), dyh)
                nn = dS * gl
                cn = _rowsum(nn)
                rm = jnp.sum(nn * dtrow, axis=1, keepdims=True)
                dG = dG + dS * (lm * dtrow)
                onsub = (sub == h).astype(F32)
                ddtT = ddtT + onsub * cn
                dAT = dAT - onsub * (cn * dtrow)
                dA = dA + rm * (lane == h).astype(F32)
            dxbc_ref[0, :, ks] = accdx
        dGb = dG.astype(BF16)
        dxbc_ref[0, :, 1024:1152] = dB + _dot_tn(dGb, cb)
        dxbc_ref[0, :, 1152:1280] = dC + _dot(dGb, bb)
        da = _dot_tn(maskf, dA + dAT.T, HI) + datot
        ddt = ddt + ddtT.T + da * q["aneg"]
        ddtr = jnp.where(q["hmask"], ddt * _sigmoid(q["pre"]), 0.0)
        ddtr_ref[...] = ddtr
        acc_ref[0, 0:1, :] += _rowsum(ddtr)
        acc_ref[0, 1:2, :] += _rowsum(da * dt) * q["aneg"]
        dh_s[...] = dhprev
        dh0_ref[0] = dhprev

    return pl.pallas_call(
        body,
        name=name,
        grid=(2, nc),
        in_specs=[
            pl.BlockSpec((Q, D), lambda d, s: (chunk(d, s), 0)),
            pl.BlockSpec((Q, DXBC), lambda d, s: (chunk(d, s), 0)),
            pl.BlockSpec((Q, 128), lambda d, s: (chunk(d, s), d)),
            pl.BlockSpec((1, 1, D, NS), lambda d, s: (d, chunk(d, s), 0, 0)),
            pl.BlockSpec((1, D, NS), lambda d, s: (d, 0, 0)),
            _resident((8, 128)),
            _resident((128, D)),
            _resident((D, 128)),
            _resident((1, D)),
        ],
        out_specs=[
            pl.BlockSpec((1, Q, DXBC), lambda d, s: (d, chunk(d, s), 0)),
            pl.BlockSpec((Q, 128), lambda d, s: (chunk(d, s), d)),
            pl.BlockSpec((1, 8, 128), lambda d, s: (d, 0, 0)),
            pl.BlockSpec((1, D, NS), lambda d, s: (d, 0, 0)),
        ],
        out_shape=(
            jax.ShapeDtypeStruct((2, L, DXBC), F32),
            jax.ShapeDtypeStruct((L, 256), F32),
            jax.ShapeDtypeStruct((2, 8, 128), F32),
            jax.ShapeDtypeStruct((2, D, NS), F32),
        ),
        scratch_shapes=[pltpu.VMEM((D, NS), F32), pltpu.VMEM((128, Q), F32), pltpu.VMEM((128, Q), F32)],
        compiler_params=_cp(2),
    )(dy, xbc, dtr, hprev_all, dh_init, par, e_mat, et_mat, dskip)


def _conv_bwd_call(dxbc2, xbc_raw, cw, acc_init, name):
    L = xbc_raw.shape[0]
    prev, nxt = _halo_specs(TM, DXBC, L)

    def body(dx_ref, cur_ref, prev_ref, next_ref, cw_ref, init_ref, dpre_ref, acc_ref, ext):
        @pl.when(pl.program_id(0) == 0)
        def _():
            acc_ref[...] = init_ref[...]

        _extended(ext, cur_ref, prev_ref, next_ref)
        pre = _conv_pre(ext, cw_ref, TM)
        sg = _sigmoid(pre)
        dpre = (dx_ref[0] + dx_ref[1]) * (sg * (1.0 + pre * (1.0 - sg)))
        dpre_ref[...] = dpre
        for k in range(5):
            acc_ref[k:k + 1, :] += _rowsum(dpre * _shifted(ext, k - 2, TM))
        acc_ref[5:6, :] += _rowsum(dpre)

    return pl.pallas_call(
        body,
        name=name,
        grid=(L // TM,),
        in_specs=[pl.BlockSpec((2, TM, DXBC), lambda i: (0, i, 0)), _tiles(TM, DXBC), prev, nxt,
                  _resident((8, DXBC)), _resident((8, DXBC))],
        out_specs=[_tiles(TM, DXBC), _const_out((8, DXBC))],
        out_shape=(jax.ShapeDtypeStruct((L, DXBC), F32), jax.ShapeDtypeStruct((8, DXBC), F32)),
        scratch_shapes=[pltpu.VMEM((TM + 16, DXBC), F32)],
        compiler_params=_cp(),
    )(dxbc2, xbc_raw, xbc_raw, xbc_raw, cw, acc_init)


def _inproj_bwd_call(dpre, cw, dz, ddtr, dup, h1, dxe_part, x0, vec, w_in, dw_init, name):
    L = x0.shape[0]
    nt = L // TM
    prev, nxt = _halo_specs(TM, DXBC, L)

    def body(cur_ref, prev_ref, next_ref, cw_ref, dz_ref, ddtr_ref, dup_ref, h1_ref, dxe_ref, x0_ref, vec_ref,
             w_ref, dwi_ref, gx_ref, gacc_ref, dw_ref, dw_s, ext):
        i = pl.program_id(0)

        @pl.when(i == 0)
        def _():
            gacc_ref[...] = jnp.zeros_like(gacc_ref)
            pltpu.sync_copy(dwi_ref, dw_s)

        def vrow(r):
            return vec_ref[r:r + 1, :]

        _extended(ext, cur_ref, prev_ref, next_ref)
        dxr = cw_ref[0:1, :] * _shifted(ext, 2, TM)
        for k in range(1, 5):
            dxr = dxr + cw_ref[k:k + 1, :] * _shifted(ext, 2 - k, TM)
        dproj = jnp.concatenate([dz_ref[...], dxr, ddtr_ref[...], dup_ref[...]], axis=1).astype(BF16)
        dh1 = _dot_nt(dproj, w_ref[...])
        _acc_tn(dw_s, h1_ref[...], dproj)
        xe, n0, rstd0 = _ln(x0_ref[...], vrow(V_EMBG), vrow(V_EMBB))
        dxe = dxe_ref[...] + dh1 * (1.0 + vrow(V_SC1))
        gacc_ref[V_SC1:V_SC1 + 1, :] += _rowsum(dh1 * xe)
        gacc_ref[V_SH1:V_SH1 + 1, :] += _rowsum(dh1)
        gacc_ref[V_EMBG:V_EMBG + 1, :] += _rowsum(dxe * n0)
        gacc_ref[V_EMBB:V_EMBB + 1, :] += _rowsum(dxe)
        gx_ref[...] = _ln_bwd(dxe, n0, rstd0, vrow(V_EMBG))

        @pl.when(i == nt - 1)
        def _():
            pltpu.sync_copy(dw_s, dw_ref)

    any_spec = pl.BlockSpec(memory_space=pl.ANY)
    return pl.pallas_call(
        body,
        name=name,
        grid=(nt,),
        in_specs=[_tiles(TM, DXBC), prev, nxt, _resident((8, DXBC)), _tiles(TM, D), _tiles(TM, 256), _tiles(TM, D),
                  _tiles(TM, D), _tiles(TM, D), _tiles(TM, D), _resident((NV, D)), _resident((D, WIN)), any_spec],
        out_specs=[_tiles(TM, D), _const_out((NV, D)), any_spec],
        out_shape=(
            jax.ShapeDtypeStruct((L, D), F32),
            jax.ShapeDtypeStruct((NV, D), F32),
            jax.ShapeDtypeStruct((D, WIN), F32),
        ),
        scratch_shapes=[pltpu.VMEM((D, WIN), F32), pltpu.VMEM((TM + 16, DXBC), F32)],
        compiler_params=_cp(),
    )(dpre, dpre, dpre, cw, dz, ddtr, dup, h1, dxe_part, x0, vec, w_in, dw_init)


def _adamw(w, g, m, v):
    m = ADAM_B1 * m + (1.0 - ADAM_B1) * g
    v = ADAM_B2 * v + (1.0 - ADAM_B2) * (g * g)
    m_hat = m / (1.0 - ADAM_B1 ** ADAM_STEP)
    v_hat = v / (1.0 - ADAM_B2 ** ADAM_STEP)
    delta = -ADAM_LR * (m_hat / (jnp.sqrt(v_hat) + ADAM_EPS) + ADAM_WD * w)
    return delta, m, v


def _adamw_shards_call(gslots, w, m, v):
    R = w.shape[0]
    tr = 8
    for cand in (512, 256, 128, 64, 32, 16, 8):
        if R % cand == 0:
            tr = cand
            break

    def body(gs_ref, w_ref, m_ref, v_ref, g_ref, d_ref, mo_ref, vo_ref):
        g = gs_ref[0]
        for i in range(1, NDEV):
            g = g + gs_ref[i]
        delta, mn, vn = _adamw(w_ref[...], g, m_ref[...], v_ref[...])
        g_ref[...] = g
        d_ref[...] = delta
        mo_ref[...] = mn
        vo_ref[...] = vn

    t = _tiles(tr, D)
    return pl.pallas_call(
        body,
        name="adamw_shards",
        grid=(R // tr,),
        in_specs=[pl.BlockSpec((NDEV, tr, D), lambda i: (0, i, 0)), t, t, t],
        out_specs=[t, t, t, t],
        out_shape=tuple(jax.ShapeDtypeStruct((R, D), F32) for _ in range(4)),
        compiler_params=_cp(),
    )(gslots, w, m, v)


def _wada_call(dm_ex, dm_ctx, silu_all, w, m, v):
    ncol = w.shape[1]

    def body(dme_ref, dmc_ref, s_ref, w_ref, m_ref, v_ref, g_ref, d_ref, mo_ref, vo_ref, ds_ref):
        dmc = _rowsum(dmc_ref[...])
        rows = lax.broadcasted_iota(jnp.int32, (8, 1), 0)
        low = jnp.where(rows == 0, dmc, 0.0)
        dm = jnp.concatenate([dme_ref[...], low], axis=0).astype(BF16)
        wv = w_ref[...]
        g = _dot_tn(s_ref[...].astype(BF16), dm)
        delta, mn, vn = _adamw(wv, g, m_ref[...], v_ref[...])
        g_ref[...] = g
        d_ref[...] = delta
        mo_ref[...] = mn
        vo_ref[...] = vn
        ds_ref[...] = _dot_nt(low.astype(BF16), wv.astype(BF16))

    return pl.pallas_call(
        body,
        name="wada_update",
        out_shape=tuple(jax.ShapeDtypeStruct((D, ncol), F32) for _ in range(4)) + (jax.ShapeDtypeStruct((8, D), F32),),
        compiler_params=pltpu.CompilerParams(vmem_limit_bytes=VMEM_LIMIT),
    )(dm_ex, dm_ctx, silu_all, w, m, v)


P_DMOD, P_DMODC, P_EMBG, P_EMBB, P_LN1G, P_LN1B, P_LN2G, P_LN2B = 0, 6, 8, 9, 10, 11, 12, 13
P_SSDG, P_PSC, P_DSK, P_CONVB, P_DTB, P_ALOG, P_LOSS, NP = 14, 15, 16, 17, 19, 20, 21, 24
S_CCTX, S_EMBG, S_EMBB, S_BADA, S_CONVB, S_DTB, S_ALOG, S_DSK = 0, 1, 2, 3, 9, 11, 12, 13
S_SSDG, S_PSC, S_LN1G, S_LN1B, S_LN2G, S_LN2B, NSM = 14, 15, 16, 17, 18, 19, 24


def _small_update_call(pall, dsil, cctx, w, m, v, et_mat):
    def body(p_ref, ds_ref, c_ref, w_ref, m_ref, v_ref, et_ref, g_ref, d_ref, mo_ref, vo_ref, loss_ref,
             tot, dsum, dsk8):
        tot[...] = p_ref[0]
        dsum[...] = ds_ref[0]
        for i in range(1, NDEV):
            tot[...] += p_ref[i]
            dsum[...] += ds_ref[i]
        cv = c_ref[...]
        sc = _sigmoid(cv)
        g_ref[...] = jnp.zeros_like(g_ref)
        g_ref[S_CCTX:S_CCTX + 1, :] = dsum[0:1, :] * (sc * (1.0 + cv * (1.0 - sc)))
        g_ref[S_EMBG:S_EMBG + 1, :] = tot[P_EMBG:P_EMBG + 1, :]
        g_ref[S_EMBB:S_EMBB + 1, :] = tot[P_EMBB:P_EMBB + 1, :]
        g_ref[S_BADA:S_BADA + 2, :] = tot[P_DMOD:P_DMOD + 2, :] + tot[P_DMODC:P_DMODC + 2, :]
        g_ref[S_BADA + 2:S_BADA + 6, :] = tot[P_DMOD + 2:P_DMOD + 6, :]
        g_ref[S_CONVB:S_CONVB + 2, :] = tot[P_CONVB:P_CONVB + 2, :]
        g_ref[S_DTB:S_DTB + 1, :] = tot[P_DTB:P_DTB + 1, :]
        g_ref[S_ALOG:S_ALOG + 1, :] = tot[P_ALOG:P_ALOG + 1, :]
        dsk8[...] = _dot(jnp.broadcast_to(tot[P_DSK:P_DSK + 1, :], (8, D)), et_ref[...], HI)
        g_ref[S_DSK:S_DSK + 1, 0:128] = dsk8[0:1, :]
        g_ref[S_SSDG:S_SSDG + 1, :] = tot[P_SSDG:P_SSDG + 1, :]
        g_ref[S_PSC:S_PSC + 1, :] = tot[P_PSC:P_PSC + 1, :]
        g_ref[S_LN1G:S_LN1G + 1, :] = tot[P_LN1G:P_LN1G + 1, :]
        g_ref[S_LN1B:S_LN1B + 1, :] = tot[P_LN1B:P_LN1B + 1, :]
        g_ref[S_LN2G:S_LN2G + 1, :] = tot[P_LN2G:P_LN2G + 1, :]
        g_ref[S_LN2B:S_LN2B + 1, :] = tot[P_LN2B:P_LN2B + 1, :]
        delta, mn, vn = _adamw(w_ref[...], g_ref[...], m_ref[...], v_ref[...])
        d_ref[...] = delta
        mo_ref[...] = mn
        vo_ref[...] = vn
        loss_ref[...] = jnp.broadcast_to(tot[P_LOSS:P_LOSS + 1, 0:128], (8, 128))

    return pl.pallas_call(
        body,
        name="small_update",
        out_shape=tuple(jax.ShapeDtypeStruct((NSM, D), F32) for _ in range(4)) + (jax.ShapeDtypeStruct((8, 128), F32),),
        scratch_shapes=[pltpu.VMEM((NP, D), F32), pltpu.VMEM((8, D), F32), pltpu.VMEM((8, 128), F32)],
        compiler_params=pltpu.CompilerParams(vmem_limit_bytes=VMEM_LIMIT),
    )(pall, dsil, cctx, w, m, v, et_mat)


def _pad_rows(flat, mult=16):
    n = flat.shape[0]
    rows = -(-n // D)
    rows = -(-rows // mult) * mult
    return jnp.pad(flat, (0, rows * D - n)).reshape(rows, D)


def _pad_rows2(flat2, mult=16):
    n = flat2.shape[1]
    rows = -(-n // D)
    rows = -(-rows // mult) * mult
    return jnp.pad(flat2, ((0, 0), (0, rows * D - n))).reshape(NDEV, rows, D)


_SHARD_SHAPES = (
    ("in_proj", (D, DIN // NDEV)),
    ("conv_w", (5, DXBC // NDEV)),
    ("pool_w", (4, PG // NDEV, PG)),
    ("w_out", (2 * D // NDEV, D)),
    ("w_gate", (D, DFF // NDEV)),
    ("w_up", (D, DFF // NDEV)),
    ("w_down", (DFF // NDEV, D)),
)


def _pack_shards(parts):
    return _pad_rows(jnp.concatenate([p.reshape(-1) for p in parts]), 256)


def _unpack_shards(rows, lead=()):
    flat = rows.reshape(lead + (-1,))
    out, off = [], 0
    for _, shp in _SHARD_SHAPES:
        n = math.prod(shp)
        out.append(flat[..., off:off + n].reshape(lead + shp))
        off += n
    return out


def _pool_constants(L):
    rows = L // GW
    t_r = jnp.arange(PT) // GW
    t_c = jnp.arange(PT) % GW
    e_r = jnp.arange(3 * PT) // GW - PR
    e_c = jnp.arange(3 * PT) % GW
    fw, bw, ic = [], [], []
    pos_r = jnp.arange(L) // GW
    pos_c = jnp.arange(L) % GW
    for w in WINDOWS:
        lo, hi = -(w // 2), w - w // 2 - 1
        dr = e_r[None, :] - t_r[:, None]
        dc = e_c[None, :] - t_c[:, None]
        fw.append(((dr >= lo) & (dr <= hi) & (dc >= lo) & (dc <= hi)).astype(BF16))
        bw.append(((-dr >= lo) & (-dr <= hi) & (-dc >= lo) & (-dc <= hi)).astype(BF16))
        cr = jnp.minimum(pos_r + hi, rows - 1) - jnp.maximum(pos_r + lo, 0) + 1
        cc = jnp.minimum(pos_c + hi, GW - 1) - jnp.maximum(pos_c + lo, 0) + 1
        ic.append(1.0 / (cr * cc).astype(F32))
    icnt = jnp.pad(jnp.stack(ic, axis=1), ((0, 0), (0, 124)))
    return jnp.stack(fw), jnp.stack(bw), icnt


def _head_matrices():
    hp = jnp.arange(D) // HP
    e = (jnp.arange(128)[:, None] == hp[None, :]).astype(F32)
    return e, e.T


def _aligned_in_proj(w):
    zpad = jnp.zeros((D, 128 - NH), w.dtype)
    return jnp.concatenate([w[:, 0:2304], w[:, 2304:2320], zpad, w[:, 2320:2336], zpad, w[:, 2336:3360]], axis=1)


def _unaligned_in_proj(dw):
    return jnp.concatenate([dw[:, 0:2304], dw[:, 2304:2320], dw[:, 2432:2448], dw[:, 2560:3584]], axis=1)


def _row(v):
    return v.reshape(1, -1).astype(F32)


def _pad_lanes(v, width=D):
    v = v.reshape(1, -1)
    return jnp.pad(v, ((0, 0), (0, width - v.shape[1])))


def kernel(x, c, ctx, c_ctx, emb_ln_g, emb_ln_b, w_ada, b_ada, in_proj, conv_w, conv_b, dt_bias, a_log, d_skip, ssd_norm_g, pool_w, pool_scale, w_out, ln1_g, ln1_b, w_gate, w_up, w_down, ln2_g, ln2_b, loss_target, m_c_ctx, m_emb_ln_g, m_emb_ln_b, m_w_ada, m_b_ada, m_in_proj, m_conv_w, m_conv_b, m_dt_bias, m_a_log, m_d_skip, m_ssd_norm_g, m_pool_w, m_pool_scale, m_w_out, m_ln1_g, m_ln1_b, m_w_gate, m_w_up, m_w_down, m_ln2_g, m_ln2_b, v_c_ctx, v_emb_ln_g, v_emb_ln_b, v_w_ada, v_b_ada, v_in_proj, v_conv_w, v_conv_b, v_dt_bias, v_a_log, v_d_skip, v_ssd_norm_g, v_pool_w, v_pool_scale, v_w_out, v_ln1_g, v_ln1_b, v_w_gate, v_w_up, v_w_down, v_ln2_g, v_ln2_b):
    me = 4 * lax.axis_index("x") + 2 * lax.axis_index("y") + lax.axis_index("c")
    x0 = x[0]
    ctx0 = ctx[0]
    tgt = loss_target[0]
    L = x0.shape[0]
    LC = ctx0.shape[0]
    ncol_ada = w_ada.shape[2]

    small_in = jnp.concatenate([c.reshape(-1), conv_w.reshape(-1)])
    small_all = _exchange(_pad_rows(small_in, 8), "gather_cond", True)
    c_all = small_all[:, 0, :]
    convw_all = small_all.reshape(NDEV, -1)[:, D:D + 5 * (DXBC // NDEV)].reshape(NDEV, 5, DXBC // NDEV)
    conv_w_full = jnp.transpose(convw_all, (1, 0, 2)).reshape(5, DXBC)

    shards = (in_proj[0], conv_w[0], pool_w[0], w_out[0], w_gate[0], w_up[0], w_down[0])
    wpack = _pack_shards([s.astype(BF16) for s in shards])
    wall = _exchange(wpack, "gather_weights", True)
    g_inp, _, g_pw, g_wo, g_wg, g_wu, g_wd = _unpack_shards(wall, (NDEV,))
    w_in = _aligned_in_proj(jnp.transpose(g_inp, (1, 0, 2)).reshape(D, DIN))
    pool_w_full = jnp.transpose(g_pw, (1, 0, 2, 3)).reshape(4, PG, PG)
    w_out_full = g_wo.reshape(2 * D, D)
    w_gate_full = jnp.transpose(g_wg, (1, 0, 2)).reshape(D, DFF)
    w_up_full = jnp.transpose(g_wu, (1, 0, 2)).reshape(D, DFF)
    w_down_full = g_wd.reshape(DFF, D)

    c_in = jnp.concatenate([c_all, c_ctx.reshape(1, D), jnp.zeros((7, D), F32)], axis=0)
    b_mine = lax.dynamic_slice(b_ada, (0, me * ncol_ada), (1, ncol_ada))
    silu_all, mod_mine = _mod_call(c_in, w_ada[0], b_mine)
    mod_all = _exchange(mod_mine, "gather_mod", True)
    mod_all = jnp.transpose(mod_all, (1, 0, 2)).reshape(16, 6 * D)
    mod_me = lax.dynamic_slice(mod_all, (me, 0), (1, 6 * D)).reshape(6, D)
    mod_ctx = mod_all[8].reshape(6, D)

    tail = jnp.concatenate([
        _row(emb_ln_g), _row(emb_ln_b), _row(ln1_g), _row(ln1_b), _row(ln2_g), _row(ln2_b),
        _row(ssd_norm_g), _row(pool_scale), _row(jnp.repeat(d_skip.reshape(-1), HP)), jnp.zeros((1, D), F32)], axis=0)
    vec = jnp.concatenate([mod_me, tail], axis=0)
    vec_ctx = jnp.concatenate([mod_ctx, tail], axis=0)

    cw = jnp.concatenate([conv_w_full, conv_b.reshape(1, DXBC), jnp.zeros((2, DXBC), F32)], axis=0)
    par = jnp.concatenate([_pad_lanes(dt_bias[0, 0], 128), _pad_lanes(dt_bias[0, 1], 128),
                           _pad_lanes(a_log[0, 0], 128), _pad_lanes(a_log[0, 1], 128),
                           jnp.zeros((4, 128), F32)], axis=0)
    e_mat, et_mat = _head_matrices()
    pmat, pmat_t, icnt = _pool_constants(L)
    dskip_row = vec[V_DSK:V_DSK + 1]

    xe_c, h1_c, _, xbcr_c, dtr_c, _ = _f1_call(ctx0, vec_ctx, w_in, "inproj_fwd_ctx")
    xbc_c = _f2_call(xbcr_c, cw, "conv_fwd_ctx")
    hzero = jnp.zeros((2, D, NS), F32)
    _, hprev_c, hfin_c = _ssd_fwd_call(xbc_c, dtr_c, hzero, par, e_mat, et_mat, "ssd_fwd_ctx")

    xe, h1, z, xbcr, dtr, up = _f1_call(x0, vec, w_in, "inproj_fwd")
    xbc = _f2_call(xbcr, cw, "conv_fwd")
    y2, hprev, _ = _ssd_fwd_call(xbc, dtr, hfin_c, par, e_mat, et_mat, "ssd_fwd")
    dpool, pun = _pool_fwd_call(up, pmat, icnt, pool_w_full)
    x1, mix, cat = _merge_call(y2, xbc, z, pun, xe, vec, w_out_full)
    dpre2, gacc_f = _ffn_fwd_call(x1, tgt, vec, w_gate_full, w_up_full, w_down_full)

    dh2p, dwg2, dwu2, dwd2 = _ffn_bwd_call(x1, dpre2, vec, w_gate_full, w_up_full, w_down_full)
    dxe_part, dy, dz, dd, dpw, gacc_m, dwo = _merge_bwd_call(
        dh2p, dpre2, x1, xe, mix, cat, y2, xbc, z, dpool, pun, vec, w_out_full, pool_w_full)
    dup = _pool_bwd_call(dd, pmat_t, icnt)
    dxbc2, ddtr, sacc, dh0 = _ssd_bwd_call(dy, xbc, dtr, hprev, hzero, par, e_mat, et_mat, dskip_row, "ssd_bwd")
    zeros_c = jnp.zeros((LC, D), F32)
    dxbc2_c, ddtr_c, sacc_c, _ = _ssd_bwd_call(zeros_c, xbc_c, dtr_c, hprev_c, dh0, par, e_mat, et_mat,
                                                jnp.zeros((1, D), F32), "ssd_bwd_ctx")
    dprec_c, cacc_c = _conv_bwd_call(dxbc2_c, xbcr_c, cw, jnp.zeros((8, DXBC), F32), "conv_bwd_ctx")
    _, gacc_c, dwin_c = _inproj_bwd_call(dprec_c, cw, zeros_c, ddtr_c, zeros_c, h1_c, zeros_c, ctx0, vec_ctx, w_in,
                                         jnp.zeros((D, WIN), F32), "inproj_bwd_ctx")
    dprec, cacc = _conv_bwd_call(dxbc2, xbcr, cw, cacc_c, "conv_bwd")
    grad_x, gacc_i, dwin = _inproj_bwd_call(dprec, cw, dz, ddtr, dup, h1, dxe_part, x0, vec, w_in, dwin_c, "inproj_bwd")

    gsum = gacc_f + gacc_m + gacc_i
    sa = sacc + sacc_c
    dtb_row = _pad_lanes(jnp.concatenate([sa[0, 0, 0:NH], sa[1, 0, 0:NH]]))
    alog_row = _pad_lanes(jnp.concatenate([sa[0, 1, 0:NH], sa[1, 1, 0:NH]]))
    convb_rows = jnp.pad(cacc[5], (0, 2 * D - DXBC)).reshape(2, D)
    pack = jnp.concatenate([
        gsum[V_SH1:V_G2 + 1],
        gacc_c[V_SH1:V_SC1 + 1],
        gsum[V_EMBG:V_EMBB + 1] + gacc_c[V_EMBG:V_EMBB + 1],
        gsum[V_LN1G:V_LN2B + 1],
        gsum[V_SSDG:V_DSK + 1],
        convb_rows, dtb_row, alog_row,
        gsum[V_LOSS:V_LOSS + 1],
        jnp.zeros((NP - 22, D), F32)], axis=0)
    pall = _exchange(pack, "gather_small_grads", True)

    dm_flat = pall[:, 0:8, :].reshape(NDEV, 8 * D)
    dm_ex = lax.dynamic_slice(dm_flat, (0, me * ncol_ada), (NDEV, ncol_ada))
    dmc_full = jnp.concatenate([dm_flat[:, 6 * D:8 * D], jnp.zeros((NDEV, 4 * D), F32)], axis=1)
    dm_ctx = lax.dynamic_slice(dmc_full, (0, me * ncol_ada), (NDEV, ncol_ada))
    g_wada, d_wada, nm_wada, nv_wada, dsil = _wada_call(dm_ex, dm_ctx, silu_all, w_ada[0], m_w_ada[0], v_w_ada[0])
    dsil_all = _exchange(dsil, "gather_dsilu", True)

    def small_pack(cc, eg, eb, ba, cb_, dtb, al, dsk, sg, ps, l1g, l1b, l2g, l2b):
        return jnp.concatenate([
            _row(cc), _row(eg), _row(eb), ba.reshape(6, D), jnp.pad(cb_.reshape(-1), (0, 2 * D - DXBC)).reshape(2, D),
            _pad_lanes(dtb.reshape(-1)), _pad_lanes(al.reshape(-1)), _pad_lanes(dsk.reshape(-1)),
            _row(sg), _row(ps), _row(l1g), _row(l1b), _row(l2g), _row(l2b), jnp.zeros((NSM - 20, D), F32)], axis=0)

    sw = small_pack(c_ctx, emb_ln_g, emb_ln_b, b_ada, conv_b, dt_bias, a_log, d_skip, ssd_norm_g, pool_scale,
                    ln1_g, ln1_b, ln2_g, ln2_b)
    sm = small_pack(m_c_ctx, m_emb_ln_g, m_emb_ln_b, m_b_ada, m_conv_b, m_dt_bias, m_a_log, m_d_skip, m_ssd_norm_g,
                    m_pool_scale, m_ln1_g, m_ln1_b, m_ln2_g, m_ln2_b)
    sv = small_pack(v_c_ctx, v_emb_ln_g, v_emb_ln_b, v_b_ada, v_conv_b, v_dt_bias, v_a_log, v_d_skip, v_ssd_norm_g,
                    v_pool_scale, v_ln1_g, v_ln1_b, v_ln2_g, v_ln2_b)
    s_g, s_d, s_m, s_v, loss8 = _small_update_call(pall, dsil_all, _row(c_ctx), sw, sm, sv, et_mat)

    def small_unpack(t):
        return (t[S_CCTX], t[S_EMBG], t[S_EMBB], t[S_BADA:S_BADA + 6].reshape(1, 6 * D),
                t[S_CONVB:S_CONVB + 2].reshape(-1)[:DXBC].reshape(1, DXBC),
                t[S_DTB, 0:2 * NH].reshape(1, 2, NH), t[S_ALOG, 0:2 * NH].reshape(1, 2, NH), t[S_DSK, 0:NH].reshape(1, NH),
                t[S_SSDG].reshape(1, D), t[S_PSC].reshape(1, D), t[S_LN1G].reshape(1, D), t[S_LN1B].reshape(1, D),
                t[S_LN2G].reshape(1, D), t[S_LN2B].reshape(1, D))

    def by_cols(dw, n):
        return jnp.transpose(dw.reshape(dw.shape[0], NDEV, n), (1, 0, 2)).reshape(NDEV, -1)

    d_inp = _unaligned_in_proj(dwin)
    d_wg = jnp.transpose(dwg2, (1, 0, 2)).reshape(D, DFF)
    d_wu = jnp.transpose(dwu2, (1, 0, 2)).reshape(D, DFF)
    d_wd = dwd2.reshape(DFF, D)
    gparts = jnp.concatenate([
        by_cols(d_inp, DIN // NDEV),
        by_cols(cacc[0:5], DXBC // NDEV),
        jnp.transpose(dpw.reshape(4, NDEV, PG // NDEV, PG), (1, 0, 2, 3)).reshape(NDEV, -1),
        dwo.reshape(NDEV, -1),
        by_cols(d_wg, DFF // NDEV),
        by_cols(d_wu, DFF // NDEV),
        d_wd.reshape(NDEV, -1)], axis=1)
    gslots = _exchange(_pad_rows2(gparts, 256), "exchange_weight_grads", False)

    wp = _pack_shards(shards)
    mp = _pack_shards((m_in_proj[0], m_conv_w[0], m_pool_w[0], m_w_out[0], m_w_gate[0], m_w_up[0], m_w_down[0]))
    vp = _pack_shards((v_in_proj[0], v_conv_w[0], v_pool_w[0], v_w_out[0], v_w_gate[0], v_w_up[0], v_w_down[0]))
    b_g, b_d, b_m, b_v = _adamw_shards_call(gslots, wp, mp, vp)

    def assemble(small, wada, big):
        (cc, eg, eb, ba, cb_, dtb, al, dsk, sg, ps, l1g, l1b, l2g, l2b) = small_unpack(small)
        inp, cwp, pw, wo, wg, wu, wd = [t[None] for t in _unpack_shards(big)]
        return (cc, eg, eb, wada[None], ba, inp, cwp, cb_, dtb, al, dsk, sg, pw, ps, wo, l1g, l1b, wg, wu, wd, l2g, l2b)

    loss = loss8[0, 0]
    return (loss, grad_x[None], *assemble(s_g, g_wada, b_g), *assemble(s_d, d_wada, b_d),
            *assemble(s_m, nm_wada, b_m), *assemble(s_v, nv_wada, b_v))
```

```python
import functools
import math

import jax
import jax.numpy as jnp
from jax import lax
from jax.experimental import pallas as pl
from jax.experimental.pallas import tpu as pltpu

F32 = jnp.float32
BF16 = jnp.bfloat16
HI = lax.Precision.HIGHEST

NDEV = 8
D = 1024
NH = 16
HP = 64
NS = 128
Q = 128
DXBC = 1280
DFF = 2816
FFC = 1408
GW = 64
PR = 8
PT = PR * GW
WINDOWS = (2, 4, 8, 16)
PG = 256
DIN = 3360
WIN = 3584
ALPHA = 2.0 ** 0.25
LN_EPS = 1e-5
TM = 256

ADAM_LR = 0.001
ADAM_B1 = 0.9
ADAM_B2 = 0.999
ADAM_EPS = 1e-08
ADAM_WD = 0.01
ADAM_STEP = 10

V_SH1, V_SC1, V_G1, V_SH2, V_SC2, V_G2 = 0, 1, 2, 3, 4, 5
V_EMBG, V_EMBB, V_LN1G, V_LN1B, V_LN2G, V_LN2B = 6, 7, 8, 9, 10, 11
V_SSDG, V_PSC, V_DSK, V_LOSS = 12, 13, 14, 15
NV = 16

VMEM_LIMIT = 60 * 1024 * 1024


def _cp(ndim=1):
    return pltpu.CompilerParams(dimension_semantics=("arbitrary",) * ndim, vmem_limit_bytes=VMEM_LIMIT)


def _dot(a, b, precision=None):
    return jnp.dot(a, b, preferred_element_type=F32, precision=precision)


def _dot_nt(a, b):
    return lax.dot_general(a, b, (((1,), (1,)), ((), ())), preferred_element_type=F32)


def _dot_tn(a, b, precision=None):
    return lax.dot_general(a, b, (((0,), (0,)), ((), ())), preferred_element_type=F32, precision=precision)


def _split2(x):
    hi = x.astype(BF16)
    return hi, (x - hi.astype(F32)).astype(BF16)


def _split_dot(m, x):
    hi, lo = _split2(x)
    return _dot(m, hi) + _dot(m, lo)


def _dot_split(x, m):
    hi, lo = _split2(x)
    return _dot(hi, m) + _dot(lo, m)


def _sigmoid(x):
    return 1.0 / (1.0 + jnp.exp(-x))


def _softplus(x):
    return jnp.maximum(x, 0.0) + jnp.log(1.0 + jnp.exp(-jnp.abs(x)))


def _ln(x, g, b):
    mu = jnp.mean(x, axis=-1, keepdims=True)
    xc = x - mu
    var = jnp.mean(xc * xc, axis=-1, keepdims=True)
    rstd = lax.rsqrt(var + LN_EPS)
    n = xc * rstd
    return n * g + b, n, rstd


def _ln_bwd(dy, n, rstd, g):
    dn = dy * g
    return rstd * (dn - jnp.mean(dn, axis=-1, keepdims=True) - n * jnp.mean(dn * n, axis=-1, keepdims=True))


def _rowsum(x):
    return jnp.sum(x, axis=0, keepdims=True)


def _resident(shape):
    nd = len(shape)
    return pl.BlockSpec(shape, lambda *_: (0,) * nd, pipeline_mode=pl.Buffered(1))


def _const_out(shape):
    nd = len(shape)
    return pl.BlockSpec(shape, lambda *_: (0,) * nd)


def _tiles(tm, width):
    return pl.BlockSpec((tm, width), lambda i: (i, 0))


def _halo_specs(tm, width, n_rows):
    r = tm // 8
    last = n_rows // 8 - 1
    prev = pl.BlockSpec((8, width), lambda i: (jnp.maximum(i * r - 1, 0), 0))
    nxt = pl.BlockSpec((8, width), lambda i: (jnp.minimum((i + 1) * r, last), 0))
    return prev, nxt


def _acc_tn(acc_ref, a, b, chunk=512):
    n = b.shape[1]
    for c0 in range(0, n, chunk):
        c1 = min(c0 + chunk, n)
        acc_ref[:, c0:c1] += _dot_tn(a, b[:, c0:c1])


def _my_coords():
    return lax.axis_index("x"), lax.axis_index("y"), lax.axis_index("c")


def _peer(k, mx, my, mc):
    kx, ky, kc = (k >> 2) & 1, (k >> 1) & 1, k & 1
    px = 1 - mx if kx else mx
    py = 1 - my if ky else my
    pc = 1 - mc if kc else mc
    return px, py, pc


class _Exchange:
    def __init__(self, srcs, dsts, send_sems, recv_sems, local_sems, gather):
        self.srcs, self.dsts, self.gather = srcs, dsts, gather
        self.send_sems, self.recv_sems, self.local_sems = send_sems, recv_sems, local_sems

    def _copies(self, outgoing):
        mx, my, mc = _my_coords()
        me = 4 * mx + 2 * my + mc
        local, remote = [], []
        for t, (src, dst) in enumerate(zip(self.srcs, self.dsts)):
            local.append(pltpu.make_async_copy(src if self.gather else src.at[me], dst.at[me], self.local_sems.at[t]))
            for k in range(1, NDEV):
                px, py, pc = _peer(k, mx, my, mc)
                pid = 4 * px + 2 * py + pc
                remote.append(pltpu.make_async_remote_copy(
                    src_ref=src if self.gather else src.at[pid],
                    dst_ref=dst.at[me] if outgoing else dst.at[pid],
                    send_sem=self.send_sems.at[t, k - 1],
                    recv_sem=self.recv_sems.at[t, k - 1],
                    device_id=(px, py, pc),
                    device_id_type=pl.DeviceIdType.MESH,
                ))
        return local, remote

    def start(self):
        local, remote = self._copies(True)
        for cp in local + remote:
            cp.start()

    def wait(self):
        local, sends = self._copies(True)
        _, recvs = self._copies(False)
        for cp in recvs:
            cp.wait_recv()
        for cp in sends:
            cp.wait_send()
        for cp in local:
            cp.wait()


def _exchange_sems(n):
    return [pltpu.SemaphoreType.DMA((n, NDEV - 1)), pltpu.SemaphoreType.DMA((n, NDEV - 1)), pltpu.SemaphoreType.DMA((n,))]


def _exchange_out_shapes(xs, gather):
    return [jax.ShapeDtypeStruct(x.shape if not gather else (NDEV,) + x.shape, x.dtype) for x in xs]


def _exchange(xs, name, gather):
    n = len(xs)

    def body(*refs):
        ex = _Exchange(refs[:n], refs[n:2 * n], *refs[2 * n:], gather)
        ex.start()
        ex.wait()

    any_spec = pl.BlockSpec(memory_space=pl.ANY)
    return pl.pallas_call(
        body,
        name=name,
        out_shape=_exchange_out_shapes(xs, gather),
        in_specs=[any_spec] * n,
        out_specs=[any_spec] * n,
        scratch_shapes=_exchange_sems(n),
    )(*xs)


def _mod_call(c_all, w_ada, b_ada):
    ncol = w_ada.shape[1]

    def body(c_ref, w_ref, b_ref, silu_ref, mod_ref):
        cv = c_ref[...]
        s = cv * _sigmoid(cv)
        silu_ref[...] = s
        mod_ref[...] = _dot(s.astype(BF16), w_ref[...].astype(BF16)) + b_ref[...]

    return pl.pallas_call(
        body,
        name="mod_fwd",
        out_shape=(jax.ShapeDtypeStruct((16, D), F32), jax.ShapeDtypeStruct((16, ncol), F32)),
    )(c_all, w_ada, b_ada)


def _f1_call(x0, vec, w_in, name):
    L = x0.shape[0]

    def body(x_ref, vec_ref, w_ref, xe_ref, h1_ref, z_ref, xbc_ref, dt_ref, up_ref):
        xe, _, _ = _ln(x_ref[...], vec_ref[V_EMBG:V_EMBG + 1, :], vec_ref[V_EMBB:V_EMBB + 1, :])
        h1 = (xe * (1.0 + vec_ref[V_SC1:V_SC1 + 1, :]) + vec_ref[V_SH1:V_SH1 + 1, :]).astype(BF16)
        proj = _dot(h1, w_ref[...])
        xe_ref[...] = xe
        h1_ref[...] = h1
        z_ref[...] = proj[:, 0:1024]
        xbc_ref[...] = proj[:, 1024:2304]
        dt_ref[...] = proj[:, 2304:2560]
        up_ref[...] = proj[:, 2560:3584]

    return pl.pallas_call(
        body,
        name=name,
        grid=(L // TM,),
        in_specs=[_tiles(TM, D), _resident((NV, D)), _resident((D, WIN))],
        out_specs=[_tiles(TM, D), _tiles(TM, D), _tiles(TM, D), _tiles(TM, DXBC), _tiles(TM, 256), _tiles(TM, D)],
        out_shape=(
            jax.ShapeDtypeStruct((L, D), F32),
            jax.ShapeDtypeStruct((L, D), BF16),
            jax.ShapeDtypeStruct((L, D), F32),
            jax.ShapeDtypeStruct((L, DXBC), F32),
            jax.ShapeDtypeStruct((L, 256), F32),
            jax.ShapeDtypeStruct((L, D), F32),
        ),
        compiler_params=_cp(),
    )(x0, vec, w_in)


def _extended(ext, cur_ref, prev_ref, next_ref):
    i = pl.program_id(0)
    n = pl.num_programs(0)
    tm = cur_ref.shape[0]
    ext[0:8, :] = jnp.where(i > 0, prev_ref[...], 0.0)
    ext[8:8 + tm, :] = cur_ref[...]
    ext[8 + tm:16 + tm, :] = jnp.where(i < n - 1, next_ref[...], 0.0)
    return ext


def _shifted(ext, offset, tm):
    return ext[8 + offset:8 + offset + tm, :]


def _conv_pre(ext, cw_ref, tm):
    acc = cw_ref[5:6, :] + cw_ref[0:1, :] * _shifted(ext, -2, tm)
    for k in range(1, 5):
        acc = acc + cw_ref[k:k + 1, :] * _shifted(ext, k - 2, tm)
    return acc


def _f2_call(xbc_raw, cw, name):
    L = xbc_raw.shape[0]
    prev, nxt = _halo_specs(TM, DXBC, L)

    def body(cur_ref, prev_ref, next_ref, cw_ref, out_ref, ext):
        pre = _conv_pre(_extended(ext, cur_ref, prev_ref, next_ref), cw_ref, TM)
        out_ref[...] = pre * _sigmoid(pre)

    return pl.pallas_call(
        body,
        name=name,
        grid=(L // TM,),
        in_specs=[_tiles(TM, DXBC), prev, nxt, _resident((8, DXBC))],
        out_specs=_tiles(TM, DXBC),
        out_shape=jax.ShapeDtypeStruct((L, DXBC), F32),
        scratch_shapes=[pltpu.VMEM((TM + 16, DXBC), F32)],
        compiler_params=_cp(),
    )(xbc_raw, xbc_raw, xbc_raw, cw)


def _ssd_common(d, dtr, par_ref):
    lane = lax.broadcasted_iota(jnp.int32, (1, 128), 1)
    hmask = lane < NH
    bias = jnp.where(d == 0, par_ref[0:1, :], par_ref[1:2, :])
    alog = jnp.where(d == 0, par_ref[2:3, :], par_ref[3:4, :])
    aneg = jnp.where(hmask, -jnp.exp(alog), 0.0)
    pre = dtr + bias
    dt = jnp.where(hmask, _softplus(pre), 0.0)
    a = dt * aneg
    row = lax.broadcasted_iota(jnp.int32, (Q, Q), 0)
    col = lax.broadcasted_iota(jnp.int32, (Q, Q), 1)
    maskf = jnp.where(d == 0, (row >= col).astype(F32), (row <= col).astype(F32))
    A = _dot(maskf, a, HI)
    atot = _rowsum(a)
    atot_col = jnp.sum(a.T, axis=1, keepdims=True)
    return dict(hmask=hmask, aneg=aneg, pre=pre, dt=dt, a=a, maskf=maskf, A=A, AT=A.T, dtT=dt.T,
                atot=atot, atot_col=atot_col, lane=lane)


def _column(v, lane, h):
    return jnp.sum(jnp.where(lane == h, v, 0.0), axis=1, keepdims=True)


def _head_expand(e_ref, v):
    return _dot_split(v, e_ref[...])


def _head_sum(et_ref, v):
    return _dot_split(v, et_ref[...])


def _state_decay(et_ref, atot_col):
    return _split_dot(et_ref[...], jnp.broadcast_to(jnp.exp(atot_col), (128, 128)))


def _ssd_fwd_call(xbc, dtr, h0, par, e_mat, et_mat, name, gathered=()):
    L = xbc.shape[0]
    nc = L // Q
    ng = len(gathered)

    def chunk(d, s):
        return jnp.where(d == 0, s, nc - 1 - s)

    def body(*refs):
        xbc_ref, dtr_ref, h0_ref, par_ref, e_ref, et_ref = refs[:6]
        y_ref, hp_ref, hf_ref = refs[6 + ng:9 + ng]
        hs, AT, dtT = refs[9 + 2 * ng:12 + 2 * ng]
        d = pl.program_id(0)
        s = pl.program_id(1)
        if ng:
            ex = _Exchange(refs[6:6 + ng], refs[9 + ng:9 + 2 * ng], *refs[12 + 2 * ng:], True)

            @pl.when((d == 0) & (s == 0))
            def _():
                ex.start()

        @pl.when(s == 0)
        def _():
            hs[...] = h0_ref[0]

        q = _ssd_common(d, dtr_ref[...], par_ref)
        A, maskf, lane = q["A"], q["maskf"], q["lane"]
        AT[...] = q["AT"]
        dtT[...] = q["dtT"]
        hprev = hs[...]
        hp_ref[0, 0] = hprev
        bb = xbc_ref[:, 1024:1152].astype(BF16)
        cb = xbc_ref[:, 1152:1280].astype(BF16)
        g = _dot_nt(cb, bb)
        yoff = _dot_nt(cb, hprev.astype(BF16)) * _head_expand(e_ref, jnp.exp(A))
        for k in range(NH // 2):
            ks = slice(128 * k, 128 * k + 128)
            xp = xbc_ref[:, ks]
            acc = yoff[:, ks]
            for half in range(2):
                h = 2 * k + half
                seg = _column(A, lane, h) - AT[h:h + 1, :]
                lm = jnp.exp(jnp.minimum(seg, 0.0)) * maskf
                sc = (g * lm * dtT[h:h + 1, :]).astype(BF16)
                inhead = (lane >= HP) if half else (lane < HP)
                acc = acc + _dot(sc, jnp.where(inhead, xp, 0.0).astype(BF16))
            y_ref[0, :, ks] = acc
        wend = jnp.exp(q["atot"] - A) * q["dt"]
        xw = (xbc_ref[:, 0:1024] * _head_expand(e_ref, wend)).astype(BF16)
        hnew = hprev * _state_decay(et_ref, q["atot_col"]) + _dot_tn(xw, bb)
        hs[...] = hnew
        hf_ref[0] = hnew
        if ng:
            @pl.when((d == 1) & (s == nc - 1))
            def _():
                ex.wait()

    any_spec = pl.BlockSpec(memory_space=pl.ANY)
    return pl.pallas_call(
        body,
        name=name,
        grid=(2, nc),
        in_specs=[
            pl.BlockSpec((Q, DXBC), lambda d, s: (chunk(d, s), 0)),
            pl.BlockSpec((Q, 128), lambda d, s: (chunk(d, s), d)),
            pl.BlockSpec((1, D, NS), lambda d, s: (d, 0, 0)),
            _resident((8, 128)),
            _resident((128, D)),
            _resident((D, 128)),
        ] + [any_spec] * ng,
        out_specs=[
            pl.BlockSpec((1, Q, D), lambda d, s: (d, chunk(d, s), 0)),
            pl.BlockSpec((1, 1, D, NS), lambda d, s: (d, chunk(d, s), 0, 0)),
            pl.BlockSpec((1, D, NS), lambda d, s: (d, 0, 0)),
        ] + [any_spec] * ng,
        out_shape=[
            jax.ShapeDtypeStruct((2, L, D), F32),
            jax.ShapeDtypeStruct((2, nc, D, NS), F32),
            jax.ShapeDtypeStruct((2, D, NS), F32),
        ] + _exchange_out_shapes(gathered, True),
        scratch_shapes=[pltpu.VMEM((D, NS), F32), pltpu.VMEM((128, Q), F32), pltpu.VMEM((128, Q), F32)]
        + (_exchange_sems(ng) if ng else []),
        compiler_params=_cp(2),
    )(xbc, dtr, h0, par, e_mat, et_mat, *gathered)


def _pool_specs(n_tiles):
    cur = pl.BlockSpec((PT, D), lambda i: (i, 0))
    prev = pl.BlockSpec((PT, D), lambda i: (jnp.maximum(i - 1, 0), 0))
    nxt = pl.BlockSpec((PT, D), lambda i: (jnp.minimum(i + 1, n_tiles - 1), 0))
    return cur, prev, nxt


def _pool_fwd_call(up, pmat, icnt, pool_w):
    L = up.shape[0]
    nt = L // PT
    cur, prev, nxt = _pool_specs(nt)

    def body(cur_ref, prev_ref, next_ref, m_ref, ic_ref, pw_ref, d_ref, pun_ref):
        i = pl.program_id(0)
        n = pl.num_programs(0)
        lane = lax.broadcasted_iota(jnp.int32, (1, 128), 1)
        icv = ic_ref[...]
        for g in range(4):
            gs = slice(PG * g, PG * g + PG)
            top = jnp.where(i > 0, prev_ref[:, gs], 0.0)
            bot = jnp.where(i < n - 1, next_ref[:, gs], 0.0)
            mid = cur_ref[:, gs]
            box = _split_dot(m_ref[g], jnp.concatenate([top, mid, bot], axis=0))
            dg = (box * _column(icv, lane, g) - mid).astype(BF16)
            d_ref[:, gs] = dg
            pun_ref[:, gs] = _dot(dg, pw_ref[g])

    return pl.pallas_call(
        body,
        name="pool_fwd",
        grid=(nt,),
        in_specs=[cur, prev, nxt, _resident((4, PT, 3 * PT)), _tiles(PT, 128), _resident((4, PG, PG))],
        out_specs=[_tiles(PT, D), _tiles(PT, D)],
        out_shape=(jax.ShapeDtypeStruct((L, D), BF16), jax.ShapeDtypeStruct((L, D), F32)),
        compiler_params=_cp(),
    )(up, up, up, pmat, icnt, pool_w)


def _gated(y2_ref, xs, z, vec_ref):
    ym = y2_ref[0] + y2_ref[1] + vec_ref[V_DSK:V_DSK + 1, :] * xs
    sz = _sigmoid(z)
    gated = ym * (z * sz)
    r = lax.rsqrt(jnp.mean(gated * gated, axis=-1, keepdims=True) + LN_EPS)
    return ym, sz, gated, r


def _merge_call(y2, xbc, z, pun, xe, vec, w_out):
    L = z.shape[0]

    def body(y2_ref, xs_ref, z_ref, pun_ref, xe_ref, vec_ref, w_ref, x1_ref, mix_ref, cat_ref):
        _, _, gated, r = _gated(y2_ref, xs_ref[...], z_ref[...], vec_ref)
        yn = gated * r * vec_ref[V_SSDG:V_SSDG + 1, :]
        p = pun_ref[...] * vec_ref[V_PSC:V_PSC + 1, :]
        cat = jnp.concatenate([yn, p], axis=1).astype(BF16)
        mix = _dot(cat, w_ref[...])
        pre1 = ALPHA * xe_ref[...] + vec_ref[V_G1:V_G1 + 1, :] * mix
        x1, _, _ = _ln(pre1, vec_ref[V_LN1G:V_LN1G + 1, :], vec_ref[V_LN1B:V_LN1B + 1, :])
        x1_ref[...] = x1
        mix_ref[...] = mix
        cat_ref[...] = cat

    return pl.pallas_call(
        body,
        name="merge_fwd",
        grid=(L // TM,),
        in_specs=[
            pl.BlockSpec((2, TM, D), lambda i: (0, i, 0)),
            _tiles(TM, D), _tiles(TM, D), _tiles(TM, D), _tiles(TM, D),
            _resident((NV, D)), _resident((2 * D, D)),
        ],
        out_specs=[_tiles(TM, D), _tiles(TM, D), _tiles(TM, 2 * D)],
        out_shape=(
            jax.ShapeDtypeStruct((L, D), F32),
            jax.ShapeDtypeStruct((L, D), F32),
            jax.ShapeDtypeStruct((L, 2 * D), BF16),
        ),
        compiler_params=_cp(),
    )(y2, xbc, z, pun, xe, vec, w_out)


def _ffn_fwd_call(x1, tgt, vec, w_gate, w_up, w_down):
    L = x1.shape[0]

    def body(x1_ref, tgt_ref, vec_ref, wg_ref, wu_ref, wd_ref, dpre_ref, gacc_ref):
        @pl.when(pl.program_id(0) == 0)
        def _():
            gacc_ref[...] = jnp.zeros_like(gacc_ref)

        x1 = x1_ref[...]
        h2 = (x1 * (1.0 + vec_ref[V_SC2:V_SC2 + 1, :]) + vec_ref[V_SH2:V_SH2 + 1, :]).astype(BF16)
        gt = _dot(h2, wg_ref[...])
        f = (gt * _sigmoid(gt) * _dot(h2, wu_ref[...])).astype(BF16)
        ffn = _dot(f, wd_ref[...])
        g2 = vec_ref[V_G2:V_G2 + 1, :]
        lng = vec_ref[V_LN2G:V_LN2G + 1, :]
        x2, n2, rstd2 = _ln(ALPHA * x1 + g2 * ffn, lng, vec_ref[V_LN2B:V_LN2B + 1, :])
        diff = x2 - tgt_ref[...]
        dx2 = diff * (1.0 / D)
        dpre2 = _ln_bwd(dx2, n2, rstd2, lng)
        dpre_ref[...] = dpre2
        gacc_ref[V_LN2G:V_LN2G + 1, :] += _rowsum(dx2 * n2)
        gacc_ref[V_LN2B:V_LN2B + 1, :] += _rowsum(dx2)
        gacc_ref[V_G2:V_G2 + 1, :] += _rowsum(dpre2 * ffn)
        gacc_ref[V_LOSS:V_LOSS + 1, :] += jnp.sum(diff * diff) * (0.5 / D)

    return pl.pallas_call(
        body,
        name="ffn_fwd",
        grid=(L // TM,),
        in_specs=[_tiles(TM, D), _tiles(TM, D), _resident((NV, D)),
                  _resident((D, DFF)), _resident((D, DFF)), _resident((DFF, D))],
        out_specs=[_tiles(TM, D), _const_out((NV, D))],
        out_shape=(jax.ShapeDtypeStruct((L, D), F32), jax.ShapeDtypeStruct((NV, D), F32)),
        compiler_params=_cp(),
    )(x1, tgt, vec, w_gate, w_up, w_down)


def _ffn_bwd_call(x1, dpre2, vec, w_gate, w_up, w_down):
    L = x1.shape[0]
    nt = L // TM
    nj = DFF // FFC

    def body(x1_ref, dpre_ref, vec_ref, wg_ref, wu_ref, wd_ref, dh2_ref, dwg_ref, dwu_ref, dwd_ref, ag, au, ad):
        j = pl.program_id(0)
        i = pl.program_id(1)

        @pl.when(i == 0)
        def _():
            ag[...] = jnp.zeros_like(ag)
            au[...] = jnp.zeros_like(au)
            ad[...] = jnp.zeros_like(ad)

        h2 = (x1_ref[...] * (1.0 + vec_ref[V_SC2:V_SC2 + 1, :]) + vec_ref[V_SH2:V_SH2 + 1, :]).astype(BF16)
        gt = _dot(h2, wg_ref[...])
        up = _dot(h2, wu_ref[...])
        sg = _sigmoid(gt)
        sl = gt * sg
        f = (sl * up).astype(BF16)
        dffn = (vec_ref[V_G2:V_G2 + 1, :] * dpre_ref[...]).astype(BF16)
        df = _dot_nt(dffn, wd_ref[...])
        dgt = (df * up * (sg * (1.0 + gt * (1.0 - sg)))).astype(BF16)
        dup = (df * sl).astype(BF16)
        dh2_ref[0] = _dot_nt(dgt, wg_ref[...]) + _dot_nt(dup, wu_ref[...])
        _acc_tn(ag, h2, dgt)
        _acc_tn(au, h2, dup)
        _acc_tn(ad, f, dffn)

        @pl.when(i == nt - 1)
        def _():
            pltpu.sync_copy(ag, dwg_ref.at[j])
            pltpu.sync_copy(au, dwu_ref.at[j])
            pltpu.sync_copy(ad, dwd_ref.at[j])

    any_spec = pl.BlockSpec(memory_space=pl.ANY)
    return pl.pallas_call(
        body,
        name="ffn_bwd",
        grid=(nj, nt),
        in_specs=[
            pl.BlockSpec((TM, D), lambda j, i: (i, 0)),
            pl.BlockSpec((TM, D), lambda j, i: (i, 0)),
            _resident((NV, D)),
            pl.BlockSpec((D, FFC), lambda j, i: (0, j)),
            pl.BlockSpec((D, FFC), lambda j, i: (0, j)),
            pl.BlockSpec((FFC, D), lambda j, i: (j, 0)),
        ],
        out_specs=[pl.BlockSpec((1, TM, D), lambda j, i: (j, i, 0)), any_spec, any_spec, any_spec],
        out_shape=(
            jax.ShapeDtypeStruct((nj, L, D), F32),
            jax.ShapeDtypeStruct((nj, D, FFC), F32),
            jax.ShapeDtypeStruct((nj, D, FFC), F32),
            jax.ShapeDtypeStruct((nj, FFC, D), F32),
        ),
        scratch_shapes=[pltpu.VMEM((D, FFC), F32), pltpu.VMEM((D, FFC), F32), pltpu.VMEM((FFC, D), F32)],
        compiler_params=_cp(2),
    )(x1, dpre2, vec, w_gate, w_up, w_down)


def _merge_bwd_call(dh2p, dpre2, x1, xe, mix, cat, y2, xbc, z, dpool, pun, vec, w_out, pool_w, scattered):
    L = z.shape[0]
    nt = L // TM
    ns = len(scattered)

    def body(*refs):
        (dh2_ref, dpre2_ref, x1_ref, xe_ref, mix_ref, cat_ref, y2_ref, xs_ref, z_ref, dpool_ref, pun_ref,
         vec_ref, w_ref, pw_ref) = refs[:14]
        dxe_ref, dy_ref, dz_ref, dd_ref, dpw_ref, gacc_ref, dwo_ref = refs[14 + ns:21 + ns]
        dwo_s = refs[21 + 2 * ns]
        ex = _Exchange(refs[14:14 + ns], refs[21 + ns:21 + 2 * ns], *refs[22 + 2 * ns:], False)
        i = pl.program_id(0)

        @pl.when(i == 0)
        def _():
            ex.start()
            gacc_ref[...] = jnp.zeros_like(gacc_ref)
            dpw_ref[...] = jnp.zeros_like(dpw_ref)
            dwo_s[...] = jnp.zeros_like(dwo_s)

        def vrow(r):
            return vec_ref[r:r + 1, :]

        def gadd(r, val):
            gacc_ref[r:r + 1, :] += _rowsum(val)

        x1 = x1_ref[...]
        dh2 = dh2_ref[0] + dh2_ref[1]
        dx1 = ALPHA * dpre2_ref[...] + dh2 * (1.0 + vrow(V_SC2))
        gadd(V_SC2, dh2 * x1)
        gadd(V_SH2, dh2)
        mix = mix_ref[...]
        _, n1, rstd1 = _ln(ALPHA * xe_ref[...] + vrow(V_G1) * mix, vrow(V_LN1G), vrow(V_LN1B))
        gadd(V_LN1G, dx1 * n1)
        gadd(V_LN1B, dx1)
        dpre1 = _ln_bwd(dx1, n1, rstd1, vrow(V_LN1G))
        dxe_ref[...] = ALPHA * dpre1
        gadd(V_G1, dpre1 * mix)
        dmix = (vrow(V_G1) * dpre1).astype(BF16)
        dcat = _dot_nt(dmix, w_ref[...])
        cat = cat_ref[...]
        for c0 in range(0, 2 * D, 512):
            dwo_s[c0:c0 + 512, :] += _dot_tn(cat[:, c0:c0 + 512], dmix)
        dyn = dcat[:, 0:D]
        dp = dcat[:, D:2 * D]
        xs = xs_ref[...]
        z = z_ref[...]
        ym, sz, gated, r = _gated(y2_ref, xs, z, vec_ref)
        gadd(V_SSDG, dyn * gated * r)
        a = dyn * vrow(V_SSDG)
        dgated = r * a - gated * (r * r * r * jnp.mean(a * gated, axis=-1, keepdims=True))
        dym = dgated * (z * sz)
        dy_ref[...] = dym
        dz_ref[...] = dgated * ym * (sz * (1.0 + z * (1.0 - sz)))
        gadd(V_DSK, dym * xs)
        gadd(V_PSC, dp * pun_ref[...])
        dps = (dp * vrow(V_PSC)).astype(BF16)
        dpool = dpool_ref[...]
        for g in range(4):
            gs = slice(PG * g, PG * g + PG)
            dd_ref[:, gs] = _dot_nt(dps[:, gs], pw_ref[g])
            dpw_ref[g] += _dot_tn(dpool[:, gs], dps[:, gs])

        @pl.when(i == nt - 1)
        def _():
            pltpu.sync_copy(dwo_s, dwo_ref)
            ex.wait()

    any_spec = pl.BlockSpec(memory_space=pl.ANY)
    return pl.pallas_call(
        body,
        name="merge_bwd",
        grid=(nt,),
        in_specs=[
            pl.BlockSpec((2, TM, D), lambda i: (0, i, 0)),
            _tiles(TM, D), _tiles(TM, D), _tiles(TM, D), _tiles(TM, D), _tiles(TM, 2 * D),
            pl.BlockSpec((2, TM, D), lambda i: (0, i, 0)),
            _tiles(TM, D), _tiles(TM, D), _tiles(TM, D), _tiles(TM, D),
            _resident((NV, D)), _resident((2 * D, D)), _resident((4, PG, PG)),
        ] + [any_spec] * ns,
        out_specs=[_tiles(TM, D), _tiles(TM, D), _tiles(TM, D), _tiles(TM, D),
                   _const_out((4, PG, PG)), _const_out((NV, D)), any_spec] + [any_spec] * ns,
        out_shape=[
            jax.ShapeDtypeStruct((L, D), F32),
            jax.ShapeDtypeStruct((L, D), F32),
            jax.ShapeDtypeStruct((L, D), F32),
            jax.ShapeDtypeStruct((L, D), F32),
            jax.ShapeDtypeStruct((4, PG, PG), F32),
            jax.ShapeDtypeStruct((NV, D), F32),
            jax.ShapeDtypeStruct((2 * D, D), F32),
        ] + _exchange_out_shapes(scattered, False),
        scratch_shapes=[pltpu.VMEM((2 * D, D), F32)] + _exchange_sems(ns),
        compiler_params=_cp(),
    )(dh2p, dpre2, x1, xe, mix, cat, y2, xbc, z, dpool, pun, vec, w_out, pool_w, *scattered)


def _pool_bwd_call(dd, pmat_t, icnt):
    L = dd.shape[0]
    nt = L // PT
    cur, prev, nxt = _pool_specs(nt)
    icur = pl.BlockSpec((PT, 128), lambda i: (i, 0))
    iprev = pl.BlockSpec((PT, 128), lambda i: (jnp.maximum(i - 1, 0), 0))
    inxt = pl.BlockSpec((PT, 128), lambda i: (jnp.minimum(i + 1, nt - 1), 0))

    def body(cur_ref, prev_ref, next_ref, ic_ref, icp_ref, icn_ref, m_ref, du_ref):
        i = pl.program_id(0)
        n = pl.num_programs(0)
        lane = lax.broadcasted_iota(jnp.int32, (1, 128), 1)
        icv, icp, icn = ic_ref[...], icp_ref[...], icn_ref[...]
        for g in range(4):
            gs = slice(PG * g, PG * g + PG)
            top = jnp.where(i > 0, prev_ref[:, gs] * _column(icp, lane, g), 0.0)
            bot = jnp.where(i < n - 1, next_ref[:, gs] * _column(icn, lane, g), 0.0)
            mid = cur_ref[:, gs]
            ext = jnp.concatenate([top, mid * _column(icv, lane, g), bot], axis=0)
            du_ref[:, gs] = _split_dot(m_ref[g], ext) - mid

    return pl.pallas_call(
        body,
        name="pool_bwd",
        grid=(nt,),
        in_specs=[cur, prev, nxt, icur, iprev, inxt, _resident((4, PT, 3 * PT))],
        out_specs=_tiles(PT, D),
        out_shape=jax.ShapeDtypeStruct((L, D), F32),
        compiler_params=_cp(),
    )(dd, dd, dd, icnt, icnt, icnt, pmat_t)


def _ssd_bwd_call(dy, xbc, dtr, hprev_all, dh_init, par, e_mat, et_mat, dskip, name):
    L = xbc.shape[0]
    nc = L // Q

    def chunk(d, s):
        return jnp.where(d == 0, nc - 1 - s, s)

    def body(dy_ref, xbc_ref, dtr_ref, hp_ref, dhi_ref, par_ref, e_ref, et_ref, dsk_ref,
             dxbc_ref, ddtr_ref, acc_ref, dh0_ref, dh_s, AT, dtT):
        d = pl.program_id(0)
        s = pl.program_id(1)

        @pl.when(s == 0)
        def _():
            dh_s[...] = dhi_ref[0]
            acc_ref[...] = jnp.zeros_like(acc_ref)

        q = _ssd_common(d, dtr_ref[...], par_ref)
        A, maskf, lane, dt, atot = q["A"], q["maskf"], q["lane"], q["dt"], q["atot"]
        AT[...] = q["AT"]
        dtT[...] = q["dtT"]
        hprev = hp_ref[0, 0]
        hpb = hprev.astype(BF16)
        dh = dh_s[...]
        dhb = dh.astype(BF16)
        xs = xbc_ref[:, 0:1024]
        bb = xbc_ref[:, 1024:1152].astype(BF16)
        cb = xbc_ref[:, 1152:1280].astype(BF16)
        dy = dy_ref[...]
        ea_f = _head_expand(e_ref, jnp.exp(A))
        ch = _dot_nt(cb, hpb)
        dch = (dy * ea_f).astype(BF16)
        dC = _dot(dch, hpb)
        dhprev = _dot_tn(dch, cb)
        dA = _head_sum(et_ref, dy * ch * ea_f)
        dec = _state_decay(et_ref, q["atot_col"])
        dhprev = dhprev + dh * dec
        hs_ = jnp.sum(_split_dot(e_ref[...], dh * hprev * dec), axis=1, keepdims=True)
        datot = jnp.max(jnp.broadcast_to(hs_, (128, 128)).T, axis=0, keepdims=True)
        ear = jnp.exp(atot - A)
        wend = ear * dt
        wf = _head_expand(e_ref, wend)
        xw = (xs * wf).astype(BF16)
        dxw = _dot_nt(bb, dhb)
        dB = _dot(xw, dhb)
        dxs = dxw * wf
        dwend = _head_sum(et_ref, dxw * xs)
        ddt = dwend * ear
        de = dwend * wend
        datot = datot + _rowsum(de)
        dA = dA - de
        g = _dot_nt(cb, bb)
        dG = jnp.zeros((Q, Q), F32)
        ddtT = jnp.zeros((128, Q), F32)
        dAT = jnp.zeros((128, Q), F32)
        sub = lax.broadcasted_iota(jnp.int32, (128, 1), 0)
        dskip = jnp.where(d == 0, dsk_ref[...], 0.0)
        for k in range(NH // 2):
            ks = slice(128 * k, 128 * k + 128)
            xp = xs[:, ks]
            dyp = dy[:, ks]
            accdx = dxs[:, ks] + dyp * dskip[:, ks]
            for half in range(2):
                h = 2 * k + half
                inhead = (lane >= HP) if half else (lane < HP)
                seg = _column(A, lane, h) - AT[h:h + 1, :]
                lm = jnp.exp(jnp.minimum(seg, 0.0)) * maskf
                dtrow = dtT[h:h + 1, :]
                gl = g * lm
                sc = gl * dtrow
                dyh = jnp.where(inhead, dyp, 0.0).astype(BF16)
                xh = jnp.where(inhead, xp, 0.0).astype(BF16)
                dS = _dot_nt(dyh, xh)
                accdx = accdx + _dot(sc.T.astype(BF16), dyh)
                nn = dS * gl
                cn = _rowsum(nn)
                rm = jnp.sum(nn * dtrow, axis=1, keepdims=True)
                dG = dG + dS * (lm * dtrow)
                onsub = (sub == h).astype(F32)
                ddtT = ddtT + onsub * cn
                dAT = dAT - onsub * (cn * dtrow)
                dA = dA + rm * (lane == h).astype(F32)
            dxbc_ref[0, :, ks] = accdx
        dGb = dG.astype(BF16)
        dxbc_ref[0, :, 1024:1152] = dB + _dot_tn(dGb, cb)
        dxbc_ref[0, :, 1152:1280] = dC + _dot(dGb, bb)
        da = _dot_tn(maskf, dA + dAT.T, HI) + datot
        ddt = ddt + ddtT.T + da * q["aneg"]
        ddtr = jnp.where(q["hmask"], ddt * _sigmoid(q["pre"]), 0.0)
        ddtr_ref[...] = ddtr
        acc_ref[0, 0:1, :] += _rowsum(ddtr)
        acc_ref[0, 1:2, :] += _rowsum(da * dt) * q["aneg"]
        dh_s[...] = dhprev
        dh0_ref[0] = dhprev

    return pl.pallas_call(
        body,
        name=name,
        grid=(2, nc),
        in_specs=[
            pl.BlockSpec((Q, D), lambda d, s: (chunk(d, s), 0)),
            pl.BlockSpec((Q, DXBC), lambda d, s: (chunk(d, s), 0)),
            pl.BlockSpec((Q, 128), lambda d, s: (chunk(d, s), d)),
            pl.BlockSpec((1, 1, D, NS), lambda d, s: (d, chunk(d, s), 0, 0)),
            pl.BlockSpec((1, D, NS), lambda d, s: (d, 0, 0)),
            _resident((8, 128)),
            _resident((128, D)),
            _resident((D, 128)),
            _resident((1, D)),
        ],
        out_specs=[
            pl.BlockSpec((1, Q, DXBC), lambda d, s: (d, chunk(d, s), 0)),
            pl.BlockSpec((Q, 128), lambda d, s: (chunk(d, s), d)),
            pl.BlockSpec((1, 8, 128), lambda d, s: (d, 0, 0)),
            pl.BlockSpec((1, D, NS), lambda d, s: (d, 0, 0)),
        ],
        out_shape=(
            jax.ShapeDtypeStruct((2, L, DXBC), F32),
            jax.ShapeDtypeStruct((L, 256), F32),
            jax.ShapeDtypeStruct((2, 8, 128), F32),
            jax.ShapeDtypeStruct((2, D, NS), F32),
        ),
        scratch_shapes=[pltpu.VMEM((D, NS), F32), pltpu.VMEM((128, Q), F32), pltpu.VMEM((128, Q), F32)],
        compiler_params=_cp(2),
    )(dy, xbc, dtr, hprev_all, dh_init, par, e_mat, et_mat, dskip)


def _conv_bwd_call(dxbc2, xbc_raw, cw, acc_init, name, scattered=()):
    L = xbc_raw.shape[0]
    nt = L // TM
    ns = len(scattered)
    prev, nxt = _halo_specs(TM, DXBC, L)

    def body(*refs):
        dx_ref, cur_ref, prev_ref, next_ref, cw_ref, init_ref = refs[:6]
        dpre_ref, acc_ref = refs[6 + ns:8 + ns]
        ext = refs[8 + 2 * ns]
        if ns:
            ex = _Exchange(refs[6:6 + ns], refs[8 + ns:8 + 2 * ns], *refs[9 + 2 * ns:], False)

        @pl.when(pl.program_id(0) == 0)
        def _():
            if ns:
                ex.start()
            acc_ref[...] = init_ref[...]

        _extended(ext, cur_ref, prev_ref, next_ref)
        pre = _conv_pre(ext, cw_ref, TM)
        sg = _sigmoid(pre)
        dpre = (dx_ref[0] + dx_ref[1]) * (sg * (1.0 + pre * (1.0 - sg)))
        dpre_ref[...] = dpre
        for k in range(5):
            acc_ref[k:k + 1, :] += _rowsum(dpre * _shifted(ext, k - 2, TM))
        acc_ref[5:6, :] += _rowsum(dpre)
        if ns:
            @pl.when(pl.program_id(0) == nt - 1)
            def _():
                ex.wait()

    any_spec = pl.BlockSpec(memory_space=pl.ANY)
    return pl.pallas_call(
        body,
        name=name,
        grid=(nt,),
        in_specs=[pl.BlockSpec((2, TM, DXBC), lambda i: (0, i, 0)), _tiles(TM, DXBC), prev, nxt,
                  _resident((8, DXBC)), _resident((8, DXBC))] + [any_spec] * ns,
        out_specs=[_tiles(TM, DXBC), _const_out((8, DXBC))] + [any_spec] * ns,
        out_shape=[jax.ShapeDtypeStruct((L, DXBC), F32), jax.ShapeDtypeStruct((8, DXBC), F32)]
        + _exchange_out_shapes(scattered, False),
        scratch_shapes=[pltpu.VMEM((TM + 16, DXBC), F32)] + (_exchange_sems(ns) if ns else []),
        compiler_params=_cp(),
    )(dxbc2, xbc_raw, xbc_raw, xbc_raw, cw, acc_init, *scattered)


def _inproj_bwd_call(dpre, cw, dz, ddtr, dup, h1, dxe_part, x0, vec, w_in, dw_init, name):
    L = x0.shape[0]
    nt = L // TM
    prev, nxt = _halo_specs(TM, DXBC, L)

    def body(cur_ref, prev_ref, next_ref, cw_ref, dz_ref, ddtr_ref, dup_ref, h1_ref, dxe_ref, x0_ref, vec_ref,
             w_ref, dwi_ref, gx_ref, gacc_ref, dw_ref, dw_s, ext):
        i = pl.program_id(0)

        @pl.when(i == 0)
        def _():
            gacc_ref[...] = jnp.zeros_like(gacc_ref)
            pltpu.sync_copy(dwi_ref, dw_s)

        def vrow(r):
            return vec_ref[r:r + 1, :]

        _extended(ext, cur_ref, prev_ref, next_ref)
        dxr = cw_ref[0:1, :] * _shifted(ext, 2, TM)
        for k in range(1, 5):
            dxr = dxr + cw_ref[k:k + 1, :] * _shifted(ext, 2 - k, TM)
        dproj = jnp.concatenate([dz_ref[...], dxr, ddtr_ref[...], dup_ref[...]], axis=1).astype(BF16)
        dh1 = _dot_nt(dproj, w_ref[...])
        _acc_tn(dw_s, h1_ref[...], dproj)
        xe, n0, rstd0 = _ln(x0_ref[...], vrow(V_EMBG), vrow(V_EMBB))
        dxe = dxe_ref[...] + dh1 * (1.0 + vrow(V_SC1))
        gacc_ref[V_SC1:V_SC1 + 1, :] += _rowsum(dh1 * xe)
        gacc_ref[V_SH1:V_SH1 + 1, :] += _rowsum(dh1)
        gacc_ref[V_EMBG:V_EMBG + 1, :] += _rowsum(dxe * n0)
        gacc_ref[V_EMBB:V_EMBB + 1, :] += _rowsum(dxe)
        gx_ref[...] = _ln_bwd(dxe, n0, rstd0, vrow(V_EMBG))

        @pl.when(i == nt - 1)
        def _():
            pltpu.sync_copy(dw_s, dw_ref)

    any_spec = pl.BlockSpec(memory_space=pl.ANY)
    return pl.pallas_call(
        body,
        name=name,
        grid=(nt,),
        in_specs=[_tiles(TM, DXBC), prev, nxt, _resident((8, DXBC)), _tiles(TM, D), _tiles(TM, 256), _tiles(TM, D),
                  _tiles(TM, D), _tiles(TM, D), _tiles(TM, D), _resident((NV, D)), _resident((D, WIN)), any_spec],
        out_specs=[_tiles(TM, D), _const_out((NV, D)), any_spec],
        out_shape=(
            jax.ShapeDtypeStruct((L, D), F32),
            jax.ShapeDtypeStruct((NV, D), F32),
            jax.ShapeDtypeStruct((D, WIN), F32),
        ),
        scratch_shapes=[pltpu.VMEM((D, WIN), F32), pltpu.VMEM((TM + 16, DXBC), F32)],
        compiler_params=_cp(),
    )(dpre, dpre, dpre, cw, dz, ddtr, dup, h1, dxe_part, x0, vec, w_in, dw_init)


def _adamw(w, g, m, v):
    m = ADAM_B1 * m + (1.0 - ADAM_B1) * g
    v = ADAM_B2 * v + (1.0 - ADAM_B2) * (g * g)
    m_hat = m / (1.0 - ADAM_B1 ** ADAM_STEP)
    v_hat = v / (1.0 - ADAM_B2 ** ADAM_STEP)
    delta = -ADAM_LR * (m_hat / (jnp.sqrt(v_hat) + ADAM_EPS) + ADAM_WD * w)
    return delta, m, v


def _adamw_shard_call(gslots, w, m, v, tr, name):
    R, C = w.shape

    def body(gs_ref, w_ref, m_ref, v_ref, g_ref, d_ref, mo_ref, vo_ref):
        g = gs_ref[0]
        for i in range(1, NDEV):
            g = g + gs_ref[i]
        delta, mn, vn = _adamw(w_ref[...], g, m_ref[...], v_ref[...])
        g_ref[...] = g
        d_ref[...] = delta
        mo_ref[...] = mn
        vo_ref[...] = vn

    t = _tiles(tr, C)
    return pl.pallas_call(
        body,
        name=name,
        grid=(R // tr,),
        in_specs=[pl.BlockSpec((NDEV, tr, C), lambda i: (0, i, 0)), t, t, t],
        out_specs=[t, t, t, t],
        out_shape=tuple(jax.ShapeDtypeStruct((R, C), F32) for _ in range(4)),
        compiler_params=_cp(),
    )(gslots, w, m, v)


def _wada_call(dm_ex, dm_ctx, silu_all, w, m, v):
    ncol = w.shape[1]

    def body(dme_ref, dmc_ref, s_ref, w_ref, m_ref, v_ref, g_ref, d_ref, mo_ref, vo_ref, ds_ref):
        dmc = _rowsum(dmc_ref[...])
        rows = lax.broadcasted_iota(jnp.int32, (8, 1), 0)
        low = jnp.where(rows == 0, dmc, 0.0)
        dm = jnp.concatenate([dme_ref[...], low], axis=0).astype(BF16)
        wv = w_ref[...]
        g = _dot_tn(s_ref[...].astype(BF16), dm)
        delta, mn, vn = _adamw(wv, g, m_ref[...], v_ref[...])
        g_ref[...] = g
        d_ref[...] = delta
        mo_ref[...] = mn
        vo_ref[...] = vn
        ds_ref[...] = _dot_nt(low.astype(BF16), wv.astype(BF16))

    return pl.pallas_call(
        body,
        name="wada_update",
        out_shape=tuple(jax.ShapeDtypeStruct((D, ncol), F32) for _ in range(4)) + (jax.ShapeDtypeStruct((8, D), F32),),
        compiler_params=pltpu.CompilerParams(vmem_limit_bytes=VMEM_LIMIT),
    )(dm_ex, dm_ctx, silu_all, w, m, v)


P_DMOD, P_DMODC, P_EMBG, P_EMBB, P_LN1G, P_LN1B, P_LN2G, P_LN2B = 0, 6, 8, 9, 10, 11, 12, 13
P_SSDG, P_PSC, P_DSK, P_CONVB, P_DTB, P_ALOG, P_LOSS, NP = 14, 15, 16, 17, 19, 20, 21, 24
S_CCTX, S_EMBG, S_EMBB, S_BADA, S_CONVB, S_DTB, S_ALOG, S_DSK = 0, 1, 2, 3, 9, 11, 12, 13
S_SSDG, S_PSC, S_LN1G, S_LN1B, S_LN2G, S_LN2B, NSM = 14, 15, 16, 17, 18, 19, 24


def _small_update_call(pall, dsil, cctx, w, m, v, et_mat):
    def body(p_ref, ds_ref, c_ref, w_ref, m_ref, v_ref, et_ref, g_ref, d_ref, mo_ref, vo_ref, loss_ref,
             tot, dsum, dsk8):
        tot[...] = p_ref[0]
        dsum[...] = ds_ref[0]
        for i in range(1, NDEV):
            tot[...] += p_ref[i]
            dsum[...] += ds_ref[i]
        cv = c_ref[...]
        sc = _sigmoid(cv)
        g_ref[...] = jnp.zeros_like(g_ref)
        g_ref[S_CCTX:S_CCTX + 1, :] = dsum[0:1, :] * (sc * (1.0 + cv * (1.0 - sc)))
        g_ref[S_EMBG:S_EMBG + 1, :] = tot[P_EMBG:P_EMBG + 1, :]
        g_ref[S_EMBB:S_EMBB + 1, :] = tot[P_EMBB:P_EMBB + 1, :]
        g_ref[S_BADA:S_BADA + 2, :] = tot[P_DMOD:P_DMOD + 2, :] + tot[P_DMODC:P_DMODC + 2, :]
        g_ref[S_BADA + 2:S_BADA + 6, :] = tot[P_DMOD + 2:P_DMOD + 6, :]
        g_ref[S_CONVB:S_CONVB + 2, :] = tot[P_CONVB:P_CONVB + 2, :]
        g_ref[S_DTB:S_DTB + 1, :] = tot[P_DTB:P_DTB + 1, :]
        g_ref[S_ALOG:S_ALOG + 1, :] = tot[P_ALOG:P_ALOG + 1, :]
        dsk8[...] = _dot(jnp.broadcast_to(tot[P_DSK:P_DSK + 1, :], (8, D)), et_ref[...].astype(F32), HI)
        g_ref[S_DSK:S_DSK + 1, 0:128] = dsk8[0:1, :]
        g_ref[S_SSDG:S_SSDG + 1, :] = tot[P_SSDG:P_SSDG + 1, :]
        g_ref[S_PSC:S_PSC + 1, :] = tot[P_PSC:P_PSC + 1, :]
        g_ref[S_LN1G:S_LN1G + 1, :] = tot[P_LN1G:P_LN1G + 1, :]
        g_ref[S_LN1B:S_LN1B + 1, :] = tot[P_LN1B:P_LN1B + 1, :]
        g_ref[S_LN2G:S_LN2G + 1, :] = tot[P_LN2G:P_LN2G + 1, :]
        g_ref[S_LN2B:S_LN2B + 1, :] = tot[P_LN2B:P_LN2B + 1, :]
        delta, mn, vn = _adamw(w_ref[...], g_ref[...], m_ref[...], v_ref[...])
        d_ref[...] = delta
        mo_ref[...] = mn
        vo_ref[...] = vn
        loss_ref[...] = jnp.broadcast_to(tot[P_LOSS:P_LOSS + 1, 0:128], (8, 128))

    return pl.pallas_call(
        body,
        name="small_update",
        out_shape=tuple(jax.ShapeDtypeStruct((NSM, D), F32) for _ in range(4)) + (jax.ShapeDtypeStruct((8, 128), F32),),
        scratch_shapes=[pltpu.VMEM((NP, D), F32), pltpu.VMEM((8, D), F32), pltpu.VMEM((8, 128), F32)],
        compiler_params=pltpu.CompilerParams(vmem_limit_bytes=VMEM_LIMIT),
    )(pall, dsil, cctx, w, m, v, et_mat)


def _pad_rows(flat, mult=16):
    n = flat.shape[0]
    rows = -(-n // D)
    rows = -(-rows // mult) * mult
    return jnp.pad(flat, (0, rows * D - n)).reshape(rows, D)


def _by_cols(dw):
    r = dw.shape[0]
    return jnp.transpose(dw.reshape(r, NDEV, -1), (1, 0, 2))


def _from_cols(g):
    return jnp.transpose(g, (1, 0, 2)).reshape(g.shape[1], -1)


def _pool_constants(L):
    rows = L // GW
    t_r = jnp.arange(PT) // GW
    t_c = jnp.arange(PT) % GW
    e_r = jnp.arange(3 * PT) // GW - PR
    e_c = jnp.arange(3 * PT) % GW
    fw, bw, ic = [], [], []
    pos_r = jnp.arange(L) // GW
    pos_c = jnp.arange(L) % GW
    for w in WINDOWS:
        lo, hi = -(w // 2), w - w // 2 - 1
        dr = e_r[None, :] - t_r[:, None]
        dc = e_c[None, :] - t_c[:, None]
        fw.append(((dr >= lo) & (dr <= hi) & (dc >= lo) & (dc <= hi)).astype(BF16))
        bw.append(((-dr >= lo) & (-dr <= hi) & (-dc >= lo) & (-dc <= hi)).astype(BF16))
        cr = jnp.minimum(pos_r + hi, rows - 1) - jnp.maximum(pos_r + lo, 0) + 1
        cc = jnp.minimum(pos_c + hi, GW - 1) - jnp.maximum(pos_c + lo, 0) + 1
        ic.append(1.0 / (cr * cc).astype(F32))
    icnt = jnp.pad(jnp.stack(ic, axis=1), ((0, 0), (0, 124)))
    return jnp.stack(fw), jnp.stack(bw), icnt


def _head_matrices():
    hp = jnp.arange(D) // HP
    e = (jnp.arange(128)[:, None] == hp[None, :]).astype(BF16)
    return e, e.T


def _aligned_in_proj(w):
    zpad = jnp.zeros((D, 128 - NH), w.dtype)
    return jnp.concatenate([w[:, 0:2304], w[:, 2304:2320], zpad, w[:, 2320:2336], zpad, w[:, 2336:3360]], axis=1)


def _unaligned_in_proj(dw):
    return jnp.concatenate([dw[:, 0:2304], dw[:, 2304:2320], dw[:, 2432:2448], dw[:, 2560:3584]], axis=1)


def _row(v):
    return v.reshape(1, -1).astype(F32)


def _pad_lanes(v, width=D):
    v = v.reshape(1, -1)
    return jnp.pad(v, ((0, 0), (0, width - v.shape[1])))


def kernel(x, c, ctx, c_ctx, emb_ln_g, emb_ln_b, w_ada, b_ada, in_proj, conv_w, conv_b, dt_bias, a_log, d_skip, ssd_norm_g, pool_w, pool_scale, w_out, ln1_g, ln1_b, w_gate, w_up, w_down, ln2_g, ln2_b, loss_target, m_c_ctx, m_emb_ln_g, m_emb_ln_b, m_w_ada, m_b_ada, m_in_proj, m_conv_w, m_conv_b, m_dt_bias, m_a_log, m_d_skip, m_ssd_norm_g, m_pool_w, m_pool_scale, m_w_out, m_ln1_g, m_ln1_b, m_w_gate, m_w_up, m_w_down, m_ln2_g, m_ln2_b, v_c_ctx, v_emb_ln_g, v_emb_ln_b, v_w_ada, v_b_ada, v_in_proj, v_conv_w, v_conv_b, v_dt_bias, v_a_log, v_d_skip, v_ssd_norm_g, v_pool_w, v_pool_scale, v_w_out, v_ln1_g, v_ln1_b, v_w_gate, v_w_up, v_w_down, v_ln2_g, v_ln2_b):
    me = 4 * lax.axis_index("x") + 2 * lax.axis_index("y") + lax.axis_index("c")
    x0 = x[0]
    ctx0 = ctx[0]
    tgt = loss_target[0]
    L = x0.shape[0]
    LC = ctx0.shape[0]
    ncol_ada = w_ada.shape[2]

    small_in = jnp.concatenate([c.reshape(-1), conv_w.reshape(-1)])
    small_all, g_inp = _exchange([_pad_rows(small_in, 8), in_proj[0].astype(BF16)], "gather_first", True)
    c_all = small_all[:, 0, :]
    convw_all = small_all.reshape(NDEV, -1)[:, D:D + 5 * (DXBC // NDEV)].reshape(NDEV, 5, DXBC // NDEV)
    conv_w_full = _from_cols(convw_all)
    w_in = _aligned_in_proj(_from_cols(g_inp))
    late_shards = [pool_w[0].astype(BF16), w_out[0].astype(BF16), w_gate[0].astype(BF16), w_up[0].astype(BF16),
                   w_down[0].astype(BF16)]

    c_in = jnp.concatenate([c_all, c_ctx.reshape(1, D), jnp.zeros((7, D), F32)], axis=0)
    b_mine = lax.dynamic_slice(b_ada, (0, me * ncol_ada), (1, ncol_ada))
    silu_all, mod_mine = _mod_call(c_in, w_ada[0], b_mine)
    (mod_all,) = _exchange([mod_mine], "gather_mod", True)
    mod_all = _from_cols(mod_all)
    mod_me = lax.dynamic_slice(mod_all, (me, 0), (1, 6 * D)).reshape(6, D)
    mod_ctx = mod_all[8].reshape(6, D)

    tail = jnp.concatenate([
        _row(emb_ln_g), _row(emb_ln_b), _row(ln1_g), _row(ln1_b), _row(ln2_g), _row(ln2_b),
        _row(ssd_norm_g), _row(pool_scale), _row(jnp.repeat(d_skip.reshape(-1), HP)), jnp.zeros((1, D), F32)], axis=0)
    vec = jnp.concatenate([mod_me, tail], axis=0)
    vec_ctx = jnp.concatenate([mod_ctx, tail], axis=0)

    cw = jnp.concatenate([conv_w_full, conv_b.reshape(1, DXBC), jnp.zeros((2, DXBC), F32)], axis=0)
    par = jnp.concatenate([_pad_lanes(dt_bias[0, 0], 128), _pad_lanes(dt_bias[0, 1], 128),
                           _pad_lanes(a_log[0, 0], 128), _pad_lanes(a_log[0, 1], 128),
                           jnp.zeros((4, 128), F32)], axis=0)
    e_mat, et_mat = _head_matrices()
    pmat, pmat_t, icnt = _pool_constants(L)
    dskip_row = vec[V_DSK:V_DSK + 1]

    xe_c, h1_c, _, xbcr_c, dtr_c, _ = _f1_call(ctx0, vec_ctx, w_in, "inproj_fwd_ctx")
    xbc_c = _f2_call(xbcr_c, cw, "conv_fwd_ctx")
    hzero = jnp.zeros((2, D, NS), F32)
    _, hprev_c, hfin_c = _ssd_fwd_call(xbc_c, dtr_c, hzero, par, e_mat, et_mat, "ssd_fwd_ctx")

    xe, h1, z, xbcr, dtr, up = _f1_call(x0, vec, w_in, "inproj_fwd")
    xbc = _f2_call(xbcr, cw, "conv_fwd")
    y2, hprev, _, g_pw, g_wo, g_wg, g_wu, g_wd = _ssd_fwd_call(xbc, dtr, hfin_c, par, e_mat, et_mat, "ssd_fwd",
                                                                late_shards)
    pool_w_full = jnp.transpose(g_pw, (1, 0, 2, 3)).reshape(4, PG, PG)
    w_out_full = g_wo.reshape(2 * D, D)
    w_gate_full = _from_cols(g_wg)
    w_up_full = _from_cols(g_wu)
    w_down_full = g_wd.reshape(DFF, D)
    dpool, pun = _pool_fwd_call(up, pmat, icnt, pool_w_full)
    x1, mix, cat = _merge_call(y2, xbc, z, pun, xe, vec, w_out_full)
    dpre2, gacc_f = _ffn_fwd_call(x1, tgt, vec, w_gate_full, w_up_full, w_down_full)

    dh2p, dwg2, dwu2, dwd2 = _ffn_bwd_call(x1, dpre2, vec, w_gate_full, w_up_full, w_down_full)
    nq = FFC // (DFF // NDEV)
    ffn_parts = [
        jnp.transpose(dwg2.reshape(-1, D, nq, DFF // NDEV), (0, 2, 1, 3)).reshape(NDEV, D, DFF // NDEV),
        jnp.transpose(dwu2.reshape(-1, D, nq, DFF // NDEV), (0, 2, 1, 3)).reshape(NDEV, D, DFF // NDEV),
        dwd2.reshape(NDEV, DFF // NDEV, D)]
    dxe_part, dy, dz, dd, dpw, gacc_m, dwo, gs_wg, gs_wu, gs_wd = _merge_bwd_call(
        dh2p, dpre2, x1, xe, mix, cat, y2, xbc, z, dpool, pun, vec, w_out_full, pool_w_full, ffn_parts)
    dup = _pool_bwd_call(dd, pmat_t, icnt)
    dxbc2, ddtr, sacc, dh0 = _ssd_bwd_call(dy, xbc, dtr, hprev, hzero, par, e_mat, et_mat, dskip_row, "ssd_bwd")
    zeros_c = jnp.zeros((LC, D), F32)
    dxbc2_c, ddtr_c, sacc_c, _ = _ssd_bwd_call(zeros_c, xbc_c, dtr_c, hprev_c, dh0, par, e_mat, et_mat,
                                                jnp.zeros((1, D), F32), "ssd_bwd_ctx")
    dprec_c, cacc_c = _conv_bwd_call(dxbc2_c, xbcr_c, cw, jnp.zeros((8, DXBC), F32), "conv_bwd_ctx")
    _, gacc_c, dwin_c = _inproj_bwd_call(dprec_c, cw, zeros_c, ddtr_c, zeros_c, h1_c, zeros_c, ctx0, vec_ctx, w_in,
                                         jnp.zeros((D, WIN), F32), "inproj_bwd_ctx")
    mix_parts = [dwo.reshape(NDEV, 2 * D // NDEV, D),
                 jnp.transpose(dpw.reshape(4, NDEV, PG // NDEV, PG), (1, 0, 2, 3)).reshape(NDEV, 4 * PG // NDEV, PG)]
    dprec, cacc, gs_wo, gs_pw = _conv_bwd_call(dxbc2, xbcr, cw, cacc_c, "conv_bwd", mix_parts)
    grad_x, gacc_i, dwin = _inproj_bwd_call(dprec, cw, dz, ddtr, dup, h1, dxe_part, x0, vec, w_in, dwin_c, "inproj_bwd")

    gsum = gacc_f + gacc_m + gacc_i
    sa = sacc + sacc_c
    dtb_row = _pad_lanes(jnp.concatenate([sa[0, 0, 0:NH], sa[1, 0, 0:NH]]))
    alog_row = _pad_lanes(jnp.concatenate([sa[0, 1, 0:NH], sa[1, 1, 0:NH]]))
    convb_rows = jnp.pad(cacc[5], (0, 2 * D - DXBC)).reshape(2, D)
    pack = jnp.concatenate([
        gsum[V_SH1:V_G2 + 1],
        gacc_c[V_SH1:V_SC1 + 1],
        gsum[V_EMBG:V_EMBB + 1] + gacc_c[V_EMBG:V_EMBB + 1],
        gsum[V_LN1G:V_LN2B + 1],
        gsum[V_SSDG:V_DSK + 1],
        convb_rows, dtb_row, alog_row,
        gsum[V_LOSS:V_LOSS + 1],
        jnp.zeros((NP - 22, D), F32)], axis=0)
    (pall,) = _exchange([pack], "gather_small_grads", True)

    dm_flat = pall[:, 0:8, :].reshape(NDEV, 8 * D)
    dm_ex = lax.dynamic_slice(dm_flat, (0, me * ncol_ada), (NDEV, ncol_ada))
    dmc_full = jnp.concatenate([dm_flat[:, 6 * D:8 * D], jnp.zeros((NDEV, 4 * D), F32)], axis=1)
    dm_ctx = lax.dynamic_slice(dmc_full, (0, me * ncol_ada), (NDEV, ncol_ada))
    g_wada, d_wada, nm_wada, nv_wada, dsil = _wada_call(dm_ex, dm_ctx, silu_all, w_ada[0], m_w_ada[0], v_w_ada[0])
    (dsil_all,) = _exchange([dsil], "gather_dsilu", True)

    def small_pack(cc, eg, eb, ba, cb_, dtb, al, dsk, sg, ps, l1g, l1b, l2g, l2b):
        return jnp.concatenate([
            _row(cc), _row(eg), _row(eb), ba.reshape(6, D), jnp.pad(cb_.reshape(-1), (0, 2 * D - DXBC)).reshape(2, D),
            _pad_lanes(dtb.reshape(-1)), _pad_lanes(al.reshape(-1)), _pad_lanes(dsk.reshape(-1)),
            _row(sg), _row(ps), _row(l1g), _row(l1b), _row(l2g), _row(l2b), jnp.zeros((NSM - 20, D), F32)], axis=0)

    sw = small_pack(c_ctx, emb_ln_g, emb_ln_b, b_ada, conv_b, dt_bias, a_log, d_skip, ssd_norm_g, pool_scale,
                    ln1_g, ln1_b, ln2_g, ln2_b)
    sm = small_pack(m_c_ctx, m_emb_ln_g, m_emb_ln_b, m_b_ada, m_conv_b, m_dt_bias, m_a_log, m_d_skip, m_ssd_norm_g,
                    m_pool_scale, m_ln1_g, m_ln1_b, m_ln2_g, m_ln2_b)
    sv = small_pack(v_c_ctx, v_emb_ln_g, v_emb_ln_b, v_b_ada, v_conv_b, v_dt_bias, v_a_log, v_d_skip, v_ssd_norm_g,
                    v_pool_scale, v_ln1_g, v_ln1_b, v_ln2_g, v_ln2_b)
    s_g, s_d, s_m, s_v, loss8 = _small_update_call(pall, dsil_all, _row(c_ctx), sw, sm, sv, et_mat)

    def small_unpack(t):
        return (t[S_CCTX], t[S_EMBG], t[S_EMBB], t[S_BADA:S_BADA + 6].reshape(1, 6 * D),
                t[S_CONVB:S_CONVB + 2].reshape(-1)[:DXBC].reshape(1, DXBC),
                t[S_DTB, 0:2 * NH].reshape(1, 2, NH), t[S_ALOG, 0:2 * NH].reshape(1, 2, NH), t[S_DSK, 0:NH].reshape(1, NH),
                t[S_SSDG].reshape(1, D), t[S_PSC].reshape(1, D), t[S_LN1G].reshape(1, D), t[S_LN1B].reshape(1, D),
                t[S_LN2G].reshape(1, D), t[S_LN2B].reshape(1, D))

    gs_inp, gs_cw = _exchange([_by_cols(_unaligned_in_proj(dwin)), _by_cols(cacc[0:5])], "exchange_last_grads", False)

    pshape = (4 * PG // NDEV, PG)
    u_inp = _adamw_shard_call(gs_inp, in_proj[0], m_in_proj[0], v_in_proj[0], 256, "adamw_in_proj")
    u_cw = _adamw_shard_call(gs_cw, conv_w[0], m_conv_w[0], v_conv_w[0], 5, "adamw_conv_w")
    u_pw = _adamw_shard_call(gs_pw, pool_w[0].reshape(pshape), m_pool_w[0].reshape(pshape), v_pool_w[0].reshape(pshape),
                             pshape[0], "adamw_pool_w")
    u_wo = _adamw_shard_call(gs_wo, w_out[0], m_w_out[0], v_w_out[0], 64, "adamw_w_out")
    u_wg = _adamw_shard_call(gs_wg, w_gate[0], m_w_gate[0], v_w_gate[0], 256, "adamw_w_gate")
    u_wu = _adamw_shard_call(gs_wu, w_up[0], m_w_up[0], v_w_up[0], 256, "adamw_w_up")
    u_wd = _adamw_shard_call(gs_wd, w_down[0], m_w_down[0], v_w_down[0], 88, "adamw_w_down")

    def assemble(k, small, wada):
        (cc, eg, eb, ba, cb_, dtb, al, dsk, sg, ps, l1g, l1b, l2g, l2b) = small_unpack(small)
        pw = u_pw[k].reshape(1, 4, PG // NDEV, PG)
        return (cc, eg, eb, wada[None], ba, u_inp[k][None], u_cw[k][None], cb_, dtb, al, dsk, sg, pw, ps,
                u_wo[k][None], l1g, l1b, u_wg[k][None], u_wu[k][None], u_wd[k][None], l2g, l2b)

    loss = loss8[0, 0]
    return (loss, grad_x[None], *assemble(0, s_g, g_wada), *assemble(1, s_d, d_wada),
            *assemble(2, s_m, nm_wada), *assemble(3, s_v, nv_wada))
```

```python
import functools
import math

import jax
import jax.numpy as jnp
from jax import lax
from jax.experimental import pallas as pl
from jax.experimental.pallas import tpu as pltpu

F32 = jnp.float32
BF16 = jnp.bfloat16
HI = lax.Precision.HIGHEST

NDEV = 8
D = 1024
NH = 16
HP = 64
NS = 128
Q = 128
DXBC = 1280
DFF = 2816
FFC = 1408
GW = 64
PR = 8
PT = PR * GW
WINDOWS = (2, 4, 8, 16)
PG = 256
DIN = 3360
WIN = 3584
ALPHA = 2.0 ** 0.25
LN_EPS = 1e-5
TM = 256

ADAM_LR = 0.001
ADAM_B1 = 0.9
ADAM_B2 = 0.999
ADAM_EPS = 1e-08
ADAM_WD = 0.01
ADAM_STEP = 10

V_SH1, V_SC1, V_G1, V_SH2, V_SC2, V_G2 = 0, 1, 2, 3, 4, 5
V_EMBG, V_EMBB, V_LN1G, V_LN1B, V_LN2G, V_LN2B = 6, 7, 8, 9, 10, 11
V_SSDG, V_PSC, V_DSK, V_LOSS = 12, 13, 14, 15
NV = 16

VMEM_LIMIT = 60 * 1024 * 1024


def _cp(ndim=1):
    return pltpu.CompilerParams(dimension_semantics=("arbitrary",) * ndim, vmem_limit_bytes=VMEM_LIMIT)


def _dot(a, b, precision=None):
    return jnp.dot(a, b, preferred_element_type=F32, precision=precision)


def _dot_nt(a, b):
    return lax.dot_general(a, b, (((1,), (1,)), ((), ())), preferred_element_type=F32)


def _dot_tn(a, b, precision=None):
    return lax.dot_general(a, b, (((0,), (0,)), ((), ())), preferred_element_type=F32, precision=precision)


def _split2(x):
    hi = x.astype(BF16)
    return hi, (x - hi.astype(F32)).astype(BF16)


def _split_dot(m, x):
    hi, lo = _split2(x)
    return _dot(m, hi) + _dot(m, lo)


def _dot_split(x, m):
    hi, lo = _split2(x)
    return _dot(hi, m) + _dot(lo, m)


def _sigmoid(x):
    return 1.0 / (1.0 + jnp.exp(-x))


def _softplus(x):
    return jnp.maximum(x, 0.0) + jnp.log(1.0 + jnp.exp(-jnp.abs(x)))


def _ln(x, g, b):
    mu = jnp.mean(x, axis=-1, keepdims=True)
    xc = x - mu
    var = jnp.mean(xc * xc, axis=-1, keepdims=True)
    rstd = lax.rsqrt(var + LN_EPS)
    n = xc * rstd
    return n * g + b, n, rstd


def _ln_bwd(dy, n, rstd, g):
    dn = dy * g
    return rstd * (dn - jnp.mean(dn, axis=-1, keepdims=True) - n * jnp.mean(dn * n, axis=-1, keepdims=True))


def _rowsum(x):
    return jnp.sum(x, axis=0, keepdims=True)


def _resident(shape):
    nd = len(shape)
    return pl.BlockSpec(shape, lambda *_: (0,) * nd, pipeline_mode=pl.Buffered(1))


def _const_out(shape):
    nd = len(shape)
    return pl.BlockSpec(shape, lambda *_: (0,) * nd)


def _tiles(tm, width):
    return pl.BlockSpec((tm, width), lambda i: (i, 0))


def _halo_specs(tm, width, n_rows):
    r = tm // 8
    last = n_rows // 8 - 1
    prev = pl.BlockSpec((8, width), lambda i: (jnp.maximum(i * r - 1, 0), 0))
    nxt = pl.BlockSpec((8, width), lambda i: (jnp.minimum((i + 1) * r, last), 0))
    return prev, nxt


def _acc_tn(acc_ref, a, b, chunk=512):
    n = b.shape[1]
    for c0 in range(0, n, chunk):
        c1 = min(c0 + chunk, n)
        acc_ref[:, c0:c1] += _dot_tn(a, b[:, c0:c1])


def _my_coords():
    return lax.axis_index("x"), lax.axis_index("y"), lax.axis_index("c")


def _peer(k, mx, my, mc):
    kx, ky, kc = (k >> 2) & 1, (k >> 1) & 1, k & 1
    px = 1 - mx if kx else mx
    py = 1 - my if ky else my
    pc = 1 - mc if kc else mc
    return px, py, pc


class _Exchange:
    def __init__(self, srcs, dsts, send_sems, recv_sems, local_sems, gather):
        self.srcs, self.dsts, self.gather = srcs, dsts, gather
        self.send_sems, self.recv_sems, self.local_sems = send_sems, recv_sems, local_sems

    def _copies(self, outgoing):
        mx, my, mc = _my_coords()
        me = 4 * mx + 2 * my + mc
        local, remote = [], []
        for t, (src, dst) in enumerate(zip(self.srcs, self.dsts)):
            local.append(pltpu.make_async_copy(src if self.gather else src.at[me], dst.at[me], self.local_sems.at[t]))
            for k in range(1, NDEV):
                px, py, pc = _peer(k, mx, my, mc)
                pid = 4 * px + 2 * py + pc
                remote.append(pltpu.make_async_remote_copy(
                    src_ref=src if self.gather else src.at[pid],
                    dst_ref=dst.at[me] if outgoing else dst.at[pid],
                    send_sem=self.send_sems.at[t, k - 1],
                    recv_sem=self.recv_sems.at[t, k - 1],
                    device_id=(px, py, pc),
                    device_id_type=pl.DeviceIdType.MESH,
                ))
        return local, remote

    def start(self):
        local, remote = self._copies(True)
        for cp in local + remote:
            cp.start()

    def wait(self):
        local, sends = self._copies(True)
        _, recvs = self._copies(False)
        for cp in recvs:
            cp.wait_recv()
        for cp in sends:
            cp.wait_send()
        for cp in local:
            cp.wait()


def _exchange_sems(n):
    return [pltpu.SemaphoreType.DMA((n, NDEV - 1)), pltpu.SemaphoreType.DMA((n, NDEV - 1)), pltpu.SemaphoreType.DMA((n,))]


def _exchange_out_shapes(xs, gather):
    return [jax.ShapeDtypeStruct(x.shape if not gather else (NDEV,) + x.shape, x.dtype) for x in xs]


def _exchange(xs, name, gather):
    n = len(xs)

    def body(*refs):
        ex = _Exchange(refs[:n], refs[n:2 * n], *refs[2 * n:], gather)
        ex.start()
        ex.wait()

    any_spec = pl.BlockSpec(memory_space=pl.ANY)
    return pl.pallas_call(
        body,
        name=name,
        out_shape=_exchange_out_shapes(xs, gather),
        in_specs=[any_spec] * n,
        out_specs=[any_spec] * n,
        scratch_shapes=_exchange_sems(n),
    )(*xs)


def _mod_call(c_all, w_ada, b_ada):
    ncol = w_ada.shape[1]

    def body(c_ref, w_ref, b_ref, silu_ref, mod_ref):
        cv = c_ref[...]
        s = cv * _sigmoid(cv)
        silu_ref[...] = s
        mod_ref[...] = _dot(s.astype(BF16), w_ref[...].astype(BF16)) + b_ref[...]

    return pl.pallas_call(
        body,
        name="mod_fwd",
        out_shape=(jax.ShapeDtypeStruct((16, D), F32), jax.ShapeDtypeStruct((16, ncol), F32)),
    )(c_all, w_ada, b_ada)


def _f1_call(x0, vec, w_in, name):
    L = x0.shape[0]

    def body(x_ref, vec_ref, w_ref, xe_ref, h1_ref, z_ref, xbc_ref, dt_ref, up_ref):
        xe, _, _ = _ln(x_ref[...], vec_ref[V_EMBG:V_EMBG + 1, :], vec_ref[V_EMBB:V_EMBB + 1, :])
        h1 = (xe * (1.0 + vec_ref[V_SC1:V_SC1 + 1, :]) + vec_ref[V_SH1:V_SH1 + 1, :]).astype(BF16)
        proj = _dot(h1, w_ref[...])
        xe_ref[...] = xe
        h1_ref[...] = h1
        z_ref[...] = proj[:, 0:1024]
        xbc_ref[...] = proj[:, 1024:2304]
        dt_ref[...] = proj[:, 2304:2560]
        up_ref[...] = proj[:, 2560:3584]

    return pl.pallas_call(
        body,
        name=name,
        grid=(L // TM,),
        in_specs=[_tiles(TM, D), _resident((NV, D)), _resident((D, WIN))],
        out_specs=[_tiles(TM, D), _tiles(TM, D), _tiles(TM, D), _tiles(TM, DXBC), _tiles(TM, 256), _tiles(TM, D)],
        out_shape=(
            jax.ShapeDtypeStruct((L, D), F32),
            jax.ShapeDtypeStruct((L, D), BF16),
            jax.ShapeDtypeStruct((L, D), F32),
            jax.ShapeDtypeStruct((L, DXBC), F32),
            jax.ShapeDtypeStruct((L, 256), F32),
            jax.ShapeDtypeStruct((L, D), F32),
        ),
        compiler_params=_cp(),
    )(x0, vec, w_in)


def _extended(ext, cur_ref, prev_ref, next_ref):
    i = pl.program_id(0)
    n = pl.num_programs(0)
    tm = cur_ref.shape[0]
    ext[0:8, :] = jnp.where(i > 0, prev_ref[...], 0.0)
    ext[8:8 + tm, :] = cur_ref[...]
    ext[8 + tm:16 + tm, :] = jnp.where(i < n - 1, next_ref[...], 0.0)
    return ext


def _shifted(ext, offset, tm):
    return ext[8 + offset:8 + offset + tm, :]


def _conv_pre(ext, cw_ref, tm):
    acc = cw_ref[5:6, :] + cw_ref[0:1, :] * _shifted(ext, -2, tm)
    for k in range(1, 5):
        acc = acc + cw_ref[k:k + 1, :] * _shifted(ext, k - 2, tm)
    return acc


def _f2_call(xbc_raw, cw, name):
    L = xbc_raw.shape[0]
    prev, nxt = _halo_specs(TM, DXBC, L)

    def body(cur_ref, prev_ref, next_ref, cw_ref, out_ref, ext):
        pre = _conv_pre(_extended(ext, cur_ref, prev_ref, next_ref), cw_ref, TM)
        out_ref[...] = pre * _sigmoid(pre)

    return pl.pallas_call(
        body,
        name=name,
        grid=(L // TM,),
        in_specs=[_tiles(TM, DXBC), prev, nxt, _resident((8, DXBC))],
        out_specs=_tiles(TM, DXBC),
        out_shape=jax.ShapeDtypeStruct((L, DXBC), F32),
        scratch_shapes=[pltpu.VMEM((TM + 16, DXBC), F32)],
        compiler_params=_cp(),
    )(xbc_raw, xbc_raw, xbc_raw, cw)


def _ssd_common(d, dtr, par_ref):
    lane = lax.broadcasted_iota(jnp.int32, (1, 128), 1)
    hmask = lane < NH
    bias = par_ref[d:d + 1, :]
    alog = par_ref[2 + d:3 + d, :]
    aneg = jnp.where(hmask, -jnp.exp(alog), 0.0)
    pre = dtr + bias
    dt = jnp.where(hmask, _softplus(pre), 0.0)
    a = dt * aneg
    row = lax.broadcasted_iota(jnp.int32, (Q, Q), 0)
    col = lax.broadcasted_iota(jnp.int32, (Q, Q), 1)
    maskf = ((row >= col) if d == 0 else (row <= col)).astype(F32)
    A = _dot(maskf, a, HI)
    atot = _rowsum(a)
    atot_col = jnp.sum(a.T, axis=1, keepdims=True)
    return dict(hmask=hmask, aneg=aneg, pre=pre, dt=dt, a=a, maskf=maskf, A=A, AT=A.T, dtT=dt.T,
                atot=atot, atot_col=atot_col, lane=lane)


def _column(v, lane, h):
    return jnp.sum(jnp.where(lane == h, v, 0.0), axis=1, keepdims=True)


def _head_expand(e_ref, v):
    return _dot_split(v, e_ref[...])


def _head_sum(et_ref, v):
    return _dot_split(v, et_ref[...])


def _state_decay(et_ref, atot_col):
    return _split_dot(et_ref[...], jnp.broadcast_to(jnp.exp(atot_col), (128, 128)))


def _ssd_fwd_call(xbc, dtr, h0, par, e_mat, et_mat, name, gathered=()):
    L = xbc.shape[0]
    nc = L // Q
    ng = len(gathered)

    def body(*refs):
        xbc_refs, dtr_refs = refs[0:2], refs[2:4]
        h0_ref, par_ref, e_ref, et_ref = refs[4:8]
        y_refs, hp_refs = refs[8 + ng:10 + ng], refs[10 + ng:12 + ng]
        hf_ref = refs[12 + ng]
        hs, AT, dtT = refs[13 + 2 * ng:16 + 2 * ng]
        s = pl.program_id(0)
        if ng:
            ex = _Exchange(refs[8:8 + ng], refs[13 + ng:13 + 2 * ng], *refs[16 + 2 * ng:], True)

        @pl.when(s == 0)
        def _():
            if ng:
                ex.start()
            hs[...] = h0_ref[...]

        for d in range(2):
            xbc_ref, y_ref = xbc_refs[d], y_refs[d]
            q = _ssd_common(d, dtr_refs[d][...], par_ref)
            A, maskf, lane = q["A"], q["maskf"], q["lane"]
            AT[d] = q["AT"]
            dtT[d] = q["dtT"]
            hprev = hs[d]
            hp_refs[d][0] = hprev
            bb = xbc_ref[:, 1024:1152].astype(BF16)
            cb = xbc_ref[:, 1152:1280].astype(BF16)
            g = _dot_nt(cb, bb)
            yoff = _dot_nt(cb, hprev.astype(BF16)) * _head_expand(e_ref, jnp.exp(A))
            for k in range(NH // 2):
                ks = slice(128 * k, 128 * k + 128)
                xp = xbc_ref[:, ks]
                acc = yoff[:, ks]
                for half in range(2):
                    h = 2 * k + half
                    seg = _column(A, lane, h) - AT[d, h:h + 1, :]
                    lm = jnp.exp(jnp.minimum(seg, 0.0)) * maskf
                    sc = (g * lm * dtT[d, h:h + 1, :]).astype(BF16)
                    inhead = (lane >= HP) if half else (lane < HP)
                    acc = acc + _dot(sc, jnp.where(inhead, xp, 0.0).astype(BF16))
                y_ref[:, ks] = acc
            wend = jnp.exp(q["atot"] - A) * q["dt"]
            xw = (xbc_ref[:, 0:1024] * _head_expand(e_ref, wend)).astype(BF16)
            hnew = hprev * _state_decay(et_ref, q["atot_col"]) + _dot_tn(xw, bb)
            hs[d] = hnew
            hf_ref[d] = hnew
        if ng:
            @pl.when(s == nc - 1)
            def _():
                ex.wait()

    any_spec = pl.BlockSpec(memory_space=pl.ANY)
    return pl.pallas_call(
        body,
        name=name,
        grid=(nc,),
        in_specs=[
            pl.BlockSpec((Q, DXBC), lambda s: (s, 0)),
            pl.BlockSpec((Q, DXBC), lambda s: (nc - 1 - s, 0)),
            pl.BlockSpec((Q, 128), lambda s: (s, 0)),
            pl.BlockSpec((Q, 128), lambda s: (nc - 1 - s, 1)),
            _resident((2, D, NS)),
            _resident((8, 128)),
            _resident((128, D)),
            _resident((D, 128)),
        ] + [any_spec] * ng,
        out_specs=[
            pl.BlockSpec((Q, D), lambda s: (s, 0)),
            pl.BlockSpec((Q, D), lambda s: (nc - 1 - s, 0)),
            pl.BlockSpec((1, D, NS), lambda s: (s, 0, 0)),
            pl.BlockSpec((1, D, NS), lambda s: (nc - 1 - s, 0, 0)),
            _const_out((2, D, NS)),
        ] + [any_spec] * ng,
        out_shape=[
            jax.ShapeDtypeStruct((L, D), F32),
            jax.ShapeDtypeStruct((L, D), F32),
            jax.ShapeDtypeStruct((nc, D, NS), F32),
            jax.ShapeDtypeStruct((nc, D, NS), F32),
            jax.ShapeDtypeStruct((2, D, NS), F32),
        ] + _exchange_out_shapes(gathered, True),
        scratch_shapes=[pltpu.VMEM((2, D, NS), F32), pltpu.VMEM((2, 128, Q), F32), pltpu.VMEM((2, 128, Q), F32)]
        + (_exchange_sems(ng) if ng else []),
        compiler_params=_cp(),
    )(xbc, xbc, dtr, dtr, h0, par, e_mat, et_mat, *gathered)


def _halo_tokens(g):
    return (WINDOWS[g] // 2) * GW


def _pool_specs(n_tiles):
    cur = pl.BlockSpec((PT, D), lambda i: (i, 0))
    prev = pl.BlockSpec((PT, D), lambda i: (jnp.maximum(i - 1, 0), 0))
    nxt = pl.BlockSpec((PT, D), lambda i: (jnp.minimum(i + 1, n_tiles - 1), 0))
    return cur, prev, nxt


def _pool_fwd_call(up, pmat, icnt, pool_w):
    L = up.shape[0]
    nt = L // PT
    cur, prev, nxt = _pool_specs(nt)

    def body(cur_ref, prev_ref, next_ref, m0_ref, m1_ref, m2_ref, m3_ref, ic_ref, pw_ref, d_ref, pun_ref):
        i = pl.program_id(0)
        n = pl.num_programs(0)
        lane = lax.broadcasted_iota(jnp.int32, (1, 128), 1)
        icv = ic_ref[...]
        for g, m_ref in enumerate((m0_ref, m1_ref, m2_ref, m3_ref)):
            gs = slice(PG * g, PG * g + PG)
            halo = _halo_tokens(g)
            top = jnp.where(i > 0, prev_ref[PT - halo:PT, gs], 0.0)
            bot = jnp.where(i < n - 1, next_ref[0:halo, gs], 0.0)
            mid = cur_ref[:, gs]
            box = _split_dot(m_ref[...], jnp.concatenate([top, mid, bot], axis=0))
            dg = (box * _column(icv, lane, g) - mid).astype(BF16)
            d_ref[:, gs] = dg
            pun_ref[:, gs] = _dot(dg, pw_ref[g])

    return pl.pallas_call(
        body,
        name="pool_fwd",
        grid=(nt,),
        in_specs=[cur, prev, nxt] + [_resident(m.shape) for m in pmat] + [_tiles(PT, 128), _resident((4, PG, PG))],
        out_specs=[_tiles(PT, D), _tiles(PT, D)],
        out_shape=(jax.ShapeDtypeStruct((L, D), BF16), jax.ShapeDtypeStruct((L, D), F32)),
        compiler_params=_cp(),
    )(up, up, up, *pmat, icnt, pool_w)


def _gated(yf_ref, yb_ref, xs, z, vec_ref):
    ym = yf_ref[...] + yb_ref[...] + vec_ref[V_DSK:V_DSK + 1, :] * xs
    sz = _sigmoid(z)
    gated = ym * (z * sz)
    r = lax.rsqrt(jnp.mean(gated * gated, axis=-1, keepdims=True) + LN_EPS)
    return ym, sz, gated, r


def _merge_call(yf, yb, xbc, z, pun, xe, vec, w_out):
    L = z.shape[0]

    def body(yf_ref, yb_ref, xs_ref, z_ref, pun_ref, xe_ref, vec_ref, w_ref, x1_ref, mix_ref, cat_ref):
        _, _, gated, r = _gated(yf_ref, yb_ref, xs_ref[...], z_ref[...], vec_ref)
        yn = gated * r * vec_ref[V_SSDG:V_SSDG + 1, :]
        p = pun_ref[...] * vec_ref[V_PSC:V_PSC + 1, :]
        cat = jnp.concatenate([yn, p], axis=1).astype(BF16)
        mix = _dot(cat, w_ref[...])
        pre1 = ALPHA * xe_ref[...] + vec_ref[V_G1:V_G1 + 1, :] * mix
        x1, _, _ = _ln(pre1, vec_ref[V_LN1G:V_LN1G + 1, :], vec_ref[V_LN1B:V_LN1B + 1, :])
        x1_ref[...] = x1
        mix_ref[...] = mix
        cat_ref[...] = cat

    return pl.pallas_call(
        body,
        name="merge_fwd",
        grid=(L // TM,),
        in_specs=[
            _tiles(TM, D), _tiles(TM, D), _tiles(TM, D), _tiles(TM, D), _tiles(TM, D), _tiles(TM, D),
            _resident((NV, D)), _resident((2 * D, D)),
        ],
        out_specs=[_tiles(TM, D), _tiles(TM, D), _tiles(TM, 2 * D)],
        out_shape=(
            jax.ShapeDtypeStruct((L, D), F32),
            jax.ShapeDtypeStruct((L, D), F32),
            jax.ShapeDtypeStruct((L, 2 * D), BF16),
        ),
        compiler_params=_cp(),
    )(yf, yb, xbc, z, pun, xe, vec, w_out)


def _ffn_fwd_call(x1, tgt, vec, w_gate, w_up, w_down):
    L = x1.shape[0]

    def body(x1_ref, tgt_ref, vec_ref, wg_ref, wu_ref, wd_ref, dpre_ref, gacc_ref):
        @pl.when(pl.program_id(0) == 0)
        def _():
            gacc_ref[...] = jnp.zeros_like(gacc_ref)

        x1 = x1_ref[...]
        h2 = (x1 * (1.0 + vec_ref[V_SC2:V_SC2 + 1, :]) + vec_ref[V_SH2:V_SH2 + 1, :]).astype(BF16)
        gt = _dot(h2, wg_ref[...])
        f = (gt * _sigmoid(gt) * _dot(h2, wu_ref[...])).astype(BF16)
        ffn = _dot(f, wd_ref[...])
        g2 = vec_ref[V_G2:V_G2 + 1, :]
        lng = vec_ref[V_LN2G:V_LN2G + 1, :]
        x2, n2, rstd2 = _ln(ALPHA * x1 + g2 * ffn, lng, vec_ref[V_LN2B:V_LN2B + 1, :])
        diff = x2 - tgt_ref[...]
        dx2 = diff * (1.0 / D)
        dpre2 = _ln_bwd(dx2, n2, rstd2, lng)
        dpre_ref[...] = dpre2
        gacc_ref[V_LN2G:V_LN2G + 1, :] += _rowsum(dx2 * n2)
        gacc_ref[V_LN2B:V_LN2B + 1, :] += _rowsum(dx2)
        gacc_ref[V_G2:V_G2 + 1, :] += _rowsum(dpre2 * ffn)
        gacc_ref[V_LOSS:V_LOSS + 1, :] += jnp.sum(diff * diff) * (0.5 / D)

    return pl.pallas_call(
        body,
        name="ffn_fwd",
        grid=(L // TM,),
        in_specs=[_tiles(TM, D), _tiles(TM, D), _resident((NV, D)),
                  _resident((D, DFF)), _resident((D, DFF)), _resident((DFF, D))],
        out_specs=[_tiles(TM, D), _const_out((NV, D))],
        out_shape=(jax.ShapeDtypeStruct((L, D), F32), jax.ShapeDtypeStruct((NV, D), F32)),
        compiler_params=_cp(),
    )(x1, tgt, vec, w_gate, w_up, w_down)


def _ffn_bwd_call(x1, dpre2, vec, w_gate, w_up, w_down):
    L = x1.shape[0]
    nt = L // TM
    nj = DFF // FFC

    def body(x1_ref, dpre_ref, vec_ref, wg_ref, wu_ref, wd_ref, dh2_ref, dwg_ref, dwu_ref, dwd_ref, ag, au, ad):
        j = pl.program_id(0)
        i = pl.program_id(1)

        @pl.when(i == 0)
        def _():
            ag[...] = jnp.zeros_like(ag)
            au[...] = jnp.zeros_like(au)
            ad[...] = jnp.zeros_like(ad)

        h2 = (x1_ref[...] * (1.0 + vec_ref[V_SC2:V_SC2 + 1, :]) + vec_ref[V_SH2:V_SH2 + 1, :]).astype(BF16)
        gt = _dot(h2, wg_ref[...])
        up = _dot(h2, wu_ref[...])
        sg = _sigmoid(gt)
        sl = gt * sg
        f = (sl * up).astype(BF16)
        dffn = (vec_ref[V_G2:V_G2 + 1, :] * dpre_ref[...]).astype(BF16)
        df = _dot_nt(dffn, wd_ref[...])
        dgt = (df * up * (sg * (1.0 + gt * (1.0 - sg)))).astype(BF16)
        dup = (df * sl).astype(BF16)
        dh2_ref[0] = _dot_nt(dgt, wg_ref[...]) + _dot_nt(dup, wu_ref[...])
        _acc_tn(ag, h2, dgt)
        _acc_tn(au, h2, dup)
        _acc_tn(ad, f, dffn)

        @pl.when(i == nt - 1)
        def _():
            pltpu.sync_copy(ag, dwg_ref.at[j])
            pltpu.sync_copy(au, dwu_ref.at[j])
            pltpu.sync_copy(ad, dwd_ref.at[j])

    any_spec = pl.BlockSpec(memory_space=pl.ANY)
    return pl.pallas_call(
        body,
        name="ffn_bwd",
        grid=(nj, nt),
        in_specs=[
            pl.BlockSpec((TM, D), lambda j, i: (i, 0)),
            pl.BlockSpec((TM, D), lambda j, i: (i, 0)),
            _resident((NV, D)),
            pl.BlockSpec((D, FFC), lambda j, i: (0, j)),
            pl.BlockSpec((D, FFC), lambda j, i: (0, j)),
            pl.BlockSpec((FFC, D), lambda j, i: (j, 0)),
        ],
        out_specs=[pl.BlockSpec((1, TM, D), lambda j, i: (j, i, 0)), any_spec, any_spec, any_spec],
        out_shape=(
            jax.ShapeDtypeStruct((nj, L, D), F32),
            jax.ShapeDtypeStruct((nj, D, FFC), F32),
            jax.ShapeDtypeStruct((nj, D, FFC), F32),
            jax.ShapeDtypeStruct((nj, FFC, D), F32),
        ),
        scratch_shapes=[pltpu.VMEM((D, FFC), F32), pltpu.VMEM((D, FFC), F32), pltpu.VMEM((FFC, D), F32)],
        compiler_params=_cp(2),
    )(x1, dpre2, vec, w_gate, w_up, w_down)


def _merge_bwd_call(dh2p, dpre2, x1, xe, mix, cat, yf, yb, xbc, z, dpool, pun, vec, w_out, pool_w, scattered):
    L = z.shape[0]
    nt = L // TM
    ns = len(scattered)

    def body(*refs):
        (dh2_ref, dpre2_ref, x1_ref, xe_ref, mix_ref, cat_ref, yf_ref, yb_ref, xs_ref, z_ref, dpool_ref, pun_ref,
         vec_ref, w_ref, pw_ref) = refs[:15]
        dxe_ref, dy_ref, dz_ref, dd_ref, dpw_ref, gacc_ref, dwo_ref = refs[15 + ns:22 + ns]
        dwo_s = refs[22 + 2 * ns]
        ex = _Exchange(refs[15:15 + ns], refs[22 + ns:22 + 2 * ns], *refs[23 + 2 * ns:], False)
        i = pl.program_id(0)

        @pl.when(i == 0)
        def _():
            ex.start()
            gacc_ref[...] = jnp.zeros_like(gacc_ref)
            dpw_ref[...] = jnp.zeros_like(dpw_ref)
            dwo_s[...] = jnp.zeros_like(dwo_s)

        def vrow(r):
            return vec_ref[r:r + 1, :]

        def gadd(r, val):
            gacc_ref[r:r + 1, :] += _rowsum(val)

        x1 = x1_ref[...]
        dh2 = dh2_ref[0] + dh2_ref[1]
        dx1 = ALPHA * dpre2_ref[...] + dh2 * (1.0 + vrow(V_SC2))
        gadd(V_SC2, dh2 * x1)
        gadd(V_SH2, dh2)
        mix = mix_ref[...]
        _, n1, rstd1 = _ln(ALPHA * xe_ref[...] + vrow(V_G1) * mix, vrow(V_LN1G), vrow(V_LN1B))
        gadd(V_LN1G, dx1 * n1)
        gadd(V_LN1B, dx1)
        dpre1 = _ln_bwd(dx1, n1, rstd1, vrow(V_LN1G))
        dxe_ref[...] = ALPHA * dpre1
        gadd(V_G1, dpre1 * mix)
        dmix = (vrow(V_G1) * dpre1).astype(BF16)
        dcat = _dot_nt(dmix, w_ref[...])
        cat = cat_ref[...]
        for c0 in range(0, 2 * D, 512):
            dwo_s[c0:c0 + 512, :] += _dot_tn(cat[:, c0:c0 + 512], dmix)
        dyn = dcat[:, 0:D]
        dp = dcat[:, D:2 * D]
        xs = xs_ref[...]
        z = z_ref[...]
        ym, sz, gated, r = _gated(yf_ref, yb_ref, xs, z, vec_ref)
        gadd(V_SSDG, dyn * gated * r)
        a = dyn * vrow(V_SSDG)
        dgated = r * a - gated * (r * r * r * jnp.mean(a * gated, axis=-1, keepdims=True))
        dym = dgated * (z * sz)
        dy_ref[...] = dym
        dz_ref[...] = dgated * ym * (sz * (1.0 + z * (1.0 - sz)))
        gadd(V_DSK, dym * xs)
        gadd(V_PSC, dp * pun_ref[...])
        dps = (dp * vrow(V_PSC)).astype(BF16)
        dpool = dpool_ref[...]
        for g in range(4):
            gs = slice(PG * g, PG * g + PG)
            dd_ref[:, gs] = _dot_nt(dps[:, gs], pw_ref[g])
            dpw_ref[g] += _dot_tn(dpool[:, gs], dps[:, gs])

        @pl.when(i == nt - 1)
        def _():
            pltpu.sync_copy(dwo_s, dwo_ref)
            ex.wait()

    any_spec = pl.BlockSpec(memory_space=pl.ANY)
    return pl.pallas_call(
        body,
        name="merge_bwd",
        grid=(nt,),
        in_specs=[
            pl.BlockSpec((2, TM, D), lambda i: (0, i, 0)),
            _tiles(TM, D), _tiles(TM, D), _tiles(TM, D), _tiles(TM, D), _tiles(TM, 2 * D),
            _tiles(TM, D), _tiles(TM, D),
            _tiles(TM, D), _tiles(TM, D), _tiles(TM, D), _tiles(TM, D),
            _resident((NV, D)), _resident((2 * D, D)), _resident((4, PG, PG)),
        ] + [any_spec] * ns,
        out_specs=[_tiles(TM, D), _tiles(TM, D), _tiles(TM, D), _tiles(TM, D),
                   _const_out((4, PG, PG)), _const_out((NV, D)), any_spec] + [any_spec] * ns,
        out_shape=[
            jax.ShapeDtypeStruct((L, D), F32),
            jax.ShapeDtypeStruct((L, D), F32),
            jax.ShapeDtypeStruct((L, D), F32),
            jax.ShapeDtypeStruct((L, D), F32),
            jax.ShapeDtypeStruct((4, PG, PG), F32),
            jax.ShapeDtypeStruct((NV, D), F32),
            jax.ShapeDtypeStruct((2 * D, D), F32),
        ] + _exchange_out_shapes(scattered, False),
        scratch_shapes=[pltpu.VMEM((2 * D, D), F32)] + _exchange_sems(ns),
        compiler_params=_cp(),
    )(dh2p, dpre2, x1, xe, mix, cat, yf, yb, xbc, z, dpool, pun, vec, w_out, pool_w, *scattered)


def _pool_bwd_call(dd, pmat_t, icnt):
    L = dd.shape[0]
    nt = L // PT
    cur, prev, nxt = _pool_specs(nt)
    icur = pl.BlockSpec((PT, 128), lambda i: (i, 0))
    iprev = pl.BlockSpec((PT, 128), lambda i: (jnp.maximum(i - 1, 0), 0))
    inxt = pl.BlockSpec((PT, 128), lambda i: (jnp.minimum(i + 1, nt - 1), 0))

    def body(cur_ref, prev_ref, next_ref, ic_ref, icp_ref, icn_ref, m0_ref, m1_ref, m2_ref, m3_ref, du_ref):
        i = pl.program_id(0)
        n = pl.num_programs(0)
        lane = lax.broadcasted_iota(jnp.int32, (1, 128), 1)
        icv = ic_ref[...]
        for g, m_ref in enumerate((m0_ref, m1_ref, m2_ref, m3_ref)):
            gs = slice(PG * g, PG * g + PG)
            halo = _halo_tokens(g)
            icp = _column(icp_ref[PT - halo:PT, :], lane, g)
            icn = _column(icn_ref[0:halo, :], lane, g)
            top = jnp.where(i > 0, prev_ref[PT - halo:PT, gs] * icp, 0.0)
            bot = jnp.where(i < n - 1, next_ref[0:halo, gs] * icn, 0.0)
            mid = cur_ref[:, gs]
            ext = jnp.concatenate([top, mid * _column(icv, lane, g), bot], axis=0)
            du_ref[:, gs] = _split_dot(m_ref[...], ext) - mid

    return pl.pallas_call(
        body,
        name="pool_bwd",
        grid=(nt,),
        in_specs=[cur, prev, nxt, icur, iprev, inxt] + [_resident(m.shape) for m in pmat_t],
        out_specs=_tiles(PT, D),
        out_shape=jax.ShapeDtypeStruct((L, D), F32),
        compiler_params=_cp(),
    )(dd, dd, dd, icnt, icnt, icnt, *pmat_t)


def _ssd_bwd_call(dy, xbc, dtr, hprev_f, hprev_b, dh_init, par, e_mat, et_mat, dskip, name):
    L = xbc.shape[0]
    nc = L // Q

    def body(dy0_ref, dy1_ref, xbc0_ref, xbc1_ref, dtr0_ref, dtr1_ref, hp0_ref, hp1_ref, dhi_ref, par_ref, e_ref,
             et_ref, dsk_ref, dx0_ref, dx1_ref, ddt0_ref, ddt1_ref, acc_ref, dh0_ref, dh_s, AT, dtT):
        s = pl.program_id(0)

        @pl.when(s == 0)
        def _():
            dh_s[...] = dhi_ref[...]
            acc_ref[...] = jnp.zeros_like(acc_ref)

        one_direction(0, dy0_ref, xbc0_ref, dtr0_ref, hp0_ref, par_ref, e_ref, et_ref, dsk_ref, dx0_ref, ddt0_ref,
                      acc_ref, dh0_ref, dh_s, AT, dtT)
        one_direction(1, dy1_ref, xbc1_ref, dtr1_ref, hp1_ref, par_ref, e_ref, et_ref, dsk_ref, dx1_ref, ddt1_ref,
                      acc_ref, dh0_ref, dh_s, AT, dtT)

    def one_direction(d, dy_ref, xbc_ref, dtr_ref, hp_ref, par_ref, e_ref, et_ref, dsk_ref, dxbc_ref, ddtr_ref,
                      acc_ref, dh0_ref, dh_s, AT_s, dtT_s):
        q = _ssd_common(d, dtr_ref[...], par_ref)
        A, maskf, lane, dt, atot = q["A"], q["maskf"], q["lane"], q["dt"], q["atot"]
        AT_s[d] = q["AT"]
        dtT_s[d] = q["dtT"]
        AT, dtT = AT_s.at[d], dtT_s.at[d]
        hprev = hp_ref[0]
        hpb = hprev.astype(BF16)
        dh = dh_s[d]
        dhb = dh.astype(BF16)
        xs = xbc_ref[:, 0:1024]
        bb = xbc_ref[:, 1024:1152].astype(BF16)
        cb = xbc_ref[:, 1152:1280].astype(BF16)
        dy = dy_ref[...]
        ea_f = _head_expand(e_ref, jnp.exp(A))
        ch = _dot_nt(cb, hpb)
        dch = (dy * ea_f).astype(BF16)
        dC = _dot(dch, hpb)
        dhprev = _dot_tn(dch, cb)
        dA = _head_sum(et_ref, dy * ch * ea_f)
        dec = _state_decay(et_ref, q["atot_col"])
        dhprev = dhprev + dh * dec
        hs_ = jnp.sum(_split_dot(e_ref[...], dh * hprev * dec), axis=1, keepdims=True)
        datot = jnp.max(jnp.broadcast_to(hs_, (128, 128)).T, axis=0, keepdims=True)
        ear = jnp.exp(atot - A)
        wend = ear * dt
        wf = _head_expand(e_ref, wend)
        xw = (xs * wf).astype(BF16)
        dxw = _dot_nt(bb, dhb)
        dB = _dot(xw, dhb)
        dxs = dxw * wf
        dwend = _head_sum(et_ref, dxw * xs)
        ddt = dwend * ear
        de = dwend * wend
        datot = datot + _rowsum(de)
        dA = dA - de
        g = _dot_nt(cb, bb)
        dG = jnp.zeros((Q, Q), F32)
        ddtT = jnp.zeros((128, Q), F32)
        dAT = jnp.zeros((128, Q), F32)
        sub = lax.broadcasted_iota(jnp.int32, (128, 1), 0)
        for k in range(NH // 2):
            ks = slice(128 * k, 128 * k + 128)
            xp = xs[:, ks]
            dyp = dy[:, ks]
            accdx = dxs[:, ks]
            if d == 0:
                accdx = accdx + dyp * dsk_ref[:, ks]
            for half in range(2):
                h = 2 * k + half
                inhead = (lane >= HP) if half else (lane < HP)
                seg = _column(A, lane, h) - AT[h:h + 1, :]
                lm = jnp.exp(jnp.minimum(seg, 0.0)) * maskf
                dtrow = dtT[h:h + 1, :]
                gl = g * lm
                sc = gl * dtrow
                dyh = jnp.where(inhead, dyp, 0.0).astype(BF16)
                xh = jnp.where(inhead, xp, 0.0).astype(BF16)
                dS = _dot_nt(dyh, xh)
                accdx = accdx + _dot(sc.T.astype(BF16), dyh)
                nn = dS * gl
                cn = _rowsum(nn)
                rm = jnp.sum(nn * dtrow, axis=1, keepdims=True)
                dG = dG + dS * (lm * dtrow)
                onsub = (sub == h).astype(F32)
                ddtT = ddtT + onsub * cn
                dAT = dAT - onsub * (cn * dtrow)
                dA = dA + rm * (lane == h).astype(F32)
            dxbc_ref[:, ks] = accdx
        dGb = dG.astype(BF16)
        dxbc_ref[:, 1024:1152] = dB + _dot_tn(dGb, cb)
        dxbc_ref[:, 1152:1280] = dC + _dot(dGb, bb)
        da = _dot_tn(maskf, dA + dAT.T, HI) + datot
        ddt = ddt + ddtT.T + da * q["aneg"]
        ddtr = jnp.where(q["hmask"], ddt * _sigmoid(q["pre"]), 0.0)
        ddtr_ref[...] = ddtr
        acc_ref[d, 0:1, :] += _rowsum(ddtr)
        acc_ref[d, 1:2, :] += _rowsum(da * dt) * q["aneg"]
        dh_s[d] = dhprev
        dh0_ref[d] = dhprev

    def back(s):
        return nc - 1 - s

    return pl.pallas_call(
        body,
        name=name,
        grid=(nc,),
        in_specs=[
            pl.BlockSpec((Q, D), lambda s: (back(s), 0)),
            pl.BlockSpec((Q, D), lambda s: (s, 0)),
            pl.BlockSpec((Q, DXBC), lambda s: (back(s), 0)),
            pl.BlockSpec((Q, DXBC), lambda s: (s, 0)),
            pl.BlockSpec((Q, 128), lambda s: (back(s), 0)),
            pl.BlockSpec((Q, 128), lambda s: (s, 1)),
            pl.BlockSpec((1, D, NS), lambda s: (back(s), 0, 0)),
            pl.BlockSpec((1, D, NS), lambda s: (s, 0, 0)),
            _resident((2, D, NS)),
            _resident((8, 128)),
            _resident((128, D)),
            _resident((D, 128)),
            _resident((1, D)),
        ],
        out_specs=[
            pl.BlockSpec((Q, DXBC), lambda s: (back(s), 0)),
            pl.BlockSpec((Q, DXBC), lambda s: (s, 0)),
            pl.BlockSpec((Q, 128), lambda s: (back(s), 0)),
            pl.BlockSpec((Q, 128), lambda s: (s, 0)),
            _const_out((2, 8, 128)),
            _const_out((2, D, NS)),
        ],
        out_shape=(
            jax.ShapeDtypeStruct((L, DXBC), F32),
            jax.ShapeDtypeStruct((L, DXBC), F32),
            jax.ShapeDtypeStruct((L, 128), F32),
            jax.ShapeDtypeStruct((L, 128), F32),
            jax.ShapeDtypeStruct((2, 8, 128), F32),
            jax.ShapeDtypeStruct((2, D, NS), F32),
        ),
        scratch_shapes=[pltpu.VMEM((2, D, NS), F32), pltpu.VMEM((2, 128, Q), F32), pltpu.VMEM((2, 128, Q), F32)],
        compiler_params=_cp(),
    )(dy, dy, xbc, xbc, dtr, dtr, hprev_f, hprev_b, dh_init, par, e_mat, et_mat, dskip)


def _conv_bwd_call(dxf, dxb, xbc_raw, cw, acc_init, name, scattered=()):
    L = xbc_raw.shape[0]
    nt = L // TM
    ns = len(scattered)
    prev, nxt = _halo_specs(TM, DXBC, L)

    def body(*refs):
        dxf_ref, dxb_ref, cur_ref, prev_ref, next_ref, cw_ref, init_ref = refs[:7]
        dpre_ref, acc_ref = refs[7 + ns:9 + ns]
        ext = refs[9 + 2 * ns]
        if ns:
            ex = _Exchange(refs[7:7 + ns], refs[9 + ns:9 + 2 * ns], *refs[10 + 2 * ns:], False)

        @pl.when(pl.program_id(0) == 0)
        def _():
            if ns:
                ex.start()
            acc_ref[...] = init_ref[...]

        _extended(ext, cur_ref, prev_ref, next_ref)
        pre = _conv_pre(ext, cw_ref, TM)
        sg = _sigmoid(pre)
        dpre = (dxf_ref[...] + dxb_ref[...]) * (sg * (1.0 + pre * (1.0 - sg)))
        dpre_ref[...] = dpre
        for k in range(5):
            acc_ref[k:k + 1, :] += _rowsum(dpre * _shifted(ext, k - 2, TM))
        acc_ref[5:6, :] += _rowsum(dpre)
        if ns:
            @pl.when(pl.program_id(0) == nt - 1)
            def _():
                ex.wait()

    any_spec = pl.BlockSpec(memory_space=pl.ANY)
    return pl.pallas_call(
        body,
        name=name,
        grid=(nt,),
        in_specs=[_tiles(TM, DXBC), _tiles(TM, DXBC), _tiles(TM, DXBC), prev, nxt,
                  _resident((8, DXBC)), _resident((8, DXBC))] + [any_spec] * ns,
        out_specs=[_tiles(TM, DXBC), _const_out((8, DXBC))] + [any_spec] * ns,
        out_shape=[jax.ShapeDtypeStruct((L, DXBC), F32), jax.ShapeDtypeStruct((8, DXBC), F32)]
        + _exchange_out_shapes(scattered, False),
        scratch_shapes=[pltpu.VMEM((TM + 16, DXBC), F32)] + (_exchange_sems(ns) if ns else []),
        compiler_params=_cp(),
    )(dxf, dxb, xbc_raw, xbc_raw, xbc_raw, cw, acc_init, *scattered)


def _inproj_bwd_call(dpre, cw, dz, ddt0, ddt1, dup, h1, dxe_part, x0, vec, w_in, dw_init, name):
    L = x0.shape[0]
    nt = L // TM
    prev, nxt = _halo_specs(TM, DXBC, L)

    def body(cur_ref, prev_ref, next_ref, cw_ref, dz_ref, ddt0_ref, ddt1_ref, dup_ref, h1_ref, dxe_ref, x0_ref,
             vec_ref, w_ref, dwi_ref, gx_ref, gacc_ref, dw_ref, dw_s, ext):
        i = pl.program_id(0)

        @pl.when(i == 0)
        def _():
            gacc_ref[...] = jnp.zeros_like(gacc_ref)
            pltpu.sync_copy(dwi_ref, dw_s)

        def vrow(r):
            return vec_ref[r:r + 1, :]

        _extended(ext, cur_ref, prev_ref, next_ref)
        dxr = cw_ref[0:1, :] * _shifted(ext, 2, TM)
        for k in range(1, 5):
            dxr = dxr + cw_ref[k:k + 1, :] * _shifted(ext, 2 - k, TM)
        dproj = jnp.concatenate([dz_ref[...], dxr, ddt0_ref[...], ddt1_ref[...], dup_ref[...]], axis=1).astype(BF16)
        dh1 = _dot_nt(dproj, w_ref[...])
        _acc_tn(dw_s, h1_ref[...], dproj)
        xe, n0, rstd0 = _ln(x0_ref[...], vrow(V_EMBG), vrow(V_EMBB))
        dxe = dxe_ref[...] + dh1 * (1.0 + vrow(V_SC1))
        gacc_ref[V_SC1:V_SC1 + 1, :] += _rowsum(dh1 * xe)
        gacc_ref[V_SH1:V_SH1 + 1, :] += _rowsum(dh1)
        gacc_ref[V_EMBG:V_EMBG + 1, :] += _rowsum(dxe * n0)
        gacc_ref[V_EMBB:V_EMBB + 1, :] += _rowsum(dxe)
        gx_ref[...] = _ln_bwd(dxe, n0, rstd0, vrow(V_EMBG))

        @pl.when(i == nt - 1)
        def _():
            pltpu.sync_copy(dw_s, dw_ref)

    any_spec = pl.BlockSpec(memory_space=pl.ANY)
    return pl.pallas_call(
        body,
        name=name,
        grid=(nt,),
        in_specs=[_tiles(TM, DXBC), prev, nxt, _resident((8, DXBC)), _tiles(TM, D), _tiles(TM, 128), _tiles(TM, 128),
                  _tiles(TM, D), _tiles(TM, D), _tiles(TM, D), _tiles(TM, D), _resident((NV, D)), _resident((D, WIN)),
                  any_spec],
        out_specs=[_tiles(TM, D), _const_out((NV, D)), any_spec],
        out_shape=(
            jax.ShapeDtypeStruct((L, D), F32),
            jax.ShapeDtypeStruct((NV, D), F32),
            jax.ShapeDtypeStruct((D, WIN), F32),
        ),
        scratch_shapes=[pltpu.VMEM((D, WIN), F32), pltpu.VMEM((TM + 16, DXBC), F32)],
        compiler_params=_cp(),
    )(dpre, dpre, dpre, cw, dz, ddt0, ddt1, dup, h1, dxe_part, x0, vec, w_in, dw_init)


def _adamw(w, g, m, v):
    m = ADAM_B1 * m + (1.0 - ADAM_B1) * g
    v = ADAM_B2 * v + (1.0 - ADAM_B2) * (g * g)
    m_hat = m / (1.0 - ADAM_B1 ** ADAM_STEP)
    v_hat = v / (1.0 - ADAM_B2 ** ADAM_STEP)
    delta = -ADAM_LR * (m_hat / (jnp.sqrt(v_hat) + ADAM_EPS) + ADAM_WD * w)
    return delta, m, v


def _adamw_shard_call(gslots, w, m, v, tr, name):
    R, C = w.shape

    def body(gs_ref, w_ref, m_ref, v_ref, g_ref, d_ref, mo_ref, vo_ref):
        g = gs_ref[0].astype(F32)
        for i in range(1, NDEV):
            g = g + gs_ref[i].astype(F32)
        delta, mn, vn = _adamw(w_ref[...], g, m_ref[...], v_ref[...])
        g_ref[...] = g
        d_ref[...] = delta
        mo_ref[...] = mn
        vo_ref[...] = vn

    t = _tiles(tr, C)
    return pl.pallas_call(
        body,
        name=name,
        grid=(R // tr,),
        in_specs=[pl.BlockSpec((NDEV, tr, C), lambda i: (0, i, 0)), t, t, t],
        out_specs=[t, t, t, t],
        out_shape=tuple(jax.ShapeDtypeStruct((R, C), F32) for _ in range(4)),
        compiler_params=_cp(),
    )(gslots, w, m, v)


def _wada_call(dm_ex, dm_ctx, silu_all, w, m, v):
    ncol = w.shape[1]

    def body(dme_ref, dmc_ref, s_ref, w_ref, m_ref, v_ref, g_ref, d_ref, mo_ref, vo_ref, ds_ref):
        dmc = _rowsum(dmc_ref[...])
        rows = lax.broadcasted_iota(jnp.int32, (8, 1), 0)
        low = jnp.where(rows == 0, dmc, 0.0)
        dm = jnp.concatenate([dme_ref[...], low], axis=0).astype(BF16)
        wv = w_ref[...]
        g = _dot_tn(s_ref[...].astype(BF16), dm)
        delta, mn, vn = _adamw(wv, g, m_ref[...], v_ref[...])
        g_ref[...] = g
        d_ref[...] = delta
        mo_ref[...] = mn
        vo_ref[...] = vn
        ds_ref[...] = _dot_nt(low.astype(BF16), wv.astype(BF16))

    return pl.pallas_call(
        body,
        name="wada_update",
        out_shape=tuple(jax.ShapeDtypeStruct((D, ncol), F32) for _ in range(4)) + (jax.ShapeDtypeStruct((8, D), F32),),
        compiler_params=pltpu.CompilerParams(vmem_limit_bytes=VMEM_LIMIT),
    )(dm_ex, dm_ctx, silu_all, w, m, v)


P_DMOD, P_DMODC, P_EMBG, P_EMBB, P_LN1G, P_LN1B, P_LN2G, P_LN2B = 0, 6, 8, 9, 10, 11, 12, 13
P_SSDG, P_PSC, P_DSK, P_CONVB, P_DTB, P_ALOG, P_LOSS, NP = 14, 15, 16, 17, 19, 20, 21, 24
S_CCTX, S_EMBG, S_EMBB, S_BADA, S_CONVB, S_DTB, S_ALOG, S_DSK = 0, 1, 2, 3, 9, 11, 12, 13
S_SSDG, S_PSC, S_LN1G, S_LN1B, S_LN2G, S_LN2B, NSM = 14, 15, 16, 17, 18, 19, 24


def _small_update_call(pall, dsil, cctx, w, m, v, et_mat):
    def body(p_ref, ds_ref, c_ref, w_ref, m_ref, v_ref, et_ref, g_ref, d_ref, mo_ref, vo_ref, loss_ref,
             tot, dsum, dsk8):
        tot[...] = p_ref[0]
        dsum[...] = ds_ref[0]
        for i in range(1, NDEV):
            tot[...] += p_ref[i]
            dsum[...] += ds_ref[i]
        cv = c_ref[...]
        sc = _sigmoid(cv)
        g_ref[...] = jnp.zeros_like(g_ref)
        g_ref[S_CCTX:S_CCTX + 1, :] = dsum[0:1, :] * (sc * (1.0 + cv * (1.0 - sc)))
        g_ref[S_EMBG:S_EMBG + 1, :] = tot[P_EMBG:P_EMBG + 1, :]
        g_ref[S_EMBB:S_EMBB + 1, :] = tot[P_EMBB:P_EMBB + 1, :]
        g_ref[S_BADA:S_BADA + 2, :] = tot[P_DMOD:P_DMOD + 2, :] + tot[P_DMODC:P_DMODC + 2, :]
        g_ref[S_BADA + 2:S_BADA + 6, :] = tot[P_DMOD + 2:P_DMOD + 6, :]
        g_ref[S_CONVB:S_CONVB + 2, :] = tot[P_CONVB:P_CONVB + 2, :]
        g_ref[S_DTB:S_DTB + 1, :] = tot[P_DTB:P_DTB + 1, :]
        g_ref[S_ALOG:S_ALOG + 1, :] = tot[P_ALOG:P_ALOG + 1, :]
        dsk8[...] = _dot(jnp.broadcast_to(tot[P_DSK:P_DSK + 1, :], (8, D)), et_ref[...].astype(F32), HI)
        g_ref[S_DSK:S_DSK + 1, 0:128] = dsk8[0:1, :]
        g_ref[S_SSDG:S_SSDG + 1, :] = tot[P_SSDG:P_SSDG + 1, :]
        g_ref[S_PSC:S_PSC + 1, :] = tot[P_PSC:P_PSC + 1, :]
        g_ref[S_LN1G:S_LN1G + 1, :] = tot[P_LN1G:P_LN1G + 1, :]
        g_ref[S_LN1B:S_LN1B + 1, :] = tot[P_LN1B:P_LN1B + 1, :]
        g_ref[S_LN2G:S_LN2G + 1, :] = tot[P_LN2G:P_LN2G + 1, :]
        g_ref[S_LN2B:S_LN2B + 1, :] = tot[P_LN2B:P_LN2B + 1, :]
        delta, mn, vn = _adamw(w_ref[...], g_ref[...], m_ref[...], v_ref[...])
        d_ref[...] = delta
        mo_ref[...] = mn
        vo_ref[...] = vn
        loss_ref[...] = jnp.broadcast_to(tot[P_LOSS:P_LOSS + 1, 0:128], (8, 128))

    return pl.pallas_call(
        body,
        name="small_update",
        out_shape=tuple(jax.ShapeDtypeStruct((NSM, D), F32) for _ in range(4)) + (jax.ShapeDtypeStruct((8, 128), F32),),
        scratch_shapes=[pltpu.VMEM((NP, D), F32), pltpu.VMEM((8, D), F32), pltpu.VMEM((8, 128), F32)],
        compiler_params=pltpu.CompilerParams(vmem_limit_bytes=VMEM_LIMIT),
    )(pall, dsil, cctx, w, m, v, et_mat)


def _pad_rows(flat, mult=16):
    n = flat.shape[0]
    rows = -(-n // D)
    rows = -(-rows // mult) * mult
    return jnp.pad(flat, (0, rows * D - n)).reshape(rows, D)


def _by_cols(dw):
    r = dw.shape[0]
    return jnp.transpose(dw.reshape(r, NDEV, -1), (1, 0, 2))


def _from_cols(g):
    return jnp.transpose(g, (1, 0, 2)).reshape(g.shape[1], -1)


def _pool_constants(L):
    rows = L // GW
    t_r = jnp.arange(PT) // GW
    t_c = jnp.arange(PT) % GW
    fw, bw, ic = [], [], []
    pos_r = jnp.arange(L) // GW
    pos_c = jnp.arange(L) % GW
    for g, w in enumerate(WINDOWS):
        lo, hi = -(w // 2), w - w // 2 - 1
        n_ext = PT + 2 * _halo_tokens(g)
        e_r = jnp.arange(n_ext) // GW - w // 2
        e_c = jnp.arange(n_ext) % GW
        dr = e_r[None, :] - t_r[:, None]
        dc = e_c[None, :] - t_c[:, None]
        fw.append(((dr >= lo) & (dr <= hi) & (dc >= lo) & (dc <= hi)).astype(BF16))
        bw.append(((-dr >= lo) & (-dr <= hi) & (-dc >= lo) & (-dc <= hi)).astype(BF16))
        cr = jnp.minimum(pos_r + hi, rows - 1) - jnp.maximum(pos_r + lo, 0) + 1
        cc = jnp.minimum(pos_c + hi, GW - 1) - jnp.maximum(pos_c + lo, 0) + 1
        ic.append(1.0 / (cr * cc).astype(F32))
    icnt = jnp.pad(jnp.stack(ic, axis=1), ((0, 0), (0, 124)))
    return fw, bw, icnt


def _head_matrices():
    hp = jnp.arange(D) // HP
    e = (jnp.arange(128)[:, None] == hp[None, :]).astype(BF16)
    return e, e.T


def _aligned_in_proj(w):
    zpad = jnp.zeros((D, 128 - NH), w.dtype)
    return jnp.concatenate([w[:, 0:2304], w[:, 2304:2320], zpad, w[:, 2320:2336], zpad, w[:, 2336:3360]], axis=1)


def _unaligned_in_proj(dw):
    return jnp.concatenate([dw[:, 0:2304], dw[:, 2304:2320], dw[:, 2432:2448], dw[:, 2560:3584]], axis=1)


def _row(v):
    return v.reshape(1, -1).astype(F32)


def _pad_lanes(v, width=D):
    v = v.reshape(1, -1)
    return jnp.pad(v, ((0, 0), (0, width - v.shape[1])))


def kernel(x, c, ctx, c_ctx, emb_ln_g, emb_ln_b, w_ada, b_ada, in_proj, conv_w, conv_b, dt_bias, a_log, d_skip, ssd_norm_g, pool_w, pool_scale, w_out, ln1_g, ln1_b, w_gate, w_up, w_down, ln2_g, ln2_b, loss_target, m_c_ctx, m_emb_ln_g, m_emb_ln_b, m_w_ada, m_b_ada, m_in_proj, m_conv_w, m_conv_b, m_dt_bias, m_a_log, m_d_skip, m_ssd_norm_g, m_pool_w, m_pool_scale, m_w_out, m_ln1_g, m_ln1_b, m_w_gate, m_w_up, m_w_down, m_ln2_g, m_ln2_b, v_c_ctx, v_emb_ln_g, v_emb_ln_b, v_w_ada, v_b_ada, v_in_proj, v_conv_w, v_conv_b, v_dt_bias, v_a_log, v_d_skip, v_ssd_norm_g, v_pool_w, v_pool_scale, v_w_out, v_ln1_g, v_ln1_b, v_w_gate, v_w_up, v_w_down, v_ln2_g, v_ln2_b):
    me = 4 * lax.axis_index("x") + 2 * lax.axis_index("y") + lax.axis_index("c")
    x0 = x[0]
    ctx0 = ctx[0]
    tgt = loss_target[0]
    L = x0.shape[0]
    LC = ctx0.shape[0]
    ncol_ada = w_ada.shape[2]

    small_in = jnp.concatenate([c.reshape(-1), conv_w.reshape(-1)])
    small_all, g_inp = _exchange([_pad_rows(small_in, 8), in_proj[0].astype(BF16)], "gather_first", True)
    c_all = small_all[:, 0, :]
    convw_all = small_all.reshape(NDEV, -1)[:, D:D + 5 * (DXBC // NDEV)].reshape(NDEV, 5, DXBC // NDEV)
    conv_w_full = _from_cols(convw_all)
    w_in = _aligned_in_proj(_from_cols(g_inp))
    late_shards = [pool_w[0].astype(BF16), w_out[0].astype(BF16), w_gate[0].astype(BF16), w_up[0].astype(BF16),
                   w_down[0].astype(BF16)]

    c_in = jnp.concatenate([c_all, c_ctx.reshape(1, D), jnp.zeros((7, D), F32)], axis=0)
    b_mine = lax.dynamic_slice(b_ada, (0, me * ncol_ada), (1, ncol_ada))
    silu_all, mod_mine = _mod_call(c_in, w_ada[0], b_mine)
    (mod_all,) = _exchange([mod_mine], "gather_mod", True)
    mod_all = _from_cols(mod_all)
    mod_me = lax.dynamic_slice(mod_all, (me, 0), (1, 6 * D)).reshape(6, D)
    mod_ctx = mod_all[8].reshape(6, D)

    tail = jnp.concatenate([
        _row(emb_ln_g), _row(emb_ln_b), _row(ln1_g), _row(ln1_b), _row(ln2_g), _row(ln2_b),
        _row(ssd_norm_g), _row(pool_scale), _row(jnp.repeat(d_skip.reshape(-1), HP)), jnp.zeros((1, D), F32)], axis=0)
    vec = jnp.concatenate([mod_me, tail], axis=0)
    vec_ctx = jnp.concatenate([mod_ctx, tail], axis=0)

    cw = jnp.concatenate([conv_w_full, conv_b.reshape(1, DXBC), jnp.zeros((2, DXBC), F32)], axis=0)
    par = jnp.concatenate([_pad_lanes(dt_bias[0, 0], 128), _pad_lanes(dt_bias[0, 1], 128),
                           _pad_lanes(a_log[0, 0], 128), _pad_lanes(a_log[0, 1], 128),
                           jnp.zeros((4, 128), F32)], axis=0)
    e_mat, et_mat = _head_matrices()
    pmat, pmat_t, icnt = _pool_constants(L)
    dskip_row = vec[V_DSK:V_DSK + 1]

    xe_c, h1_c, _, xbcr_c, dtr_c, _ = _f1_call(ctx0, vec_ctx, w_in, "inproj_fwd_ctx")
    xbc_c = _f2_call(xbcr_c, cw, "conv_fwd_ctx")
    hzero = jnp.zeros((2, D, NS), F32)
    _, _, hpf_c, hpb_c, hfin_c = _ssd_fwd_call(xbc_c, dtr_c, hzero, par, e_mat, et_mat, "ssd_fwd_ctx")

    xe, h1, z, xbcr, dtr, up = _f1_call(x0, vec, w_in, "inproj_fwd")
    xbc = _f2_call(xbcr, cw, "conv_fwd")
    yf, yb, hpf, hpb, _, g_pw, g_wo, g_wg, g_wu, g_wd = _ssd_fwd_call(xbc, dtr, hfin_c, par, e_mat, et_mat, "ssd_fwd",
                                                                       late_shards)
    pool_w_full = jnp.transpose(g_pw, (1, 0, 2, 3)).reshape(4, PG, PG)
    w_out_full = g_wo.reshape(2 * D, D)
    w_gate_full = _from_cols(g_wg)
    w_up_full = _from_cols(g_wu)
    w_down_full = g_wd.reshape(DFF, D)
    dpool, pun = _pool_fwd_call(up, pmat, icnt, pool_w_full)
    x1, mix, cat = _merge_call(yf, yb, xbc, z, pun, xe, vec, w_out_full)
    dpre2, gacc_f = _ffn_fwd_call(x1, tgt, vec, w_gate_full, w_up_full, w_down_full)

    dh2p, dwg2, dwu2, dwd2 = _ffn_bwd_call(x1, dpre2, vec, w_gate_full, w_up_full, w_down_full)
    nq = FFC // (DFF // NDEV)
    ffn_parts = [
        jnp.transpose(dwg2.reshape(-1, D, nq, DFF // NDEV), (0, 2, 1, 3)).reshape(NDEV, D, DFF // NDEV),
        jnp.transpose(dwu2.reshape(-1, D, nq, DFF // NDEV), (0, 2, 1, 3)).reshape(NDEV, D, DFF // NDEV),
        dwd2.reshape(NDEV, DFF // NDEV, D)]
    dxe_part, dy, dz, dd, dpw, gacc_m, dwo, gs_wg, gs_wu, gs_wd = _merge_bwd_call(
        dh2p, dpre2, x1, xe, mix, cat, yf, yb, xbc, z, dpool, pun, vec, w_out_full, pool_w_full, ffn_parts)
    dup = _pool_bwd_call(dd, pmat_t, icnt)
    dxf, dxb, ddt0, ddt1, sacc, dh0 = _ssd_bwd_call(dy, xbc, dtr, hpf, hpb, hzero, par, e_mat, et_mat, dskip_row,
                                                     "ssd_bwd")
    zeros_c = jnp.zeros((LC, D), F32)
    dxf_c, dxb_c, ddt0_c, ddt1_c, sacc_c, _ = _ssd_bwd_call(zeros_c, xbc_c, dtr_c, hpf_c, hpb_c, dh0, par, e_mat, et_mat,
                                                            jnp.zeros((1, D), F32), "ssd_bwd_ctx")
    dprec_c, cacc_c = _conv_bwd_call(dxf_c, dxb_c, xbcr_c, cw, jnp.zeros((8, DXBC), F32), "conv_bwd_ctx")
    _, gacc_c, dwin_c = _inproj_bwd_call(dprec_c, cw, zeros_c, ddt0_c, ddt1_c, zeros_c, h1_c, zeros_c, ctx0, vec_ctx,
                                         w_in, jnp.zeros((D, WIN), F32), "inproj_bwd_ctx")
    mix_parts = [dwo.reshape(NDEV, 2 * D // NDEV, D),
                 jnp.transpose(dpw.reshape(4, NDEV, PG // NDEV, PG), (1, 0, 2, 3)).reshape(NDEV, 4 * PG // NDEV, PG)]
    dprec, cacc, gs_wo, gs_pw = _conv_bwd_call(dxf, dxb, xbcr, cw, cacc_c, "conv_bwd", mix_parts)
    grad_x, gacc_i, dwin = _inproj_bwd_call(dprec, cw, dz, ddt0, ddt1, dup, h1, dxe_part, x0, vec, w_in, dwin_c,
                                            "inproj_bwd")

    gsum = gacc_f + gacc_m + gacc_i
    sa = sacc + sacc_c
    dtb_row = _pad_lanes(jnp.concatenate([sa[0, 0, 0:NH], sa[1, 0, 0:NH]]))
    alog_row = _pad_lanes(jnp.concatenate([sa[0, 1, 0:NH], sa[1, 1, 0:NH]]))
    convb_rows = jnp.pad(cacc[5], (0, 2 * D - DXBC)).reshape(2, D)
    pack = jnp.concatenate([
        gsum[V_SH1:V_G2 + 1],
        gacc_c[V_SH1:V_SC1 + 1],
        gsum[V_EMBG:V_EMBB + 1] + gacc_c[V_EMBG:V_EMBB + 1],
        gsum[V_LN1G:V_LN2B + 1],
        gsum[V_SSDG:V_DSK + 1],
        convb_rows, dtb_row, alog_row,
        gsum[V_LOSS:V_LOSS + 1],
        jnp.zeros((NP - 22, D), F32)], axis=0)
    (pall,) = _exchange([pack], "gather_small_grads", True)

    dm_flat = pall[:, 0:8, :].reshape(NDEV, 8 * D)
    dm_ex = lax.dynamic_slice(dm_flat, (0, me * ncol_ada), (NDEV, ncol_ada))
    dmc_full = jnp.concatenate([dm_flat[:, 6 * D:8 * D], jnp.zeros((NDEV, 4 * D), F32)], axis=1)
    dm_ctx = lax.dynamic_slice(dmc_full, (0, me * ncol_ada), (NDEV, ncol_ada))
    g_wada, d_wada, nm_wada, nv_wada, dsil = _wada_call(dm_ex, dm_ctx, silu_all, w_ada[0], m_w_ada[0], v_w_ada[0])
    (dsil_all,) = _exchange([dsil], "gather_dsilu", True)

    def small_pack(cc, eg, eb, ba, cb_, dtb, al, dsk, sg, ps, l1g, l1b, l2g, l2b):
        return jnp.concatenate([
            _row(cc), _row(eg), _row(eb), ba.reshape(6, D), jnp.pad(cb_.reshape(-1), (0, 2 * D - DXBC)).reshape(2, D),
            _pad_lanes(dtb.reshape(-1)), _pad_lanes(al.reshape(-1)), _pad_lanes(dsk.reshape(-1)),
            _row(sg), _row(ps), _row(l1g), _row(l1b), _row(l2g), _row(l2b), jnp.zeros((NSM - 20, D), F32)], axis=0)

    sw = small_pack(c_ctx, emb_ln_g, emb_ln_b, b_ada, conv_b, dt_bias, a_log, d_skip, ssd_norm_g, pool_scale,
                    ln1_g, ln1_b, ln2_g, ln2_b)
    sm = small_pack(m_c_ctx, m_emb_ln_g, m_emb_ln_b, m_b_ada, m_conv_b, m_dt_bias, m_a_log, m_d_skip, m_ssd_norm_g,
                    m_pool_scale, m_ln1_g, m_ln1_b, m_ln2_g, m_ln2_b)
    sv = small_pack(v_c_ctx, v_emb_ln_g, v_emb_ln_b, v_b_ada, v_conv_b, v_dt_bias, v_a_log, v_d_skip, v_ssd_norm_g,
                    v_pool_scale, v_ln1_g, v_ln1_b, v_ln2_g, v_ln2_b)
    s_g, s_d, s_m, s_v, loss8 = _small_update_call(pall, dsil_all, _row(c_ctx), sw, sm, sv, et_mat)

    def small_unpack(t):
        return (t[S_CCTX], t[S_EMBG], t[S_EMBB], t[S_BADA:S_BADA + 6].reshape(1, 6 * D),
                t[S_CONVB:S_CONVB + 2].reshape(-1)[:DXBC].reshape(1, DXBC),
                t[S_DTB, 0:2 * NH].reshape(1, 2, NH), t[S_ALOG, 0:2 * NH].reshape(1, 2, NH), t[S_DSK, 0:NH].reshape(1, NH),
                t[S_SSDG].reshape(1, D), t[S_PSC].reshape(1, D), t[S_LN1G].reshape(1, D), t[S_LN1B].reshape(1, D),
                t[S_LN2G].reshape(1, D), t[S_LN2B].reshape(1, D))

    gs_inp, gs_cw = _exchange([_by_cols(_unaligned_in_proj(dwin)).astype(BF16), _by_cols(cacc[0:5])],
                              "exchange_last_grads", False)

    pshape = (4 * PG // NDEV, PG)
    u_inp = _adamw_shard_call(gs_inp, in_proj[0], m_in_proj[0], v_in_proj[0], 256, "adamw_in_proj")
    u_cw = _adamw_shard_call(gs_cw, conv_w[0], m_conv_w[0], v_conv_w[0], 5, "adamw_conv_w")
    u_pw = _adamw_shard_call(gs_pw, pool_w[0].reshape(pshape), m_pool_w[0].reshape(pshape), v_pool_w[0].reshape(pshape),
                             pshape[0], "adamw_pool_w")
    u_wo = _adamw_shard_call(gs_wo, w_out[0], m_w_out[0], v_w_out[0], 64, "adamw_w_out")
    u_wg = _adamw_shard_call(gs_wg, w_gate[0], m_w_gate[0], v_w_gate[0], 256, "adamw_w_gate")
    u_wu = _adamw_shard_call(gs_wu, w_up[0], m_w_up[0], v_w_up[0], 256, "adamw_w_up")
    u_wd = _adamw_shard_call(gs_wd, w_down[0], m_w_down[0], v_w_down[0], 88, "adamw_w_down")

    def assemble(k, small, wada):
        (cc, eg, eb, ba, cb_, dtb, al, dsk, sg, ps, l1g, l1b, l2g, l2b) = small_unpack(small)
        pw = u_pw[k].reshape(1, 4, PG // NDEV, PG)
        return (cc, eg, eb, wada[None], ba, u_inp[k][None], u_cw[k][None], cb_, dtb, al, dsk, sg, pw, ps,
                u_wo[k][None], l1g, l1b, u_wg[k][None], u_wu[k][None], u_wd[k][None], l2g, l2b)

    loss = loss8[0, 0]
    return (loss, grad_x[None], *assemble(0, s_g, g_wada), *assemble(1, s_d, d_wada),
            *assemble(2, s_m, nm_wada), *assemble(3, s_v, nv_wada))
```

```python
import functools
import math

import jax
import jax.numpy as jnp
from jax import lax
from jax.experimental import pallas as pl
from jax.experimental.pallas import tpu as pltpu

F32 = jnp.float32
BF16 = jnp.bfloat16
HI = lax.Precision.HIGHEST

NDEV = 8
D = 1024
NH = 16
HP = 64
NS = 128
Q = 128
DXBC = 1280
DFF = 2816
FFC = 1408
GW = 64
PR = 8
PT = PR * GW
WINDOWS = (2, 4, 8, 16)
PG = 256
DIN = 3360
WIN = 3584
ALPHA = 2.0 ** 0.25
LN_EPS = 1e-5
TM = 256

ADAM_LR = 0.001
ADAM_B1 = 0.9
ADAM_B2 = 0.999
ADAM_EPS = 1e-08
ADAM_WD = 0.01
ADAM_STEP = 10

V_SH1, V_SC1, V_G1, V_SH2, V_SC2, V_G2 = 0, 1, 2, 3, 4, 5
V_EMBG, V_EMBB, V_LN1G, V_LN1B, V_LN2G, V_LN2B = 6, 7, 8, 9, 10, 11
V_SSDG, V_PSC, V_DSK, V_LOSS = 12, 13, 14, 15
NV = 16

VMEM_LIMIT = 60 * 1024 * 1024


def _cp(ndim=1):
    return pltpu.CompilerParams(dimension_semantics=("arbitrary",) * ndim, vmem_limit_bytes=VMEM_LIMIT)


def _dot(a, b, precision=None):
    return jnp.dot(a, b, preferred_element_type=F32, precision=precision)


def _dot_nt(a, b):
    return lax.dot_general(a, b, (((1,), (1,)), ((), ())), preferred_element_type=F32)


def _dot_tn(a, b, precision=None):
    return lax.dot_general(a, b, (((0,), (0,)), ((), ())), preferred_element_type=F32, precision=precision)


def _split2(x):
    hi = x.astype(BF16)
    return hi, (x - hi.astype(F32)).astype(BF16)


def _split_dot(m, x):
    hi, lo = _split2(x)
    return _dot(m, hi) + _dot(m, lo)


def _dot_split(x, m):
    hi, lo = _split2(x)
    return _dot(hi, m) + _dot(lo, m)


def _sigmoid(x):
    return 1.0 / (1.0 + jnp.exp(-x))


def _softplus(x):
    return jnp.maximum(x, 0.0) + jnp.log(1.0 + jnp.exp(-jnp.abs(x)))


def _ln(x, g, b):
    mu = jnp.mean(x, axis=-1, keepdims=True)
    xc = x - mu
    var = jnp.mean(xc * xc, axis=-1, keepdims=True)
    rstd = lax.rsqrt(var + LN_EPS)
    n = xc * rstd
    return n * g + b, n, rstd


def _ln_bwd(dy, n, rstd, g):
    dn = dy * g
    return rstd * (dn - jnp.mean(dn, axis=-1, keepdims=True) - n * jnp.mean(dn * n, axis=-1, keepdims=True))


def _rowsum(x):
    return jnp.sum(x, axis=0, keepdims=True)


def _resident(shape):
    nd = len(shape)
    return pl.BlockSpec(shape, lambda *_: (0,) * nd, pipeline_mode=pl.Buffered(1))


def _const_out(shape):
    nd = len(shape)
    return pl.BlockSpec(shape, lambda *_: (0,) * nd)


def _tiles(tm, width):
    return pl.BlockSpec((tm, width), lambda i: (i, 0))


def _halo_specs(tm, width, n_rows):
    r = tm // 8
    last = n_rows // 8 - 1
    prev = pl.BlockSpec((8, width), lambda i: (jnp.maximum(i * r - 1, 0), 0))
    nxt = pl.BlockSpec((8, width), lambda i: (jnp.minimum((i + 1) * r, last), 0))
    return prev, nxt


def _acc_tn(acc_ref, a, b, chunk=512):
    n = b.shape[1]
    for c0 in range(0, n, chunk):
        c1 = min(c0 + chunk, n)
        acc_ref[:, c0:c1] += _dot_tn(a, b[:, c0:c1])


def _my_coords():
    return lax.axis_index("x"), lax.axis_index("y"), lax.axis_index("c")


def _peer(k, mx, my, mc):
    kx, ky, kc = (k >> 2) & 1, (k >> 1) & 1, k & 1
    px = 1 - mx if kx else mx
    py = 1 - my if ky else my
    pc = 1 - mc if kc else mc
    return px, py, pc


class _Exchange:
    def __init__(self, srcs, dsts, send_sems, recv_sems, local_sems, gather):
        self.srcs, self.dsts, self.gather = srcs, dsts, gather
        self.send_sems, self.recv_sems, self.local_sems = send_sems, recv_sems, local_sems

    def _copies(self, outgoing):
        mx, my, mc = _my_coords()
        me = 4 * mx + 2 * my + mc
        local, remote = [], []
        for t, (src, dst) in enumerate(zip(self.srcs, self.dsts)):
            local.append(pltpu.make_async_copy(src if self.gather else src.at[me], dst.at[me], self.local_sems.at[t]))
            for k in range(1, NDEV):
                px, py, pc = _peer(k, mx, my, mc)
                pid = 4 * px + 2 * py + pc
                remote.append(pltpu.make_async_remote_copy(
                    src_ref=src if self.gather else src.at[pid],
                    dst_ref=dst.at[me] if outgoing else dst.at[pid],
                    send_sem=self.send_sems.at[t, k - 1],
                    recv_sem=self.recv_sems.at[t, k - 1],
                    device_id=(px, py, pc),
                    device_id_type=pl.DeviceIdType.MESH,
                ))
        return local, remote

    def start(self):
        local, remote = self._copies(True)
        for cp in local + remote:
            cp.start()

    def wait(self):
        local, sends = self._copies(True)
        _, recvs = self._copies(False)
        for cp in recvs:
            cp.wait_recv()
        for cp in sends:
            cp.wait_send()
        for cp in local:
            cp.wait()


def _exchange_sems(n):
    return [pltpu.SemaphoreType.DMA((n, NDEV - 1)), pltpu.SemaphoreType.DMA((n, NDEV - 1)), pltpu.SemaphoreType.DMA((n,))]


def _exchange_out_shapes(xs, gather):
    return [jax.ShapeDtypeStruct(x.shape if not gather else (NDEV,) + x.shape, x.dtype) for x in xs]


def _exchange(xs, name, gather):
    n = len(xs)

    def body(*refs):
        ex = _Exchange(refs[:n], refs[n:2 * n], *refs[2 * n:], gather)
        ex.start()
        ex.wait()

    any_spec = pl.BlockSpec(memory_space=pl.ANY)
    return pl.pallas_call(
        body,
        name=name,
        out_shape=_exchange_out_shapes(xs, gather),
        in_specs=[any_spec] * n,
        out_specs=[any_spec] * n,
        scratch_shapes=_exchange_sems(n),
    )(*xs)


def _mod_call(c_all, w_ada, b_ada):
    ncol = w_ada.shape[1]

    def body(c_ref, w_ref, b_ref, silu_ref, mod_ref):
        cv = c_ref[...]
        s = cv * _sigmoid(cv)
        silu_ref[...] = s
        mod_ref[...] = _dot(s.astype(BF16), w_ref[...].astype(BF16)) + b_ref[...]

    return pl.pallas_call(
        body,
        name="mod_fwd",
        out_shape=(jax.ShapeDtypeStruct((16, D), F32), jax.ShapeDtypeStruct((16, ncol), F32)),
    )(c_all, w_ada, b_ada)


def _emb_ln_call(x0, gb, name, gathered=()):
    L = x0.shape[0]
    nt = L // TM
    ng = len(gathered)

    def body(*refs):
        x_ref, gb_ref = refs[:2]
        xe_ref = refs[2 + ng]
        i = pl.program_id(0)
        if ng:
            ex = _Exchange(refs[2:2 + ng], refs[3 + ng:3 + 2 * ng], *refs[3 + 2 * ng:], True)

            @pl.when(i == 0)
            def _():
                ex.start()

        xe_ref[...] = _ln(x_ref[...], gb_ref[0:1, :], gb_ref[1:2, :])[0]
        if ng:
            @pl.when(i == nt - 1)
            def _():
                ex.wait()

    any_spec = pl.BlockSpec(memory_space=pl.ANY)
    return pl.pallas_call(
        body,
        name=name,
        grid=(nt,),
        in_specs=[_tiles(TM, D), _resident((8, D))] + [any_spec] * ng,
        out_specs=[_tiles(TM, D)] + [any_spec] * ng,
        out_shape=[jax.ShapeDtypeStruct((L, D), F32)] + _exchange_out_shapes(gathered, True),
        scratch_shapes=_exchange_sems(ng) if ng else [],
        compiler_params=_cp(),
    )(x0, gb, *gathered)


def _f1_call(xe, vec, w_in, name):
    L = xe.shape[0]

    def body(xe_ref, vec_ref, w_ref, h1_ref, z_ref, xbc_ref, dt_ref, up_ref):
        h1 = (xe_ref[...] * (1.0 + vec_ref[V_SC1:V_SC1 + 1, :]) + vec_ref[V_SH1:V_SH1 + 1, :]).astype(BF16)
        proj = _dot(h1, w_ref[...])
        h1_ref[...] = h1
        z_ref[...] = proj[:, 0:1024]
        xbc_ref[...] = proj[:, 1024:2304]
        dt_ref[...] = proj[:, 2304:2560]
        up_ref[...] = proj[:, 2560:3584]

    return pl.pallas_call(
        body,
        name=name,
        grid=(L // TM,),
        in_specs=[_tiles(TM, D), _resident((NV, D)), _resident((D, WIN))],
        out_specs=[_tiles(TM, D), _tiles(TM, D), _tiles(TM, DXBC), _tiles(TM, 256), _tiles(TM, D)],
        out_shape=(
            jax.ShapeDtypeStruct((L, D), BF16),
            jax.ShapeDtypeStruct((L, D), F32),
            jax.ShapeDtypeStruct((L, DXBC), F32),
            jax.ShapeDtypeStruct((L, 256), F32),
            jax.ShapeDtypeStruct((L, D), F32),
        ),
        compiler_params=_cp(),
    )(xe, vec, w_in)


def _extended(ext, cur_ref, prev_ref, next_ref):
    i = pl.program_id(0)
    n = pl.num_programs(0)
    tm = cur_ref.shape[0]
    ext[0:8, :] = jnp.where(i > 0, prev_ref[...], 0.0)
    ext[8:8 + tm, :] = cur_ref[...]
    ext[8 + tm:16 + tm, :] = jnp.where(i < n - 1, next_ref[...], 0.0)
    return ext


def _shifted(ext, offset, tm):
    return ext[8 + offset:8 + offset + tm, :]


def _conv_pre(ext, cw_ref, tm):
    acc = cw_ref[5:6, :] + cw_ref[0:1, :] * _shifted(ext, -2, tm)
    for k in range(1, 5):
        acc = acc + cw_ref[k:k + 1, :] * _shifted(ext, k - 2, tm)
    return acc


def _f2_call(xbc_raw, cw, name):
    L = xbc_raw.shape[0]
    prev, nxt = _halo_specs(TM, DXBC, L)

    def body(cur_ref, prev_ref, next_ref, cw_ref, out_ref, ext):
        pre = _conv_pre(_extended(ext, cur_ref, prev_ref, next_ref), cw_ref, TM)
        out_ref[...] = pre * _sigmoid(pre)

    return pl.pallas_call(
        body,
        name=name,
        grid=(L // TM,),
        in_specs=[_tiles(TM, DXBC), prev, nxt, _resident((8, DXBC))],
        out_specs=_tiles(TM, DXBC),
        out_shape=jax.ShapeDtypeStruct((L, DXBC), F32),
        scratch_shapes=[pltpu.VMEM((TM + 16, DXBC), F32)],
        compiler_params=_cp(),
    )(xbc_raw, xbc_raw, xbc_raw, cw)


def _ssd_common(d, dtr, par_ref):
    lane = lax.broadcasted_iota(jnp.int32, (1, 128), 1)
    hmask = lane < NH
    bias = par_ref[d:d + 1, :]
    alog = par_ref[2 + d:3 + d, :]
    aneg = jnp.where(hmask, -jnp.exp(alog), 0.0)
    pre = dtr + bias
    dt = jnp.where(hmask, _softplus(pre), 0.0)
    a = dt * aneg
    row = lax.broadcasted_iota(jnp.int32, (Q, Q), 0)
    col = lax.broadcasted_iota(jnp.int32, (Q, Q), 1)
    maskf = ((row >= col) if d == 0 else (row <= col)).astype(F32)
    A = _dot(maskf, a, HI)
    atot = _rowsum(a)
    atot_col = jnp.sum(a.T, axis=1, keepdims=True)
    return dict(hmask=hmask, aneg=aneg, pre=pre, dt=dt, a=a, maskf=maskf, A=A, AT=A.T, dtT=dt.T,
                atot=atot, atot_col=atot_col, lane=lane)


def _column(v, lane, h):
    return jnp.sum(jnp.where(lane == h, v, 0.0), axis=1, keepdims=True)


def _head_expand(e_ref, v):
    return _dot_split(v, e_ref[...])


def _head_sum(et_ref, v):
    return _dot_split(v, et_ref[...])


def _state_decay(et_ref, atot_col):
    return _split_dot(et_ref[...], jnp.broadcast_to(jnp.exp(atot_col), (128, 128)))


def _ssd_fwd_call(xbc, dtr, h0, par, e_mat, et_mat, name, gathered=()):
    L = xbc.shape[0]
    nc = L // Q
    ng = len(gathered)

    def body(*refs):
        xbc_refs, dtr_refs = refs[0:2], refs[2:4]
        h0_ref, par_ref, e_ref, et_ref = refs[4:8]
        y_refs, hp_refs = refs[8 + ng:10 + ng], refs[10 + ng:12 + ng]
        hf_ref = refs[12 + ng]
        hs, AT, dtT = refs[13 + 2 * ng:16 + 2 * ng]
        s = pl.program_id(0)
        if ng:
            ex = _Exchange(refs[8:8 + ng], refs[13 + ng:13 + 2 * ng], *refs[16 + 2 * ng:], True)

        @pl.when(s == 0)
        def _():
            if ng:
                ex.start()
            hs[...] = h0_ref[...]

        for d in range(2):
            xbc_ref, y_ref = xbc_refs[d], y_refs[d]
            q = _ssd_common(d, dtr_refs[d][...], par_ref)
            A, maskf, lane = q["A"], q["maskf"], q["lane"]
            AT[d] = q["AT"]
            dtT[d] = q["dtT"]
            hprev = hs[d]
            hp_refs[d][0] = hprev
            bb = xbc_ref[:, 1024:1152].astype(BF16)
            cb = xbc_ref[:, 1152:1280].astype(BF16)
            g = _dot_nt(cb, bb)
            yoff = _dot_nt(cb, hprev.astype(BF16)) * _head_expand(e_ref, jnp.exp(A))
            for k in range(NH // 2):
                ks = slice(128 * k, 128 * k + 128)
                xp = xbc_ref[:, ks]
                acc = yoff[:, ks]
                for half in range(2):
                    h = 2 * k + half
                    seg = _column(A, lane, h) - AT[d, h:h + 1, :]
                    lm = jnp.exp(jnp.minimum(seg, 0.0)) * maskf
                    sc = (g * lm * dtT[d, h:h + 1, :]).astype(BF16)
                    inhead = (lane >= HP) if half else (lane < HP)
                    acc = acc + _dot(sc, jnp.where(inhead, xp, 0.0).astype(BF16))
                y_ref[:, ks] = acc
            wend = jnp.exp(q["atot"] - A) * q["dt"]
            xw = (xbc_ref[:, 0:1024] * _head_expand(e_ref, wend)).astype(BF16)
            hnew = hprev * _state_decay(et_ref, q["atot_col"]) + _dot_tn(xw, bb)
            hs[d] = hnew
            hf_ref[d] = hnew
        if ng:
            @pl.when(s == nc - 1)
            def _():
                ex.wait()

    any_spec = pl.BlockSpec(memory_space=pl.ANY)
    return pl.pallas_call(
        body,
        name=name,
        grid=(nc,),
        in_specs=[
            pl.BlockSpec((Q, DXBC), lambda s: (s, 0)),
            pl.BlockSpec((Q, DXBC), lambda s: (nc - 1 - s, 0)),
            pl.BlockSpec((Q, 128), lambda s: (s, 0)),
            pl.BlockSpec((Q, 128), lambda s: (nc - 1 - s, 1)),
            _resident((2, D, NS)),
            _resident((8, 128)),
            _resident((128, D)),
            _resident((D, 128)),
        ] + [any_spec] * ng,
        out_specs=[
            pl.BlockSpec((Q, D), lambda s: (s, 0)),
            pl.BlockSpec((Q, D), lambda s: (nc - 1 - s, 0)),
            pl.BlockSpec((1, D, NS), lambda s: (s, 0, 0)),
            pl.BlockSpec((1, D, NS), lambda s: (nc - 1 - s, 0, 0)),
            _const_out((2, D, NS)),
        ] + [any_spec] * ng,
        out_shape=[
            jax.ShapeDtypeStruct((L, D), F32),
            jax.ShapeDtypeStruct((L, D), F32),
            jax.ShapeDtypeStruct((nc, D, NS), F32),
            jax.ShapeDtypeStruct((nc, D, NS), F32),
            jax.ShapeDtypeStruct((2, D, NS), F32),
        ] + _exchange_out_shapes(gathered, True),
        scratch_shapes=[pltpu.VMEM((2, D, NS), F32), pltpu.VMEM((2, 128, Q), F32), pltpu.VMEM((2, 128, Q), F32)]
        + (_exchange_sems(ng) if ng else []),
        compiler_params=_cp(),
    )(xbc, xbc, dtr, dtr, h0, par, e_mat, et_mat, *gathered)


def _halo_tokens(g):
    return (WINDOWS[g] // 2) * GW


def _pool_specs(n_tiles):
    cur = pl.BlockSpec((PT, D), lambda i: (i, 0))
    prev = pl.BlockSpec((PT, D), lambda i: (jnp.maximum(i - 1, 0), 0))
    nxt = pl.BlockSpec((PT, D), lambda i: (jnp.minimum(i + 1, n_tiles - 1), 0))
    return cur, prev, nxt


def _pool_fwd_call(up, pmat, icnt, pool_w):
    L = up.shape[0]
    nt = L // PT
    cur, prev, nxt = _pool_specs(nt)

    def body(cur_ref, prev_ref, next_ref, m0_ref, m1_ref, m2_ref, m3_ref, ic_ref, pw_ref, d_ref, pun_ref):
        i = pl.program_id(0)
        n = pl.num_programs(0)
        lane = lax.broadcasted_iota(jnp.int32, (1, 128), 1)
        icv = ic_ref[...]
        for g, m_ref in enumerate((m0_ref, m1_ref, m2_ref, m3_ref)):
            gs = slice(PG * g, PG * g + PG)
            halo = _halo_tokens(g)
            top = jnp.where(i > 0, prev_ref[PT - halo:PT, gs], 0.0)
            bot = jnp.where(i < n - 1, next_ref[0:halo, gs], 0.0)
            mid = cur_ref[:, gs]
            box = _split_dot(m_ref[...], jnp.concatenate([top, mid, bot], axis=0))
            dg = (box * _column(icv, lane, g) - mid).astype(BF16)
            d_ref[:, gs] = dg
            pun_ref[:, gs] = _dot(dg, pw_ref[g])

    return pl.pallas_call(
        body,
        name="pool_fwd",
        grid=(nt,),
        in_specs=[cur, prev, nxt] + [_resident(m.shape) for m in pmat] + [_tiles(PT, 128), _resident((4, PG, PG))],
        out_specs=[_tiles(PT, D), _tiles(PT, D)],
        out_shape=(jax.ShapeDtypeStruct((L, D), BF16), jax.ShapeDtypeStruct((L, D), F32)),
        compiler_params=_cp(),
    )(up, up, up, *pmat, icnt, pool_w)


def _gated(yf_ref, yb_ref, xs, z, vec_ref):
    ym = yf_ref[...] + yb_ref[...] + vec_ref[V_DSK:V_DSK + 1, :] * xs
    sz = _sigmoid(z)
    gated = ym * (z * sz)
    r = lax.rsqrt(jnp.mean(gated * gated, axis=-1, keepdims=True) + LN_EPS)
    return ym, sz, gated, r


def _merge_call(yf, yb, xbc, z, pun, xe, vec, w_out):
    L = z.shape[0]

    def body(yf_ref, yb_ref, xs_ref, z_ref, pun_ref, xe_ref, vec_ref, w_ref, x1_ref, mix_ref, cat_ref):
        _, _, gated, r = _gated(yf_ref, yb_ref, xs_ref[...], z_ref[...], vec_ref)
        yn = gated * r * vec_ref[V_SSDG:V_SSDG + 1, :]
        p = pun_ref[...] * vec_ref[V_PSC:V_PSC + 1, :]
        cat = jnp.concatenate([yn, p], axis=1).astype(BF16)
        mix = _dot(cat, w_ref[...])
        pre1 = ALPHA * xe_ref[...] + vec_ref[V_G1:V_G1 + 1, :] * mix
        x1, _, _ = _ln(pre1, vec_ref[V_LN1G:V_LN1G + 1, :], vec_ref[V_LN1B:V_LN1B + 1, :])
        x1_ref[...] = x1
        mix_ref[...] = mix
        cat_ref[...] = cat

    return pl.pallas_call(
        body,
        name="merge_fwd",
        grid=(L // TM,),
        in_specs=[
            _tiles(TM, D), _tiles(TM, D), _tiles(TM, D), _tiles(TM, D), _tiles(TM, D), _tiles(TM, D),
            _resident((NV, D)), _resident((2 * D, D)),
        ],
        out_specs=[_tiles(TM, D), _tiles(TM, D), _tiles(TM, 2 * D)],
        out_shape=(
            jax.ShapeDtypeStruct((L, D), F32),
            jax.ShapeDtypeStruct((L, D), F32),
            jax.ShapeDtypeStruct((L, 2 * D), BF16),
        ),
        compiler_params=_cp(),
    )(yf, yb, xbc, z, pun, xe, vec, w_out)


def _ffn_fwd_call(x1, tgt, vec, w_gate, w_up, w_down):
    L = x1.shape[0]

    def body(x1_ref, tgt_ref, vec_ref, wg_ref, wu_ref, wd_ref, dpre_ref, gacc_ref, gt_ref, up_ref):
        @pl.when(pl.program_id(0) == 0)
        def _():
            gacc_ref[...] = jnp.zeros_like(gacc_ref)

        x1 = x1_ref[...]
        h2 = (x1 * (1.0 + vec_ref[V_SC2:V_SC2 + 1, :]) + vec_ref[V_SH2:V_SH2 + 1, :]).astype(BF16)
        gt = _dot(h2, wg_ref[...])
        up = _dot(h2, wu_ref[...])
        gt_ref[...] = gt.astype(BF16)
        up_ref[...] = up.astype(BF16)
        f = (gt * _sigmoid(gt) * up).astype(BF16)
        ffn = _dot(f, wd_ref[...])
        g2 = vec_ref[V_G2:V_G2 + 1, :]
        lng = vec_ref[V_LN2G:V_LN2G + 1, :]
        x2, n2, rstd2 = _ln(ALPHA * x1 + g2 * ffn, lng, vec_ref[V_LN2B:V_LN2B + 1, :])
        diff = x2 - tgt_ref[...]
        dx2 = diff * (1.0 / D)
        dpre2 = _ln_bwd(dx2, n2, rstd2, lng)
        dpre_ref[...] = dpre2
        gacc_ref[V_LN2G:V_LN2G + 1, :] += _rowsum(dx2 * n2)
        gacc_ref[V_LN2B:V_LN2B + 1, :] += _rowsum(dx2)
        gacc_ref[V_G2:V_G2 + 1, :] += _rowsum(dpre2 * ffn)
        gacc_ref[V_LOSS:V_LOSS + 1, :] += jnp.sum(diff * diff) * (0.5 / D)

    return pl.pallas_call(
        body,
        name="ffn_fwd",
        grid=(L // TM,),
        in_specs=[_tiles(TM, D), _tiles(TM, D), _resident((NV, D)),
                  _resident((D, DFF)), _resident((D, DFF)), _resident((DFF, D))],
        out_specs=[_tiles(TM, D), _const_out((NV, D)), _tiles(TM, DFF), _tiles(TM, DFF)],
        out_shape=(jax.ShapeDtypeStruct((L, D), F32), jax.ShapeDtypeStruct((NV, D), F32),
                   jax.ShapeDtypeStruct((L, DFF), BF16), jax.ShapeDtypeStruct((L, DFF), BF16)),
        compiler_params=_cp(),
    )(x1, tgt, vec, w_gate, w_up, w_down)


def _ffn_bwd_call(x1, dpre2, gt_b, up_b, vec, w_gate, w_up, w_down):
    L = x1.shape[0]
    nt = L // TM
    nj = DFF // FFC

    def body(x1_ref, dpre_ref, gt_ref, up_ref, vec_ref, wg_ref, wu_ref, wd_ref, dh2_ref, dwg_ref, dwu_ref, dwd_ref,
             ag, au, ad):
        j = pl.program_id(0)
        i = pl.program_id(1)

        @pl.when(i == 0)
        def _():
            ag[...] = jnp.zeros_like(ag)
            au[...] = jnp.zeros_like(au)
            ad[...] = jnp.zeros_like(ad)

        h2 = (x1_ref[...] * (1.0 + vec_ref[V_SC2:V_SC2 + 1, :]) + vec_ref[V_SH2:V_SH2 + 1, :]).astype(BF16)
        gt = gt_ref[...].astype(F32)
        up = up_ref[...].astype(F32)
        sg = _sigmoid(gt)
        sl = gt * sg
        f = (sl * up).astype(BF16)
        dffn = (vec_ref[V_G2:V_G2 + 1, :] * dpre_ref[...]).astype(BF16)
        df = _dot_nt(dffn, wd_ref[...])
        dgt = (df * up * (sg * (1.0 + gt * (1.0 - sg)))).astype(BF16)
        dup = (df * sl).astype(BF16)
        dh2_ref[0] = _dot_nt(dgt, wg_ref[...]) + _dot_nt(dup, wu_ref[...])
        _acc_tn(ag, h2, dgt)
        _acc_tn(au, h2, dup)
        _acc_tn(ad, f, dffn)

        @pl.when(i == nt - 1)
        def _():
            pltpu.sync_copy(ag, dwg_ref.at[j])
            pltpu.sync_copy(au, dwu_ref.at[j])
            pltpu.sync_copy(ad, dwd_ref.at[j])

    any_spec = pl.BlockSpec(memory_space=pl.ANY)
    return pl.pallas_call(
        body,
        name="ffn_bwd",
        grid=(nj, nt),
        in_specs=[
            pl.BlockSpec((TM, D), lambda j, i: (i, 0)),
            pl.BlockSpec((TM, D), lambda j, i: (i, 0)),
            pl.BlockSpec((TM, FFC), lambda j, i: (i, j)),
            pl.BlockSpec((TM, FFC), lambda j, i: (i, j)),
            _resident((NV, D)),
            pl.BlockSpec((D, FFC), lambda j, i: (0, j)),
            pl.BlockSpec((D, FFC), lambda j, i: (0, j)),
            pl.BlockSpec((FFC, D), lambda j, i: (j, 0)),
        ],
        out_specs=[pl.BlockSpec((1, TM, D), lambda j, i: (j, i, 0)), any_spec, any_spec, any_spec],
        out_shape=(
            jax.ShapeDtypeStruct((nj, L, D), F32),
            jax.ShapeDtypeStruct((nj, D, FFC), F32),
            jax.ShapeDtypeStruct((nj, D, FFC), F32),
            jax.ShapeDtypeStruct((nj, FFC, D), F32),
        ),
        scratch_shapes=[pltpu.VMEM((D, FFC), F32), pltpu.VMEM((D, FFC), F32), pltpu.VMEM((FFC, D), F32)],
        compiler_params=_cp(2),
    )(x1, dpre2, gt_b, up_b, vec, w_gate, w_up, w_down)


def _merge_bwd_call(dh2p, dpre2, xe, mix, cat, yf, yb, xbc, z, dpool, pun, vec, w_out, pool_w, scattered):
    L = z.shape[0]
    nt = L // TM
    ns = len(scattered)

    def body(*refs):
        (dh2_ref, dpre2_ref, xe_ref, mix_ref, cat_ref, yf_ref, yb_ref, xs_ref, z_ref, dpool_ref, pun_ref,
         vec_ref, w_ref, pw_ref) = refs[:14]
        dxe_ref, dy_ref, dz_ref, dd_ref, dpw_ref, gacc_ref, dwo_ref = refs[14 + ns:21 + ns]
        dwo_s = refs[21 + 2 * ns]
        ex = _Exchange(refs[14:14 + ns], refs[21 + ns:21 + 2 * ns], *refs[22 + 2 * ns:], False)
        i = pl.program_id(0)

        @pl.when(i == 0)
        def _():
            ex.start()
            gacc_ref[...] = jnp.zeros_like(gacc_ref)
            dpw_ref[...] = jnp.zeros_like(dpw_ref)
            dwo_s[...] = jnp.zeros_like(dwo_s)

        def vrow(r):
            return vec_ref[r:r + 1, :]

        def gadd(r, val):
            gacc_ref[r:r + 1, :] += _rowsum(val)

        dh2 = dh2_ref[0] + dh2_ref[1]
        dx1 = ALPHA * dpre2_ref[...] + dh2 * (1.0 + vrow(V_SC2))
        mix = mix_ref[...]
        x1, n1, rstd1 = _ln(ALPHA * xe_ref[...] + vrow(V_G1) * mix, vrow(V_LN1G), vrow(V_LN1B))
        gadd(V_SC2, dh2 * x1)
        gadd(V_SH2, dh2)
        gadd(V_LN1G, dx1 * n1)
        gadd(V_LN1B, dx1)
        dpre1 = _ln_bwd(dx1, n1, rstd1, vrow(V_LN1G))
        dxe_ref[...] = ALPHA * dpre1
        gadd(V_G1, dpre1 * mix)
        dmix = (vrow(V_G1) * dpre1).astype(BF16)
        dcat = _dot_nt(dmix, w_ref[...])
        cat = cat_ref[...]
        for c0 in range(0, 2 * D, 512):
            dwo_s[c0:c0 + 512, :] += _dot_tn(cat[:, c0:c0 + 512], dmix)
        dyn = dcat[:, 0:D]
        dp = dcat[:, D:2 * D]
        xs = xs_ref[...]
        z = z_ref[...]
        ym, sz, gated, r = _gated(yf_ref, yb_ref, xs, z, vec_ref)
        gadd(V_SSDG, dyn * gated * r)
        a = dyn * vrow(V_SSDG)
        dgated = r * a - gated * (r * r * r * jnp.mean(a * gated, axis=-1, keepdims=True))
        dym = dgated * (z * sz)
        dy_ref[...] = dym
        dz_ref[...] = dgated * ym * (sz * (1.0 + z * (1.0 - sz)))
        gadd(V_DSK, dym * xs)
        gadd(V_PSC, dp * pun_ref[...])
        dps = (dp * vrow(V_PSC)).astype(BF16)
        dpool = dpool_ref[...]
        for g in range(4):
            gs = slice(PG * g, PG * g + PG)
            dd_ref[:, gs] = _dot_nt(dps[:, gs], pw_ref[g])
            dpw_ref[g] += _dot_tn(dpool[:, gs], dps[:, gs])

        @pl.when(i == nt - 1)
        def _():
            pltpu.sync_copy(dwo_s, dwo_ref)
            ex.wait()

    any_spec = pl.BlockSpec(memory_space=pl.ANY)
    return pl.pallas_call(
        body,
        name="merge_bwd",
        grid=(nt,),
        in_specs=[
            pl.BlockSpec((2, TM, D), lambda i: (0, i, 0)),
            _tiles(TM, D), _tiles(TM, D), _tiles(TM, D), _tiles(TM, 2 * D),
            _tiles(TM, D), _tiles(TM, D),
            _tiles(TM, D), _tiles(TM, D), _tiles(TM, D), _tiles(TM, D),
            _resident((NV, D)), _resident((2 * D, D)), _resident((4, PG, PG)),
        ] + [any_spec] * ns,
        out_specs=[_tiles(TM, D), _tiles(TM, D), _tiles(TM, D), _tiles(TM, D),
                   _const_out((4, PG, PG)), _const_out((NV, D)), any_spec] + [any_spec] * ns,
        out_shape=[
            jax.ShapeDtypeStruct((L, D), F32),
            jax.ShapeDtypeStruct((L, D), F32),
            jax.ShapeDtypeStruct((L, D), F32),
            jax.ShapeDtypeStruct((L, D), F32),
            jax.ShapeDtypeStruct((4, PG, PG), F32),
            jax.ShapeDtypeStruct((NV, D), F32),
            jax.ShapeDtypeStruct((2 * D, D), F32),
        ] + _exchange_out_shapes(scattered, False),
        scratch_shapes=[pltpu.VMEM((2 * D, D), F32)] + _exchange_sems(ns),
        compiler_params=_cp(),
    )(dh2p, dpre2, xe, mix, cat, yf, yb, xbc, z, dpool, pun, vec, w_out, pool_w, *scattered)


def _pool_bwd_call(dd, pmat_t, icnt):
    L = dd.shape[0]
    nt = L // PT
    cur, prev, nxt = _pool_specs(nt)
    icur = pl.BlockSpec((PT, 128), lambda i: (i, 0))
    iprev = pl.BlockSpec((PT, 128), lambda i: (jnp.maximum(i - 1, 0), 0))
    inxt = pl.BlockSpec((PT, 128), lambda i: (jnp.minimum(i + 1, nt - 1), 0))

    def body(cur_ref, prev_ref, next_ref, ic_ref, icp_ref, icn_ref, m0_ref, m1_ref, m2_ref, m3_ref, du_ref):
        i = pl.program_id(0)
        n = pl.num_programs(0)
        lane = lax.broadcasted_iota(jnp.int32, (1, 128), 1)
        icv = ic_ref[...]
        for g, m_ref in enumerate((m0_ref, m1_ref, m2_ref, m3_ref)):
            gs = slice(PG * g, PG * g + PG)
            halo = _halo_tokens(g)
            icp = _column(icp_ref[PT - halo:PT, :], lane, g)
            icn = _column(icn_ref[0:halo, :], lane, g)
            top = jnp.where(i > 0, prev_ref[PT - halo:PT, gs] * icp, 0.0)
            bot = jnp.where(i < n - 1, next_ref[0:halo, gs] * icn, 0.0)
            mid = cur_ref[:, gs]
            ext = jnp.concatenate([top, mid * _column(icv, lane, g), bot], axis=0)
            du_ref[:, gs] = _split_dot(m_ref[...], ext) - mid

    return pl.pallas_call(
        body,
        name="pool_bwd",
        grid=(nt,),
        in_specs=[cur, prev, nxt, icur, iprev, inxt] + [_resident(m.shape) for m in pmat_t],
        out_specs=_tiles(PT, D),
        out_shape=jax.ShapeDtypeStruct((L, D), F32),
        compiler_params=_cp(),
    )(dd, dd, dd, icnt, icnt, icnt, *pmat_t)


def _ssd_bwd_call(dy, xbc, dtr, hprev_f, hprev_b, dh_init, par, e_mat, et_mat, dskip, name):
    L = xbc.shape[0]
    nc = L // Q

    def body(dy0_ref, dy1_ref, xbc0_ref, xbc1_ref, dtr0_ref, dtr1_ref, hp0_ref, hp1_ref, dhi_ref, par_ref, e_ref,
             et_ref, dsk_ref, dx0_ref, dx1_ref, ddt0_ref, ddt1_ref, acc_ref, dh0_ref, dh_s, AT, dtT, ddtT_s, dAT_s):
        s = pl.program_id(0)

        @pl.when(s == 0)
        def _():
            dh_s[...] = dhi_ref[...]
            acc_ref[...] = jnp.zeros_like(acc_ref)
            ddtT_s[...] = jnp.zeros_like(ddtT_s)
            dAT_s[...] = jnp.zeros_like(dAT_s)

        one_direction(0, dy0_ref, xbc0_ref, dtr0_ref, hp0_ref, par_ref, e_ref, et_ref, dsk_ref, dx0_ref, ddt0_ref,
                      acc_ref, dh0_ref, dh_s, AT, dtT, ddtT_s, dAT_s)
        one_direction(1, dy1_ref, xbc1_ref, dtr1_ref, hp1_ref, par_ref, e_ref, et_ref, dsk_ref, dx1_ref, ddt1_ref,
                      acc_ref, dh0_ref, dh_s, AT, dtT, ddtT_s, dAT_s)

    def one_direction(d, dy_ref, xbc_ref, dtr_ref, hp_ref, par_ref, e_ref, et_ref, dsk_ref, dxbc_ref, ddtr_ref,
                      acc_ref, dh0_ref, dh_s, AT_s, dtT_s, ddtT_s, dAT_s):
        q = _ssd_common(d, dtr_ref[...], par_ref)
        A, maskf, lane, dt, atot = q["A"], q["maskf"], q["lane"], q["dt"], q["atot"]
        AT_s[d] = q["AT"]
        dtT_s[d] = q["dtT"]
        AT, dtT = AT_s.at[d], dtT_s.at[d]
        hprev = hp_ref[0]
        hpb = hprev.astype(BF16)
        dh = dh_s[d]
        dhb = dh.astype(BF16)
        xs = xbc_ref[:, 0:1024]
        bb = xbc_ref[:, 1024:1152].astype(BF16)
        cb = xbc_ref[:, 1152:1280].astype(BF16)
        dy = dy_ref[...]
        ea_f = _head_expand(e_ref, jnp.exp(A))
        ch = _dot_nt(cb, hpb)
        dch = (dy * ea_f).astype(BF16)
        dC = _dot(dch, hpb)
        dhprev = _dot_tn(dch, cb)
        dA = _head_sum(et_ref, dy * ch * ea_f)
        dec = _state_decay(et_ref, q["atot_col"])
        dhprev = dhprev + dh * dec
        hs_ = jnp.sum(_split_dot(e_ref[...], dh * hprev * dec), axis=1, keepdims=True)
        datot = jnp.max(jnp.broadcast_to(hs_, (128, 128)).T, axis=0, keepdims=True)
        ear = jnp.exp(atot - A)
        wend = ear * dt
        wf = _head_expand(e_ref, wend)
        xw = (xs * wf).astype(BF16)
        dxw = _dot_nt(bb, dhb)
        dB = _dot(xw, dhb)
        dxs = dxw * wf
        dwend = _head_sum(et_ref, dxw * xs)
        ddt = dwend * ear
        de = dwend * wend
        datot = datot + _rowsum(de)
        dA = dA - de
        g = _dot_nt(cb, bb)
        dG = jnp.zeros((Q, Q), F32)
        for k in range(NH // 2):
            ks = slice(128 * k, 128 * k + 128)
            xp = xs[:, ks]
            dyp = dy[:, ks]
            accdx = dxs[:, ks]
            if d == 0:
                accdx = accdx + dyp * dsk_ref[:, ks]
            for half in range(2):
                h = 2 * k + half
                inhead = (lane >= HP) if half else (lane < HP)
                seg = _column(A, lane, h) - AT[h:h + 1, :]
                lm = jnp.exp(jnp.minimum(seg, 0.0)) * maskf
                dtrow = dtT[h:h + 1, :]
                gl = g * lm
                sc = gl * dtrow
                dyh = jnp.where(inhead, dyp, 0.0).astype(BF16)
                xh = jnp.where(inhead, xp, 0.0).astype(BF16)
                dS = _dot_nt(dyh, xh)
                accdx = accdx + _dot(sc.T.astype(BF16), dyh)
                nn = dS * gl
                cn = _rowsum(nn)
                rm = jnp.sum(nn * dtrow, axis=1, keepdims=True)
                dG = dG + dS * (lm * dtrow)
                ddtT_s[d, h:h + 1, :] = cn
                dAT_s[d, h:h + 1, :] = -(cn * dtrow)
                dA = dA + rm * (lane == h).astype(F32)
            dxbc_ref[:, ks] = accdx
        dGb = dG.astype(BF16)
        dxbc_ref[:, 1024:1152] = dB + _dot_tn(dGb, cb)
        dxbc_ref[:, 1152:1280] = dC + _dot(dGb, bb)
        da = _dot_tn(maskf, dA + dAT_s[d].T, HI) + datot
        ddt = ddt + ddtT_s[d].T + da * q["aneg"]
        ddtr = jnp.where(q["hmask"], ddt * _sigmoid(q["pre"]), 0.0)
        ddtr_ref[...] = ddtr
        acc_ref[d, 0:1, :] += _rowsum(ddtr)
        acc_ref[d, 1:2, :] += _rowsum(da * dt) * q["aneg"]
        dh_s[d] = dhprev
        dh0_ref[d] = dhprev

    def back(s):
        return nc - 1 - s

    return pl.pallas_call(
        body,
        name=name,
        grid=(nc,),
        in_specs=[
            pl.BlockSpec((Q, D), lambda s: (back(s), 0)),
            pl.BlockSpec((Q, D), lambda s: (s, 0)),
            pl.BlockSpec((Q, DXBC), lambda s: (back(s), 0)),
            pl.BlockSpec((Q, DXBC), lambda s: (s, 0)),
            pl.BlockSpec((Q, 128), lambda s: (back(s), 0)),
            pl.BlockSpec((Q, 128), lambda s: (s, 1)),
            pl.BlockSpec((1, D, NS), lambda s: (back(s), 0, 0)),
            pl.BlockSpec((1, D, NS), lambda s: (s, 0, 0)),
            _resident((2, D, NS)),
            _resident((8, 128)),
            _resident((128, D)),
            _resident((D, 128)),
            _resident((1, D)),
        ],
        out_specs=[
            pl.BlockSpec((Q, DXBC), lambda s: (back(s), 0)),
            pl.BlockSpec((Q, DXBC), lambda s: (s, 0)),
            pl.BlockSpec((Q, 128), lambda s: (back(s), 0)),
            pl.BlockSpec((Q, 128), lambda s: (s, 0)),
            _const_out((2, 8, 128)),
            _const_out((2, D, NS)),
        ],
        out_shape=(
            jax.ShapeDtypeStruct((L, DXBC), F32),
            jax.ShapeDtypeStruct((L, DXBC), F32),
            jax.ShapeDtypeStruct((L, 128), F32),
            jax.ShapeDtypeStruct((L, 128), F32),
            jax.ShapeDtypeStruct((2, 8, 128), F32),
            jax.ShapeDtypeStruct((2, D, NS), F32),
        ),
        scratch_shapes=[pltpu.VMEM((2, D, NS), F32)] + [pltpu.VMEM((2, 128, Q), F32)] * 4,
        compiler_params=_cp(),
    )(dy, dy, xbc, xbc, dtr, dtr, hprev_f, hprev_b, dh_init, par, e_mat, et_mat, dskip)


def _conv_bwd_call(dxf, dxb, xbc_raw, cw, acc_init, name, scattered=()):
    L = xbc_raw.shape[0]
    nt = L // TM
    ns = len(scattered)
    prev, nxt = _halo_specs(TM, DXBC, L)

    def body(*refs):
        dxf_ref, dxb_ref, cur_ref, prev_ref, next_ref, cw_ref, init_ref = refs[:7]
        dpre_ref, acc_ref = refs[7 + ns:9 + ns]
        ext = refs[9 + 2 * ns]
        if ns:
            ex = _Exchange(refs[7:7 + ns], refs[9 + ns:9 + 2 * ns], *refs[10 + 2 * ns:], False)

        @pl.when(pl.program_id(0) == 0)
        def _():
            if ns:
                ex.start()
            acc_ref[...] = init_ref[...]

        _extended(ext, cur_ref, prev_ref, next_ref)
        pre = _conv_pre(ext, cw_ref, TM)
        sg = _sigmoid(pre)
        dpre = (dxf_ref[...] + dxb_ref[...]) * (sg * (1.0 + pre * (1.0 - sg)))
        dpre_ref[...] = dpre
        for k in range(5):
            acc_ref[k:k + 1, :] += _rowsum(dpre * _shifted(ext, k - 2, TM))
        acc_ref[5:6, :] += _rowsum(dpre)
        if ns:
            @pl.when(pl.program_id(0) == nt - 1)
            def _():
                ex.wait()

    any_spec = pl.BlockSpec(memory_space=pl.ANY)
    return pl.pallas_call(
        body,
        name=name,
        grid=(nt,),
        in_specs=[_tiles(TM, DXBC), _tiles(TM, DXBC), _tiles(TM, DXBC), prev, nxt,
                  _resident((8, DXBC)), _resident((8, DXBC))] + [any_spec] * ns,
        out_specs=[_tiles(TM, DXBC), _const_out((8, DXBC))] + [any_spec] * ns,
        out_shape=[jax.ShapeDtypeStruct((L, DXBC), F32), jax.ShapeDtypeStruct((8, DXBC), F32)]
        + _exchange_out_shapes(scattered, False),
        scratch_shapes=[pltpu.VMEM((TM + 16, DXBC), F32)] + (_exchange_sems(ns) if ns else []),
        compiler_params=_cp(),
    )(dxf, dxb, xbc_raw, xbc_raw, xbc_raw, cw, acc_init, *scattered)


def _inproj_bwd_call(dpre, cw, dz, ddt0, ddt1, dup, h1, dxe_part, x0, vec, w_in, dw_init, name):
    L = x0.shape[0]
    nt = L // TM
    prev, nxt = _halo_specs(TM, DXBC, L)

    def body(cur_ref, prev_ref, next_ref, cw_ref, dz_ref, ddt0_ref, ddt1_ref, dup_ref, h1_ref, dxe_ref, x0_ref,
             vec_ref, w_ref, dwi_ref, gx_ref, gacc_ref, dw_ref, dw_s, ext):
        i = pl.program_id(0)

        @pl.when(i == 0)
        def _():
            gacc_ref[...] = jnp.zeros_like(gacc_ref)
            pltpu.sync_copy(dwi_ref, dw_s)

        def vrow(r):
            return vec_ref[r:r + 1, :]

        _extended(ext, cur_ref, prev_ref, next_ref)
        dxr = cw_ref[0:1, :] * _shifted(ext, 2, TM)
        for k in range(1, 5):
            dxr = dxr + cw_ref[k:k + 1, :] * _shifted(ext, 2 - k, TM)
        dproj = jnp.concatenate([dz_ref[...], dxr, ddt0_ref[...], ddt1_ref[...], dup_ref[...]], axis=1).astype(BF16)
        dh1 = _dot_nt(dproj, w_ref[...])
        _acc_tn(dw_s, h1_ref[...], dproj)
        xe, n0, rstd0 = _ln(x0_ref[...], vrow(V_EMBG), vrow(V_EMBB))
        dxe = dxe_ref[...] + dh1 * (1.0 + vrow(V_SC1))
        gacc_ref[V_SC1:V_SC1 + 1, :] += _rowsum(dh1 * xe)
        gacc_ref[V_SH1:V_SH1 + 1, :] += _rowsum(dh1)
        gacc_ref[V_EMBG:V_EMBG + 1, :] += _rowsum(dxe * n0)
        gacc_ref[V_EMBB:V_EMBB + 1, :] += _rowsum(dxe)
        gx_ref[...] = _ln_bwd(dxe, n0, rstd0, vrow(V_EMBG))

        @pl.when(i == nt - 1)
        def _():
            pltpu.sync_copy(dw_s, dw_ref)

    any_spec = pl.BlockSpec(memory_space=pl.ANY)
    return pl.pallas_call(
        body,
        name=name,
        grid=(nt,),
        in_specs=[_tiles(TM, DXBC), prev, nxt, _resident((8, DXBC)), _tiles(TM, D), _tiles(TM, 128), _tiles(TM, 128),
                  _tiles(TM, D), _tiles(TM, D), _tiles(TM, D), _tiles(TM, D), _resident((NV, D)), _resident((D, WIN)),
                  any_spec],
        out_specs=[_tiles(TM, D), _const_out((NV, D)), any_spec],
        out_shape=(
            jax.ShapeDtypeStruct((L, D), F32),
            jax.ShapeDtypeStruct((NV, D), F32),
            jax.ShapeDtypeStruct((D, WIN), F32),
        ),
        scratch_shapes=[pltpu.VMEM((D, WIN), F32), pltpu.VMEM((TM + 16, DXBC), F32)],
        compiler_params=_cp(),
    )(dpre, dpre, dpre, cw, dz, ddt0, ddt1, dup, h1, dxe_part, x0, vec, w_in, dw_init)


def _adamw(w, g, m, v):
    m = ADAM_B1 * m + (1.0 - ADAM_B1) * g
    v = ADAM_B2 * v + (1.0 - ADAM_B2) * (g * g)
    m_hat = m / (1.0 - ADAM_B1 ** ADAM_STEP)
    v_hat = v / (1.0 - ADAM_B2 ** ADAM_STEP)
    delta = -ADAM_LR * (m_hat / (jnp.sqrt(v_hat) + ADAM_EPS) + ADAM_WD * w)
    return delta, m, v


def _adamw_shard_call(gslots, w, m, v, tr, name):
    R, C = w.shape

    def body(gs_ref, w_ref, m_ref, v_ref, g_ref, d_ref, mo_ref, vo_ref):
        g = gs_ref[0].astype(F32)
        for i in range(1, NDEV):
            g = g + gs_ref[i].astype(F32)
        delta, mn, vn = _adamw(w_ref[...], g, m_ref[...], v_ref[...])
        g_ref[...] = g
        d_ref[...] = delta
        mo_ref[...] = mn
        vo_ref[...] = vn

    t = _tiles(tr, C)
    return pl.pallas_call(
        body,
        name=name,
        grid=(R // tr,),
        in_specs=[pl.BlockSpec((NDEV, tr, C), lambda i: (0, i, 0)), t, t, t],
        out_specs=[t, t, t, t],
        out_shape=tuple(jax.ShapeDtypeStruct((R, C), F32) for _ in range(4)),
        compiler_params=_cp(),
    )(gslots, w, m, v)


def _wada_call(dm_ex, dm_ctx, silu_all, w, m, v):
    ncol = w.shape[1]

    def body(dme_ref, dmc_ref, s_ref, w_ref, m_ref, v_ref, g_ref, d_ref, mo_ref, vo_ref, ds_ref):
        dmc = _rowsum(dmc_ref[...])
        rows = lax.broadcasted_iota(jnp.int32, (8, 1), 0)
        low = jnp.where(rows == 0, dmc, 0.0)
        dm = jnp.concatenate([dme_ref[...], low], axis=0).astype(BF16)
        wv = w_ref[...]
        g = _dot_tn(s_ref[...].astype(BF16), dm)
        delta, mn, vn = _adamw(wv, g, m_ref[...], v_ref[...])
        g_ref[...] = g
        d_ref[...] = delta
        mo_ref[...] = mn
        vo_ref[...] = vn
        ds_ref[...] = _dot_nt(low.astype(BF16), wv.astype(BF16))

    return pl.pallas_call(
        body,
        name="wada_update",
        out_shape=tuple(jax.ShapeDtypeStruct((D, ncol), F32) for _ in range(4)) + (jax.ShapeDtypeStruct((8, D), F32),),
        compiler_params=pltpu.CompilerParams(vmem_limit_bytes=VMEM_LIMIT),
    )(dm_ex, dm_ctx, silu_all, w, m, v)


P_DMOD, P_DMODC, P_EMBG, P_EMBB, P_LN1G, P_LN1B, P_LN2G, P_LN2B = 0, 6, 8, 9, 10, 11, 12, 13
P_SSDG, P_PSC, P_DSK, P_CONVB, P_DTB, P_ALOG, P_LOSS, NP = 14, 15, 16, 17, 19, 20, 21, 24
S_CCTX, S_EMBG, S_EMBB, S_BADA, S_CONVB, S_DTB, S_ALOG, S_DSK = 0, 1, 2, 3, 9, 11, 12, 13
S_SSDG, S_PSC, S_LN1G, S_LN1B, S_LN2G, S_LN2B, NSM = 14, 15, 16, 17, 18, 19, 24


def _small_update_call(pall, dsil, cctx, w, m, v, et_mat):
    def body(p_ref, ds_ref, c_ref, w_ref, m_ref, v_ref, et_ref, g_ref, d_ref, mo_ref, vo_ref, loss_ref,
             tot, dsum, dsk8):
        tot[...] = p_ref[0]
        dsum[...] = ds_ref[0]
        for i in range(1, NDEV):
            tot[...] += p_ref[i]
            dsum[...] += ds_ref[i]
        cv = c_ref[...]
        sc = _sigmoid(cv)
        g_ref[...] = jnp.zeros_like(g_ref)
        g_ref[S_CCTX:S_CCTX + 1, :] = dsum[0:1, :] * (sc * (1.0 + cv * (1.0 - sc)))
        g_ref[S_EMBG:S_EMBG + 1, :] = tot[P_EMBG:P_EMBG + 1, :]
        g_ref[S_EMBB:S_EMBB + 1, :] = tot[P_EMBB:P_EMBB + 1, :]
        g_ref[S_BADA:S_BADA + 2, :] = tot[P_DMOD:P_DMOD + 2, :] + tot[P_DMODC:P_DMODC + 2, :]
        g_ref[S_BADA + 2:S_BADA + 6, :] = tot[P_DMOD + 2:P_DMOD + 6, :]
        g_ref[S_CONVB:S_CONVB + 2, :] = tot[P_CONVB:P_CONVB + 2, :]
        g_ref[S_DTB:S_DTB + 1, :] = tot[P_DTB:P_DTB + 1, :]
        g_ref[S_ALOG:S_ALOG + 1, :] = tot[P_ALOG:P_ALOG + 1, :]
        dsk8[...] = _dot(jnp.broadcast_to(tot[P_DSK:P_DSK + 1, :], (8, D)), et_ref[...].astype(F32), HI)
        g_ref[S_DSK:S_DSK + 1, 0:128] = dsk8[0:1, :]
        g_ref[S_SSDG:S_SSDG + 1, :] = tot[P_SSDG:P_SSDG + 1, :]
        g_ref[S_PSC:S_PSC + 1, :] = tot[P_PSC:P_PSC + 1, :]
        g_ref[S_LN1G:S_LN1G + 1, :] = tot[P_LN1G:P_LN1G + 1, :]
        g_ref[S_LN1B:S_LN1B + 1, :] = tot[P_LN1B:P_LN1B + 1, :]
        g_ref[S_LN2G:S_LN2G + 1, :] = tot[P_LN2G:P_LN2G + 1, :]
        g_ref[S_LN2B:S_LN2B + 1, :] = tot[P_LN2B:P_LN2B + 1, :]
        delta, mn, vn = _adamw(w_ref[...], g_ref[...], m_ref[...], v_ref[...])
        d_ref[...] = delta
        mo_ref[...] = mn
        vo_ref[...] = vn
        loss_ref[...] = jnp.broadcast_to(tot[P_LOSS:P_LOSS + 1, 0:128], (8, 128))

    return pl.pallas_call(
        body,
        name="small_update",
        out_shape=tuple(jax.ShapeDtypeStruct((NSM, D), F32) for _ in range(4)) + (jax.ShapeDtypeStruct((8, 128), F32),),
        scratch_shapes=[pltpu.VMEM((NP, D), F32), pltpu.VMEM((8, D), F32), pltpu.VMEM((8, 128), F32)],
        compiler_params=pltpu.CompilerParams(vmem_limit_bytes=VMEM_LIMIT),
    )(pall, dsil, cctx, w, m, v, et_mat)


def _pad_rows(flat, mult=16):
    n = flat.shape[0]
    rows = -(-n // D)
    rows = -(-rows // mult) * mult
    return jnp.pad(flat, (0, rows * D - n)).reshape(rows, D)


def _by_cols(dw):
    r = dw.shape[0]
    return jnp.transpose(dw.reshape(r, NDEV, -1), (1, 0, 2))


def _from_cols(g):
    return jnp.transpose(g, (1, 0, 2)).reshape(g.shape[1], -1)


def _pool_constants(L):
    rows = L // GW
    t_r = jnp.arange(PT) // GW
    t_c = jnp.arange(PT) % GW
    fw, bw, ic = [], [], []
    pos_r = jnp.arange(L) // GW
    pos_c = jnp.arange(L) % GW
    for g, w in enumerate(WINDOWS):
        lo, hi = -(w // 2), w - w // 2 - 1
        n_ext = PT + 2 * _halo_tokens(g)
        e_r = jnp.arange(n_ext) // GW - w // 2
        e_c = jnp.arange(n_ext) % GW
        dr = e_r[None, :] - t_r[:, None]
        dc = e_c[None, :] - t_c[:, None]
        fw.append(((dr >= lo) & (dr <= hi) & (dc >= lo) & (dc <= hi)).astype(BF16))
        bw.append(((-dr >= lo) & (-dr <= hi) & (-dc >= lo) & (-dc <= hi)).astype(BF16))
        cr = jnp.minimum(pos_r + hi, rows - 1) - jnp.maximum(pos_r + lo, 0) + 1
        cc = jnp.minimum(pos_c + hi, GW - 1) - jnp.maximum(pos_c + lo, 0) + 1
        ic.append(1.0 / (cr * cc).astype(F32))
    icnt = jnp.pad(jnp.stack(ic, axis=1), ((0, 0), (0, 124)))
    return fw, bw, icnt


def _head_matrices():
    hp = jnp.arange(D) // HP
    e = (jnp.arange(128)[:, None] == hp[None, :]).astype(BF16)
    return e, e.T


def _aligned_in_proj(w):
    zpad = jnp.zeros((D, 128 - NH), w.dtype)
    return jnp.concatenate([w[:, 0:2304], w[:, 2304:2320], zpad, w[:, 2320:2336], zpad, w[:, 2336:3360]], axis=1)


def _unaligned_in_proj(dw):
    return jnp.concatenate([dw[:, 0:2304], dw[:, 2304:2320], dw[:, 2432:2448], dw[:, 2560:3584]], axis=1)


def _row(v):
    return v.reshape(1, -1).astype(F32)


def _pad_lanes(v, width=D):
    v = v.reshape(1, -1)
    return jnp.pad(v, ((0, 0), (0, width - v.shape[1])))


def kernel(x, c, ctx, c_ctx, emb_ln_g, emb_ln_b, w_ada, b_ada, in_proj, conv_w, conv_b, dt_bias, a_log, d_skip, ssd_norm_g, pool_w, pool_scale, w_out, ln1_g, ln1_b, w_gate, w_up, w_down, ln2_g, ln2_b, loss_target, m_c_ctx, m_emb_ln_g, m_emb_ln_b, m_w_ada, m_b_ada, m_in_proj, m_conv_w, m_conv_b, m_dt_bias, m_a_log, m_d_skip, m_ssd_norm_g, m_pool_w, m_pool_scale, m_w_out, m_ln1_g, m_ln1_b, m_w_gate, m_w_up, m_w_down, m_ln2_g, m_ln2_b, v_c_ctx, v_emb_ln_g, v_emb_ln_b, v_w_ada, v_b_ada, v_in_proj, v_conv_w, v_conv_b, v_dt_bias, v_a_log, v_d_skip, v_ssd_norm_g, v_pool_w, v_pool_scale, v_w_out, v_ln1_g, v_ln1_b, v_w_gate, v_w_up, v_w_down, v_ln2_g, v_ln2_b):
    me = 4 * lax.axis_index("x") + 2 * lax.axis_index("y") + lax.axis_index("c")
    x0 = x[0]
    ctx0 = ctx[0]
    tgt = loss_target[0]
    L = x0.shape[0]
    LC = ctx0.shape[0]
    ncol_ada = w_ada.shape[2]

    small_in = jnp.concatenate([c.reshape(-1), conv_w.reshape(-1)])
    gb = jnp.concatenate([_row(emb_ln_g), _row(emb_ln_b), jnp.zeros((6, D), F32)], axis=0)
    xe, small_all, g_inp = _emb_ln_call(x0, gb, "emb_ln", [_pad_rows(small_in, 8), in_proj[0].astype(BF16)])
    (xe_c,) = _emb_ln_call(ctx0, gb, "emb_ln_ctx")
    c_all = small_all[:, 0, :]
    convw_all = small_all.reshape(NDEV, -1)[:, D:D + 5 * (DXBC // NDEV)].reshape(NDEV, 5, DXBC // NDEV)
    conv_w_full = _from_cols(convw_all)
    w_in = _aligned_in_proj(_from_cols(g_inp))
    late_shards = [pool_w[0].astype(BF16), w_out[0].astype(BF16), w_gate[0].astype(BF16), w_up[0].astype(BF16),
                   w_down[0].astype(BF16)]

    c_in = jnp.concatenate([c_all, c_ctx.reshape(1, D), jnp.zeros((7, D), F32)], axis=0)
    b_mine = lax.dynamic_slice(b_ada, (0, me * ncol_ada), (1, ncol_ada))
    silu_all, mod_mine = _mod_call(c_in, w_ada[0], b_mine)
    (mod_all,) = _exchange([mod_mine], "gather_mod", True)
    mod_all = _from_cols(mod_all)
    mod_me = lax.dynamic_slice(mod_all, (me, 0), (1, 6 * D)).reshape(6, D)
    mod_ctx = mod_all[8].reshape(6, D)

    tail = jnp.concatenate([
        _row(emb_ln_g), _row(emb_ln_b), _row(ln1_g), _row(ln1_b), _row(ln2_g), _row(ln2_b),
        _row(ssd_norm_g), _row(pool_scale), _row(jnp.repeat(d_skip.reshape(-1), HP)), jnp.zeros((1, D), F32)], axis=0)
    vec = jnp.concatenate([mod_me, tail], axis=0)
    vec_ctx = jnp.concatenate([mod_ctx, tail], axis=0)

    cw = jnp.concatenate([conv_w_full, conv_b.reshape(1, DXBC), jnp.zeros((2, DXBC), F32)], axis=0)
    par = jnp.concatenate([_pad_lanes(dt_bias[0, 0], 128), _pad_lanes(dt_bias[0, 1], 128),
                           _pad_lanes(a_log[0, 0], 128), _pad_lanes(a_log[0, 1], 128),
                           jnp.zeros((4, 128), F32)], axis=0)
    e_mat, et_mat = _head_matrices()
    pmat, pmat_t, icnt = _pool_constants(L)
    dskip_row = vec[V_DSK:V_DSK + 1]

    h1_c, _, xbcr_c, dtr_c, _ = _f1_call(xe_c, vec_ctx, w_in, "inproj_fwd_ctx")
    xbc_c = _f2_call(xbcr_c, cw, "conv_fwd_ctx")
    hzero = jnp.zeros((2, D, NS), F32)
    _, _, hpf_c, hpb_c, hfin_c = _ssd_fwd_call(xbc_c, dtr_c, hzero, par, e_mat, et_mat, "ssd_fwd_ctx")

    h1, z, xbcr, dtr, up = _f1_call(xe, vec, w_in, "inproj_fwd")
    xbc = _f2_call(xbcr, cw, "conv_fwd")
    yf, yb, hpf, hpb, _, g_pw, g_wo, g_wg, g_wu, g_wd = _ssd_fwd_call(xbc, dtr, hfin_c, par, e_mat, et_mat, "ssd_fwd",
                                                                       late_shards)
    pool_w_full = jnp.transpose(g_pw, (1, 0, 2, 3)).reshape(4, PG, PG)
    w_out_full = g_wo.reshape(2 * D, D)
    w_gate_full = _from_cols(g_wg)
    w_up_full = _from_cols(g_wu)
    w_down_full = g_wd.reshape(DFF, D)
    dpool, pun = _pool_fwd_call(up, pmat, icnt, pool_w_full)
    x1, mix, cat = _merge_call(yf, yb, xbc, z, pun, xe, vec, w_out_full)
    dpre2, gacc_f, gt_b, up_b = _ffn_fwd_call(x1, tgt, vec, w_gate_full, w_up_full, w_down_full)

    dh2p, dwg2, dwu2, dwd2 = _ffn_bwd_call(x1, dpre2, gt_b, up_b, vec, w_gate_full, w_up_full, w_down_full)
    nq = FFC // (DFF // NDEV)
    ffn_parts = [
        jnp.transpose(dwg2.reshape(-1, D, nq, DFF // NDEV), (0, 2, 1, 3)).reshape(NDEV, D, DFF // NDEV),
        jnp.transpose(dwu2.reshape(-1, D, nq, DFF // NDEV), (0, 2, 1, 3)).reshape(NDEV, D, DFF // NDEV),
        dwd2.reshape(NDEV, DFF // NDEV, D)]
    dxe_part, dy, dz, dd, dpw, gacc_m, dwo, gs_wg, gs_wu, gs_wd = _merge_bwd_call(
        dh2p, dpre2, xe, mix, cat, yf, yb, xbc, z, dpool, pun, vec, w_out_full, pool_w_full, ffn_parts)
    dup = _pool_bwd_call(dd, pmat_t, icnt)
    dxf, dxb, ddt0, ddt1, sacc, dh0 = _ssd_bwd_call(dy, xbc, dtr, hpf, hpb, hzero, par, e_mat, et_mat, dskip_row,
                                                     "ssd_bwd")
    zeros_c = jnp.zeros((LC, D), F32)
    dxf_c, dxb_c, ddt0_c, ddt1_c, sacc_c, _ = _ssd_bwd_call(zeros_c, xbc_c, dtr_c, hpf_c, hpb_c, dh0, par, e_mat, et_mat,
                                                            jnp.zeros((1, D), F32), "ssd_bwd_ctx")
    dprec_c, cacc_c = _conv_bwd_call(dxf_c, dxb_c, xbcr_c, cw, jnp.zeros((8, DXBC), F32), "conv_bwd_ctx")
    _, gacc_c, dwin_c = _inproj_bwd_call(dprec_c, cw, zeros_c, ddt0_c, ddt1_c, zeros_c, h1_c, zeros_c, ctx0, vec_ctx,
                                         w_in, jnp.zeros((D, WIN), F32), "inproj_bwd_ctx")
    mix_parts = [dwo.reshape(NDEV, 2 * D // NDEV, D),
                 jnp.transpose(dpw.reshape(4, NDEV, PG // NDEV, PG), (1, 0, 2, 3)).reshape(NDEV, 4 * PG // NDEV, PG)]
    dprec, cacc, gs_wo, gs_pw = _conv_bwd_call(dxf, dxb, xbcr, cw, cacc_c, "conv_bwd", mix_parts)
    grad_x, gacc_i, dwin = _inproj_bwd_call(dprec, cw, dz, ddt0, ddt1, dup, h1, dxe_part, x0, vec, w_in, dwin_c,
                                            "inproj_bwd")

    gsum = gacc_f + gacc_m + gacc_i
    sa = sacc + sacc_c
    dtb_row = _pad_lanes(jnp.concatenate([sa[0, 0, 0:NH], sa[1, 0, 0:NH]]))
    alog_row = _pad_lanes(jnp.concatenate([sa[0, 1, 0:NH], sa[1, 1, 0:NH]]))
    convb_rows = jnp.pad(cacc[5], (0, 2 * D - DXBC)).reshape(2, D)
    pack = jnp.concatenate([
        gsum[V_SH1:V_G2 + 1],
        gacc_c[V_SH1:V_SC1 + 1],
        gsum[V_EMBG:V_EMBB + 1] + gacc_c[V_EMBG:V_EMBB + 1],
        gsum[V_LN1G:V_LN2B + 1],
        gsum[V_SSDG:V_DSK + 1],
        convb_rows, dtb_row, alog_row,
        gsum[V_LOSS:V_LOSS + 1],
        jnp.zeros((NP - 22, D), F32)], axis=0)
    (pall,) = _exchange([pack], "gather_small_grads", True)

    dm_flat = pall[:, 0:8, :].reshape(NDEV, 8 * D)
    dm_ex = lax.dynamic_slice(dm_flat, (0, me * ncol_ada), (NDEV, ncol_ada))
    dmc_full = jnp.concatenate([dm_flat[:, 6 * D:8 * D], jnp.zeros((NDEV, 4 * D), F32)], axis=1)
    dm_ctx = lax.dynamic_slice(dmc_full, (0, me * ncol_ada), (NDEV, ncol_ada))
    g_wada, d_wada, nm_wada, nv_wada, dsil = _wada_call(dm_ex, dm_ctx, silu_all, w_ada[0], m_w_ada[0], v_w_ada[0])
    (dsil_all,) = _exchange([dsil], "gather_dsilu", True)

    def small_pack(cc, eg, eb, ba, cb_, dtb, al, dsk, sg, ps, l1g, l1b, l2g, l2b):
        return jnp.concatenate([
            _row(cc), _row(eg), _row(eb), ba.reshape(6, D), jnp.pad(cb_.reshape(-1), (0, 2 * D - DXBC)).reshape(2, D),
            _pad_lanes(dtb.reshape(-1)), _pad_lanes(al.reshape(-1)), _pad_lanes(dsk.reshape(-1)),
            _row(sg), _row(ps), _row(l1g), _row(l1b), _row(l2g), _row(l2b), jnp.zeros((NSM - 20, D), F32)], axis=0)

    sw = small_pack(c_ctx, emb_ln_g, emb_ln_b, b_ada, conv_b, dt_bias, a_log, d_skip, ssd_norm_g, pool_scale,
                    ln1_g, ln1_b, ln2_g, ln2_b)
    sm = small_pack(m_c_ctx, m_emb_ln_g, m_emb_ln_b, m_b_ada, m_conv_b, m_dt_bias, m_a_log, m_d_skip, m_ssd_norm_g,
                    m_pool_scale, m_ln1_g, m_ln1_b, m_ln2_g, m_ln2_b)
    sv = small_pack(v_c_ctx, v_emb_ln_g, v_emb_ln_b, v_b_ada, v_conv_b, v_dt_bias, v_a_log, v_d_skip, v_ssd_norm_g,
                    v_pool_scale, v_ln1_g, v_ln1_b, v_ln2_g, v_ln2_b)
    s_g, s_d, s_m, s_v, loss8 = _small_update_call(pall, dsil_all, _row(c_ctx), sw, sm, sv, et_mat)

    def small_unpack(t):
        return (t[S_CCTX], t[S_EMBG], t[S_EMBB], t[S_BADA:S_BADA + 6].reshape(1, 6 * D),
                t[S_CONVB:S_CONVB + 2].reshape(-1)[:DXBC].reshape(1, DXBC),
                t[S_DTB, 0:2 * NH].reshape(1, 2, NH), t[S_ALOG, 0:2 * NH].reshape(1, 2, NH), t[S_DSK, 0:NH].reshape(1, NH),
                t[S_SSDG].reshape(1, D), t[S_PSC].reshape(1, D), t[S_LN1G].reshape(1, D), t[S_LN1B].reshape(1, D),
                t[S_LN2G].reshape(1, D), t[S_LN2B].reshape(1, D))

    gs_inp, gs_cw = _exchange([_by_cols(_unaligned_in_proj(dwin)).astype(BF16), _by_cols(cacc[0:5])],
                              "exchange_last_grads", False)

    pshape = (4 * PG // NDEV, PG)
    u_inp = _adamw_shard_call(gs_inp, in_proj[0], m_in_proj[0], v_in_proj[0], 256, "adamw_in_proj")
    u_cw = _adamw_shard_call(gs_cw, conv_w[0], m_conv_w[0], v_conv_w[0], 5, "adamw_conv_w")
    u_pw = _adamw_shard_call(gs_pw, pool_w[0].reshape(pshape), m_pool_w[0].reshape(pshape), v_pool_w[0].reshape(pshape),
                             pshape[0], "adamw_pool_w")
    u_wo = _adamw_shard_call(gs_wo, w_out[0], m_w_out[0], v_w_out[0], 64, "adamw_w_out")
    u_wg = _adamw_shard_call(gs_wg, w_gate[0], m_w_gate[0], v_w_gate[0], 256, "adamw_w_gate")
    u_wu = _adamw_shard_call(gs_wu, w_up[0], m_w_up[0], v_w_up[0], 256, "adamw_w_up")
    u_wd = _adamw_shard_call(gs_wd, w_down[0], m_w_down[0], v_w_down[0], 88, "adamw_w_down")

    def assemble(k, small, wada):
        (cc, eg, eb, ba, cb_, dtb, al, dsk, sg, ps, l1g, l1b, l2g, l2b) = small_unpack(small)
        pw = u_pw[k].reshape(1, 4, PG // NDEV, PG)
        return (cc, eg, eb, wada[None], ba, u_inp[k][None], u_cw[k][None], cb_, dtb, al, dsk, sg, pw, ps,
                u_wo[k][None], l1g, l1b, u_wg[k][None], u_wu[k][None], u_wd[k][None], l2g, l2b)

    loss = loss8[0, 0]
    return (loss, grad_x[None], *assemble(0, s_g, g_wada), *assemble(1, s_d, d_wada),
            *assemble(2, s_m, nm_wada), *assemble(3, s_v, nv_wada))
```

```python
import functools
import math

import jax
import jax.numpy as jnp
from jax import lax
from jax.experimental import pallas as pl
from jax.experimental.pallas import tpu as pltpu

F32 = jnp.float32
BF16 = jnp.bfloat16
HI = lax.Precision.HIGHEST

NDEV = 8
D = 1024
NH = 16
HP = 64
NS = 128
Q = 128
DXBC = 1280
DFF = 2816
FFC = 1408
GW = 64
PR = 8
PT = PR * GW
WINDOWS = (2, 4, 8, 16)
PG = 256
DIN = 3360
WIN = 3584
ALPHA = 2.0 ** 0.25
LN_EPS = 1e-5
TM = 256

ADAM_LR = 0.001
ADAM_B1 = 0.9
ADAM_B2 = 0.999
ADAM_EPS = 1e-08
ADAM_WD = 0.01
ADAM_STEP = 10

V_SH1, V_SC1, V_G1, V_SH2, V_SC2, V_G2 = 0, 1, 2, 3, 4, 5
V_EMBG, V_EMBB, V_LN1G, V_LN1B, V_LN2G, V_LN2B = 6, 7, 8, 9, 10, 11
V_SSDG, V_PSC, V_DSK, V_LOSS = 12, 13, 14, 15
NV = 16

VMEM_LIMIT = 60 * 1024 * 1024


def _cp(ndim=1):
    return pltpu.CompilerParams(dimension_semantics=("arbitrary",) * ndim, vmem_limit_bytes=VMEM_LIMIT)


def _dot(a, b, precision=None):
    return jnp.dot(a, b, preferred_element_type=F32, precision=precision)


def _dot_nt(a, b):
    return lax.dot_general(a, b, (((1,), (1,)), ((), ())), preferred_element_type=F32)


def _dot_tn(a, b, precision=None):
    return lax.dot_general(a, b, (((0,), (0,)), ((), ())), preferred_element_type=F32, precision=precision)


def _split2(x):
    hi = x.astype(BF16)
    return hi, (x - hi.astype(F32)).astype(BF16)


def _split_dot(m, x):
    hi, lo = _split2(x)
    return _dot(m, hi) + _dot(m, lo)


def _dot_split(x, m):
    hi, lo = _split2(x)
    return _dot(hi, m) + _dot(lo, m)


def _sigmoid(x):
    return 1.0 / (1.0 + jnp.exp(-x))


def _softplus(x):
    return jnp.maximum(x, 0.0) + jnp.log(1.0 + jnp.exp(-jnp.abs(x)))


def _ln(x, g, b):
    mu = jnp.mean(x, axis=-1, keepdims=True)
    xc = x - mu
    var = jnp.mean(xc * xc, axis=-1, keepdims=True)
    rstd = lax.rsqrt(var + LN_EPS)
    n = xc * rstd
    return n * g + b, n, rstd


def _ln_bwd(dy, n, rstd, g):
    dn = dy * g
    return rstd * (dn - jnp.mean(dn, axis=-1, keepdims=True) - n * jnp.mean(dn * n, axis=-1, keepdims=True))


def _rowsum(x):
    return jnp.sum(x, axis=0, keepdims=True)


def _resident(shape):
    nd = len(shape)
    return pl.BlockSpec(shape, lambda *_: (0,) * nd, pipeline_mode=pl.Buffered(1))


def _const_out(shape):
    nd = len(shape)
    return pl.BlockSpec(shape, lambda *_: (0,) * nd)


def _tiles(tm, width):
    return pl.BlockSpec((tm, width), lambda i: (i, 0))


def _halo_specs(tm, width, n_rows):
    r = tm // 8
    last = n_rows // 8 - 1
    prev = pl.BlockSpec((8, width), lambda i: (jnp.maximum(i * r - 1, 0), 0))
    nxt = pl.BlockSpec((8, width), lambda i: (jnp.minimum((i + 1) * r, last), 0))
    return prev, nxt


def _acc_tn(acc_ref, a, b, chunk=512):
    n = b.shape[1]
    for c0 in range(0, n, chunk):
        c1 = min(c0 + chunk, n)
        acc_ref[:, c0:c1] += _dot_tn(a, b[:, c0:c1])


def _my_coords():
    return lax.axis_index("x"), lax.axis_index("y"), lax.axis_index("c")


def _peer(k, mx, my, mc):
    kx, ky, kc = (k >> 2) & 1, (k >> 1) & 1, k & 1
    px = 1 - mx if kx else mx
    py = 1 - my if ky else my
    pc = 1 - mc if kc else mc
    return px, py, pc


class _Exchange:
    def __init__(self, srcs, dsts, send_sems, recv_sems, local_sems, gather):
        self.srcs, self.dsts, self.gather = srcs, dsts, gather
        self.send_sems, self.recv_sems, self.local_sems = send_sems, recv_sems, local_sems

    def _copies(self, outgoing):
        mx, my, mc = _my_coords()
        me = 4 * mx + 2 * my + mc
        local, remote = [], []
        for t, (src, dst) in enumerate(zip(self.srcs, self.dsts)):
            local.append(pltpu.make_async_copy(src if self.gather else src.at[me], dst.at[me], self.local_sems.at[t]))
            for k in range(1, NDEV):
                px, py, pc = _peer(k, mx, my, mc)
                pid = 4 * px + 2 * py + pc
                remote.append(pltpu.make_async_remote_copy(
                    src_ref=src if self.gather else src.at[pid],
                    dst_ref=dst.at[me] if outgoing else dst.at[pid],
                    send_sem=self.send_sems.at[t, k - 1],
                    recv_sem=self.recv_sems.at[t, k - 1],
                    device_id=(px, py, pc),
                    device_id_type=pl.DeviceIdType.MESH,
                ))
        return local, remote

    def start(self):
        local, remote = self._copies(True)
        for cp in local + remote:
            cp.start()

    def wait(self):
        local, sends = self._copies(True)
        _, recvs = self._copies(False)
        for cp in recvs:
            cp.wait_recv()
        for cp in sends:
            cp.wait_send()
        for cp in local:
            cp.wait()


def _exchange_sems(n):
    return [pltpu.SemaphoreType.DMA((n, NDEV - 1)), pltpu.SemaphoreType.DMA((n, NDEV - 1)), pltpu.SemaphoreType.DMA((n,))]


def _exchange_out_shapes(xs, gather):
    return [jax.ShapeDtypeStruct(x.shape if not gather else (NDEV,) + x.shape, x.dtype) for x in xs]


def _exchange(xs, name, gather):
    n = len(xs)

    def body(*refs):
        ex = _Exchange(refs[:n], refs[n:2 * n], *refs[2 * n:], gather)
        ex.start()
        ex.wait()

    any_spec = pl.BlockSpec(memory_space=pl.ANY)
    return pl.pallas_call(
        body,
        name=name,
        out_shape=_exchange_out_shapes(xs, gather),
        in_specs=[any_spec] * n,
        out_specs=[any_spec] * n,
        scratch_shapes=_exchange_sems(n),
    )(*xs)


def _mod_call(c_all, w_ada, b_ada):
    ncol = w_ada.shape[1]

    def body(c_ref, w_ref, b_ref, silu_ref, mod_ref):
        cv = c_ref[...]
        s = cv * _sigmoid(cv)
        silu_ref[...] = s
        mod_ref[...] = _dot(s.astype(BF16), w_ref[...].astype(BF16)) + b_ref[...]

    return pl.pallas_call(
        body,
        name="mod_fwd",
        out_shape=(jax.ShapeDtypeStruct((16, D), F32), jax.ShapeDtypeStruct((16, ncol), F32)),
    )(c_all, w_ada, b_ada)


def _emb_ln_call(x0, gb, name, gathered=()):
    L = x0.shape[0]
    nt = L // TM
    ng = len(gathered)

    def body(*refs):
        x_ref, gb_ref = refs[:2]
        xe_ref = refs[2 + ng]
        i = pl.program_id(0)
        if ng:
            ex = _Exchange(refs[2:2 + ng], refs[3 + ng:3 + 2 * ng], *refs[3 + 2 * ng:], True)

            @pl.when(i == 0)
            def _():
                ex.start()

        xe_ref[...] = _ln(x_ref[...], gb_ref[0:1, :], gb_ref[1:2, :])[0]
        if ng:
            @pl.when(i == nt - 1)
            def _():
                ex.wait()

    any_spec = pl.BlockSpec(memory_space=pl.ANY)
    return pl.pallas_call(
        body,
        name=name,
        grid=(nt,),
        in_specs=[_tiles(TM, D), _resident((8, D))] + [any_spec] * ng,
        out_specs=[_tiles(TM, D)] + [any_spec] * ng,
        out_shape=[jax.ShapeDtypeStruct((L, D), F32)] + _exchange_out_shapes(gathered, True),
        scratch_shapes=_exchange_sems(ng) if ng else [],
        compiler_params=_cp(),
    )(x0, gb, *gathered)


def _f1_call(xe, vec, w_in, name):
    L = xe.shape[0]

    def body(xe_ref, vec_ref, w_ref, h1_ref, z_ref, xbc_ref, dt_ref, up_ref):
        h1 = (xe_ref[...] * (1.0 + vec_ref[V_SC1:V_SC1 + 1, :]) + vec_ref[V_SH1:V_SH1 + 1, :]).astype(BF16)
        proj = _dot(h1, w_ref[...])
        h1_ref[...] = h1
        z_ref[...] = proj[:, 0:1024]
        xbc_ref[...] = proj[:, 1024:2304]
        dt_ref[...] = proj[:, 2304:2560]
        up_ref[...] = proj[:, 2560:3584]

    return pl.pallas_call(
        body,
        name=name,
        grid=(L // TM,),
        in_specs=[_tiles(TM, D), _resident((NV, D)), _resident((D, WIN))],
        out_specs=[_tiles(TM, D), _tiles(TM, D), _tiles(TM, DXBC), _tiles(TM, 256), _tiles(TM, D)],
        out_shape=(
            jax.ShapeDtypeStruct((L, D), BF16),
            jax.ShapeDtypeStruct((L, D), F32),
            jax.ShapeDtypeStruct((L, DXBC), F32),
            jax.ShapeDtypeStruct((L, 256), F32),
            jax.ShapeDtypeStruct((L, D), F32),
        ),
        compiler_params=_cp(),
    )(xe, vec, w_in)


def _extended(ext, cur_ref, prev_ref, next_ref):
    i = pl.program_id(0)
    n = pl.num_programs(0)
    tm = cur_ref.shape[0]
    ext[0:8, :] = jnp.where(i > 0, prev_ref[...], 0.0)
    ext[8:8 + tm, :] = cur_ref[...]
    ext[8 + tm:16 + tm, :] = jnp.where(i < n - 1, next_ref[...], 0.0)
    return ext


def _shifted(ext, offset, tm):
    return ext[8 + offset:8 + offset + tm, :]


def _conv_pre(ext, cw_ref, tm):
    acc = cw_ref[5:6, :] + cw_ref[0:1, :] * _shifted(ext, -2, tm)
    for k in range(1, 5):
        acc = acc + cw_ref[k:k + 1, :] * _shifted(ext, k - 2, tm)
    return acc


def _f2_call(xbc_raw, cw, name):
    L = xbc_raw.shape[0]
    prev, nxt = _halo_specs(TM, DXBC, L)

    def body(cur_ref, prev_ref, next_ref, cw_ref, out_ref, ext):
        pre = _conv_pre(_extended(ext, cur_ref, prev_ref, next_ref), cw_ref, TM)
        out_ref[...] = pre * _sigmoid(pre)

    return pl.pallas_call(
        body,
        name=name,
        grid=(L // TM,),
        in_specs=[_tiles(TM, DXBC), prev, nxt, _resident((8, DXBC))],
        out_specs=_tiles(TM, DXBC),
        out_shape=jax.ShapeDtypeStruct((L, DXBC), F32),
        scratch_shapes=[pltpu.VMEM((TM + 16, DXBC), F32)],
        compiler_params=_cp(),
    )(xbc_raw, xbc_raw, xbc_raw, cw)


def _ssd_common(d, dtr, par_ref):
    lane = lax.broadcasted_iota(jnp.int32, (1, 128), 1)
    hmask = lane < NH
    bias = par_ref[d:d + 1, :]
    alog = par_ref[2 + d:3 + d, :]
    aneg = jnp.where(hmask, -jnp.exp(alog), 0.0)
    pre = dtr + bias
    dt = jnp.where(hmask, _softplus(pre), 0.0)
    a = dt * aneg
    row = lax.broadcasted_iota(jnp.int32, (Q, Q), 0)
    col = lax.broadcasted_iota(jnp.int32, (Q, Q), 1)
    maskf = ((row >= col) if d == 0 else (row <= col)).astype(F32)
    A = _dot(maskf, a, HI)
    atot = _rowsum(a)
    atot_col = jnp.sum(a.T, axis=1, keepdims=True)
    return dict(hmask=hmask, aneg=aneg, pre=pre, dt=dt, a=a, maskf=maskf, A=A, AT=A.T, dtT=dt.T,
                atot=atot, atot_col=atot_col, lane=lane)


def _column(v, lane, h):
    return jnp.sum(jnp.where(lane == h, v, 0.0), axis=1, keepdims=True)


def _head_expand(e_ref, v):
    hi, lo = _split2(v)
    return _dot(jnp.concatenate([hi, lo], axis=1), e_ref[...])


def _head_sum(et_ref, v):
    return _dot(v.astype(BF16), et_ref[:, 0:128])


def _state_decay(et_ref, atot_col):
    hi, lo = _split2(jnp.broadcast_to(jnp.exp(atot_col), (128, 128)))
    return _dot(et_ref[...], jnp.concatenate([hi, lo], axis=0))


def _ssd_fwd_call(xbc, dtr, h0, par, e_mat, et_mat, name, gathered=()):
    L = xbc.shape[0]
    nc = L // Q
    ng = len(gathered)

    def body(*refs):
        xbc_refs, dtr_refs = refs[0:2], refs[2:4]
        h0_ref, par_ref, e_ref, et_ref = refs[4:8]
        y_refs, hp_refs = refs[8 + ng:10 + ng], refs[10 + ng:12 + ng]
        hf_ref = refs[12 + ng]
        hs, AT, dtT = refs[13 + 2 * ng:16 + 2 * ng]
        s = pl.program_id(0)
        if ng:
            ex = _Exchange(refs[8:8 + ng], refs[13 + ng:13 + 2 * ng], *refs[16 + 2 * ng:], True)

        @pl.when(s == 0)
        def _():
            if ng:
                ex.start()
            hs[...] = h0_ref[...]

        for d in range(2):
            xbc_ref, y_ref = xbc_refs[d], y_refs[d]
            q = _ssd_common(d, dtr_refs[d][...], par_ref)
            A, maskf, lane = q["A"], q["maskf"], q["lane"]
            AT[d] = q["AT"]
            dtT[d] = q["dtT"]
            hprev = hs[d]
            hp_refs[d][0] = hprev
            bb = xbc_ref[:, 1024:1152].astype(BF16)
            cb = xbc_ref[:, 1152:1280].astype(BF16)
            g = _dot_nt(cb, bb)
            yoff = _dot_nt(cb, hprev.astype(BF16)) * _head_expand(e_ref, jnp.exp(A))
            for k in range(NH // 2):
                ks = slice(128 * k, 128 * k + 128)
                xp = xbc_ref[:, ks]
                scs, xhs = [], []
                for half in range(2):
                    h = 2 * k + half
                    seg = _column(A, lane, h) - AT[d, h:h + 1, :]
                    lm = jnp.exp(jnp.minimum(seg, 0.0)) * maskf
                    scs.append((g * lm * dtT[d, h:h + 1, :]).astype(BF16))
                    inhead = (lane >= HP) if half else (lane < HP)
                    xhs.append(jnp.where(inhead, xp, 0.0).astype(BF16))
                y_ref[:, ks] = yoff[:, ks] + _dot(jnp.concatenate(scs, axis=1), jnp.concatenate(xhs, axis=0))
            wend = jnp.exp(q["atot"] - A) * q["dt"]
            xw = (xbc_ref[:, 0:1024] * _head_expand(e_ref, wend)).astype(BF16)
            hnew = hprev * _state_decay(et_ref, q["atot_col"]) + _dot_tn(xw, bb)
            hs[d] = hnew
            hf_ref[d] = hnew
        if ng:
            @pl.when(s == nc - 1)
            def _():
                ex.wait()

    any_spec = pl.BlockSpec(memory_space=pl.ANY)
    return pl.pallas_call(
        body,
        name=name,
        grid=(nc,),
        in_specs=[
            pl.BlockSpec((Q, DXBC), lambda s: (s, 0)),
            pl.BlockSpec((Q, DXBC), lambda s: (nc - 1 - s, 0)),
            pl.BlockSpec((Q, 128), lambda s: (s, 0)),
            pl.BlockSpec((Q, 128), lambda s: (nc - 1 - s, 1)),
            _resident((2, D, NS)),
            _resident((8, 128)),
            _resident((256, D)),
            _resident((D, 256)),
        ] + [any_spec] * ng,
        out_specs=[
            pl.BlockSpec((Q, D), lambda s: (s, 0)),
            pl.BlockSpec((Q, D), lambda s: (nc - 1 - s, 0)),
            pl.BlockSpec((1, D, NS), lambda s: (s, 0, 0)),
            pl.BlockSpec((1, D, NS), lambda s: (nc - 1 - s, 0, 0)),
            _const_out((2, D, NS)),
        ] + [any_spec] * ng,
        out_shape=[
            jax.ShapeDtypeStruct((L, D), F32),
            jax.ShapeDtypeStruct((L, D), F32),
            jax.ShapeDtypeStruct((nc, D, NS), F32),
            jax.ShapeDtypeStruct((nc, D, NS), F32),
            jax.ShapeDtypeStruct((2, D, NS), F32),
        ] + _exchange_out_shapes(gathered, True),
        scratch_shapes=[pltpu.VMEM((2, D, NS), F32), pltpu.VMEM((2, 128, Q), F32), pltpu.VMEM((2, 128, Q), F32)]
        + (_exchange_sems(ng) if ng else []),
        compiler_params=_cp(),
    )(xbc, xbc, dtr, dtr, h0, par, e_mat, et_mat, *gathered)


def _halo_tokens(g):
    return (WINDOWS[g] // 2) * GW


def _pool_specs(n_tiles):
    cur = pl.BlockSpec((PT, D), lambda i: (i, 0))
    prev = pl.BlockSpec((PT, D), lambda i: (jnp.maximum(i - 1, 0), 0))
    nxt = pl.BlockSpec((PT, D), lambda i: (jnp.minimum(i + 1, n_tiles - 1), 0))
    return cur, prev, nxt


def _pool_fwd_call(up, pmat, icnt, pool_w):
    L = up.shape[0]
    nt = L // PT
    cur, prev, nxt = _pool_specs(nt)

    def body(cur_ref, prev_ref, next_ref, m0_ref, m1_ref, m2_ref, m3_ref, ic_ref, pw_ref, d_ref, pun_ref):
        i = pl.program_id(0)
        n = pl.num_programs(0)
        lane = lax.broadcasted_iota(jnp.int32, (1, 128), 1)
        icv = ic_ref[...]
        for g, m_ref in enumerate((m0_ref, m1_ref, m2_ref, m3_ref)):
            gs = slice(PG * g, PG * g + PG)
            halo = _halo_tokens(g)
            top = jnp.where(i > 0, prev_ref[PT - halo:PT, gs], 0.0)
            bot = jnp.where(i < n - 1, next_ref[0:halo, gs], 0.0)
            mid = cur_ref[:, gs]
            box = _split_dot(m_ref[...], jnp.concatenate([top, mid, bot], axis=0))
            dg = (box * _column(icv, lane, g) - mid).astype(BF16)
            d_ref[:, gs] = dg
            pun_ref[:, gs] = _dot(dg, pw_ref[g])

    return pl.pallas_call(
        body,
        name="pool_fwd",
        grid=(nt,),
        in_specs=[cur, prev, nxt] + [_resident(m.shape) for m in pmat] + [_tiles(PT, 128), _resident((4, PG, PG))],
        out_specs=[_tiles(PT, D), _tiles(PT, D)],
        out_shape=(jax.ShapeDtypeStruct((L, D), BF16), jax.ShapeDtypeStruct((L, D), F32)),
        compiler_params=_cp(),
    )(up, up, up, *pmat, icnt, pool_w)


def _gated(yf_ref, yb_ref, xs, z, vec_ref):
    ym = yf_ref[...] + yb_ref[...] + vec_ref[V_DSK:V_DSK + 1, :] * xs
    sz = _sigmoid(z)
    gated = ym * (z * sz)
    r = lax.rsqrt(jnp.mean(gated * gated, axis=-1, keepdims=True) + LN_EPS)
    return ym, sz, gated, r


def _merge_call(yf, yb, xbc, z, pun, xe, vec, w_out):
    L = z.shape[0]

    def body(yf_ref, yb_ref, xs_ref, z_ref, pun_ref, xe_ref, vec_ref, w_ref, x1_ref, mix_ref, cat_ref):
        _, _, gated, r = _gated(yf_ref, yb_ref, xs_ref[...], z_ref[...], vec_ref)
        yn = gated * r * vec_ref[V_SSDG:V_SSDG + 1, :]
        p = pun_ref[...] * vec_ref[V_PSC:V_PSC + 1, :]
        cat = jnp.concatenate([yn, p], axis=1).astype(BF16)
        mix = _dot(cat, w_ref[...])
        pre1 = ALPHA * xe_ref[...] + vec_ref[V_G1:V_G1 + 1, :] * mix
        x1, _, _ = _ln(pre1, vec_ref[V_LN1G:V_LN1G + 1, :], vec_ref[V_LN1B:V_LN1B + 1, :])
        x1_ref[...] = x1
        mix_ref[...] = mix
        cat_ref[...] = cat

    return pl.pallas_call(
        body,
        name="merge_fwd",
        grid=(L // TM,),
        in_specs=[
            _tiles(TM, D), _tiles(TM, D), _tiles(TM, D), _tiles(TM, D), _tiles(TM, D), _tiles(TM, D),
            _resident((NV, D)), _resident((2 * D, D)),
        ],
        out_specs=[_tiles(TM, D), _tiles(TM, D), _tiles(TM, 2 * D)],
        out_shape=(
            jax.ShapeDtypeStruct((L, D), F32),
            jax.ShapeDtypeStruct((L, D), F32),
            jax.ShapeDtypeStruct((L, 2 * D), BF16),
        ),
        compiler_params=_cp(),
    )(yf, yb, xbc, z, pun, xe, vec, w_out)


def _ffn_fwd_call(x1, tgt, vec, w_gate, w_up, w_down):
    L = x1.shape[0]

    def body(x1_ref, tgt_ref, vec_ref, wg_ref, wu_ref, wd_ref, dpre_ref, gacc_ref, gt_ref, up_ref):
        @pl.when(pl.program_id(0) == 0)
        def _():
            gacc_ref[...] = jnp.zeros_like(gacc_ref)

        x1 = x1_ref[...]
        h2 = (x1 * (1.0 + vec_ref[V_SC2:V_SC2 + 1, :]) + vec_ref[V_SH2:V_SH2 + 1, :]).astype(BF16)
        gt = _dot(h2, wg_ref[...])
        up = _dot(h2, wu_ref[...])
        gt_ref[...] = gt.astype(BF16)
        up_ref[...] = up.astype(BF16)
        f = (gt * _sigmoid(gt) * up).astype(BF16)
        ffn = _dot(f, wd_ref[...])
        g2 = vec_ref[V_G2:V_G2 + 1, :]
        lng = vec_ref[V_LN2G:V_LN2G + 1, :]
        x2, n2, rstd2 = _ln(ALPHA * x1 + g2 * ffn, lng, vec_ref[V_LN2B:V_LN2B + 1, :])
        diff = x2 - tgt_ref[...]
        dx2 = diff * (1.0 / D)
        dpre2 = _ln_bwd(dx2, n2, rstd2, lng)
        dpre_ref[...] = dpre2
        gacc_ref[V_LN2G:V_LN2G + 1, :] += _rowsum(dx2 * n2)
        gacc_ref[V_LN2B:V_LN2B + 1, :] += _rowsum(dx2)
        gacc_ref[V_G2:V_G2 + 1, :] += _rowsum(dpre2 * ffn)
        gacc_ref[V_LOSS:V_LOSS + 1, :] += jnp.sum(diff * diff) * (0.5 / D)

    return pl.pallas_call(
        body,
        name="ffn_fwd",
        grid=(L // TM,),
        in_specs=[_tiles(TM, D), _tiles(TM, D), _resident((NV, D)),
                  _resident((D, DFF)), _resident((D, DFF)), _resident((DFF, D))],
        out_specs=[_tiles(TM, D), _const_out((NV, D)), _tiles(TM, DFF), _tiles(TM, DFF)],
        out_shape=(jax.ShapeDtypeStruct((L, D), F32), jax.ShapeDtypeStruct((NV, D), F32),
                   jax.ShapeDtypeStruct((L, DFF), BF16), jax.ShapeDtypeStruct((L, DFF), BF16)),
        compiler_params=_cp(),
    )(x1, tgt, vec, w_gate, w_up, w_down)


def _ffn_bwd_call(x1, dpre2, gt_b, up_b, vec, w_gate, w_up, w_down):
    L = x1.shape[0]
    nt = L // TM
    nj = DFF // FFC

    def body(x1_ref, dpre_ref, gt_ref, up_ref, vec_ref, wg_ref, wu_ref, wd_ref, dh2_ref, dwg_ref, dwu_ref, dwd_ref,
             ag, au, ad):
        j = pl.program_id(0)
        i = pl.program_id(1)

        @pl.when(i == 0)
        def _():
            ag[...] = jnp.zeros_like(ag)
            au[...] = jnp.zeros_like(au)
            ad[...] = jnp.zeros_like(ad)

        h2 = (x1_ref[...] * (1.0 + vec_ref[V_SC2:V_SC2 + 1, :]) + vec_ref[V_SH2:V_SH2 + 1, :]).astype(BF16)
        gt = gt_ref[...].astype(F32)
        up = up_ref[...].astype(F32)
        sg = _sigmoid(gt)
        sl = gt * sg
        f = (sl * up).astype(BF16)
        dffn = (vec_ref[V_G2:V_G2 + 1, :] * dpre_ref[...]).astype(BF16)
        df = _dot_nt(dffn, wd_ref[...])
        dgt = (df * up * (sg * (1.0 + gt * (1.0 - sg)))).astype(BF16)
        dup = (df * sl).astype(BF16)
        dh2_ref[0] = _dot_nt(dgt, wg_ref[...]) + _dot_nt(dup, wu_ref[...])
        _acc_tn(ag, h2, dgt)
        _acc_tn(au, h2, dup)
        _acc_tn(ad, f, dffn)

        @pl.when(i == nt - 1)
        def _():
            pltpu.sync_copy(ag, dwg_ref.at[j])
            pltpu.sync_copy(au, dwu_ref.at[j])
            pltpu.sync_copy(ad, dwd_ref.at[j])

    any_spec = pl.BlockSpec(memory_space=pl.ANY)
    return pl.pallas_call(
        body,
        name="ffn_bwd",
        grid=(nj, nt),
        in_specs=[
            pl.BlockSpec((TM, D), lambda j, i: (i, 0)),
            pl.BlockSpec((TM, D), lambda j, i: (i, 0)),
            pl.BlockSpec((TM, FFC), lambda j, i: (i, j)),
            pl.BlockSpec((TM, FFC), lambda j, i: (i, j)),
            _resident((NV, D)),
            pl.BlockSpec((D, FFC), lambda j, i: (0, j)),
            pl.BlockSpec((D, FFC), lambda j, i: (0, j)),
            pl.BlockSpec((FFC, D), lambda j, i: (j, 0)),
        ],
        out_specs=[pl.BlockSpec((1, TM, D), lambda j, i: (j, i, 0)), any_spec, any_spec, any_spec],
        out_shape=(
            jax.ShapeDtypeStruct((nj, L, D), F32),
            jax.ShapeDtypeStruct((nj, D, FFC), F32),
            jax.ShapeDtypeStruct((nj, D, FFC), F32),
            jax.ShapeDtypeStruct((nj, FFC, D), F32),
        ),
        scratch_shapes=[pltpu.VMEM((D, FFC), F32), pltpu.VMEM((D, FFC), F32), pltpu.VMEM((FFC, D), F32)],
        compiler_params=_cp(2),
    )(x1, dpre2, gt_b, up_b, vec, w_gate, w_up, w_down)


def _merge_bwd_call(dh2p, dpre2, xe, mix, cat, yf, yb, xbc, z, dpool, pun, vec, w_out, pool_w, scattered):
    L = z.shape[0]
    nt = L // TM
    ns = len(scattered)

    def body(*refs):
        (dh2_ref, dpre2_ref, xe_ref, mix_ref, cat_ref, yf_ref, yb_ref, xs_ref, z_ref, dpool_ref, pun_ref,
         vec_ref, w_ref, pw_ref) = refs[:14]
        dxe_ref, dy_ref, dz_ref, dd_ref, dpw_ref, gacc_ref, dwo_ref = refs[14 + ns:21 + ns]
        dwo_s = refs[21 + 2 * ns]
        ex = _Exchange(refs[14:14 + ns], refs[21 + ns:21 + 2 * ns], *refs[22 + 2 * ns:], False)
        i = pl.program_id(0)

        @pl.when(i == 0)
        def _():
            ex.start()
            gacc_ref[...] = jnp.zeros_like(gacc_ref)
            dpw_ref[...] = jnp.zeros_like(dpw_ref)
            dwo_s[...] = jnp.zeros_like(dwo_s)

        def vrow(r):
            return vec_ref[r:r + 1, :]

        def gadd(r, val):
            gacc_ref[r:r + 1, :] += _rowsum(val)

        dh2 = dh2_ref[0] + dh2_ref[1]
        dx1 = ALPHA * dpre2_ref[...] + dh2 * (1.0 + vrow(V_SC2))
        mix = mix_ref[...]
        x1, n1, rstd1 = _ln(ALPHA * xe_ref[...] + vrow(V_G1) * mix, vrow(V_LN1G), vrow(V_LN1B))
        gadd(V_SC2, dh2 * x1)
        gadd(V_SH2, dh2)
        gadd(V_LN1G, dx1 * n1)
        gadd(V_LN1B, dx1)
        dpre1 = _ln_bwd(dx1, n1, rstd1, vrow(V_LN1G))
        dxe_ref[...] = ALPHA * dpre1
        gadd(V_G1, dpre1 * mix)
        dmix = (vrow(V_G1) * dpre1).astype(BF16)
        dcat = _dot_nt(dmix, w_ref[...])
        cat = cat_ref[...]
        for c0 in range(0, 2 * D, 512):
            dwo_s[c0:c0 + 512, :] += _dot_tn(cat[:, c0:c0 + 512], dmix)
        dyn = dcat[:, 0:D]
        dp = dcat[:, D:2 * D]
        xs = xs_ref[...]
        z = z_ref[...]
        ym, sz, gated, r = _gated(yf_ref, yb_ref, xs, z, vec_ref)
        gadd(V_SSDG, dyn * gated * r)
        a = dyn * vrow(V_SSDG)
        dgated = r * a - gated * (r * r * r * jnp.mean(a * gated, axis=-1, keepdims=True))
        dym = dgated * (z * sz)
        dy_ref[...] = dym
        dz_ref[...] = dgated * ym * (sz * (1.0 + z * (1.0 - sz)))
        gadd(V_DSK, dym * xs)
        gadd(V_PSC, dp * pun_ref[...])
        dps = (dp * vrow(V_PSC)).astype(BF16)
        dpool = dpool_ref[...]
        for g in range(4):
            gs = slice(PG * g, PG * g + PG)
            dd_ref[:, gs] = _dot_nt(dps[:, gs], pw_ref[g])
            dpw_ref[g] += _dot_tn(dpool[:, gs], dps[:, gs])

        @pl.when(i == nt - 1)
        def _():
            pltpu.sync_copy(dwo_s, dwo_ref)
            ex.wait()

    any_spec = pl.BlockSpec(memory_space=pl.ANY)
    return pl.pallas_call(
        body,
        name="merge_bwd",
        grid=(nt,),
        in_specs=[
            pl.BlockSpec((2, TM, D), lambda i: (0, i, 0)),
            _tiles(TM, D), _tiles(TM, D), _tiles(TM, D), _tiles(TM, 2 * D),
            _tiles(TM, D), _tiles(TM, D),
            _tiles(TM, D), _tiles(TM, D), _tiles(TM, D), _tiles(TM, D),
            _resident((NV, D)), _resident((2 * D, D)), _resident((4, PG, PG)),
        ] + [any_spec] * ns,
        out_specs=[_tiles(TM, D), _tiles(TM, D), _tiles(TM, D), _tiles(TM, D),
                   _const_out((4, PG, PG)), _const_out((NV, D)), any_spec] + [any_spec] * ns,
        out_shape=[
            jax.ShapeDtypeStruct((L, D), F32),
            jax.ShapeDtypeStruct((L, D), F32),
            jax.ShapeDtypeStruct((L, D), F32),
            jax.ShapeDtypeStruct((L, D), F32),
            jax.ShapeDtypeStruct((4, PG, PG), F32),
            jax.ShapeDtypeStruct((NV, D), F32),
            jax.ShapeDtypeStruct((2 * D, D), F32),
        ] + _exchange_out_shapes(scattered, False),
        scratch_shapes=[pltpu.VMEM((2 * D, D), F32)] + _exchange_sems(ns),
        compiler_params=_cp(),
    )(dh2p, dpre2, xe, mix, cat, yf, yb, xbc, z, dpool, pun, vec, w_out, pool_w, *scattered)


def _pool_bwd_call(dd, pmat_t, icnt):
    L = dd.shape[0]
    nt = L // PT
    cur, prev, nxt = _pool_specs(nt)
    icur = pl.BlockSpec((PT, 128), lambda i: (i, 0))
    iprev = pl.BlockSpec((PT, 128), lambda i: (jnp.maximum(i - 1, 0), 0))
    inxt = pl.BlockSpec((PT, 128), lambda i: (jnp.minimum(i + 1, nt - 1), 0))

    def body(cur_ref, prev_ref, next_ref, ic_ref, icp_ref, icn_ref, m0_ref, m1_ref, m2_ref, m3_ref, du_ref):
        i = pl.program_id(0)
        n = pl.num_programs(0)
        lane = lax.broadcasted_iota(jnp.int32, (1, 128), 1)
        icv = ic_ref[...]
        for g, m_ref in enumerate((m0_ref, m1_ref, m2_ref, m3_ref)):
            gs = slice(PG * g, PG * g + PG)
            halo = _halo_tokens(g)
            icp = _column(icp_ref[PT - halo:PT, :], lane, g)
            icn = _column(icn_ref[0:halo, :], lane, g)
            top = jnp.where(i > 0, prev_ref[PT - halo:PT, gs] * icp, 0.0)
            bot = jnp.where(i < n - 1, next_ref[0:halo, gs] * icn, 0.0)
            mid = cur_ref[:, gs]
            ext = jnp.concatenate([top, mid * _column(icv, lane, g), bot], axis=0)
            du_ref[:, gs] = _split_dot(m_ref[...], ext) - mid

    return pl.pallas_call(
        body,
        name="pool_bwd",
        grid=(nt,),
        in_specs=[cur, prev, nxt, icur, iprev, inxt] + [_resident(m.shape) for m in pmat_t],
        out_specs=_tiles(PT, D),
        out_shape=jax.ShapeDtypeStruct((L, D), F32),
        compiler_params=_cp(),
    )(dd, dd, dd, icnt, icnt, icnt, *pmat_t)


def _ssd_bwd_call(dy, xbc, dtr, hprev_f, hprev_b, dh_init, par, e_mat, et_mat, dskip, name):
    L = xbc.shape[0]
    nc = L // Q

    def body(dy0_ref, dy1_ref, xbc0_ref, xbc1_ref, dtr0_ref, dtr1_ref, hp0_ref, hp1_ref, dhi_ref, par_ref, e_ref,
             et_ref, dsk_ref, dx0_ref, dx1_ref, ddt0_ref, ddt1_ref, acc_ref, dh0_ref, dh_s, AT, dtT, ddtT_s, dAT_s):
        s = pl.program_id(0)

        @pl.when(s == 0)
        def _():
            dh_s[...] = dhi_ref[...]
            acc_ref[...] = jnp.zeros_like(acc_ref)
            ddtT_s[...] = jnp.zeros_like(ddtT_s)
            dAT_s[...] = jnp.zeros_like(dAT_s)

        one_direction(0, dy0_ref, xbc0_ref, dtr0_ref, hp0_ref, par_ref, e_ref, et_ref, dsk_ref, dx0_ref, ddt0_ref,
                      acc_ref, dh0_ref, dh_s, AT, dtT, ddtT_s, dAT_s)
        one_direction(1, dy1_ref, xbc1_ref, dtr1_ref, hp1_ref, par_ref, e_ref, et_ref, dsk_ref, dx1_ref, ddt1_ref,
                      acc_ref, dh0_ref, dh_s, AT, dtT, ddtT_s, dAT_s)

    def one_direction(d, dy_ref, xbc_ref, dtr_ref, hp_ref, par_ref, e_ref, et_ref, dsk_ref, dxbc_ref, ddtr_ref,
                      acc_ref, dh0_ref, dh_s, AT_s, dtT_s, ddtT_s, dAT_s):
        q = _ssd_common(d, dtr_ref[...], par_ref)
        A, maskf, lane, dt, atot = q["A"], q["maskf"], q["lane"], q["dt"], q["atot"]
        AT_s[d] = q["AT"]
        dtT_s[d] = q["dtT"]
        AT, dtT = AT_s.at[d], dtT_s.at[d]
        hprev = hp_ref[0]
        hpb = hprev.astype(BF16)
        dh = dh_s[d]
        dhb = dh.astype(BF16)
        xs = xbc_ref[:, 0:1024]
        bb = xbc_ref[:, 1024:1152].astype(BF16)
        cb = xbc_ref[:, 1152:1280].astype(BF16)
        dy = dy_ref[...]
        ea_f = _head_expand(e_ref, jnp.exp(A))
        ch = _dot_nt(cb, hpb)
        dch = (dy * ea_f).astype(BF16)
        dC = _dot(dch, hpb)
        dhprev = _dot_tn(dch, cb)
        dA = _head_sum(et_ref, dy * ch * ea_f)
        dec = _state_decay(et_ref, q["atot_col"])
        dhprev = dhprev + dh * dec
        hs_ = jnp.sum(_dot(e_ref[0:128, :], (dh * hprev * dec).astype(BF16)), axis=1, keepdims=True)
        datot = jnp.max(jnp.broadcast_to(hs_, (128, 128)).T, axis=0, keepdims=True)
        ear = jnp.exp(atot - A)
        wend = ear * dt
        wf = _head_expand(e_ref, wend)
        xw = (xs * wf).astype(BF16)
        dxw = _dot_nt(bb, dhb)
        dB = _dot(xw, dhb)
        dxs = dxw * wf
        dwend = _head_sum(et_ref, dxw * xs)
        ddt = dwend * ear
        de = dwend * wend
        datot = datot + _rowsum(de)
        dA = dA - de
        g = _dot_nt(cb, bb)
        dG = jnp.zeros((Q, Q), F32)
        for k in range(NH // 2):
            ks = slice(128 * k, 128 * k + 128)
            xp = xs[:, ks]
            dyp = dy[:, ks]
            accdx = dxs[:, ks]
            if d == 0:
                accdx = accdx + dyp * dsk_ref[:, ks]
            scts, dyhs = [], []
            for half in range(2):
                h = 2 * k + half
                inhead = (lane >= HP) if half else (lane < HP)
                seg = _column(A, lane, h) - AT[h:h + 1, :]
                lm = jnp.exp(jnp.minimum(seg, 0.0)) * maskf
                dtrow = dtT[h:h + 1, :]
                gl = g * lm
                sc = gl * dtrow
                dyh = jnp.where(inhead, dyp, 0.0).astype(BF16)
                xh = jnp.where(inhead, xp, 0.0).astype(BF16)
                dS = _dot_nt(dyh, xh)
                scts.append(sc.T.astype(BF16))
                dyhs.append(dyh)
                nn = dS * gl
                cn = _rowsum(nn)
                rm = jnp.sum(nn * dtrow, axis=1, keepdims=True)
                dG = dG + dS * (lm * dtrow)
                ddtT_s[d, h:h + 1, :] = cn
                dAT_s[d, h:h + 1, :] = -(cn * dtrow)
                dA = dA + rm * (lane == h).astype(F32)
            dxbc_ref[:, ks] = accdx + _dot(jnp.concatenate(scts, axis=1), jnp.concatenate(dyhs, axis=0))
        dGb = dG.astype(BF16)
        dxbc_ref[:, 1024:1152] = dB + _dot_tn(dGb, cb)
        dxbc_ref[:, 1152:1280] = dC + _dot(dGb, bb)
        da = _dot_tn(maskf, dA + dAT_s[d].T, HI) + datot
        ddt = ddt + ddtT_s[d].T + da * q["aneg"]
        ddtr = jnp.where(q["hmask"], ddt * _sigmoid(q["pre"]), 0.0)
        ddtr_ref[...] = ddtr
        acc_ref[d, 0:1, :] += _rowsum(ddtr)
        acc_ref[d, 1:2, :] += _rowsum(da * dt) * q["aneg"]
        dh_s[d] = dhprev
        dh0_ref[d] = dhprev

    def back(s):
        return nc - 1 - s

    return pl.pallas_call(
        body,
        name=name,
        grid=(nc,),
        in_specs=[
            pl.BlockSpec((Q, D), lambda s: (back(s), 0)),
            pl.BlockSpec((Q, D), lambda s: (s, 0)),
            pl.BlockSpec((Q, DXBC), lambda s: (back(s), 0)),
            pl.BlockSpec((Q, DXBC), lambda s: (s, 0)),
            pl.BlockSpec((Q, 128), lambda s: (back(s), 0)),
            pl.BlockSpec((Q, 128), lambda s: (s, 1)),
            pl.BlockSpec((1, D, NS), lambda s: (back(s), 0, 0)),
            pl.BlockSpec((1, D, NS), lambda s: (s, 0, 0)),
            _resident((2, D, NS)),
            _resident((8, 128)),
            _resident((256, D)),
            _resident((D, 256)),
            _resident((1, D)),
        ],
        out_specs=[
            pl.BlockSpec((Q, DXBC), lambda s: (back(s), 0)),
            pl.BlockSpec((Q, DXBC), lambda s: (s, 0)),
            pl.BlockSpec((Q, 128), lambda s: (back(s), 0)),
            pl.BlockSpec((Q, 128), lambda s: (s, 0)),
            _const_out((2, 8, 128)),
            _const_out((2, D, NS)),
        ],
        out_shape=(
            jax.ShapeDtypeStruct((L, DXBC), F32),
            jax.ShapeDtypeStruct((L, DXBC), F32),
            jax.ShapeDtypeStruct((L, 128), F32),
            jax.ShapeDtypeStruct((L, 128), F32),
            jax.ShapeDtypeStruct((2, 8, 128), F32),
            jax.ShapeDtypeStruct((2, D, NS), F32),
        ),
        scratch_shapes=[pltpu.VMEM((2, D, NS), F32)] + [pltpu.VMEM((2, 128, Q), F32)] * 4,
        compiler_params=_cp(),
    )(dy, dy, xbc, xbc, dtr, dtr, hprev_f, hprev_b, dh_init, par, e_mat, et_mat, dskip)


def _conv_bwd_call(dxf, dxb, xbc_raw, cw, acc_init, name, scattered=()):
    L = xbc_raw.shape[0]
    nt = L // TM
    ns = len(scattered)
    prev, nxt = _halo_specs(TM, DXBC, L)

    def body(*refs):
        dxf_ref, dxb_ref, cur_ref, prev_ref, next_ref, cw_ref, init_ref = refs[:7]
        dpre_ref, acc_ref = refs[7 + ns:9 + ns]
        ext = refs[9 + 2 * ns]
        if ns:
            ex = _Exchange(refs[7:7 + ns], refs[9 + ns:9 + 2 * ns], *refs[10 + 2 * ns:], False)

        @pl.when(pl.program_id(0) == 0)
        def _():
            if ns:
                ex.start()
            acc_ref[...] = init_ref[...]

        _extended(ext, cur_ref, prev_ref, next_ref)
        pre = _conv_pre(ext, cw_ref, TM)
        sg = _sigmoid(pre)
        dpre = (dxf_ref[...] + dxb_ref[...]) * (sg * (1.0 + pre * (1.0 - sg)))
        dpre_ref[...] = dpre
        for k in range(5):
            acc_ref[k:k + 1, :] += _rowsum(dpre * _shifted(ext, k - 2, TM))
        acc_ref[5:6, :] += _rowsum(dpre)
        if ns:
            @pl.when(pl.program_id(0) == nt - 1)
            def _():
                ex.wait()

    any_spec = pl.BlockSpec(memory_space=pl.ANY)
    return pl.pallas_call(
        body,
        name=name,
        grid=(nt,),
        in_specs=[_tiles(TM, DXBC), _tiles(TM, DXBC), _tiles(TM, DXBC), prev, nxt,
                  _resident((8, DXBC)), _resident((8, DXBC))] + [any_spec] * ns,
        out_specs=[_tiles(TM, DXBC), _const_out((8, DXBC))] + [any_spec] * ns,
        out_shape=[jax.ShapeDtypeStruct((L, DXBC), F32), jax.ShapeDtypeStruct((8, DXBC), F32)]
        + _exchange_out_shapes(scattered, False),
        scratch_shapes=[pltpu.VMEM((TM + 16, DXBC), F32)] + (_exchange_sems(ns) if ns else []),
        compiler_params=_cp(),
    )(dxf, dxb, xbc_raw, xbc_raw, xbc_raw, cw, acc_init, *scattered)


def _inproj_bwd_call(dpre, cw, dz, ddt0, ddt1, dup, h1, dxe_part, x0, vec, w_in, dw_init, name):
    L = x0.shape[0]
    nt = L // TM
    prev, nxt = _halo_specs(TM, DXBC, L)

    def body(cur_ref, prev_ref, next_ref, cw_ref, dz_ref, ddt0_ref, ddt1_ref, dup_ref, h1_ref, dxe_ref, x0_ref,
             vec_ref, w_ref, dwi_ref, gx_ref, gacc_ref, dw_ref, dw_s, ext):
        i = pl.program_id(0)

        @pl.when(i == 0)
        def _():
            gacc_ref[...] = jnp.zeros_like(gacc_ref)
            pltpu.sync_copy(dwi_ref, dw_s)

        def vrow(r):
            return vec_ref[r:r + 1, :]

        _extended(ext, cur_ref, prev_ref, next_ref)
        dxr = cw_ref[0:1, :] * _shifted(ext, 2, TM)
        for k in range(1, 5):
            dxr = dxr + cw_ref[k:k + 1, :] * _shifted(ext, 2 - k, TM)
        dproj = jnp.concatenate([dz_ref[...], dxr, ddt0_ref[...], ddt1_ref[...], dup_ref[...]], axis=1).astype(BF16)
        dh1 = _dot_nt(dproj, w_ref[...])
        _acc_tn(dw_s, h1_ref[...], dproj)
        xe, n0, rstd0 = _ln(x0_ref[...], vrow(V_EMBG), vrow(V_EMBB))
        dxe = dxe_ref[...] + dh1 * (1.0 + vrow(V_SC1))
        gacc_ref[V_SC1:V_SC1 + 1, :] += _rowsum(dh1 * xe)
        gacc_ref[V_SH1:V_SH1 + 1, :] += _rowsum(dh1)
        gacc_ref[V_EMBG:V_EMBG + 1, :] += _rowsum(dxe * n0)
        gacc_ref[V_EMBB:V_EMBB + 1, :] += _rowsum(dxe)
        gx_ref[...] = _ln_bwd(dxe, n0, rstd0, vrow(V_EMBG))

        @pl.when(i == nt - 1)
        def _():
            pltpu.sync_copy(dw_s, dw_ref)

    any_spec = pl.BlockSpec(memory_space=pl.ANY)
    return pl.pallas_call(
        body,
        name=name,
        grid=(nt,),
        in_specs=[_tiles(TM, DXBC), prev, nxt, _resident((8, DXBC)), _tiles(TM, D), _tiles(TM, 128), _tiles(TM, 128),
                  _tiles(TM, D), _tiles(TM, D), _tiles(TM, D), _tiles(TM, D), _resident((NV, D)), _resident((D, WIN)),
                  any_spec],
        out_specs=[_tiles(TM, D), _const_out((NV, D)), any_spec],
        out_shape=(
            jax.ShapeDtypeStruct((L, D), F32),
            jax.ShapeDtypeStruct((NV, D), F32),
            jax.ShapeDtypeStruct((D, WIN), F32),
        ),
        scratch_shapes=[pltpu.VMEM((D, WIN), F32), pltpu.VMEM((TM + 16, DXBC), F32)],
        compiler_params=_cp(),
    )(dpre, dpre, dpre, cw, dz, ddt0, ddt1, dup, h1, dxe_part, x0, vec, w_in, dw_init)


def _adamw(w, g, m, v):
    m = ADAM_B1 * m + (1.0 - ADAM_B1) * g
    v = ADAM_B2 * v + (1.0 - ADAM_B2) * (g * g)
    m_hat = m / (1.0 - ADAM_B1 ** ADAM_STEP)
    v_hat = v / (1.0 - ADAM_B2 ** ADAM_STEP)
    delta = -ADAM_LR * (m_hat / (jnp.sqrt(v_hat) + ADAM_EPS) + ADAM_WD * w)
    return delta, m, v


def _adamw_shard_call(gslots, w, m, v, tr, name):
    R, C = w.shape

    def body(gs_ref, w_ref, m_ref, v_ref, g_ref, d_ref, mo_ref, vo_ref):
        g = gs_ref[0].astype(F32)
        for i in range(1, NDEV):
            g = g + gs_ref[i].astype(F32)
        delta, mn, vn = _adamw(w_ref[...], g, m_ref[...], v_ref[...])
        g_ref[...] = g
        d_ref[...] = delta
        mo_ref[...] = mn
        vo_ref[...] = vn

    t = _tiles(tr, C)
    return pl.pallas_call(
        body,
        name=name,
        grid=(R // tr,),
        in_specs=[pl.BlockSpec((NDEV, tr, C), lambda i: (0, i, 0)), t, t, t],
        out_specs=[t, t, t, t],
        out_shape=tuple(jax.ShapeDtypeStruct((R, C), F32) for _ in range(4)),
        compiler_params=_cp(),
    )(gslots, w, m, v)


def _wada_call(dm_ex, dm_ctx, silu_all, w, m, v):
    ncol = w.shape[1]

    def body(dme_ref, dmc_ref, s_ref, w_ref, m_ref, v_ref, g_ref, d_ref, mo_ref, vo_ref, ds_ref):
        dmc = _rowsum(dmc_ref[...])
        rows = lax.broadcasted_iota(jnp.int32, (8, 1), 0)
        low = jnp.where(rows == 0, dmc, 0.0)
        dm = jnp.concatenate([dme_ref[...], low], axis=0).astype(BF16)
        wv = w_ref[...]
        g = _dot_tn(s_ref[...].astype(BF16), dm)
        delta, mn, vn = _adamw(wv, g, m_ref[...], v_ref[...])
        g_ref[...] = g
        d_ref[...] = delta
        mo_ref[...] = mn
        vo_ref[...] = vn
        ds_ref[...] = _dot_nt(low.astype(BF16), wv.astype(BF16))

    return pl.pallas_call(
        body,
        name="wada_update",
        out_shape=tuple(jax.ShapeDtypeStruct((D, ncol), F32) for _ in range(4)) + (jax.ShapeDtypeStruct((8, D), F32),),
        compiler_params=pltpu.CompilerParams(vmem_limit_bytes=VMEM_LIMIT),
    )(dm_ex, dm_ctx, silu_all, w, m, v)


P_DMOD, P_DMODC, P_EMBG, P_EMBB, P_LN1G, P_LN1B, P_LN2G, P_LN2B = 0, 6, 8, 9, 10, 11, 12, 13
P_SSDG, P_PSC, P_DSK, P_CONVB, P_DTB, P_ALOG, P_LOSS, NP = 14, 15, 16, 17, 19, 20, 21, 24
S_CCTX, S_EMBG, S_EMBB, S_BADA, S_CONVB, S_DTB, S_ALOG, S_DSK = 0, 1, 2, 3, 9, 11, 12, 13
S_SSDG, S_PSC, S_LN1G, S_LN1B, S_LN2G, S_LN2B, NSM = 14, 15, 16, 17, 18, 19, 24


def _small_update_call(pall, dsil, cctx, w, m, v, et_mat):
    def body(p_ref, ds_ref, c_ref, w_ref, m_ref, v_ref, et_ref, g_ref, d_ref, mo_ref, vo_ref, loss_ref,
             tot, dsum, dsk8):
        tot[...] = p_ref[0]
        dsum[...] = ds_ref[0]
        for i in range(1, NDEV):
            tot[...] += p_ref[i]
            dsum[...] += ds_ref[i]
        cv = c_ref[...]
        sc = _sigmoid(cv)
        g_ref[...] = jnp.zeros_like(g_ref)
        g_ref[S_CCTX:S_CCTX + 1, :] = dsum[0:1, :] * (sc * (1.0 + cv * (1.0 - sc)))
        g_ref[S_EMBG:S_EMBG + 1, :] = tot[P_EMBG:P_EMBG + 1, :]
        g_ref[S_EMBB:S_EMBB + 1, :] = tot[P_EMBB:P_EMBB + 1, :]
        g_ref[S_BADA:S_BADA + 2, :] = tot[P_DMOD:P_DMOD + 2, :] + tot[P_DMODC:P_DMODC + 2, :]
        g_ref[S_BADA + 2:S_BADA + 6, :] = tot[P_DMOD + 2:P_DMOD + 6, :]
        g_ref[S_CONVB:S_CONVB + 2, :] = tot[P_CONVB:P_CONVB + 2, :]
        g_ref[S_DTB:S_DTB + 1, :] = tot[P_DTB:P_DTB + 1, :]
        g_ref[S_ALOG:S_ALOG + 1, :] = tot[P_ALOG:P_ALOG + 1, :]
        dsk8[...] = _dot(jnp.broadcast_to(tot[P_DSK:P_DSK + 1, :], (8, D)), et_ref[:, 0:128].astype(F32), HI)
        g_ref[S_DSK:S_DSK + 1, 0:128] = dsk8[0:1, :]
        g_ref[S_SSDG:S_SSDG + 1, :] = tot[P_SSDG:P_SSDG + 1, :]
        g_ref[S_PSC:S_PSC + 1, :] = tot[P_PSC:P_PSC + 1, :]
        g_ref[S_LN1G:S_LN1G + 1, :] = tot[P_LN1G:P_LN1G + 1, :]
        g_ref[S_LN1B:S_LN1B + 1, :] = tot[P_LN1B:P_LN1B + 1, :]
        g_ref[S_LN2G:S_LN2G + 1, :] = tot[P_LN2G:P_LN2G + 1, :]
        g_ref[S_LN2B:S_LN2B + 1, :] = tot[P_LN2B:P_LN2B + 1, :]
        delta, mn, vn = _adamw(w_ref[...], g_ref[...], m_ref[...], v_ref[...])
        d_ref[...] = delta
        mo_ref[...] = mn
        vo_ref[...] = vn
        loss_ref[...] = jnp.broadcast_to(tot[P_LOSS:P_LOSS + 1, 0:128], (8, 128))

    return pl.pallas_call(
        body,
        name="small_update",
        out_shape=tuple(jax.ShapeDtypeStruct((NSM, D), F32) for _ in range(4)) + (jax.ShapeDtypeStruct((8, 128), F32),),
        scratch_shapes=[pltpu.VMEM((NP, D), F32), pltpu.VMEM((8, D), F32), pltpu.VMEM((8, 128), F32)],
        compiler_params=pltpu.CompilerParams(vmem_limit_bytes=VMEM_LIMIT),
    )(pall, dsil, cctx, w, m, v, et_mat)


def _pad_rows(flat, mult=16):
    n = flat.shape[0]
    rows = -(-n // D)
    rows = -(-rows // mult) * mult
    return jnp.pad(flat, (0, rows * D - n)).reshape(rows, D)


def _by_cols(dw):
    r = dw.shape[0]
    return jnp.transpose(dw.reshape(r, NDEV, -1), (1, 0, 2))


def _from_cols(g):
    return jnp.transpose(g, (1, 0, 2)).reshape(g.shape[1], -1)


def _pool_constants(L):
    rows = L // GW
    t_r = jnp.arange(PT) // GW
    t_c = jnp.arange(PT) % GW
    fw, bw, ic = [], [], []
    pos_r = jnp.arange(L) // GW
    pos_c = jnp.arange(L) % GW
    for g, w in enumerate(WINDOWS):
        lo, hi = -(w // 2), w - w // 2 - 1
        n_ext = PT + 2 * _halo_tokens(g)
        e_r = jnp.arange(n_ext) // GW - w // 2
        e_c = jnp.arange(n_ext) % GW
        dr = e_r[None, :] - t_r[:, None]
        dc = e_c[None, :] - t_c[:, None]
        fw.append(((dr >= lo) & (dr <= hi) & (dc >= lo) & (dc <= hi)).astype(BF16))
        bw.append(((-dr >= lo) & (-dr <= hi) & (-dc >= lo) & (-dc <= hi)).astype(BF16))
        cr = jnp.minimum(pos_r + hi, rows - 1) - jnp.maximum(pos_r + lo, 0) + 1
        cc = jnp.minimum(pos_c + hi, GW - 1) - jnp.maximum(pos_c + lo, 0) + 1
        ic.append(1.0 / (cr * cc).astype(F32))
    icnt = jnp.pad(jnp.stack(ic, axis=1), ((0, 0), (0, 124)))
    return fw, bw, icnt


def _head_matrices():
    hp = jnp.arange(D) // HP
    e = (jnp.arange(128)[:, None] == hp[None, :]).astype(BF16)
    return jnp.concatenate([e, e], axis=0), jnp.concatenate([e.T, e.T], axis=1)


def _aligned_in_proj(w):
    zpad = jnp.zeros((D, 128 - NH), w.dtype)
    return jnp.concatenate([w[:, 0:2304], w[:, 2304:2320], zpad, w[:, 2320:2336], zpad, w[:, 2336:3360]], axis=1)


def _unaligned_in_proj(dw):
    return jnp.concatenate([dw[:, 0:2304], dw[:, 2304:2320], dw[:, 2432:2448], dw[:, 2560:3584]], axis=1)


def _row(v):
    return v.reshape(1, -1).astype(F32)


def _pad_lanes(v, width=D):
    v = v.reshape(1, -1)
    return jnp.pad(v, ((0, 0), (0, width - v.shape[1])))


def kernel(x, c, ctx, c_ctx, emb_ln_g, emb_ln_b, w_ada, b_ada, in_proj, conv_w, conv_b, dt_bias, a_log, d_skip, ssd_norm_g, pool_w, pool_scale, w_out, ln1_g, ln1_b, w_gate, w_up, w_down, ln2_g, ln2_b, loss_target, m_c_ctx, m_emb_ln_g, m_emb_ln_b, m_w_ada, m_b_ada, m_in_proj, m_conv_w, m_conv_b, m_dt_bias, m_a_log, m_d_skip, m_ssd_norm_g, m_pool_w, m_pool_scale, m_w_out, m_ln1_g, m_ln1_b, m_w_gate, m_w_up, m_w_down, m_ln2_g, m_ln2_b, v_c_ctx, v_emb_ln_g, v_emb_ln_b, v_w_ada, v_b_ada, v_in_proj, v_conv_w, v_conv_b, v_dt_bias, v_a_log, v_d_skip, v_ssd_norm_g, v_pool_w, v_pool_scale, v_w_out, v_ln1_g, v_ln1_b, v_w_gate, v_w_up, v_w_down, v_ln2_g, v_ln2_b):
    me = 4 * lax.axis_index("x") + 2 * lax.axis_index("y") + lax.axis_index("c")
    x0 = x[0]
    ctx0 = ctx[0]
    tgt = loss_target[0]
    L = x0.shape[0]
    LC = ctx0.shape[0]
    ncol_ada = w_ada.shape[2]

    small_in = jnp.concatenate([c.reshape(-1), conv_w.reshape(-1)])
    gb = jnp.concatenate([_row(emb_ln_g), _row(emb_ln_b), jnp.zeros((6, D), F32)], axis=0)
    xe, small_all, g_inp = _emb_ln_call(x0, gb, "emb_ln", [_pad_rows(small_in, 8), in_proj[0].astype(BF16)])
    (xe_c,) = _emb_ln_call(ctx0, gb, "emb_ln_ctx")
    c_all = small_all[:, 0, :]
    convw_all = small_all.reshape(NDEV, -1)[:, D:D + 5 * (DXBC // NDEV)].reshape(NDEV, 5, DXBC // NDEV)
    conv_w_full = _from_cols(convw_all)
    w_in = _aligned_in_proj(_from_cols(g_inp))
    late_shards = [pool_w[0].astype(BF16), w_out[0].astype(BF16), w_gate[0].astype(BF16), w_up[0].astype(BF16),
                   w_down[0].astype(BF16)]

    c_in = jnp.concatenate([c_all, c_ctx.reshape(1, D), jnp.zeros((7, D), F32)], axis=0)
    b_mine = lax.dynamic_slice(b_ada, (0, me * ncol_ada), (1, ncol_ada))
    silu_all, mod_mine = _mod_call(c_in, w_ada[0], b_mine)
    (mod_all,) = _exchange([mod_mine], "gather_mod", True)
    mod_all = _from_cols(mod_all)
    mod_me = lax.dynamic_slice(mod_all, (me, 0), (1, 6 * D)).reshape(6, D)
    mod_ctx = mod_all[8].reshape(6, D)

    tail = jnp.concatenate([
        _row(emb_ln_g), _row(emb_ln_b), _row(ln1_g), _row(ln1_b), _row(ln2_g), _row(ln2_b),
        _row(ssd_norm_g), _row(pool_scale), _row(jnp.repeat(d_skip.reshape(-1), HP)), jnp.zeros((1, D), F32)], axis=0)
    vec = jnp.concatenate([mod_me, tail], axis=0)
    vec_ctx = jnp.concatenate([mod_ctx, tail], axis=0)

    cw = jnp.concatenate([conv_w_full, conv_b.reshape(1, DXBC), jnp.zeros((2, DXBC), F32)], axis=0)
    par = jnp.concatenate([_pad_lanes(dt_bias[0, 0], 128), _pad_lanes(dt_bias[0, 1], 128),
                           _pad_lanes(a_log[0, 0], 128), _pad_lanes(a_log[0, 1], 128),
                           jnp.zeros((4, 128), F32)], axis=0)
    e_mat, et_mat = _head_matrices()
    pmat, pmat_t, icnt = _pool_constants(L)
    dskip_row = vec[V_DSK:V_DSK + 1]

    h1_c, _, xbcr_c, dtr_c, _ = _f1_call(xe_c, vec_ctx, w_in, "inproj_fwd_ctx")
    xbc_c = _f2_call(xbcr_c, cw, "conv_fwd_ctx")
    hzero = jnp.zeros((2, D, NS), F32)
    _, _, hpf_c, hpb_c, hfin_c = _ssd_fwd_call(xbc_c, dtr_c, hzero, par, e_mat, et_mat, "ssd_fwd_ctx")

    h1, z, xbcr, dtr, up = _f1_call(xe, vec, w_in, "inproj_fwd")
    xbc = _f2_call(xbcr, cw, "conv_fwd")
    yf, yb, hpf, hpb, _, g_pw, g_wo, g_wg, g_wu, g_wd = _ssd_fwd_call(xbc, dtr, hfin_c, par, e_mat, et_mat, "ssd_fwd",
                                                                       late_shards)
    pool_w_full = jnp.transpose(g_pw, (1, 0, 2, 3)).reshape(4, PG, PG)
    w_out_full = g_wo.reshape(2 * D, D)
    w_gate_full = _from_cols(g_wg)
    w_up_full = _from_cols(g_wu)
    w_down_full = g_wd.reshape(DFF, D)
    dpool, pun = _pool_fwd_call(up, pmat, icnt, pool_w_full)
    x1, mix, cat = _merge_call(yf, yb, xbc, z, pun, xe, vec, w_out_full)
    dpre2, gacc_f, gt_b, up_b = _ffn_fwd_call(x1, tgt, vec, w_gate_full, w_up_full, w_down_full)

    dh2p, dwg2, dwu2, dwd2 = _ffn_bwd_call(x1, dpre2, gt_b, up_b, vec, w_gate_full, w_up_full, w_down_full)
    nq = FFC // (DFF // NDEV)
    ffn_parts = [
        jnp.transpose(dwg2.reshape(-1, D, nq, DFF // NDEV), (0, 2, 1, 3)).reshape(NDEV, D, DFF // NDEV),
        jnp.transpose(dwu2.reshape(-1, D, nq, DFF // NDEV), (0, 2, 1, 3)).reshape(NDEV, D, DFF // NDEV),
        dwd2.reshape(NDEV, DFF // NDEV, D)]
    dxe_part, dy, dz, dd, dpw, gacc_m, dwo, gs_wg, gs_wu, gs_wd = _merge_bwd_call(
        dh2p, dpre2, xe, mix, cat, yf, yb, xbc, z, dpool, pun, vec, w_out_full, pool_w_full, ffn_parts)
    dup = _pool_bwd_call(dd, pmat_t, icnt)
    dxf, dxb, ddt0, ddt1, sacc, dh0 = _ssd_bwd_call(dy, xbc, dtr, hpf, hpb, hzero, par, e_mat, et_mat, dskip_row,
                                                     "ssd_bwd")
    zeros_c = jnp.zeros((LC, D), F32)
    dxf_c, dxb_c, ddt0_c, ddt1_c, sacc_c, _ = _ssd_bwd_call(zeros_c, xbc_c, dtr_c, hpf_c, hpb_c, dh0, par, e_mat, et_mat,
                                                            jnp.zeros((1, D), F32), "ssd_bwd_ctx")
    dprec_c, cacc_c = _conv_bwd_call(dxf_c, dxb_c, xbcr_c, cw, jnp.zeros((8, DXBC), F32), "conv_bwd_ctx")
    _, gacc_c, dwin_c = _inproj_bwd_call(dprec_c, cw, zeros_c, ddt0_c, ddt1_c, zeros_c, h1_c, zeros_c, ctx0, vec_ctx,
                                         w_in, jnp.zeros((D, WIN), F32), "inproj_bwd_ctx")
    mix_parts = [dwo.reshape(NDEV, 2 * D // NDEV, D),
                 jnp.transpose(dpw.reshape(4, NDEV, PG // NDEV, PG), (1, 0, 2, 3)).reshape(NDEV, 4 * PG // NDEV, PG)]
    dprec, cacc, gs_wo, gs_pw = _conv_bwd_call(dxf, dxb, xbcr, cw, cacc_c, "conv_bwd", mix_parts)
    grad_x, gacc_i, dwin = _inproj_bwd_call(dprec, cw, dz, ddt0, ddt1, dup, h1, dxe_part, x0, vec, w_in, dwin_c,
                                            "inproj_bwd")

    gsum = gacc_f + gacc_m + gacc_i
    sa = sacc + sacc_c
    dtb_row = _pad_lanes(jnp.concatenate([sa[0, 0, 0:NH], sa[1, 0, 0:NH]]))
    alog_row = _pad_lanes(jnp.concatenate([sa[0, 1, 0:NH], sa[1, 1, 0:NH]]))
    convb_rows = jnp.pad(cacc[5], (0, 2 * D - DXBC)).reshape(2, D)
    pack = jnp.concatenate([
        gsum[V_SH1:V_G2 + 1],
        gacc_c[V_SH1:V_SC1 + 1],
        gsum[V_EMBG:V_EMBB + 1] + gacc_c[V_EMBG:V_EMBB + 1],
        gsum[V_LN1G:V_LN2B + 1],
        gsum[V_SSDG:V_DSK + 1],
        convb_rows, dtb_row, alog_row,
        gsum[V_LOSS:V_LOSS + 1],
        jnp.zeros((NP - 22, D), F32)], axis=0)
    (pall,) = _exchange([pack], "gather_small_grads", True)

    dm_flat = pall[:, 0:8, :].reshape(NDEV, 8 * D)
    dm_ex = lax.dynamic_slice(dm_flat, (0, me * ncol_ada), (NDEV, ncol_ada))
    dmc_full = jnp.concatenate([dm_flat[:, 6 * D:8 * D], jnp.zeros((NDEV, 4 * D), F32)], axis=1)
    dm_ctx = lax.dynamic_slice(dmc_full, (0, me * ncol_ada), (NDEV, ncol_ada))
    g_wada, d_wada, nm_wada, nv_wada, dsil = _wada_call(dm_ex, dm_ctx, silu_all, w_ada[0], m_w_ada[0], v_w_ada[0])
    (dsil_all,) = _exchange([dsil], "gather_dsilu", True)

    def small_pack(cc, eg, eb, ba, cb_, dtb, al, dsk, sg, ps, l1g, l1b, l2g, l2b):
        return jnp.concatenate([
            _row(cc), _row(eg), _row(eb), ba.reshape(6, D), jnp.pad(cb_.reshape(-1), (0, 2 * D - DXBC)).reshape(2, D),
            _pad_lanes(dtb.reshape(-1)), _pad_lanes(al.reshape(-1)), _pad_lanes(dsk.reshape(-1)),
            _row(sg), _row(ps), _row(l1g), _row(l1b), _row(l2g), _row(l2b), jnp.zeros((NSM - 20, D), F32)], axis=0)

    sw = small_pack(c_ctx, emb_ln_g, emb_ln_b, b_ada, conv_b, dt_bias, a_log, d_skip, ssd_norm_g, pool_scale,
                    ln1_g, ln1_b, ln2_g, ln2_b)
    sm = small_pack(m_c_ctx, m_emb_ln_g, m_emb_ln_b, m_b_ada, m_conv_b, m_dt_bias, m_a_log, m_d_skip, m_ssd_norm_g,
                    m_pool_scale, m_ln1_g, m_ln1_b, m_ln2_g, m_ln2_b)
    sv = small_pack(v_c_ctx, v_emb_ln_g, v_emb_ln_b, v_b_ada, v_conv_b, v_dt_bias, v_a_log, v_d_skip, v_ssd_norm_g,
                    v_pool_scale, v_ln1_g, v_ln1_b, v_ln2_g, v_ln2_b)
    s_g, s_d, s_m, s_v, loss8 = _small_update_call(pall, dsil_all, _row(c_ctx), sw, sm, sv, et_mat)

    def small_unpack(t):
        return (t[S_CCTX], t[S_EMBG], t[S_EMBB], t[S_BADA:S_BADA + 6].reshape(1, 6 * D),
                t[S_CONVB:S_CONVB + 2].reshape(-1)[:DXBC].reshape(1, DXBC),
                t[S_DTB, 0:2 * NH].reshape(1, 2, NH), t[S_ALOG, 0:2 * NH].reshape(1, 2, NH), t[S_DSK, 0:NH].reshape(1, NH),
                t[S_SSDG].reshape(1, D), t[S_PSC].reshape(1, D), t[S_LN1G].reshape(1, D), t[S_LN1B].reshape(1, D),
                t[S_LN2G].reshape(1, D), t[S_LN2B].reshape(1, D))

    gs_inp, gs_cw = _exchange([_by_cols(_unaligned_in_proj(dwin)).astype(BF16), _by_cols(cacc[0:5])],
                              "exchange_last_grads", False)

    pshape = (4 * PG // NDEV, PG)
    u_inp = _adamw_shard_call(gs_inp, in_proj[0], m_in_proj[0], v_in_proj[0], 256, "adamw_in_proj")
    u_cw = _adamw_shard_call(gs_cw, conv_w[0], m_conv_w[0], v_conv_w[0], 5, "adamw_conv_w")
    u_pw = _adamw_shard_call(gs_pw, pool_w[0].reshape(pshape), m_pool_w[0].reshape(pshape), v_pool_w[0].reshape(pshape),
                             pshape[0], "adamw_pool_w")
    u_wo = _adamw_shard_call(gs_wo, w_out[0], m_w_out[0], v_w_out[0], 64, "adamw_w_out")
    u_wg = _adamw_shard_call(gs_wg, w_gate[0], m_w_gate[0], v_w_gate[0], 256, "adamw_w_gate")
    u_wu = _adamw_shard_call(gs_wu, w_up[0], m_w_up[0], v_w_up[0], 256, "adamw_w_up")
    u_wd = _adamw_shard_call(gs_wd, w_down[0], m_w_down[0], v_w_down[0], 88, "adamw_w_down")

    def assemble(k, small, wada):
        (cc, eg, eb, ba, cb_, dtb, al, dsk, sg, ps, l1g, l1b, l2g, l2b) = small_unpack(small)
        pw = u_pw[k].reshape(1, 4, PG // NDEV, PG)
        return (cc, eg, eb, wada[None], ba, u_inp[k][None], u_cw[k][None], cb_, dtb, al, dsk, sg, pw, ps,
                u_wo[k][None], l1g, l1b, u_wg[k][None], u_wu[k][None], u_wd[k][None], l2g, l2b)

    loss = loss8[0, 0]
    return (loss, grad_x[None], *assemble(0, s_g, g_wada), *assemble(1, s_d, d_wada),
            *assemble(2, s_m, nm_wada), *assemble(3, s_v, nv_wada))
```

```python
import functools
import math

import jax
import jax.numpy as jnp
from jax import lax
from jax.experimental import pallas as pl
from jax.experimental.pallas import tpu as pltpu

F32 = jnp.float32
BF16 = jnp.bfloat16
HI = lax.Precision.HIGHEST

NDEV = 8
D = 1024
NH = 16
HP = 64
NS = 128
Q = 128
DXBC = 1280
DFF = 2816
FFC = 1408
GW = 64
PR = 8
PT = PR * GW
WINDOWS = (2, 4, 8, 16)
PG = 256
DIN = 3360
WIN = 3584
ALPHA = 2.0 ** 0.25
LN_EPS = 1e-5
TM = 256

ADAM_LR = 0.001
ADAM_B1 = 0.9
ADAM_B2 = 0.999
ADAM_EPS = 1e-08
ADAM_WD = 0.01
ADAM_STEP = 10

V_SH1, V_SC1, V_G1, V_SH2, V_SC2, V_G2 = 0, 1, 2, 3, 4, 5
V_EMBG, V_EMBB, V_LN1G, V_LN1B, V_LN2G, V_LN2B = 6, 7, 8, 9, 10, 11
V_SSDG, V_PSC, V_DSK, V_LOSS = 12, 13, 14, 15
NV = 16

VMEM_LIMIT = 60 * 1024 * 1024


def _cp(ndim=1):
    return pltpu.CompilerParams(dimension_semantics=("arbitrary",) * ndim, vmem_limit_bytes=VMEM_LIMIT)


def _dot(a, b, precision=None):
    return jnp.dot(a, b, preferred_element_type=F32, precision=precision)


def _dot_nt(a, b):
    return lax.dot_general(a, b, (((1,), (1,)), ((), ())), preferred_element_type=F32)


def _dot_tn(a, b, precision=None):
    return lax.dot_general(a, b, (((0,), (0,)), ((), ())), preferred_element_type=F32, precision=precision)


def _split2(x):
    hi = x.astype(BF16)
    return hi, (x - hi.astype(F32)).astype(BF16)


def _split_dot(m, x):
    hi, lo = _split2(x)
    return _dot(m, hi) + _dot(m, lo)


def _dot_split(x, m):
    hi, lo = _split2(x)
    return _dot(hi, m) + _dot(lo, m)


def _sigmoid(x):
    return 1.0 / (1.0 + jnp.exp(-x))


def _softplus(x):
    return jnp.maximum(x, 0.0) + jnp.log(1.0 + jnp.exp(-jnp.abs(x)))


def _ln(x, g, b):
    mu = jnp.mean(x, axis=-1, keepdims=True)
    xc = x - mu
    var = jnp.mean(xc * xc, axis=-1, keepdims=True)
    rstd = lax.rsqrt(var + LN_EPS)
    n = xc * rstd
    return n * g + b, n, rstd


def _ln_bwd(dy, n, rstd, g):
    dn = dy * g
    return rstd * (dn - jnp.mean(dn, axis=-1, keepdims=True) - n * jnp.mean(dn * n, axis=-1, keepdims=True))


def _rowsum(x):
    return jnp.sum(x, axis=0, keepdims=True)


def _resident(shape):
    nd = len(shape)
    return pl.BlockSpec(shape, lambda *_: (0,) * nd, pipeline_mode=pl.Buffered(1))


def _const_out(shape):
    nd = len(shape)
    return pl.BlockSpec(shape, lambda *_: (0,) * nd)


def _tiles(tm, width):
    return pl.BlockSpec((tm, width), lambda i: (i, 0))


def _halo_specs(tm, width, n_rows):
    r = tm // 8
    last = n_rows // 8 - 1
    prev = pl.BlockSpec((8, width), lambda i: (jnp.maximum(i * r - 1, 0), 0))
    nxt = pl.BlockSpec((8, width), lambda i: (jnp.minimum((i + 1) * r, last), 0))
    return prev, nxt


def _acc_tn(acc_ref, a, b, chunk=512):
    n = b.shape[1]
    for c0 in range(0, n, chunk):
        c1 = min(c0 + chunk, n)
        acc_ref[:, c0:c1] += _dot_tn(a, b[:, c0:c1])


def _my_coords():
    return lax.axis_index("x"), lax.axis_index("y"), lax.axis_index("c")


def _peer(k, mx, my, mc):
    kx, ky, kc = (k >> 2) & 1, (k >> 1) & 1, k & 1
    px = 1 - mx if kx else mx
    py = 1 - my if ky else my
    pc = 1 - mc if kc else mc
    return px, py, pc


class _Exchange:
    def __init__(self, srcs, dsts, send_sems, recv_sems, local_sems, gather):
        self.srcs, self.dsts, self.gather = srcs, dsts, gather
        self.send_sems, self.recv_sems, self.local_sems = send_sems, recv_sems, local_sems

    def _copies(self, outgoing):
        mx, my, mc = _my_coords()
        me = 4 * mx + 2 * my + mc
        local, remote = [], []
        for t, (src, dst) in enumerate(zip(self.srcs, self.dsts)):
            local.append(pltpu.make_async_copy(src if self.gather else src.at[me], dst.at[me], self.local_sems.at[t]))
            for k in range(1, NDEV):
                px, py, pc = _peer(k, mx, my, mc)
                pid = 4 * px + 2 * py + pc
                remote.append(pltpu.make_async_remote_copy(
                    src_ref=src if self.gather else src.at[pid],
                    dst_ref=dst.at[me] if outgoing else dst.at[pid],
                    send_sem=self.send_sems.at[t, k - 1],
                    recv_sem=self.recv_sems.at[t, k - 1],
                    device_id=(px, py, pc),
                    device_id_type=pl.DeviceIdType.MESH,
                ))
        return local, remote

    def start(self):
        local, remote = self._copies(True)
        for cp in local + remote:
            cp.start()

    def wait(self):
        local, sends = self._copies(True)
        _, recvs = self._copies(False)
        for cp in recvs:
            cp.wait_recv()
        for cp in sends:
            cp.wait_send()
        for cp in local:
            cp.wait()


class _ChipGather:
    def __init__(self, srcs, dsts, send_sems, recv_sems, local_sems):
        self.srcs, self.dsts = srcs, dsts
        self.send_sems, self.recv_sems, self.local_sems = send_sems, recv_sems, local_sems

    def _places(self):
        x, y, c = _my_coords()
        return (x, y, c), (x, y, 1 - c), [(1 - x, y), (x, 1 - y), (1 - x, 1 - y)]

    def _copy(self, t, k, block, to, own=False):
        slot = self.dsts[t].at[4 * block[0] + 2 * block[1] + block[2]]
        return pltpu.make_async_remote_copy(
            src_ref=self.srcs[t] if own else slot, dst_ref=slot,
            send_sem=self.send_sems.at[t, k], recv_sem=self.recv_sems.at[t, k],
            device_id=to, device_id_type=pl.DeviceIdType.MESH)

    def _local(self, t, me):
        return pltpu.make_async_copy(self.srcs[t], self.dsts[t].at[4 * me[0] + 2 * me[1] + me[2]], self.local_sems.at[t])

    def start(self):
        me, sib, chips = self._places()
        for t in range(len(self.srcs)):
            self._local(t, me).start()
            self._copy(t, 0, me, sib, own=True).start()
            for j, chip in enumerate(chips):
                self._copy(t, 1 + j, me, (*chip, me[2]), own=True).start()

    def wait(self):
        me, sib, chips = self._places()
        n = len(self.srcs)
        for t in range(n):
            for j, chip in enumerate(chips):
                self._copy(t, 1 + j, (*chip, me[2]), me).wait_recv()
                self._copy(t, 4 + j, (*chip, me[2]), sib).start()
        for t in range(n):
            self._copy(t, 0, sib, me).wait_recv()
            for j, chip in enumerate(chips):
                self._copy(t, 4 + j, (*chip, sib[2]), me).wait_recv()
            self._copy(t, 0, me, sib, own=True).wait_send()
            for j, chip in enumerate(chips):
                self._copy(t, 1 + j, me, (*chip, me[2]), own=True).wait_send()
                self._copy(t, 4 + j, (*chip, me[2]), sib).wait_send()
            self._local(t, me).wait()


def _exchange_sems(n):
    return [pltpu.SemaphoreType.DMA((n, NDEV - 1)), pltpu.SemaphoreType.DMA((n, NDEV - 1)), pltpu.SemaphoreType.DMA((n,))]


def _exchange_out_shapes(xs, gather):
    return [jax.ShapeDtypeStruct(x.shape if not gather else (NDEV,) + x.shape, x.dtype) for x in xs]


def _exchange(xs, name, gather):
    n = len(xs)

    def body(*refs):
        ex = _Exchange(refs[:n], refs[n:2 * n], *refs[2 * n:], gather)
        ex.start()
        ex.wait()

    any_spec = pl.BlockSpec(memory_space=pl.ANY)
    return pl.pallas_call(
        body,
        name=name,
        out_shape=_exchange_out_shapes(xs, gather),
        in_specs=[any_spec] * n,
        out_specs=[any_spec] * n,
        scratch_shapes=_exchange_sems(n),
    )(*xs)


def _mod_call(c_all, w_ada, b_ada):
    ncol = w_ada.shape[1]

    def body(c_ref, w_ref, b_ref, silu_ref, mod_ref):
        cv = c_ref[...]
        s = cv * _sigmoid(cv)
        silu_ref[...] = s
        mod_ref[...] = _dot(s.astype(BF16), w_ref[...].astype(BF16)) + b_ref[...]

    return pl.pallas_call(
        body,
        name="mod_fwd",
        out_shape=(jax.ShapeDtypeStruct((16, D), F32), jax.ShapeDtypeStruct((16, ncol), F32)),
    )(c_all, w_ada, b_ada)


def _emb_ln_call(x0, gb, name, gathered=()):
    L = x0.shape[0]
    nt = L // TM
    ng = len(gathered)

    def body(*refs):
        x_ref, gb_ref = refs[:2]
        xe_ref = refs[2 + ng]
        i = pl.program_id(0)
        if ng:
            ex = _ChipGather(refs[2:2 + ng], refs[3 + ng:3 + 2 * ng], *refs[3 + 2 * ng:])

            @pl.when(i == 0)
            def _():
                ex.start()

        xe_ref[...] = _ln(x_ref[...], gb_ref[0:1, :], gb_ref[1:2, :])[0]
        if ng:
            @pl.when(i == nt - 1)
            def _():
                ex.wait()

    any_spec = pl.BlockSpec(memory_space=pl.ANY)
    return pl.pallas_call(
        body,
        name=name,
        grid=(nt,),
        in_specs=[_tiles(TM, D), _resident((8, D))] + [any_spec] * ng,
        out_specs=[_tiles(TM, D)] + [any_spec] * ng,
        out_shape=[jax.ShapeDtypeStruct((L, D), F32)] + _exchange_out_shapes(gathered, True),
        scratch_shapes=_exchange_sems(ng) if ng else [],
        compiler_params=_cp(),
    )(x0, gb, *gathered)


def _f1_call(xe, vec, w_in, name):
    L = xe.shape[0]

    def body(xe_ref, vec_ref, w_ref, h1_ref, z_ref, xbc_ref, dt_ref, up_ref):
        h1 = (xe_ref[...] * (1.0 + vec_ref[V_SC1:V_SC1 + 1, :]) + vec_ref[V_SH1:V_SH1 + 1, :]).astype(BF16)
        proj = _dot(h1, w_ref[...])
        h1_ref[...] = h1
        z_ref[...] = proj[:, 0:1024]
        xbc_ref[...] = proj[:, 1024:2304]
        dt_ref[...] = proj[:, 2304:2560]
        up_ref[...] = proj[:, 2560:3584]

    return pl.pallas_call(
        body,
        name=name,
        grid=(L // TM,),
        in_specs=[_tiles(TM, D), _resident((NV, D)), _resident((D, WIN))],
        out_specs=[_tiles(TM, D), _tiles(TM, D), _tiles(TM, DXBC), _tiles(TM, 256), _tiles(TM, D)],
        out_shape=(
            jax.ShapeDtypeStruct((L, D), BF16),
            jax.ShapeDtypeStruct((L, D), F32),
            jax.ShapeDtypeStruct((L, DXBC), F32),
            jax.ShapeDtypeStruct((L, 256), F32),
            jax.ShapeDtypeStruct((L, D), F32),
        ),
        compiler_params=_cp(),
    )(xe, vec, w_in)


def _extended(ext, cur_ref, prev_ref, next_ref):
    i = pl.program_id(0)
    n = pl.num_programs(0)
    tm = cur_ref.shape[0]
    ext[0:8, :] = jnp.where(i > 0, prev_ref[...], 0.0)
    ext[8:8 + tm, :] = cur_ref[...]
    ext[8 + tm:16 + tm, :] = jnp.where(i < n - 1, next_ref[...], 0.0)
    return ext


def _shifted(ext, offset, tm):
    return ext[8 + offset:8 + offset + tm, :]


def _conv_pre(ext, cw_ref, tm):
    acc = cw_ref[5:6, :] + cw_ref[0:1, :] * _shifted(ext, -2, tm)
    for k in range(1, 5):
        acc = acc + cw_ref[k:k + 1, :] * _shifted(ext, k - 2, tm)
    return acc


def _f2_call(xbc_raw, cw, name):
    L = xbc_raw.shape[0]
    prev, nxt = _halo_specs(TM, DXBC, L)

    def body(cur_ref, prev_ref, next_ref, cw_ref, out_ref, ext):
        pre = _conv_pre(_extended(ext, cur_ref, prev_ref, next_ref), cw_ref, TM)
        out_ref[...] = pre * _sigmoid(pre)

    return pl.pallas_call(
        body,
        name=name,
        grid=(L // TM,),
        in_specs=[_tiles(TM, DXBC), prev, nxt, _resident((8, DXBC))],
        out_specs=_tiles(TM, DXBC),
        out_shape=jax.ShapeDtypeStruct((L, DXBC), F32),
        scratch_shapes=[pltpu.VMEM((TM + 16, DXBC), F32)],
        compiler_params=_cp(),
    )(xbc_raw, xbc_raw, xbc_raw, cw)


def _ssd_common(d, dtr, par_ref):
    lane = lax.broadcasted_iota(jnp.int32, (1, 128), 1)
    hmask = lane < NH
    bias = par_ref[d:d + 1, :]
    alog = par_ref[2 + d:3 + d, :]
    aneg = jnp.where(hmask, -jnp.exp(alog), 0.0)
    pre = dtr + bias
    dt = jnp.where(hmask, _softplus(pre), 0.0)
    a = dt * aneg
    row = lax.broadcasted_iota(jnp.int32, (Q, Q), 0)
    col = lax.broadcasted_iota(jnp.int32, (Q, Q), 1)
    maskf = ((row >= col) if d == 0 else (row <= col)).astype(F32)
    A = _dot(maskf, a, HI)
    atot = _rowsum(a)
    return dict(hmask=hmask, aneg=aneg, pre=pre, dt=dt, a=a, maskf=maskf, A=A, AT=A.T, dtT=dt.T,
                atot=atot, lane=lane)


def _column(v, lane, h):
    return jnp.sum(jnp.where(lane == h, v, 0.0), axis=1, keepdims=True)


def _head_expand(e_ref, v):
    hi, lo = _split2(v)
    return _dot(jnp.concatenate([hi, lo], axis=1), e_ref[...])


def _head_sum(et_ref, v):
    return _dot(v.astype(BF16), et_ref[:, 0:128])


def _state_decay(atot, lane):
    ea = jnp.exp(atot)
    return jnp.concatenate([jnp.broadcast_to(_column(ea, lane, h), (HP, NS)) for h in range(NH)], axis=0)


def _ssd_fwd_call(xbc, dtr, h0, par, e_mat, et_mat, name, gathered=()):
    L = xbc.shape[0]
    nc = L // Q
    ng = len(gathered)

    def body(*refs):
        xbc_refs, dtr_refs = refs[0:2], refs[2:4]
        h0_ref, par_ref, e_ref, et_ref = refs[4:8]
        y_refs, hp_refs = refs[8 + ng:10 + ng], refs[10 + ng:12 + ng]
        hf_ref = refs[12 + ng]
        hs, AT, dtT = refs[13 + 2 * ng:16 + 2 * ng]
        s = pl.program_id(0)
        if ng:
            ex = _Exchange(refs[8:8 + ng], refs[13 + ng:13 + 2 * ng], *refs[16 + 2 * ng:], True)

        @pl.when(s == 0)
        def _():
            if ng:
                ex.start()
            hs[...] = h0_ref[...]

        for d in range(2):
            xbc_ref, y_ref = xbc_refs[d], y_refs[d]
            q = _ssd_common(d, dtr_refs[d][...], par_ref)
            A, maskf, lane = q["A"], q["maskf"], q["lane"]
            AT[d] = q["AT"]
            dtT[d] = q["dtT"]
            hprev = hs[d]
            hp_refs[d][0] = hprev
            bb = xbc_ref[:, 1024:1152].astype(BF16)
            cb = xbc_ref[:, 1152:1280].astype(BF16)
            g = _dot_nt(cb, bb)
            yoff = _dot_nt(cb, hprev.astype(BF16)) * _head_expand(e_ref, jnp.exp(A))
            for k in range(NH // 2):
                ks = slice(128 * k, 128 * k + 128)
                xp = xbc_ref[:, ks]
                scs, xhs = [], []
                for half in range(2):
                    h = 2 * k + half
                    seg = _column(A, lane, h) - AT[d, h:h + 1, :]
                    lm = jnp.exp(jnp.minimum(seg, 0.0)) * maskf
                    scs.append((g * lm * dtT[d, h:h + 1, :]).astype(BF16))
                    inhead = (lane >= HP) if half else (lane < HP)
                    xhs.append(jnp.where(inhead, xp, 0.0).astype(BF16))
                y_ref[:, ks] = yoff[:, ks] + _dot(jnp.concatenate(scs, axis=1), jnp.concatenate(xhs, axis=0))
            wend = jnp.exp(q["atot"] - A) * q["dt"]
            xw = (xbc_ref[:, 0:1024] * _head_expand(e_ref, wend)).astype(BF16)
            hnew = hprev * _state_decay(q["atot"], lane) + _dot_tn(xw, bb)
            hs[d] = hnew
            hf_ref[d] = hnew
        if ng:
            @pl.when(s == nc - 1)
            def _():
                ex.wait()

    any_spec = pl.BlockSpec(memory_space=pl.ANY)
    return pl.pallas_call(
        body,
        name=name,
        grid=(nc,),
        in_specs=[
            pl.BlockSpec((Q, DXBC), lambda s: (s, 0)),
            pl.BlockSpec((Q, DXBC), lambda s: (nc - 1 - s, 0)),
            pl.BlockSpec((Q, 128), lambda s: (s, 0)),
            pl.BlockSpec((Q, 128), lambda s: (nc - 1 - s, 1)),
            _resident((2, D, NS)),
            _resident((8, 128)),
            _resident((256, D)),
            _resident((D, 256)),
        ] + [any_spec] * ng,
        out_specs=[
            pl.BlockSpec((Q, D), lambda s: (s, 0)),
            pl.BlockSpec((Q, D), lambda s: (nc - 1 - s, 0)),
            pl.BlockSpec((1, D, NS), lambda s: (s, 0, 0)),
            pl.BlockSpec((1, D, NS), lambda s: (nc - 1 - s, 0, 0)),
            _const_out((2, D, NS)),
        ] + [any_spec] * ng,
        out_shape=[
            jax.ShapeDtypeStruct((L, D), F32),
            jax.ShapeDtypeStruct((L, D), F32),
            jax.ShapeDtypeStruct((nc, D, NS), F32),
            jax.ShapeDtypeStruct((nc, D, NS), F32),
            jax.ShapeDtypeStruct((2, D, NS), F32),
        ] + _exchange_out_shapes(gathered, True),
        scratch_shapes=[pltpu.VMEM((2, D, NS), F32), pltpu.VMEM((2, 128, Q), F32), pltpu.VMEM((2, 128, Q), F32)]
        + (_exchange_sems(ng) if ng else []),
        compiler_params=_cp(),
    )(xbc, xbc, dtr, dtr, h0, par, e_mat, et_mat, *gathered)


def _halo_tokens(g):
    return (WINDOWS[g] // 2) * GW


def _pool_specs(n_tiles):
    cur = pl.BlockSpec((PT, D), lambda i: (i, 0))
    prev = pl.BlockSpec((PT, D), lambda i: (jnp.maximum(i - 1, 0), 0))
    nxt = pl.BlockSpec((PT, D), lambda i: (jnp.minimum(i + 1, n_tiles - 1), 0))
    return cur, prev, nxt


def _pool_fwd_call(up, pmat, icnt, pool_w):
    L = up.shape[0]
    nt = L // PT
    cur, prev, nxt = _pool_specs(nt)

    def body(cur_ref, prev_ref, next_ref, m0_ref, m1_ref, m2_ref, m3_ref, ic_ref, pw_ref, d_ref, pun_ref):
        i = pl.program_id(0)
        n = pl.num_programs(0)
        lane = lax.broadcasted_iota(jnp.int32, (1, 128), 1)
        icv = ic_ref[...]
        for g, m_ref in enumerate((m0_ref, m1_ref, m2_ref, m3_ref)):
            gs = slice(PG * g, PG * g + PG)
            halo = _halo_tokens(g)
            top = jnp.where(i > 0, prev_ref[PT - halo:PT, gs], 0.0)
            bot = jnp.where(i < n - 1, next_ref[0:halo, gs], 0.0)
            mid = cur_ref[:, gs]
            box = _split_dot(m_ref[...], jnp.concatenate([top, mid, bot], axis=0))
            dg = (box * _column(icv, lane, g) - mid).astype(BF16)
            d_ref[:, gs] = dg
            pun_ref[:, gs] = _dot(dg, pw_ref[g])

    return pl.pallas_call(
        body,
        name="pool_fwd",
        grid=(nt,),
        in_specs=[cur, prev, nxt] + [_resident(m.shape) for m in pmat] + [_tiles(PT, 128), _resident((4, PG, PG))],
        out_specs=[_tiles(PT, D), _tiles(PT, D)],
        out_shape=(jax.ShapeDtypeStruct((L, D), BF16), jax.ShapeDtypeStruct((L, D), F32)),
        compiler_params=_cp(),
    )(up, up, up, *pmat, icnt, pool_w)


def _gated(yf_ref, yb_ref, xs, z, vec_ref):
    ym = yf_ref[...] + yb_ref[...] + vec_ref[V_DSK:V_DSK + 1, :] * xs
    sz = _sigmoid(z)
    gated = ym * (z * sz)
    r = lax.rsqrt(jnp.mean(gated * gated, axis=-1, keepdims=True) + LN_EPS)
    return ym, sz, gated, r


def _merge_call(yf, yb, xbc, z, pun, xe, vec, w_out):
    L = z.shape[0]

    def body(yf_ref, yb_ref, xs_ref, z_ref, pun_ref, xe_ref, vec_ref, w_ref, x1_ref, mix_ref, cat_ref):
        _, _, gated, r = _gated(yf_ref, yb_ref, xs_ref[...], z_ref[...], vec_ref)
        yn = gated * r * vec_ref[V_SSDG:V_SSDG + 1, :]
        p = pun_ref[...] * vec_ref[V_PSC:V_PSC + 1, :]
        cat = jnp.concatenate([yn, p], axis=1).astype(BF16)
        mix = _dot(cat, w_ref[...])
        pre1 = ALPHA * xe_ref[...] + vec_ref[V_G1:V_G1 + 1, :] * mix
        x1, _, _ = _ln(pre1, vec_ref[V_LN1G:V_LN1G + 1, :], vec_ref[V_LN1B:V_LN1B + 1, :])
        x1_ref[...] = x1
        mix_ref[...] = mix
        cat_ref[...] = cat

    return pl.pallas_call(
        body,
        name="merge_fwd",
        grid=(L // TM,),
        in_specs=[
            _tiles(TM, D), _tiles(TM, D), _tiles(TM, D), _tiles(TM, D), _tiles(TM, D), _tiles(TM, D),
            _resident((NV, D)), _resident((2 * D, D)),
        ],
        out_specs=[_tiles(TM, D), _tiles(TM, D), _tiles(TM, 2 * D)],
        out_shape=(
            jax.ShapeDtypeStruct((L, D), F32),
            jax.ShapeDtypeStruct((L, D), F32),
            jax.ShapeDtypeStruct((L, 2 * D), BF16),
        ),
        compiler_params=_cp(),
    )(yf, yb, xbc, z, pun, xe, vec, w_out)


def _ffn_fwd_call(x1, tgt, vec, w_gate, w_up, w_down):
    L = x1.shape[0]

    def body(x1_ref, tgt_ref, vec_ref, wg_ref, wu_ref, wd_ref, dpre_ref, gacc_ref, gt_ref, up_ref):
        @pl.when(pl.program_id(0) == 0)
        def _():
            gacc_ref[...] = jnp.zeros_like(gacc_ref)

        x1 = x1_ref[...]
        h2 = (x1 * (1.0 + vec_ref[V_SC2:V_SC2 + 1, :]) + vec_ref[V_SH2:V_SH2 + 1, :]).astype(BF16)
        gt = _dot(h2, wg_ref[...])
        up = _dot(h2, wu_ref[...])
        gt_ref[...] = gt.astype(BF16)
        up_ref[...] = up.astype(BF16)
        f = (gt * _sigmoid(gt) * up).astype(BF16)
        ffn = _dot(f, wd_ref[...])
        g2 = vec_ref[V_G2:V_G2 + 1, :]
        lng = vec_ref[V_LN2G:V_LN2G + 1, :]
        x2, n2, rstd2 = _ln(ALPHA * x1 + g2 * ffn, lng, vec_ref[V_LN2B:V_LN2B + 1, :])
        diff = x2 - tgt_ref[...]
        dx2 = diff * (1.0 / D)
        dpre2 = _ln_bwd(dx2, n2, rstd2, lng)
        dpre_ref[...] = dpre2
        gacc_ref[V_LN2G:V_LN2G + 1, :] += _rowsum(dx2 * n2)
        gacc_ref[V_LN2B:V_LN2B + 1, :] += _rowsum(dx2)
        gacc_ref[V_G2:V_G2 + 1, :] += _rowsum(dpre2 * ffn)
        gacc_ref[V_LOSS:V_LOSS + 1, :] += jnp.sum(diff * diff) * (0.5 / D)

    return pl.pallas_call(
        body,
        name="ffn_fwd",
        grid=(L // TM,),
        in_specs=[_tiles(TM, D), _tiles(TM, D), _resident((NV, D)),
                  _resident((D, DFF)), _resident((D, DFF)), _resident((DFF, D))],
        out_specs=[_tiles(TM, D), _const_out((NV, D)), _tiles(TM, DFF), _tiles(TM, DFF)],
        out_shape=(jax.ShapeDtypeStruct((L, D), F32), jax.ShapeDtypeStruct((NV, D), F32),
                   jax.ShapeDtypeStruct((L, DFF), BF16), jax.ShapeDtypeStruct((L, DFF), BF16)),
        compiler_params=_cp(),
    )(x1, tgt, vec, w_gate, w_up, w_down)


def _ffn_bwd_call(x1, dpre2, gt_b, up_b, vec, w_gate, w_up, w_down):
    L = x1.shape[0]
    nt = L // TM
    nj = DFF // FFC

    def body(x1_ref, dpre_ref, gt_ref, up_ref, vec_ref, wg_ref, wu_ref, wd_ref, dh2_ref, dwg_ref, dwu_ref, dwd_ref,
             ag, au, ad):
        j = pl.program_id(0)
        i = pl.program_id(1)

        @pl.when(i == 0)
        def _():
            ag[...] = jnp.zeros_like(ag)
            au[...] = jnp.zeros_like(au)
            ad[...] = jnp.zeros_like(ad)

        h2 = (x1_ref[...] * (1.0 + vec_ref[V_SC2:V_SC2 + 1, :]) + vec_ref[V_SH2:V_SH2 + 1, :]).astype(BF16)
        gt = gt_ref[...].astype(F32)
        up = up_ref[...].astype(F32)
        sg = _sigmoid(gt)
        sl = gt * sg
        f = (sl * up).astype(BF16)
        dffn = (vec_ref[V_G2:V_G2 + 1, :] * dpre_ref[...]).astype(BF16)
        df = _dot_nt(dffn, wd_ref[...])
        dgt = (df * up * (sg * (1.0 + gt * (1.0 - sg)))).astype(BF16)
        dup = (df * sl).astype(BF16)
        dh2_ref[0] = _dot_nt(dgt, wg_ref[...]) + _dot_nt(dup, wu_ref[...])
        _acc_tn(ag, h2, dgt)
        _acc_tn(au, h2, dup)
        _acc_tn(ad, f, dffn)

        @pl.when(i == nt - 1)
        def _():
            pltpu.sync_copy(ag, dwg_ref.at[j])
            pltpu.sync_copy(au, dwu_ref.at[j])
            pltpu.sync_copy(ad, dwd_ref.at[j])

    any_spec = pl.BlockSpec(memory_space=pl.ANY)
    return pl.pallas_call(
        body,
        name="ffn_bwd",
        grid=(nj, nt),
        in_specs=[
            pl.BlockSpec((TM, D), lambda j, i: (i, 0)),
            pl.BlockSpec((TM, D), lambda j, i: (i, 0)),
            pl.BlockSpec((TM, FFC), lambda j, i: (i, j)),
            pl.BlockSpec((TM, FFC), lambda j, i: (i, j)),
            _resident((NV, D)),
            pl.BlockSpec((D, FFC), lambda j, i: (0, j)),
            pl.BlockSpec((D, FFC), lambda j, i: (0, j)),
            pl.BlockSpec((FFC, D), lambda j, i: (j, 0)),
        ],
        out_specs=[pl.BlockSpec((1, TM, D), lambda j, i: (j, i, 0)), any_spec, any_spec, any_spec],
        out_shape=(
            jax.ShapeDtypeStruct((nj, L, D), F32),
            jax.ShapeDtypeStruct((nj, D, FFC), F32),
            jax.ShapeDtypeStruct((nj, D, FFC), F32),
            jax.ShapeDtypeStruct((nj, FFC, D), F32),
        ),
        scratch_shapes=[pltpu.VMEM((D, FFC), F32), pltpu.VMEM((D, FFC), F32), pltpu.VMEM((FFC, D), F32)],
        compiler_params=_cp(2),
    )(x1, dpre2, gt_b, up_b, vec, w_gate, w_up, w_down)


def _merge_bwd_call(dh2p, dpre2, xe, mix, cat, yf, yb, xbc, z, dpool, pun, vec, w_out, pool_w, scattered):
    L = z.shape[0]
    nt = L // TM
    ns = len(scattered)

    def body(*refs):
        (dh2_ref, dpre2_ref, xe_ref, mix_ref, cat_ref, yf_ref, yb_ref, xs_ref, z_ref, dpool_ref, pun_ref,
         vec_ref, w_ref, pw_ref) = refs[:14]
        dxe_ref, dy_ref, dz_ref, dd_ref, dpw_ref, gacc_ref, dwo_ref = refs[14 + ns:21 + ns]
        dwo_s = refs[21 + 2 * ns]
        ex = _Exchange(refs[14:14 + ns], refs[21 + ns:21 + 2 * ns], *refs[22 + 2 * ns:], False)
        i = pl.program_id(0)

        @pl.when(i == 0)
        def _():
            ex.start()
            gacc_ref[...] = jnp.zeros_like(gacc_ref)
            dpw_ref[...] = jnp.zeros_like(dpw_ref)
            dwo_s[...] = jnp.zeros_like(dwo_s)

        def vrow(r):
            return vec_ref[r:r + 1, :]

        def gadd(r, val):
            gacc_ref[r:r + 1, :] += _rowsum(val)

        dh2 = dh2_ref[0] + dh2_ref[1]
        dx1 = ALPHA * dpre2_ref[...] + dh2 * (1.0 + vrow(V_SC2))
        mix = mix_ref[...]
        x1, n1, rstd1 = _ln(ALPHA * xe_ref[...] + vrow(V_G1) * mix, vrow(V_LN1G), vrow(V_LN1B))
        gadd(V_SC2, dh2 * x1)
        gadd(V_SH2, dh2)
        gadd(V_LN1G, dx1 * n1)
        gadd(V_LN1B, dx1)
        dpre1 = _ln_bwd(dx1, n1, rstd1, vrow(V_LN1G))
        dxe_ref[...] = ALPHA * dpre1
        gadd(V_G1, dpre1 * mix)
        dmix = (vrow(V_G1) * dpre1).astype(BF16)
        dcat = _dot_nt(dmix, w_ref[...])
        cat = cat_ref[...]
        for c0 in range(0, 2 * D, 512):
            dwo_s[c0:c0 + 512, :] += _dot_tn(cat[:, c0:c0 + 512], dmix)
        dyn = dcat[:, 0:D]
        dp = dcat[:, D:2 * D]
        xs = xs_ref[...]
        z = z_ref[...]
        ym, sz, gated, r = _gated(yf_ref, yb_ref, xs, z, vec_ref)
        gadd(V_SSDG, dyn * gated * r)
        a = dyn * vrow(V_SSDG)
        dgated = r * a - gated * (r * r * r * jnp.mean(a * gated, axis=-1, keepdims=True))
        dym = dgated * (z * sz)
        dy_ref[...] = dym
        dz_ref[...] = dgated * ym * (sz * (1.0 + z * (1.0 - sz)))
        gadd(V_DSK, dym * xs)
        gadd(V_PSC, dp * pun_ref[...])
        dps = (dp * vrow(V_PSC)).astype(BF16)
        dpool = dpool_ref[...]
        for g in range(4):
            gs = slice(PG * g, PG * g + PG)
            dd_ref[:, gs] = _dot_nt(dps[:, gs], pw_ref[g])
            dpw_ref[g] += _dot_tn(dpool[:, gs], dps[:, gs])

        @pl.when(i == nt - 1)
        def _():
            pltpu.sync_copy(dwo_s, dwo_ref)
            ex.wait()

    any_spec = pl.BlockSpec(memory_space=pl.ANY)
    return pl.pallas_call(
        body,
        name="merge_bwd",
        grid=(nt,),
        in_specs=[
            pl.BlockSpec((2, TM, D), lambda i: (0, i, 0)),
            _tiles(TM, D), _tiles(TM, D), _tiles(TM, D), _tiles(TM, 2 * D),
            _tiles(TM, D), _tiles(TM, D),
            _tiles(TM, D), _tiles(TM, D), _tiles(TM, D), _tiles(TM, D),
            _resident((NV, D)), _resident((2 * D, D)), _resident((4, PG, PG)),
        ] + [any_spec] * ns,
        out_specs=[_tiles(TM, D), _tiles(TM, D), _tiles(TM, D), _tiles(TM, D),
                   _const_out((4, PG, PG)), _const_out((NV, D)), any_spec] + [any_spec] * ns,
        out_shape=[
            jax.ShapeDtypeStruct((L, D), F32),
            jax.ShapeDtypeStruct((L, D), F32),
            jax.ShapeDtypeStruct((L, D), F32),
            jax.ShapeDtypeStruct((L, D), F32),
            jax.ShapeDtypeStruct((4, PG, PG), F32),
            jax.ShapeDtypeStruct((NV, D), F32),
            jax.ShapeDtypeStruct((2 * D, D), F32),
        ] + _exchange_out_shapes(scattered, False),
        scratch_shapes=[pltpu.VMEM((2 * D, D), F32)] + _exchange_sems(ns),
        compiler_params=_cp(),
    )(dh2p, dpre2, xe, mix, cat, yf, yb, xbc, z, dpool, pun, vec, w_out, pool_w, *scattered)


def _pool_bwd_call(dd, pmat_t, icnt):
    L = dd.shape[0]
    nt = L // PT
    cur, prev, nxt = _pool_specs(nt)
    icur = pl.BlockSpec((PT, 128), lambda i: (i, 0))
    iprev = pl.BlockSpec((PT, 128), lambda i: (jnp.maximum(i - 1, 0), 0))
    inxt = pl.BlockSpec((PT, 128), lambda i: (jnp.minimum(i + 1, nt - 1), 0))

    def body(cur_ref, prev_ref, next_ref, ic_ref, icp_ref, icn_ref, m0_ref, m1_ref, m2_ref, m3_ref, du_ref):
        i = pl.program_id(0)
        n = pl.num_programs(0)
        lane = lax.broadcasted_iota(jnp.int32, (1, 128), 1)
        icv = ic_ref[...]
        for g, m_ref in enumerate((m0_ref, m1_ref, m2_ref, m3_ref)):
            gs = slice(PG * g, PG * g + PG)
            halo = _halo_tokens(g)
            icp = _column(icp_ref[PT - halo:PT, :], lane, g)
            icn = _column(icn_ref[0:halo, :], lane, g)
            top = jnp.where(i > 0, prev_ref[PT - halo:PT, gs] * icp, 0.0)
            bot = jnp.where(i < n - 1, next_ref[0:halo, gs] * icn, 0.0)
            mid = cur_ref[:, gs]
            ext = jnp.concatenate([top, mid * _column(icv, lane, g), bot], axis=0)
            du_ref[:, gs] = _split_dot(m_ref[...], ext) - mid

    return pl.pallas_call(
        body,
        name="pool_bwd",
        grid=(nt,),
        in_specs=[cur, prev, nxt, icur, iprev, inxt] + [_resident(m.shape) for m in pmat_t],
        out_specs=_tiles(PT, D),
        out_shape=jax.ShapeDtypeStruct((L, D), F32),
        compiler_params=_cp(),
    )(dd, dd, dd, icnt, icnt, icnt, *pmat_t)


def _ssd_bwd_call(dy, xbc, dtr, hprev_f, hprev_b, dh_init, par, e_mat, et_mat, dskip, name):
    L = xbc.shape[0]
    nc = L // Q

    def body(dy0_ref, dy1_ref, xbc0_ref, xbc1_ref, dtr0_ref, dtr1_ref, hp0_ref, hp1_ref, dhi_ref, par_ref, e_ref,
             et_ref, dsk_ref, dx0_ref, dx1_ref, ddt0_ref, ddt1_ref, acc_ref, dh0_ref, dh_s, AT, dtT, ddtT_s, dAT_s):
        s = pl.program_id(0)

        @pl.when(s == 0)
        def _():
            dh_s[...] = dhi_ref[...]
            acc_ref[...] = jnp.zeros_like(acc_ref)
            ddtT_s[...] = jnp.zeros_like(ddtT_s)
            dAT_s[...] = jnp.zeros_like(dAT_s)

        one_direction(0, dy0_ref, xbc0_ref, dtr0_ref, hp0_ref, par_ref, e_ref, et_ref, dsk_ref, dx0_ref, ddt0_ref,
                      acc_ref, dh0_ref, dh_s, AT, dtT, ddtT_s, dAT_s)
        one_direction(1, dy1_ref, xbc1_ref, dtr1_ref, hp1_ref, par_ref, e_ref, et_ref, dsk_ref, dx1_ref, ddt1_ref,
                      acc_ref, dh0_ref, dh_s, AT, dtT, ddtT_s, dAT_s)

    def one_direction(d, dy_ref, xbc_ref, dtr_ref, hp_ref, par_ref, e_ref, et_ref, dsk_ref, dxbc_ref, ddtr_ref,
                      acc_ref, dh0_ref, dh_s, AT_s, dtT_s, ddtT_s, dAT_s):
        q = _ssd_common(d, dtr_ref[...], par_ref)
        A, maskf, lane, dt, atot = q["A"], q["maskf"], q["lane"], q["dt"], q["atot"]
        AT_s[d] = q["AT"]
        dtT_s[d] = q["dtT"]
        AT, dtT = AT_s.at[d], dtT_s.at[d]
        hprev = hp_ref[0]
        hpb = hprev.astype(BF16)
        dh = dh_s[d]
        dhb = dh.astype(BF16)
        xs = xbc_ref[:, 0:1024]
        bb = xbc_ref[:, 1024:1152].astype(BF16)
        cb = xbc_ref[:, 1152:1280].astype(BF16)
        dy = dy_ref[...]
        ea_f = _head_expand(e_ref, jnp.exp(A))
        ch = _dot_nt(cb, hpb)
        dch = (dy * ea_f).astype(BF16)
        dC = _dot(dch, hpb)
        dhprev = _dot_tn(dch, cb)
        dA = _head_sum(et_ref, dy * ch * ea_f)
        dec = _state_decay(atot, lane)
        dhprev = dhprev + dh * dec
        dhh = dh * hprev * dec
        datot = jnp.zeros((1, 128), F32)
        for h in range(NH):
            tot_h = jnp.sum(_rowsum(dhh[HP * h:HP * h + HP, :]), axis=1, keepdims=True)
            datot = datot + jnp.where(lane == h, tot_h, 0.0)
        ear = jnp.exp(atot - A)
        wend = ear * dt
        wf = _head_expand(e_ref, wend)
        xw = (xs * wf).astype(BF16)
        dxw = _dot_nt(bb, dhb)
        dB = _dot(xw, dhb)
        dxs = dxw * wf
        dwend = _head_sum(et_ref, dxw * xs)
        ddt = dwend * ear
        de = dwend * wend
        datot = datot + _rowsum(de)
        dA = dA - de
        g = _dot_nt(cb, bb)
        dG = jnp.zeros((Q, Q), F32)
        for k in range(NH // 2):
            ks = slice(128 * k, 128 * k + 128)
            xp = xs[:, ks]
            dyp = dy[:, ks]
            accdx = dxs[:, ks]
            if d == 0:
                accdx = accdx + dyp * dsk_ref[:, ks]
            scts, dyhs = [], []
            for half in range(2):
                h = 2 * k + half
                inhead = (lane >= HP) if half else (lane < HP)
                seg = _column(A, lane, h) - AT[h:h + 1, :]
                lm = jnp.exp(jnp.minimum(seg, 0.0)) * maskf
                dtrow = dtT[h:h + 1, :]
                gl = g * lm
                sc = gl * dtrow
                dyh = jnp.where(inhead, dyp, 0.0).astype(BF16)
                xh = jnp.where(inhead, xp, 0.0).astype(BF16)
                dS = _dot_nt(dyh, xh)
                scts.append(sc.T.astype(BF16))
                dyhs.append(dyh)
                nn = dS * gl
                cn = _rowsum(nn)
                rm = jnp.sum(nn * dtrow, axis=1, keepdims=True)
                dG = dG + dS * (lm * dtrow)
                ddtT_s[d, h:h + 1, :] = cn
                dAT_s[d, h:h + 1, :] = -(cn * dtrow)
                dA = dA + rm * (lane == h).astype(F32)
            dxbc_ref[:, ks] = accdx + _dot(jnp.concatenate(scts, axis=1), jnp.concatenate(dyhs, axis=0))
        dGb = dG.astype(BF16)
        dxbc_ref[:, 1024:1152] = dB + _dot_tn(dGb, cb)
        dxbc_ref[:, 1152:1280] = dC + _dot(dGb, bb)
        da = _dot_tn(maskf, dA + dAT_s[d].T, HI) + datot
        ddt = ddt + ddtT_s[d].T + da * q["aneg"]
        ddtr = jnp.where(q["hmask"], ddt * _sigmoid(q["pre"]), 0.0)
        ddtr_ref[...] = ddtr
        acc_ref[d, 0:1, :] += _rowsum(ddtr)
        acc_ref[d, 1:2, :] += _rowsum(da * dt) * q["aneg"]
        dh_s[d] = dhprev
        dh0_ref[d] = dhprev

    def back(s):
        return nc - 1 - s

    return pl.pallas_call(
        body,
        name=name,
        grid=(nc,),
        in_specs=[
            pl.BlockSpec((Q, D), lambda s: (back(s), 0)),
            pl.BlockSpec((Q, D), lambda s: (s, 0)),
            pl.BlockSpec((Q, DXBC), lambda s: (back(s), 0)),
            pl.BlockSpec((Q, DXBC), lambda s: (s, 0)),
            pl.BlockSpec((Q, 128), lambda s: (back(s), 0)),
            pl.BlockSpec((Q, 128), lambda s: (s, 1)),
            pl.BlockSpec((1, D, NS), lambda s: (back(s), 0, 0)),
            pl.BlockSpec((1, D, NS), lambda s: (s, 0, 0)),
            _resident((2, D, NS)),
            _resident((8, 128)),
            _resident((256, D)),
            _resident((D, 256)),
            _resident((1, D)),
        ],
        out_specs=[
            pl.BlockSpec((Q, DXBC), lambda s: (back(s), 0)),
            pl.BlockSpec((Q, DXBC), lambda s: (s, 0)),
            pl.BlockSpec((Q, 128), lambda s: (back(s), 0)),
            pl.BlockSpec((Q, 128), lambda s: (s, 0)),
            _const_out((2, 8, 128)),
            _const_out((2, D, NS)),
        ],
        out_shape=(
            jax.ShapeDtypeStruct((L, DXBC), F32),
            jax.ShapeDtypeStruct((L, DXBC), F32),
            jax.ShapeDtypeStruct((L, 128), F32),
            jax.ShapeDtypeStruct((L, 128), F32),
            jax.ShapeDtypeStruct((2, 8, 128), F32),
            jax.ShapeDtypeStruct((2, D, NS), F32),
        ),
        scratch_shapes=[pltpu.VMEM((2, D, NS), F32)] + [pltpu.VMEM((2, 128, Q), F32)] * 4,
        compiler_params=_cp(),
    )(dy, dy, xbc, xbc, dtr, dtr, hprev_f, hprev_b, dh_init, par, e_mat, et_mat, dskip)


def _conv_bwd_call(dxf, dxb, xbc_raw, cw, acc_init, name, scattered=()):
    L = xbc_raw.shape[0]
    nt = L // TM
    ns = len(scattered)
    prev, nxt = _halo_specs(TM, DXBC, L)

    def body(*refs):
        dxf_ref, dxb_ref, cur_ref, prev_ref, next_ref, cw_ref, init_ref = refs[:7]
        dpre_ref, acc_ref = refs[7 + ns:9 + ns]
        ext = refs[9 + 2 * ns]
        if ns:
            ex = _Exchange(refs[7:7 + ns], refs[9 + ns:9 + 2 * ns], *refs[10 + 2 * ns:], False)

        @pl.when(pl.program_id(0) == 0)
        def _():
            if ns:
                ex.start()
            acc_ref[...] = init_ref[...]

        _extended(ext, cur_ref, prev_ref, next_ref)
        taps = [_shifted(ext, k - 2, TM) for k in range(5)]
        pre = cw_ref[5:6, :] + cw_ref[0:1, :] * taps[0]
        for k in range(1, 5):
            pre = pre + cw_ref[k:k + 1, :] * taps[k]
        sg = _sigmoid(pre)
        dpre = (dxf_ref[...] + dxb_ref[...]) * (sg * (1.0 + pre * (1.0 - sg)))
        dpre_ref[...] = dpre
        for k in range(5):
            acc_ref[k:k + 1, :] += _rowsum(dpre * taps[k])
        acc_ref[5:6, :] += _rowsum(dpre)
        if ns:
            @pl.when(pl.program_id(0) == nt - 1)
            def _():
                ex.wait()

    any_spec = pl.BlockSpec(memory_space=pl.ANY)
    return pl.pallas_call(
        body,
        name=name,
        grid=(nt,),
        in_specs=[_tiles(TM, DXBC), _tiles(TM, DXBC), _tiles(TM, DXBC), prev, nxt,
                  _resident((8, DXBC)), _resident((8, DXBC))] + [any_spec] * ns,
        out_specs=[_tiles(TM, DXBC), _const_out((8, DXBC))] + [any_spec] * ns,
        out_shape=[jax.ShapeDtypeStruct((L, DXBC), F32), jax.ShapeDtypeStruct((8, DXBC), F32)]
        + _exchange_out_shapes(scattered, False),
        scratch_shapes=[pltpu.VMEM((TM + 16, DXBC), F32)] + (_exchange_sems(ns) if ns else []),
        compiler_params=_cp(),
    )(dxf, dxb, xbc_raw, xbc_raw, xbc_raw, cw, acc_init, *scattered)


def _inproj_bwd_call(dpre, cw, dz, ddt0, ddt1, dup, h1, dxe_part, x0, vec, w_in, dw_init, name):
    L = x0.shape[0]
    nt = L // TM
    prev, nxt = _halo_specs(TM, DXBC, L)

    def body(cur_ref, prev_ref, next_ref, cw_ref, dz_ref, ddt0_ref, ddt1_ref, dup_ref, h1_ref, dxe_ref, x0_ref,
             vec_ref, w_ref, dwi_ref, gx_ref, gacc_ref, dw_ref, dw_s, ext):
        i = pl.program_id(0)

        @pl.when(i == 0)
        def _():
            gacc_ref[...] = jnp.zeros_like(gacc_ref)
            pltpu.sync_copy(dwi_ref, dw_s)

        def vrow(r):
            return vec_ref[r:r + 1, :]

        _extended(ext, cur_ref, prev_ref, next_ref)
        dxr = cw_ref[0:1, :] * _shifted(ext, 2, TM)
        for k in range(1, 5):
            dxr = dxr + cw_ref[k:k + 1, :] * _shifted(ext, 2 - k, TM)
        dproj = jnp.concatenate([dz_ref[...], dxr, ddt0_ref[...], ddt1_ref[...], dup_ref[...]], axis=1).astype(BF16)
        dh1 = _dot_nt(dproj, w_ref[...])
        _acc_tn(dw_s, h1_ref[...], dproj)
        xe, n0, rstd0 = _ln(x0_ref[...], vrow(V_EMBG), vrow(V_EMBB))
        dxe = dxe_ref[...] + dh1 * (1.0 + vrow(V_SC1))
        gacc_ref[V_SC1:V_SC1 + 1, :] += _rowsum(dh1 * xe)
        gacc_ref[V_SH1:V_SH1 + 1, :] += _rowsum(dh1)
        gacc_ref[V_EMBG:V_EMBG + 1, :] += _rowsum(dxe * n0)
        gacc_ref[V_EMBB:V_EMBB + 1, :] += _rowsum(dxe)
        gx_ref[...] = _ln_bwd(dxe, n0, rstd0, vrow(V_EMBG))

        @pl.when(i == nt - 1)
        def _():
            pltpu.sync_copy(dw_s, dw_ref)

    any_spec = pl.BlockSpec(memory_space=pl.ANY)
    return pl.pallas_call(
        body,
        name=name,
        grid=(nt,),
        in_specs=[_tiles(TM, DXBC), prev, nxt, _resident((8, DXBC)), _tiles(TM, D), _tiles(TM, 128), _tiles(TM, 128),
                  _tiles(TM, D), _tiles(TM, D), _tiles(TM, D), _tiles(TM, D), _resident((NV, D)), _resident((D, WIN)),
                  any_spec],
        out_specs=[_tiles(TM, D), _const_out((NV, D)), any_spec],
        out_shape=(
            jax.ShapeDtypeStruct((L, D), F32),
            jax.ShapeDtypeStruct((NV, D), F32),
            jax.ShapeDtypeStruct((D, WIN), F32),
        ),
        scratch_shapes=[pltpu.VMEM((D, WIN), F32), pltpu.VMEM((TM + 16, DXBC), F32)],
        compiler_params=_cp(),
    )(dpre, dpre, dpre, cw, dz, ddt0, ddt1, dup, h1, dxe_part, x0, vec, w_in, dw_init)


def _adamw(w, g, m, v):
    m = ADAM_B1 * m + (1.0 - ADAM_B1) * g
    v = ADAM_B2 * v + (1.0 - ADAM_B2) * (g * g)
    m_hat = m / (1.0 - ADAM_B1 ** ADAM_STEP)
    v_hat = v / (1.0 - ADAM_B2 ** ADAM_STEP)
    delta = -ADAM_LR * (m_hat / (jnp.sqrt(v_hat) + ADAM_EPS) + ADAM_WD * w)
    return delta, m, v


def _adamw_shard_call(gslots, w, m, v, tr, name):
    R, C = w.shape

    def body(gs_ref, w_ref, m_ref, v_ref, g_ref, d_ref, mo_ref, vo_ref):
        g = gs_ref[0].astype(F32)
        for i in range(1, NDEV):
            g = g + gs_ref[i].astype(F32)
        delta, mn, vn = _adamw(w_ref[...], g, m_ref[...], v_ref[...])
        g_ref[...] = g
        d_ref[...] = delta
        mo_ref[...] = mn
        vo_ref[...] = vn

    t = _tiles(tr, C)
    return pl.pallas_call(
        body,
        name=name,
        grid=(R // tr,),
        in_specs=[pl.BlockSpec((NDEV, tr, C), lambda i: (0, i, 0)), t, t, t],
        out_specs=[t, t, t, t],
        out_shape=tuple(jax.ShapeDtypeStruct((R, C), F32) for _ in range(4)),
        compiler_params=_cp(),
    )(gslots, w, m, v)


def _wada_call(dm_ex, dm_ctx, silu_all, w, m, v):
    ncol = w.shape[1]

    def body(dme_ref, dmc_ref, s_ref, w_ref, m_ref, v_ref, g_ref, d_ref, mo_ref, vo_ref, ds_ref):
        dmc = _rowsum(dmc_ref[...])
        rows = lax.broadcasted_iota(jnp.int32, (8, 1), 0)
        low = jnp.where(rows == 0, dmc, 0.0)
        dm = jnp.concatenate([dme_ref[...], low], axis=0).astype(BF16)
        wv = w_ref[...]
        g = _dot_tn(s_ref[...].astype(BF16), dm)
        delta, mn, vn = _adamw(wv, g, m_ref[...], v_ref[...])
        g_ref[...] = g
        d_ref[...] = delta
        mo_ref[...] = mn
        vo_ref[...] = vn
        ds_ref[...] = _dot_nt(low.astype(BF16), wv.astype(BF16))

    return pl.pallas_call(
        body,
        name="wada_update",
        out_shape=tuple(jax.ShapeDtypeStruct((D, ncol), F32) for _ in range(4)) + (jax.ShapeDtypeStruct((8, D), F32),),
        compiler_params=pltpu.CompilerParams(vmem_limit_bytes=VMEM_LIMIT),
    )(dm_ex, dm_ctx, silu_all, w, m, v)


P_DMOD, P_DMODC, P_EMBG, P_EMBB, P_LN1G, P_LN1B, P_LN2G, P_LN2B = 0, 6, 8, 9, 10, 11, 12, 13
P_SSDG, P_PSC, P_DSK, P_CONVB, P_DTB, P_ALOG, P_LOSS, NP = 14, 15, 16, 17, 19, 20, 21, 24
S_CCTX, S_EMBG, S_EMBB, S_BADA, S_CONVB, S_DTB, S_ALOG, S_DSK = 0, 1, 2, 3, 9, 11, 12, 13
S_SSDG, S_PSC, S_LN1G, S_LN1B, S_LN2G, S_LN2B, NSM = 14, 15, 16, 17, 18, 19, 24


def _small_update_call(pall, dsil, cctx, w, m, v, et_mat):
    def body(p_ref, ds_ref, c_ref, w_ref, m_ref, v_ref, et_ref, g_ref, d_ref, mo_ref, vo_ref, loss_ref,
             tot, dsum, dsk8):
        tot[...] = p_ref[0]
        dsum[...] = ds_ref[0]
        for i in range(1, NDEV):
            tot[...] += p_ref[i]
            dsum[...] += ds_ref[i]
        cv = c_ref[...]
        sc = _sigmoid(cv)
        g_ref[...] = jnp.zeros_like(g_ref)
        g_ref[S_CCTX:S_CCTX + 1, :] = dsum[0:1, :] * (sc * (1.0 + cv * (1.0 - sc)))
        g_ref[S_EMBG:S_EMBG + 1, :] = tot[P_EMBG:P_EMBG + 1, :]
        g_ref[S_EMBB:S_EMBB + 1, :] = tot[P_EMBB:P_EMBB + 1, :]
        g_ref[S_BADA:S_BADA + 2, :] = tot[P_DMOD:P_DMOD + 2, :] + tot[P_DMODC:P_DMODC + 2, :]
        g_ref[S_BADA + 2:S_BADA + 6, :] = tot[P_DMOD + 2:P_DMOD + 6, :]
        g_ref[S_CONVB:S_CONVB + 2, :] = tot[P_CONVB:P_CONVB + 2, :]
        g_ref[S_DTB:S_DTB + 1, :] = tot[P_DTB:P_DTB + 1, :]
        g_ref[S_ALOG:S_ALOG + 1, :] = tot[P_ALOG:P_ALOG + 1, :]
        dsk8[...] = _dot(jnp.broadcast_to(tot[P_DSK:P_DSK + 1, :], (8, D)), et_ref[:, 0:128].astype(F32), HI)
        g_ref[S_DSK:S_DSK + 1, 0:128] = dsk8[0:1, :]
        g_ref[S_SSDG:S_SSDG + 1, :] = tot[P_SSDG:P_SSDG + 1, :]
        g_ref[S_PSC:S_PSC + 1, :] = tot[P_PSC:P_PSC + 1, :]
        g_ref[S_LN1G:S_LN1G + 1, :] = tot[P_LN1G:P_LN1G + 1, :]
        g_ref[S_LN1B:S_LN1B + 1, :] = tot[P_LN1B:P_LN1B + 1, :]
        g_ref[S_LN2G:S_LN2G + 1, :] = tot[P_LN2G:P_LN2G + 1, :]
        g_ref[S_LN2B:S_LN2B + 1, :] = tot[P_LN2B:P_LN2B + 1, :]
        delta, mn, vn = _adamw(w_ref[...], g_ref[...], m_ref[...], v_ref[...])
        d_ref[...] = delta
        mo_ref[...] = mn
        vo_ref[...] = vn
        loss_ref[...] = jnp.broadcast_to(tot[P_LOSS:P_LOSS + 1, 0:128], (8, 128))

    return pl.pallas_call(
        body,
        name="small_update",
        out_shape=tuple(jax.ShapeDtypeStruct((NSM, D), F32) for _ in range(4)) + (jax.ShapeDtypeStruct((8, 128), F32),),
        scratch_shapes=[pltpu.VMEM((NP, D), F32), pltpu.VMEM((8, D), F32), pltpu.VMEM((8, 128), F32)],
        compiler_params=pltpu.CompilerParams(vmem_limit_bytes=VMEM_LIMIT),
    )(pall, dsil, cctx, w, m, v, et_mat)


def _pad_rows(flat, mult=16):
    n = flat.shape[0]
    rows = -(-n // D)
    rows = -(-rows // mult) * mult
    return jnp.pad(flat, (0, rows * D - n)).reshape(rows, D)


def _by_cols(dw):
    r = dw.shape[0]
    return jnp.transpose(dw.reshape(r, NDEV, -1), (1, 0, 2))


def _from_cols(g):
    return jnp.transpose(g, (1, 0, 2)).reshape(g.shape[1], -1)


def _pool_constants(L):
    rows = L // GW
    t_r = jnp.arange(PT) // GW
    t_c = jnp.arange(PT) % GW
    fw, bw, ic = [], [], []
    pos_r = jnp.arange(L) // GW
    pos_c = jnp.arange(L) % GW
    for g, w in enumerate(WINDOWS):
        lo, hi = -(w // 2), w - w // 2 - 1
        n_ext = PT + 2 * _halo_tokens(g)
        e_r = jnp.arange(n_ext) // GW - w // 2
        e_c = jnp.arange(n_ext) % GW
        dr = e_r[None, :] - t_r[:, None]
        dc = e_c[None, :] - t_c[:, None]
        fw.append(((dr >= lo) & (dr <= hi) & (dc >= lo) & (dc <= hi)).astype(BF16))
        bw.append(((-dr >= lo) & (-dr <= hi) & (-dc >= lo) & (-dc <= hi)).astype(BF16))
        cr = jnp.minimum(pos_r + hi, rows - 1) - jnp.maximum(pos_r + lo, 0) + 1
        cc = jnp.minimum(pos_c + hi, GW - 1) - jnp.maximum(pos_c + lo, 0) + 1
        ic.append(1.0 / (cr * cc).astype(F32))
    icnt = jnp.pad(jnp.stack(ic, axis=1), ((0, 0), (0, 124)))
    return fw, bw, icnt


def _head_matrices():
    hp = jnp.arange(D) // HP
    e = (jnp.arange(128)[:, None] == hp[None, :]).astype(BF16)
    return jnp.concatenate([e, e], axis=0), jnp.concatenate([e.T, e.T], axis=1)


def _aligned_in_proj(w):
    zpad = jnp.zeros((D, 128 - NH), w.dtype)
    return jnp.concatenate([w[:, 0:2304], w[:, 2304:2320], zpad, w[:, 2320:2336], zpad, w[:, 2336:3360]], axis=1)


def _unaligned_in_proj(dw):
    return jnp.concatenate([dw[:, 0:2304], dw[:, 2304:2320], dw[:, 2432:2448], dw[:, 2560:3584]], axis=1)


def _row(v):
    return v.reshape(1, -1).astype(F32)


def _pad_lanes(v, width=D):
    v = v.reshape(1, -1)
    return jnp.pad(v, ((0, 0), (0, width - v.shape[1])))


def kernel(x, c, ctx, c_ctx, emb_ln_g, emb_ln_b, w_ada, b_ada, in_proj, conv_w, conv_b, dt_bias, a_log, d_skip, ssd_norm_g, pool_w, pool_scale, w_out, ln1_g, ln1_b, w_gate, w_up, w_down, ln2_g, ln2_b, loss_target, m_c_ctx, m_emb_ln_g, m_emb_ln_b, m_w_ada, m_b_ada, m_in_proj, m_conv_w, m_conv_b, m_dt_bias, m_a_log, m_d_skip, m_ssd_norm_g, m_pool_w, m_pool_scale, m_w_out, m_ln1_g, m_ln1_b, m_w_gate, m_w_up, m_w_down, m_ln2_g, m_ln2_b, v_c_ctx, v_emb_ln_g, v_emb_ln_b, v_w_ada, v_b_ada, v_in_proj, v_conv_w, v_conv_b, v_dt_bias, v_a_log, v_d_skip, v_ssd_norm_g, v_pool_w, v_pool_scale, v_w_out, v_ln1_g, v_ln1_b, v_w_gate, v_w_up, v_w_down, v_ln2_g, v_ln2_b):
    me = 4 * lax.axis_index("x") + 2 * lax.axis_index("y") + lax.axis_index("c")
    x0 = x[0]
    ctx0 = ctx[0]
    tgt = loss_target[0]
    L = x0.shape[0]
    LC = ctx0.shape[0]
    ncol_ada = w_ada.shape[2]

    small_in = jnp.concatenate([c.reshape(-1), conv_w.reshape(-1)])
    gb = jnp.concatenate([_row(emb_ln_g), _row(emb_ln_b), jnp.zeros((6, D), F32)], axis=0)
    xe, small_all, g_inp = _emb_ln_call(x0, gb, "emb_ln", [_pad_rows(small_in, 8), in_proj[0].astype(BF16)])
    (xe_c,) = _emb_ln_call(ctx0, gb, "emb_ln_ctx")
    c_all = small_all[:, 0, :]
    convw_all = small_all.reshape(NDEV, -1)[:, D:D + 5 * (DXBC // NDEV)].reshape(NDEV, 5, DXBC // NDEV)
    conv_w_full = _from_cols(convw_all)
    w_in = _aligned_in_proj(_from_cols(g_inp))
    late_shards = [pool_w[0].astype(BF16), w_out[0].astype(BF16), w_gate[0].astype(BF16), w_up[0].astype(BF16),
                   w_down[0].astype(BF16)]

    c_in = jnp.concatenate([c_all, c_ctx.reshape(1, D), jnp.zeros((7, D), F32)], axis=0)
    b_mine = lax.dynamic_slice(b_ada, (0, me * ncol_ada), (1, ncol_ada))
    silu_all, mod_mine = _mod_call(c_in, w_ada[0], b_mine)
    (mod_all,) = _exchange([mod_mine], "gather_mod", True)
    mod_all = _from_cols(mod_all)
    mod_me = lax.dynamic_slice(mod_all, (me, 0), (1, 6 * D)).reshape(6, D)
    mod_ctx = mod_all[8].reshape(6, D)

    tail = jnp.concatenate([
        _row(emb_ln_g), _row(emb_ln_b), _row(ln1_g), _row(ln1_b), _row(ln2_g), _row(ln2_b),
        _row(ssd_norm_g), _row(pool_scale), _row(jnp.repeat(d_skip.reshape(-1), HP)), jnp.zeros((1, D), F32)], axis=0)
    vec = jnp.concatenate([mod_me, tail], axis=0)
    vec_ctx = jnp.concatenate([mod_ctx, tail], axis=0)

    cw = jnp.concatenate([conv_w_full, conv_b.reshape(1, DXBC), jnp.zeros((2, DXBC), F32)], axis=0)
    par = jnp.concatenate([_pad_lanes(dt_bias[0, 0], 128), _pad_lanes(dt_bias[0, 1], 128),
                           _pad_lanes(a_log[0, 0], 128), _pad_lanes(a_log[0, 1], 128),
                           jnp.zeros((4, 128), F32)], axis=0)
    e_mat, et_mat = _head_matrices()
    pmat, pmat_t, icnt = _pool_constants(L)
    dskip_row = vec[V_DSK:V_DSK + 1]

    h1_c, _, xbcr_c, dtr_c, _ = _f1_call(xe_c, vec_ctx, w_in, "inproj_fwd_ctx")
    xbc_c = _f2_call(xbcr_c, cw, "conv_fwd_ctx")
    hzero = jnp.zeros((2, D, NS), F32)
    _, _, hpf_c, hpb_c, hfin_c = _ssd_fwd_call(xbc_c, dtr_c, hzero, par, e_mat, et_mat, "ssd_fwd_ctx")

    h1, z, xbcr, dtr, up = _f1_call(xe, vec, w_in, "inproj_fwd")
    xbc = _f2_call(xbcr, cw, "conv_fwd")
    yf, yb, hpf, hpb, _, g_pw, g_wo, g_wg, g_wu, g_wd = _ssd_fwd_call(xbc, dtr, hfin_c, par, e_mat, et_mat, "ssd_fwd",
                                                                       late_shards)
    pool_w_full = jnp.transpose(g_pw, (1, 0, 2, 3)).reshape(4, PG, PG)
    w_out_full = g_wo.reshape(2 * D, D)
    w_gate_full = _from_cols(g_wg)
    w_up_full = _from_cols(g_wu)
    w_down_full = g_wd.reshape(DFF, D)
    dpool, pun = _pool_fwd_call(up, pmat, icnt, pool_w_full)
    x1, mix, cat = _merge_call(yf, yb, xbc, z, pun, xe, vec, w_out_full)
    dpre2, gacc_f, gt_b, up_b = _ffn_fwd_call(x1, tgt, vec, w_gate_full, w_up_full, w_down_full)

    dh2p, dwg2, dwu2, dwd2 = _ffn_bwd_call(x1, dpre2, gt_b, up_b, vec, w_gate_full, w_up_full, w_down_full)
    nq = FFC // (DFF // NDEV)
    ffn_parts = [
        jnp.transpose(dwg2.reshape(-1, D, nq, DFF // NDEV), (0, 2, 1, 3)).reshape(NDEV, D, DFF // NDEV),
        jnp.transpose(dwu2.reshape(-1, D, nq, DFF // NDEV), (0, 2, 1, 3)).reshape(NDEV, D, DFF // NDEV),
        dwd2.reshape(NDEV, DFF // NDEV, D)]
    dxe_part, dy, dz, dd, dpw, gacc_m, dwo, gs_wg, gs_wu, gs_wd = _merge_bwd_call(
        dh2p, dpre2, xe, mix, cat, yf, yb, xbc, z, dpool, pun, vec, w_out_full, pool_w_full, ffn_parts)
    dup = _pool_bwd_call(dd, pmat_t, icnt)
    dxf, dxb, ddt0, ddt1, sacc, dh0 = _ssd_bwd_call(dy, xbc, dtr, hpf, hpb, hzero, par, e_mat, et_mat, dskip_row,
                                                     "ssd_bwd")
    zeros_c = jnp.zeros((LC, D), F32)
    dxf_c, dxb_c, ddt0_c, ddt1_c, sacc_c, _ = _ssd_bwd_call(zeros_c, xbc_c, dtr_c, hpf_c, hpb_c, dh0, par, e_mat, et_mat,
                                                            jnp.zeros((1, D), F32), "ssd_bwd_ctx")
    dprec_c, cacc_c = _conv_bwd_call(dxf_c, dxb_c, xbcr_c, cw, jnp.zeros((8, DXBC), F32), "conv_bwd_ctx")
    _, gacc_c, dwin_c = _inproj_bwd_call(dprec_c, cw, zeros_c, ddt0_c, ddt1_c, zeros_c, h1_c, zeros_c, ctx0, vec_ctx,
                                         w_in, jnp.zeros((D, WIN), F32), "inproj_bwd_ctx")
    mix_parts = [dwo.reshape(NDEV, 2 * D // NDEV, D),
                 jnp.transpose(dpw.reshape(4, NDEV, PG // NDEV, PG), (1, 0, 2, 3)).reshape(NDEV, 4 * PG // NDEV, PG)]
    dprec, cacc, gs_wo, gs_pw = _conv_bwd_call(dxf, dxb, xbcr, cw, cacc_c, "conv_bwd", mix_parts)
    grad_x, gacc_i, dwin = _inproj_bwd_call(dprec, cw, dz, ddt0, ddt1, dup, h1, dxe_part, x0, vec, w_in, dwin_c,
                                            "inproj_bwd")

    gsum = gacc_f + gacc_m + gacc_i
    sa = sacc + sacc_c
    dtb_row = _pad_lanes(jnp.concatenate([sa[0, 0, 0:NH], sa[1, 0, 0:NH]]))
    alog_row = _pad_lanes(jnp.concatenate([sa[0, 1, 0:NH], sa[1, 1, 0:NH]]))
    convb_rows = jnp.pad(cacc[5], (0, 2 * D - DXBC)).reshape(2, D)
    pack = jnp.concatenate([
        gsum[V_SH1:V_G2 + 1],
        gacc_c[V_SH1:V_SC1 + 1],
        gsum[V_EMBG:V_EMBB + 1] + gacc_c[V_EMBG:V_EMBB + 1],
        gsum[V_LN1G:V_LN2B + 1],
        gsum[V_SSDG:V_DSK + 1],
        convb_rows, dtb_row, alog_row,
        gsum[V_LOSS:V_LOSS + 1],
        jnp.zeros((NP - 22, D), F32)], axis=0)
    (pall,) = _exchange([pack], "gather_small_grads", True)

    dm_flat = pall[:, 0:8, :].reshape(NDEV, 8 * D)
    dm_ex = lax.dynamic_slice(dm_flat, (0, me * ncol_ada), (NDEV, ncol_ada))
    dmc_full = jnp.concatenate([dm_flat[:, 6 * D:8 * D], jnp.zeros((NDEV, 4 * D), F32)], axis=1)
    dm_ctx = lax.dynamic_slice(dmc_full, (0, me * ncol_ada), (NDEV, ncol_ada))
    g_wada, d_wada, nm_wada, nv_wada, dsil = _wada_call(dm_ex, dm_ctx, silu_all, w_ada[0], m_w_ada[0], v_w_ada[0])
    (dsil_all,) = _exchange([dsil], "gather_dsilu", True)

    def small_pack(cc, eg, eb, ba, cb_, dtb, al, dsk, sg, ps, l1g, l1b, l2g, l2b):
        return jnp.concatenate([
            _row(cc), _row(eg), _row(eb), ba.reshape(6, D), jnp.pad(cb_.reshape(-1), (0, 2 * D - DXBC)).reshape(2, D),
            _pad_lanes(dtb.reshape(-1)), _pad_lanes(al.reshape(-1)), _pad_lanes(dsk.reshape(-1)),
            _row(sg), _row(ps), _row(l1g), _row(l1b), _row(l2g), _row(l2b), jnp.zeros((NSM - 20, D), F32)], axis=0)

    sw = small_pack(c_ctx, emb_ln_g, emb_ln_b, b_ada, conv_b, dt_bias, a_log, d_skip, ssd_norm_g, pool_scale,
                    ln1_g, ln1_b, ln2_g, ln2_b)
    sm = small_pack(m_c_ctx, m_emb_ln_g, m_emb_ln_b, m_b_ada, m_conv_b, m_dt_bias, m_a_log, m_d_skip, m_ssd_norm_g,
                    m_pool_scale, m_ln1_g, m_ln1_b, m_ln2_g, m_ln2_b)
    sv = small_pack(v_c_ctx, v_emb_ln_g, v_emb_ln_b, v_b_ada, v_conv_b, v_dt_bias, v_a_log, v_d_skip, v_ssd_norm_g,
                    v_pool_scale, v_ln1_g, v_ln1_b, v_ln2_g, v_ln2_b)
    s_g, s_d, s_m, s_v, loss8 = _small_update_call(pall, dsil_all, _row(c_ctx), sw, sm, sv, et_mat)

    def small_unpack(t):
        return (t[S_CCTX], t[S_EMBG], t[S_EMBB], t[S_BADA:S_BADA + 6].reshape(1, 6 * D),
                t[S_CONVB:S_CONVB + 2].reshape(-1)[:DXBC].reshape(1, DXBC),
                t[S_DTB, 0:2 * NH].reshape(1, 2, NH), t[S_ALOG, 0:2 * NH].reshape(1, 2, NH), t[S_DSK, 0:NH].reshape(1, NH),
                t[S_SSDG].reshape(1, D), t[S_PSC].reshape(1, D), t[S_LN1G].reshape(1, D), t[S_LN1B].reshape(1, D),
                t[S_LN2G].reshape(1, D), t[S_LN2B].reshape(1, D))

    gs_inp, gs_cw = _exchange([_by_cols(_unaligned_in_proj(dwin)).astype(BF16), _by_cols(cacc[0:5])],
                              "exchange_last_grads", False)

    pshape = (4 * PG // NDEV, PG)
    u_inp = _adamw_shard_call(gs_inp, in_proj[0], m_in_proj[0], v_in_proj[0], 256, "adamw_in_proj")
    u_cw = _adamw_shard_call(gs_cw, conv_w[0], m_conv_w[0], v_conv_w[0], 5, "adamw_conv_w")
    u_pw = _adamw_shard_call(gs_pw, pool_w[0].reshape(pshape), m_pool_w[0].reshape(pshape), v_pool_w[0].reshape(pshape),
                             pshape[0], "adamw_pool_w")
    u_wo = _adamw_shard_call(gs_wo, w_out[0], m_w_out[0], v_w_out[0], 64, "adamw_w_out")
    u_wg = _adamw_shard_call(gs_wg, w_gate[0], m_w_gate[0], v_w_gate[0], 256, "adamw_w_gate")
    u_wu = _adamw_shard_call(gs_wu, w_up[0], m_w_up[0], v_w_up[0], 256, "adamw_w_up")
    u_wd = _adamw_shard_call(gs_wd, w_down[0], m_w_down[0], v_w_down[0], 88, "adamw_w_down")

    def assemble(k, small, wada):
        (cc, eg, eb, ba, cb_, dtb, al, dsk, sg, ps, l1g, l1b, l2g, l2b) = small_unpack(small)
        pw = u_pw[k].reshape(1, 4, PG // NDEV, PG)
        return (cc, eg, eb, wada[None], ba, u_inp[k][None], u_cw[k][None], cb_, dtb, al, dsk, sg, pw, ps,
                u_wo[k][None], l1g, l1b, u_wg[k][None], u_wu[k][None], u_wd[k][None], l2g, l2b)

    loss = loss8[0, 0]
    return (loss, grad_x[None], *assemble(0, s_g, g_wada), *assemble(1, s_d, d_wada),
            *assemble(2, s_m, nm_wada), *assemble(3, s_v, nv_wada))
```

```python
import functools
import math

import jax
import jax.numpy as jnp
from jax import lax
from jax.experimental import pallas as pl
from jax.experimental.pallas import tpu as pltpu

F32 = jnp.float32
BF16 = jnp.bfloat16
HI = lax.Precision.HIGHEST

NDEV = 8
D = 1024
NH = 16
HP = 64
NS = 128
Q = 128
DXBC = 1280
DFF = 2816
FFC = 1408
GW = 64
PR = 8
PT = PR * GW
WINDOWS = (2, 4, 8, 16)
PG = 256
DIN = 3360
WIN = 3584
ALPHA = 2.0 ** 0.25
LN_EPS = 1e-5
TM = 256

ADAM_LR = 0.001
ADAM_B1 = 0.9
ADAM_B2 = 0.999
ADAM_EPS = 1e-08
ADAM_WD = 0.01
ADAM_STEP = 10

V_SH1, V_SC1, V_G1, V_SH2, V_SC2, V_G2 = 0, 1, 2, 3, 4, 5
V_EMBG, V_EMBB, V_LN1G, V_LN1B, V_LN2G, V_LN2B = 6, 7, 8, 9, 10, 11
V_SSDG, V_PSC, V_DSK, V_LOSS = 12, 13, 14, 15
NV = 16

VMEM_LIMIT = 60 * 1024 * 1024


def _cp(ndim=1):
    return pltpu.CompilerParams(dimension_semantics=("arbitrary",) * ndim, vmem_limit_bytes=VMEM_LIMIT)


def _dot(a, b, precision=None):
    return jnp.dot(a, b, preferred_element_type=F32, precision=precision)


def _dot_nt(a, b):
    return lax.dot_general(a, b, (((1,), (1,)), ((), ())), preferred_element_type=F32)


def _dot_tn(a, b, precision=None):
    return lax.dot_general(a, b, (((0,), (0,)), ((), ())), preferred_element_type=F32, precision=precision)


def _split2(x):
    hi = x.astype(BF16)
    return hi, (x - hi.astype(F32)).astype(BF16)


def _split_dot(m, x):
    hi, lo = _split2(x)
    return _dot(m, hi) + _dot(m, lo)


def _dot_split(x, m):
    hi, lo = _split2(x)
    return _dot(hi, m) + _dot(lo, m)


def _sigmoid(x):
    return 1.0 / (1.0 + jnp.exp(-x))


def _softplus(x):
    return jnp.maximum(x, 0.0) + jnp.log(1.0 + jnp.exp(-jnp.abs(x)))


def _ln(x, g, b):
    mu = jnp.mean(x, axis=-1, keepdims=True)
    xc = x - mu
    var = jnp.mean(xc * xc, axis=-1, keepdims=True)
    rstd = lax.rsqrt(var + LN_EPS)
    n = xc * rstd
    return n * g + b, n, rstd


def _ln_bwd(dy, n, rstd, g):
    dn = dy * g
    return rstd * (dn - jnp.mean(dn, axis=-1, keepdims=True) - n * jnp.mean(dn * n, axis=-1, keepdims=True))


def _rowsum(x):
    return jnp.sum(x, axis=0, keepdims=True)


def _resident(shape):
    nd = len(shape)
    return pl.BlockSpec(shape, lambda *_: (0,) * nd, pipeline_mode=pl.Buffered(1))


def _const_out(shape):
    nd = len(shape)
    return pl.BlockSpec(shape, lambda *_: (0,) * nd)


def _tiles(tm, width):
    return pl.BlockSpec((tm, width), lambda i: (i, 0))


def _halo_specs(tm, width, n_rows):
    r = tm // 8
    last = n_rows // 8 - 1
    prev = pl.BlockSpec((8, width), lambda i: (jnp.maximum(i * r - 1, 0), 0))
    nxt = pl.BlockSpec((8, width), lambda i: (jnp.minimum((i + 1) * r, last), 0))
    return prev, nxt


def _acc_tn(acc_ref, a, b, chunk=512):
    n = b.shape[1]
    for c0 in range(0, n, chunk):
        c1 = min(c0 + chunk, n)
        acc_ref[:, c0:c1] += _dot_tn(a, b[:, c0:c1])


def _my_coords():
    return lax.axis_index("x"), lax.axis_index("y"), lax.axis_index("c")


def _peer(k, mx, my, mc):
    kx, ky, kc = (k >> 2) & 1, (k >> 1) & 1, k & 1
    px = 1 - mx if kx else mx
    py = 1 - my if ky else my
    pc = 1 - mc if kc else mc
    return px, py, pc


class _Exchange:
    def __init__(self, srcs, dsts, send_sems, recv_sems, local_sems, gather):
        self.srcs, self.dsts, self.gather = srcs, dsts, gather
        self.send_sems, self.recv_sems, self.local_sems = send_sems, recv_sems, local_sems

    def _copies(self, outgoing):
        mx, my, mc = _my_coords()
        me = 4 * mx + 2 * my + mc
        local, remote = [], []
        for t, (src, dst) in enumerate(zip(self.srcs, self.dsts)):
            local.append(pltpu.make_async_copy(src if self.gather else src.at[me], dst.at[me], self.local_sems.at[t]))
            for k in range(1, NDEV):
                px, py, pc = _peer(k, mx, my, mc)
                pid = 4 * px + 2 * py + pc
                remote.append(pltpu.make_async_remote_copy(
                    src_ref=src if self.gather else src.at[pid],
                    dst_ref=dst.at[me] if outgoing else dst.at[pid],
                    send_sem=self.send_sems.at[t, k - 1],
                    recv_sem=self.recv_sems.at[t, k - 1],
                    device_id=(px, py, pc),
                    device_id_type=pl.DeviceIdType.MESH,
                ))
        return local, remote

    def start(self):
        local, remote = self._copies(True)
        for cp in local + remote:
            cp.start()

    def wait(self):
        local, sends = self._copies(True)
        _, recvs = self._copies(False)
        for cp in recvs:
            cp.wait_recv()
        for cp in sends:
            cp.wait_send()
        for cp in local:
            cp.wait()


class _ChipGather:
    def __init__(self, srcs, dsts, send_sems, recv_sems, local_sems):
        self.srcs, self.dsts = srcs, dsts
        self.send_sems, self.recv_sems, self.local_sems = send_sems, recv_sems, local_sems

    def _places(self):
        x, y, c = _my_coords()
        return (x, y, c), (x, y, 1 - c), [(1 - x, y), (x, 1 - y), (1 - x, 1 - y)]

    def _copy(self, t, k, block, to, own=False):
        slot = self.dsts[t].at[4 * block[0] + 2 * block[1] + block[2]]
        return pltpu.make_async_remote_copy(
            src_ref=self.srcs[t] if own else slot, dst_ref=slot,
            send_sem=self.send_sems.at[t, k], recv_sem=self.recv_sems.at[t, k],
            device_id=to, device_id_type=pl.DeviceIdType.MESH)

    def _local(self, t, me):
        return pltpu.make_async_copy(self.srcs[t], self.dsts[t].at[4 * me[0] + 2 * me[1] + me[2]], self.local_sems.at[t])

    def start(self):
        me, sib, chips = self._places()
        for t in range(len(self.srcs)):
            self._local(t, me).start()
            self._copy(t, 0, me, sib, own=True).start()
            for j, chip in enumerate(chips):
                self._copy(t, 1 + j, me, (*chip, me[2]), own=True).start()

    def wait(self):
        me, sib, chips = self._places()
        n = len(self.srcs)
        for t in range(n):
            for j, chip in enumerate(chips):
                self._copy(t, 1 + j, (*chip, me[2]), me).wait_recv()
                self._copy(t, 4 + j, (*chip, me[2]), sib).start()
        for t in range(n):
            self._copy(t, 0, sib, me).wait_recv()
            for j, chip in enumerate(chips):
                self._copy(t, 4 + j, (*chip, sib[2]), me).wait_recv()
            self._copy(t, 0, me, sib, own=True).wait_send()
            for j, chip in enumerate(chips):
                self._copy(t, 1 + j, me, (*chip, me[2]), own=True).wait_send()
                self._copy(t, 4 + j, (*chip, me[2]), sib).wait_send()
            self._local(t, me).wait()


def _exchange_sems(n):
    return [pltpu.SemaphoreType.DMA((n, NDEV - 1)), pltpu.SemaphoreType.DMA((n, NDEV - 1)), pltpu.SemaphoreType.DMA((n,))]


def _exchange_out_shapes(xs, gather):
    return [jax.ShapeDtypeStruct(x.shape if not gather else (NDEV,) + x.shape, x.dtype) for x in xs]


def _exchange(xs, name, gather):
    n = len(xs)

    def body(*refs):
        ex = _Exchange(refs[:n], refs[n:2 * n], *refs[2 * n:], gather)
        ex.start()
        ex.wait()

    any_spec = pl.BlockSpec(memory_space=pl.ANY)
    return pl.pallas_call(
        body,
        name=name,
        out_shape=_exchange_out_shapes(xs, gather),
        in_specs=[any_spec] * n,
        out_specs=[any_spec] * n,
        scratch_shapes=_exchange_sems(n),
    )(*xs)


def _mod_call(c_all, w_ada, b_ada):
    ncol = w_ada.shape[1]

    def body(c_ref, w_ref, b_ref, silu_ref, mod_ref):
        cv = c_ref[...]
        s = cv * _sigmoid(cv)
        silu_ref[...] = s
        mod_ref[...] = _dot(s.astype(BF16), w_ref[...].astype(BF16)) + b_ref[...]

    return pl.pallas_call(
        body,
        name="mod_fwd",
        out_shape=(jax.ShapeDtypeStruct((16, D), F32), jax.ShapeDtypeStruct((16, ncol), F32)),
    )(c_all, w_ada, b_ada)


def _emb_ln_call(x0, gb, name, gathered=()):
    L = x0.shape[0]
    nt = L // TM
    ng = len(gathered)

    def body(*refs):
        x_ref, gb_ref = refs[:2]
        xe_ref = refs[2 + ng]
        i = pl.program_id(0)
        if ng:
            ex = _ChipGather(refs[2:2 + ng], refs[3 + ng:3 + 2 * ng], *refs[3 + 2 * ng:])

            @pl.when(i == 0)
            def _():
                ex.start()

        xe_ref[...] = _ln(x_ref[...], gb_ref[0:1, :], gb_ref[1:2, :])[0]
        if ng:
            @pl.when(i == nt - 1)
            def _():
                ex.wait()

    any_spec = pl.BlockSpec(memory_space=pl.ANY)
    return pl.pallas_call(
        body,
        name=name,
        grid=(nt,),
        in_specs=[_tiles(TM, D), _resident((8, D))] + [any_spec] * ng,
        out_specs=[_tiles(TM, D)] + [any_spec] * ng,
        out_shape=[jax.ShapeDtypeStruct((L, D), F32)] + _exchange_out_shapes(gathered, True),
        scratch_shapes=_exchange_sems(ng) if ng else [],
        compiler_params=_cp(),
    )(x0, gb, *gathered)


def _f1_call(xe, vec, w_in, name):
    L = xe.shape[0]

    def body(xe_ref, vec_ref, w_ref, h1_ref, z_ref, xbc_ref, dt_ref, up_ref):
        h1 = (xe_ref[...] * (1.0 + vec_ref[V_SC1:V_SC1 + 1, :]) + vec_ref[V_SH1:V_SH1 + 1, :]).astype(BF16)
        proj = _dot(h1, w_ref[...])
        h1_ref[...] = h1
        z_ref[...] = proj[:, 0:1024]
        xbc_ref[...] = proj[:, 1024:2304]
        dt_ref[...] = proj[:, 2304:2560]
        up_ref[...] = proj[:, 2560:3584]

    return pl.pallas_call(
        body,
        name=name,
        grid=(L // TM,),
        in_specs=[_tiles(TM, D), _resident((NV, D)), _resident((D, WIN))],
        out_specs=[_tiles(TM, D), _tiles(TM, D), _tiles(TM, DXBC), _tiles(TM, 256), _tiles(TM, D)],
        out_shape=(
            jax.ShapeDtypeStruct((L, D), BF16),
            jax.ShapeDtypeStruct((L, D), F32),
            jax.ShapeDtypeStruct((L, DXBC), F32),
            jax.ShapeDtypeStruct((L, 256), F32),
            jax.ShapeDtypeStruct((L, D), F32),
        ),
        compiler_params=_cp(),
    )(xe, vec, w_in)


def _extended(ext, cur_ref, prev_ref, next_ref):
    i = pl.program_id(0)
    n = pl.num_programs(0)
    tm = cur_ref.shape[0]
    ext[0:8, :] = jnp.where(i > 0, prev_ref[...], 0.0)
    ext[8:8 + tm, :] = cur_ref[...]
    ext[8 + tm:16 + tm, :] = jnp.where(i < n - 1, next_ref[...], 0.0)
    return ext


def _shifted(ext, offset, tm):
    return ext[8 + offset:8 + offset + tm, :]


def _conv_pre(ext, cw_ref, tm):
    acc = cw_ref[5:6, :] + cw_ref[0:1, :] * _shifted(ext, -2, tm)
    for k in range(1, 5):
        acc = acc + cw_ref[k:k + 1, :] * _shifted(ext, k - 2, tm)
    return acc


def _f2_call(xbc_raw, cw, name):
    L = xbc_raw.shape[0]
    prev, nxt = _halo_specs(TM, DXBC, L)

    def body(cur_ref, prev_ref, next_ref, cw_ref, out_ref, ext):
        pre = _conv_pre(_extended(ext, cur_ref, prev_ref, next_ref), cw_ref, TM)
        out_ref[...] = pre * _sigmoid(pre)

    return pl.pallas_call(
        body,
        name=name,
        grid=(L // TM,),
        in_specs=[_tiles(TM, DXBC), prev, nxt, _resident((8, DXBC))],
        out_specs=_tiles(TM, DXBC),
        out_shape=jax.ShapeDtypeStruct((L, DXBC), F32),
        scratch_shapes=[pltpu.VMEM((TM + 16, DXBC), F32)],
        compiler_params=_cp(),
    )(xbc_raw, xbc_raw, xbc_raw, cw)


def _ssd_common(d, dtr, par_ref):
    lane = lax.broadcasted_iota(jnp.int32, (1, 128), 1)
    hmask = lane < NH
    bias = par_ref[d:d + 1, :]
    alog = par_ref[2 + d:3 + d, :]
    aneg = jnp.where(hmask, -jnp.exp(alog), 0.0)
    pre = dtr + bias
    dt = jnp.where(hmask, _softplus(pre), 0.0)
    a = dt * aneg
    row = lax.broadcasted_iota(jnp.int32, (Q, Q), 0)
    col = lax.broadcasted_iota(jnp.int32, (Q, Q), 1)
    maskf = ((row >= col) if d == 0 else (row <= col)).astype(F32)
    mask_t = ((row <= col) if d == 0 else (row >= col)).astype(F32)
    A = _tri_dot(maskf, a)
    atot = _rowsum(a)
    return dict(hmask=hmask, aneg=aneg, pre=pre, dt=dt, a=a, maskf=maskf, mask_t=mask_t, A=A, AT=A.T, dtT=dt.T,
                atot=atot, lane=lane)


def _tri_dot(mask, x):
    t1 = x.astype(BF16)
    r1 = x - t1.astype(F32)
    t2 = r1.astype(BF16)
    t3 = (r1 - t2.astype(F32)).astype(BF16)
    mb = mask.astype(BF16)
    return _dot(jnp.concatenate([mb, mb, mb], axis=1), jnp.concatenate([t1, t2, t3], axis=0))


def _column(v, lane, h):
    return jnp.sum(jnp.where(lane == h, v, 0.0), axis=1, keepdims=True)


def _head_expand(e_ref, v):
    hi, lo = _split2(v)
    return _dot(jnp.concatenate([hi, lo], axis=1), e_ref[...])


def _head_sum(et_ref, v):
    return _dot(v.astype(BF16), et_ref[:, 0:128])


def _state_decay(atot, lane):
    ea = jnp.exp(atot)
    return jnp.concatenate([jnp.broadcast_to(_column(ea, lane, h), (HP, NS)) for h in range(NH)], axis=0)


def _ssd_fwd_call(xbc, dtr, h0, par, e_mat, et_mat, name, gathered=()):
    L = xbc.shape[0]
    nc = L // Q
    ng = len(gathered)

    def body(*refs):
        xbc_refs, dtr_refs = refs[0:2], refs[2:4]
        h0_ref, par_ref, e_ref, et_ref = refs[4:8]
        y_refs, hp_refs = refs[8 + ng:10 + ng], refs[10 + ng:12 + ng]
        hf_ref = refs[12 + ng]
        hs, AT, dtT = refs[13 + 2 * ng:16 + 2 * ng]
        s = pl.program_id(0)
        if ng:
            ex = _Exchange(refs[8:8 + ng], refs[13 + ng:13 + 2 * ng], *refs[16 + 2 * ng:], True)

        @pl.when(s == 0)
        def _():
            if ng:
                ex.start()
            hs[...] = h0_ref[...]

        for d in range(2):
            xbc_ref, y_ref = xbc_refs[d], y_refs[d]
            q = _ssd_common(d, dtr_refs[d][...], par_ref)
            A, maskf, lane = q["A"], q["maskf"], q["lane"]
            AT[d] = q["AT"]
            dtT[d] = q["dtT"]
            hprev = hs[d]
            hp_refs[d][0] = hprev
            bb = xbc_ref[:, 1024:1152].astype(BF16)
            cb = xbc_ref[:, 1152:1280].astype(BF16)
            g = _dot_nt(cb, bb)
            yoff = _dot_nt(cb, hprev.astype(BF16)) * _head_expand(e_ref, jnp.exp(A))
            for k in range(NH // 2):
                ks = slice(128 * k, 128 * k + 128)
                xp = xbc_ref[:, ks]
                scs, xhs = [], []
                for half in range(2):
                    h = 2 * k + half
                    seg = _column(A, lane, h) - AT[d, h:h + 1, :]
                    lm = jnp.exp(jnp.minimum(seg, 0.0)) * maskf
                    scs.append((g * lm * dtT[d, h:h + 1, :]).astype(BF16))
                    inhead = (lane >= HP) if half else (lane < HP)
                    xhs.append(jnp.where(inhead, xp, 0.0).astype(BF16))
                y_ref[:, ks] = yoff[:, ks] + _dot(jnp.concatenate(scs, axis=1), jnp.concatenate(xhs, axis=0))
            wend = jnp.exp(q["atot"] - A) * q["dt"]
            xw = (xbc_ref[:, 0:1024] * _head_expand(e_ref, wend)).astype(BF16)
            hnew = hprev * _state_decay(q["atot"], lane) + _dot_tn(xw, bb)
            hs[d] = hnew
            hf_ref[d] = hnew
        if ng:
            @pl.when(s == nc - 1)
            def _():
                ex.wait()

    any_spec = pl.BlockSpec(memory_space=pl.ANY)
    return pl.pallas_call(
        body,
        name=name,
        grid=(nc,),
        in_specs=[
            pl.BlockSpec((Q, DXBC), lambda s: (s, 0)),
            pl.BlockSpec((Q, DXBC), lambda s: (nc - 1 - s, 0)),
            pl.BlockSpec((Q, 128), lambda s: (s, 0)),
            pl.BlockSpec((Q, 128), lambda s: (nc - 1 - s, 1)),
            _resident((2, D, NS)),
            _resident((8, 128)),
            _resident((256, D)),
            _resident((D, 256)),
        ] + [any_spec] * ng,
        out_specs=[
            pl.BlockSpec((Q, D), lambda s: (s, 0)),
            pl.BlockSpec((Q, D), lambda s: (nc - 1 - s, 0)),
            pl.BlockSpec((1, D, NS), lambda s: (s, 0, 0)),
            pl.BlockSpec((1, D, NS), lambda s: (nc - 1 - s, 0, 0)),
            _const_out((2, D, NS)),
        ] + [any_spec] * ng,
        out_shape=[
            jax.ShapeDtypeStruct((L, D), F32),
            jax.ShapeDtypeStruct((L, D), F32),
            jax.ShapeDtypeStruct((nc, D, NS), F32),
            jax.ShapeDtypeStruct((nc, D, NS), F32),
            jax.ShapeDtypeStruct((2, D, NS), F32),
        ] + _exchange_out_shapes(gathered, True),
        scratch_shapes=[pltpu.VMEM((2, D, NS), F32), pltpu.VMEM((2, 128, Q), F32), pltpu.VMEM((2, 128, Q), F32)]
        + (_exchange_sems(ng) if ng else []),
        compiler_params=_cp(),
    )(xbc, xbc, dtr, dtr, h0, par, e_mat, et_mat, *gathered)


def _halo_tokens(g):
    return (WINDOWS[g] // 2) * GW


def _pool_specs(n_tiles):
    cur = pl.BlockSpec((PT, D), lambda i: (i, 0))
    prev = pl.BlockSpec((PT, D), lambda i: (jnp.maximum(i - 1, 0), 0))
    nxt = pl.BlockSpec((PT, D), lambda i: (jnp.minimum(i + 1, n_tiles - 1), 0))
    return cur, prev, nxt


def _pool_fwd_call(up, pmat, icnt, pool_w):
    L = up.shape[0]
    nt = L // PT
    cur, prev, nxt = _pool_specs(nt)

    def body(cur_ref, prev_ref, next_ref, m0_ref, m1_ref, m2_ref, m3_ref, ic_ref, pw_ref, d_ref, pun_ref):
        i = pl.program_id(0)
        n = pl.num_programs(0)
        lane = lax.broadcasted_iota(jnp.int32, (1, 128), 1)
        icv = ic_ref[...]
        for g, m_ref in enumerate((m0_ref, m1_ref, m2_ref, m3_ref)):
            gs = slice(PG * g, PG * g + PG)
            halo = _halo_tokens(g)
            top = jnp.where(i > 0, prev_ref[PT - halo:PT, gs], 0.0)
            bot = jnp.where(i < n - 1, next_ref[0:halo, gs], 0.0)
            mid = cur_ref[:, gs]
            box = _split_dot(m_ref[...], jnp.concatenate([top, mid, bot], axis=0))
            dg = (box * _column(icv, lane, g) - mid).astype(BF16)
            d_ref[:, gs] = dg
            pun_ref[:, gs] = _dot(dg, pw_ref[g])

    return pl.pallas_call(
        body,
        name="pool_fwd",
        grid=(nt,),
        in_specs=[cur, prev, nxt] + [_resident(m.shape) for m in pmat] + [_tiles(PT, 128), _resident((4, PG, PG))],
        out_specs=[_tiles(PT, D), _tiles(PT, D)],
        out_shape=(jax.ShapeDtypeStruct((L, D), BF16), jax.ShapeDtypeStruct((L, D), F32)),
        compiler_params=_cp(),
    )(up, up, up, *pmat, icnt, pool_w)


def _gated(yf_ref, yb_ref, xs, z, vec_ref):
    ym = yf_ref[...] + yb_ref[...] + vec_ref[V_DSK:V_DSK + 1, :] * xs
    sz = _sigmoid(z)
    gated = ym * (z * sz)
    r = lax.rsqrt(jnp.mean(gated * gated, axis=-1, keepdims=True) + LN_EPS)
    return ym, sz, gated, r


def _merge_call(yf, yb, xbc, z, pun, xe, vec, w_out):
    L = z.shape[0]

    def body(yf_ref, yb_ref, xs_ref, z_ref, pun_ref, xe_ref, vec_ref, w_ref, x1_ref, mix_ref, cat_ref):
        _, _, gated, r = _gated(yf_ref, yb_ref, xs_ref[...], z_ref[...], vec_ref)
        yn = gated * r * vec_ref[V_SSDG:V_SSDG + 1, :]
        p = pun_ref[...] * vec_ref[V_PSC:V_PSC + 1, :]
        cat = jnp.concatenate([yn, p], axis=1).astype(BF16)
        mix = _dot(cat, w_ref[...])
        pre1 = ALPHA * xe_ref[...] + vec_ref[V_G1:V_G1 + 1, :] * mix
        x1, _, _ = _ln(pre1, vec_ref[V_LN1G:V_LN1G + 1, :], vec_ref[V_LN1B:V_LN1B + 1, :])
        x1_ref[...] = x1
        mix_ref[...] = mix
        cat_ref[...] = cat

    return pl.pallas_call(
        body,
        name="merge_fwd",
        grid=(L // TM,),
        in_specs=[
            _tiles(TM, D), _tiles(TM, D), _tiles(TM, D), _tiles(TM, D), _tiles(TM, D), _tiles(TM, D),
            _resident((NV, D)), _resident((2 * D, D)),
        ],
        out_specs=[_tiles(TM, D), _tiles(TM, D), _tiles(TM, 2 * D)],
        out_shape=(
            jax.ShapeDtypeStruct((L, D), F32),
            jax.ShapeDtypeStruct((L, D), F32),
            jax.ShapeDtypeStruct((L, 2 * D), BF16),
        ),
        compiler_params=_cp(),
    )(yf, yb, xbc, z, pun, xe, vec, w_out)


def _ffn_fwd_call(x1, tgt, vec, w_gate, w_up, w_down):
    L = x1.shape[0]

    def body(x1_ref, tgt_ref, vec_ref, wg_ref, wu_ref, wd_ref, dpre_ref, gacc_ref, gt_ref, up_ref):
        @pl.when(pl.program_id(0) == 0)
        def _():
            gacc_ref[...] = jnp.zeros_like(gacc_ref)

        x1 = x1_ref[...]
        h2 = (x1 * (1.0 + vec_ref[V_SC2:V_SC2 + 1, :]) + vec_ref[V_SH2:V_SH2 + 1, :]).astype(BF16)
        gt = _dot(h2, wg_ref[...])
        up = _dot(h2, wu_ref[...])
        gt_ref[...] = gt.astype(BF16)
        up_ref[...] = up.astype(BF16)
        f = (gt * _sigmoid(gt) * up).astype(BF16)
        ffn = _dot(f, wd_ref[...])
        g2 = vec_ref[V_G2:V_G2 + 1, :]
        lng = vec_ref[V_LN2G:V_LN2G + 1, :]
        x2, n2, rstd2 = _ln(ALPHA * x1 + g2 * ffn, lng, vec_ref[V_LN2B:V_LN2B + 1, :])
        diff = x2 - tgt_ref[...]
        dx2 = diff * (1.0 / D)
        dpre2 = _ln_bwd(dx2, n2, rstd2, lng)
        dpre_ref[...] = dpre2
        gacc_ref[V_LN2G:V_LN2G + 1, :] += _rowsum(dx2 * n2)
        gacc_ref[V_LN2B:V_LN2B + 1, :] += _rowsum(dx2)
        gacc_ref[V_G2:V_G2 + 1, :] += _rowsum(dpre2 * ffn)
        gacc_ref[V_LOSS:V_LOSS + 1, :] += jnp.sum(diff * diff) * (0.5 / D)

    return pl.pallas_call(
        body,
        name="ffn_fwd",
        grid=(L // TM,),
        in_specs=[_tiles(TM, D), _tiles(TM, D), _resident((NV, D)),
                  _resident((D, DFF)), _resident((D, DFF)), _resident((DFF, D))],
        out_specs=[_tiles(TM, D), _const_out((NV, D)), _tiles(TM, DFF), _tiles(TM, DFF)],
        out_shape=(jax.ShapeDtypeStruct((L, D), F32), jax.ShapeDtypeStruct((NV, D), F32),
                   jax.ShapeDtypeStruct((L, DFF), BF16), jax.ShapeDtypeStruct((L, DFF), BF16)),
        compiler_params=_cp(),
    )(x1, tgt, vec, w_gate, w_up, w_down)


def _ffn_bwd_call(x1, dpre2, gt_b, up_b, vec, w_gate, w_up, w_down):
    L = x1.shape[0]
    nt = L // TM
    nj = DFF // FFC

    def body(x1_ref, dpre_ref, gt_ref, up_ref, vec_ref, wg_ref, wu_ref, wd_ref, dh2_ref, dwg_ref, dwu_ref, dwd_ref,
             ag, au, ad):
        j = pl.program_id(0)
        i = pl.program_id(1)

        @pl.when(i == 0)
        def _():
            ag[...] = jnp.zeros_like(ag)
            au[...] = jnp.zeros_like(au)
            ad[...] = jnp.zeros_like(ad)

        h2 = (x1_ref[...] * (1.0 + vec_ref[V_SC2:V_SC2 + 1, :]) + vec_ref[V_SH2:V_SH2 + 1, :]).astype(BF16)
        gt = gt_ref[...].astype(F32)
        up = up_ref[...].astype(F32)
        sg = _sigmoid(gt)
        sl = gt * sg
        f = (sl * up).astype(BF16)
        dffn = (vec_ref[V_G2:V_G2 + 1, :] * dpre_ref[...]).astype(BF16)
        df = _dot_nt(dffn, wd_ref[...])
        dgt = (df * up * (sg * (1.0 + gt * (1.0 - sg)))).astype(BF16)
        dup = (df * sl).astype(BF16)
        dh2_ref[0] = _dot_nt(dgt, wg_ref[...]) + _dot_nt(dup, wu_ref[...])
        _acc_tn(ag, h2, dgt)
        _acc_tn(au, h2, dup)
        _acc_tn(ad, f, dffn)

        @pl.when(i == nt - 1)
        def _():
            pltpu.sync_copy(ag, dwg_ref.at[j])
            pltpu.sync_copy(au, dwu_ref.at[j])
            pltpu.sync_copy(ad, dwd_ref.at[j])

    any_spec = pl.BlockSpec(memory_space=pl.ANY)
    return pl.pallas_call(
        body,
        name="ffn_bwd",
        grid=(nj, nt),
        in_specs=[
            pl.BlockSpec((TM, D), lambda j, i: (i, 0)),
            pl.BlockSpec((TM, D), lambda j, i: (i, 0)),
            pl.BlockSpec((TM, FFC), lambda j, i: (i, j)),
            pl.BlockSpec((TM, FFC), lambda j, i: (i, j)),
            _resident((NV, D)),
            pl.BlockSpec((D, FFC), lambda j, i: (0, j)),
            pl.BlockSpec((D, FFC), lambda j, i: (0, j)),
            pl.BlockSpec((FFC, D), lambda j, i: (j, 0)),
        ],
        out_specs=[pl.BlockSpec((1, TM, D), lambda j, i: (j, i, 0)), any_spec, any_spec, any_spec],
        out_shape=(
            jax.ShapeDtypeStruct((nj, L, D), F32),
            jax.ShapeDtypeStruct((nj, D, FFC), F32),
            jax.ShapeDtypeStruct((nj, D, FFC), F32),
            jax.ShapeDtypeStruct((nj, FFC, D), F32),
        ),
        scratch_shapes=[pltpu.VMEM((D, FFC), F32), pltpu.VMEM((D, FFC), F32), pltpu.VMEM((FFC, D), F32)],
        compiler_params=_cp(2),
    )(x1, dpre2, gt_b, up_b, vec, w_gate, w_up, w_down)


def _merge_bwd_call(dh2p, dpre2, xe, mix, cat, yf, yb, xbc, z, dpool, pun, vec, w_out, pool_w, scattered):
    L = z.shape[0]
    nt = L // TM
    ns = len(scattered)

    def body(*refs):
        (dh2_ref, dpre2_ref, xe_ref, mix_ref, cat_ref, yf_ref, yb_ref, xs_ref, z_ref, dpool_ref, pun_ref,
         vec_ref, w_ref, pw_ref) = refs[:14]
        dxe_ref, dy_ref, dz_ref, dd_ref, dpw_ref, gacc_ref, dwo_ref = refs[14 + ns:21 + ns]
        dwo_s = refs[21 + 2 * ns]
        ex = _Exchange(refs[14:14 + ns], refs[21 + ns:21 + 2 * ns], *refs[22 + 2 * ns:], False)
        i = pl.program_id(0)

        @pl.when(i == 0)
        def _():
            ex.start()
            gacc_ref[...] = jnp.zeros_like(gacc_ref)
            dpw_ref[...] = jnp.zeros_like(dpw_ref)
            dwo_s[...] = jnp.zeros_like(dwo_s)

        def vrow(r):
            return vec_ref[r:r + 1, :]

        def gadd(r, val):
            gacc_ref[r:r + 1, :] += _rowsum(val)

        dh2 = dh2_ref[0] + dh2_ref[1]
        dx1 = ALPHA * dpre2_ref[...] + dh2 * (1.0 + vrow(V_SC2))
        mix = mix_ref[...]
        x1, n1, rstd1 = _ln(ALPHA * xe_ref[...] + vrow(V_G1) * mix, vrow(V_LN1G), vrow(V_LN1B))
        gadd(V_SC2, dh2 * x1)
        gadd(V_SH2, dh2)
        gadd(V_LN1G, dx1 * n1)
        gadd(V_LN1B, dx1)
        dpre1 = _ln_bwd(dx1, n1, rstd1, vrow(V_LN1G))
        dxe_ref[...] = ALPHA * dpre1
        gadd(V_G1, dpre1 * mix)
        dmix = (vrow(V_G1) * dpre1).astype(BF16)
        dcat = _dot_nt(dmix, w_ref[...])
        cat = cat_ref[...]
        for c0 in range(0, 2 * D, 512):
            dwo_s[c0:c0 + 512, :] += _dot_tn(cat[:, c0:c0 + 512], dmix)
        dyn = dcat[:, 0:D]
        dp = dcat[:, D:2 * D]
        xs = xs_ref[...]
        z = z_ref[...]
        ym, sz, gated, r = _gated(yf_ref, yb_ref, xs, z, vec_ref)
        gadd(V_SSDG, dyn * gated * r)
        a = dyn * vrow(V_SSDG)
        dgated = r * a - gated * (r * r * r * jnp.mean(a * gated, axis=-1, keepdims=True))
        dym = dgated * (z * sz)
        dy_ref[...] = dym
        dz_ref[...] = dgated * ym * (sz * (1.0 + z * (1.0 - sz)))
        gadd(V_DSK, dym * xs)
        gadd(V_PSC, dp * pun_ref[...])
        dps = (dp * vrow(V_PSC)).astype(BF16)
        dpool = dpool_ref[...]
        for g in range(4):
            gs = slice(PG * g, PG * g + PG)
            dd_ref[:, gs] = _dot_nt(dps[:, gs], pw_ref[g])
            dpw_ref[g] += _dot_tn(dpool[:, gs], dps[:, gs])

        @pl.when(i == nt - 1)
        def _():
            pltpu.sync_copy(dwo_s, dwo_ref)
            ex.wait()

    any_spec = pl.BlockSpec(memory_space=pl.ANY)
    return pl.pallas_call(
        body,
        name="merge_bwd",
        grid=(nt,),
        in_specs=[
            pl.BlockSpec((2, TM, D), lambda i: (0, i, 0)),
            _tiles(TM, D), _tiles(TM, D), _tiles(TM, D), _tiles(TM, 2 * D),
            _tiles(TM, D), _tiles(TM, D),
            _tiles(TM, D), _tiles(TM, D), _tiles(TM, D), _tiles(TM, D),
            _resident((NV, D)), _resident((2 * D, D)), _resident((4, PG, PG)),
        ] + [any_spec] * ns,
        out_specs=[_tiles(TM, D), _tiles(TM, D), _tiles(TM, D), _tiles(TM, D),
                   _const_out((4, PG, PG)), _const_out((NV, D)), any_spec] + [any_spec] * ns,
        out_shape=[
            jax.ShapeDtypeStruct((L, D), F32),
            jax.ShapeDtypeStruct((L, D), F32),
            jax.ShapeDtypeStruct((L, D), F32),
            jax.ShapeDtypeStruct((L, D), F32),
            jax.ShapeDtypeStruct((4, PG, PG), F32),
            jax.ShapeDtypeStruct((NV, D), F32),
            jax.ShapeDtypeStruct((2 * D, D), F32),
        ] + _exchange_out_shapes(scattered, False),
        scratch_shapes=[pltpu.VMEM((2 * D, D), F32)] + _exchange_sems(ns),
        compiler_params=_cp(),
    )(dh2p, dpre2, xe, mix, cat, yf, yb, xbc, z, dpool, pun, vec, w_out, pool_w, *scattered)


def _pool_bwd_call(dd, pmat_t, icnt):
    L = dd.shape[0]
    nt = L // PT
    cur, prev, nxt = _pool_specs(nt)
    icur = pl.BlockSpec((PT, 128), lambda i: (i, 0))
    iprev = pl.BlockSpec((PT, 128), lambda i: (jnp.maximum(i - 1, 0), 0))
    inxt = pl.BlockSpec((PT, 128), lambda i: (jnp.minimum(i + 1, nt - 1), 0))

    def body(cur_ref, prev_ref, next_ref, ic_ref, icp_ref, icn_ref, m0_ref, m1_ref, m2_ref, m3_ref, du_ref):
        i = pl.program_id(0)
        n = pl.num_programs(0)
        lane = lax.broadcasted_iota(jnp.int32, (1, 128), 1)
        icv = ic_ref[...]
        for g, m_ref in enumerate((m0_ref, m1_ref, m2_ref, m3_ref)):
            gs = slice(PG * g, PG * g + PG)
            halo = _halo_tokens(g)
            icp = _column(icp_ref[PT - halo:PT, :], lane, g)
            icn = _column(icn_ref[0:halo, :], lane, g)
            top = jnp.where(i > 0, prev_ref[PT - halo:PT, gs] * icp, 0.0)
            bot = jnp.where(i < n - 1, next_ref[0:halo, gs] * icn, 0.0)
            mid = cur_ref[:, gs]
            ext = jnp.concatenate([top, mid * _column(icv, lane, g), bot], axis=0)
            du_ref[:, gs] = _split_dot(m_ref[...], ext) - mid

    return pl.pallas_call(
        body,
        name="pool_bwd",
        grid=(nt,),
        in_specs=[cur, prev, nxt, icur, iprev, inxt] + [_resident(m.shape) for m in pmat_t],
        out_specs=_tiles(PT, D),
        out_shape=jax.ShapeDtypeStruct((L, D), F32),
        compiler_params=_cp(),
    )(dd, dd, dd, icnt, icnt, icnt, *pmat_t)


def _ssd_bwd_call(dy, xbc, dtr, hprev_f, hprev_b, dh_init, par, e_mat, et_mat, dskip, name):
    L = xbc.shape[0]
    nc = L // Q

    def body(dy0_ref, dy1_ref, xbc0_ref, xbc1_ref, dtr0_ref, dtr1_ref, hp0_ref, hp1_ref, dhi_ref, par_ref, e_ref,
             et_ref, dsk_ref, dx0_ref, dx1_ref, ddt0_ref, ddt1_ref, acc_ref, dh0_ref, dh_s, AT, dtT, ddtT_s, dAT_s):
        s = pl.program_id(0)

        @pl.when(s == 0)
        def _():
            dh_s[...] = dhi_ref[...]
            acc_ref[...] = jnp.zeros_like(acc_ref)
            ddtT_s[...] = jnp.zeros_like(ddtT_s)
            dAT_s[...] = jnp.zeros_like(dAT_s)

        one_direction(0, dy0_ref, xbc0_ref, dtr0_ref, hp0_ref, par_ref, e_ref, et_ref, dsk_ref, dx0_ref, ddt0_ref,
                      acc_ref, dh0_ref, dh_s, AT, dtT, ddtT_s, dAT_s)
        one_direction(1, dy1_ref, xbc1_ref, dtr1_ref, hp1_ref, par_ref, e_ref, et_ref, dsk_ref, dx1_ref, ddt1_ref,
                      acc_ref, dh0_ref, dh_s, AT, dtT, ddtT_s, dAT_s)

    def one_direction(d, dy_ref, xbc_ref, dtr_ref, hp_ref, par_ref, e_ref, et_ref, dsk_ref, dxbc_ref, ddtr_ref,
                      acc_ref, dh0_ref, dh_s, AT_s, dtT_s, ddtT_s, dAT_s):
        q = _ssd_common(d, dtr_ref[...], par_ref)
        A, maskf, lane, dt, atot = q["A"], q["maskf"], q["lane"], q["dt"], q["atot"]
        AT_s[d] = q["AT"]
        dtT_s[d] = q["dtT"]
        AT, dtT = AT_s.at[d], dtT_s.at[d]
        hprev = hp_ref[0]
        hpb = hprev.astype(BF16)
        dh = dh_s[d]
        dhb = dh.astype(BF16)
        xs = xbc_ref[:, 0:1024]
        bb = xbc_ref[:, 1024:1152].astype(BF16)
        cb = xbc_ref[:, 1152:1280].astype(BF16)
        dy = dy_ref[...]
        ea_f = _head_expand(e_ref, jnp.exp(A))
        ch = _dot_nt(cb, hpb)
        dch = (dy * ea_f).astype(BF16)
        dC = _dot(dch, hpb)
        dhprev = _dot_tn(dch, cb)
        dA = _head_sum(et_ref, dy * ch * ea_f)
        dec = _state_decay(atot, lane)
        dhprev = dhprev + dh * dec
        dhh = dh * hprev * dec
        datot = jnp.zeros((1, 128), F32)
        for h in range(NH):
            tot_h = jnp.sum(_rowsum(dhh[HP * h:HP * h + HP, :]), axis=1, keepdims=True)
            datot = datot + jnp.where(lane == h, tot_h, 0.0)
        ear = jnp.exp(atot - A)
        wend = ear * dt
        wf = _head_expand(e_ref, wend)
        xw = (xs * wf).astype(BF16)
        dxw = _dot_nt(bb, dhb)
        dB = _dot(xw, dhb)
        dxs = dxw * wf
        dwend = _head_sum(et_ref, dxw * xs)
        ddt = dwend * ear
        de = dwend * wend
        datot = datot + _rowsum(de)
        dA = dA - de
        g = _dot_nt(cb, bb)
        dG = jnp.zeros((Q, Q), F32)
        for k in range(NH // 2):
            ks = slice(128 * k, 128 * k + 128)
            xp = xs[:, ks]
            dyp = dy[:, ks]
            accdx = dxs[:, ks]
            if d == 0:
                accdx = accdx + dyp * dsk_ref[:, ks]
            scts, dyhs = [], []
            for half in range(2):
                h = 2 * k + half
                inhead = (lane >= HP) if half else (lane < HP)
                seg = _column(A, lane, h) - AT[h:h + 1, :]
                lm = jnp.exp(jnp.minimum(seg, 0.0)) * maskf
                dtrow = dtT[h:h + 1, :]
                gl = g * lm
                sc = gl * dtrow
                dyh = jnp.where(inhead, dyp, 0.0).astype(BF16)
                xh = jnp.where(inhead, xp, 0.0).astype(BF16)
                dS = _dot_nt(dyh, xh)
                scts.append(sc.T.astype(BF16))
                dyhs.append(dyh)
                nn = dS * gl
                cn = _rowsum(nn)
                rm = jnp.sum(nn * dtrow, axis=1, keepdims=True)
                dG = dG + dS * (lm * dtrow)
                ddtT_s[d, h:h + 1, :] = cn
                dAT_s[d, h:h + 1, :] = -(cn * dtrow)
                dA = dA + rm * (lane == h).astype(F32)
            dxbc_ref[:, ks] = accdx + _dot(jnp.concatenate(scts, axis=1), jnp.concatenate(dyhs, axis=0))
        dGb = dG.astype(BF16)
        dxbc_ref[:, 1024:1152] = dB + _dot_tn(dGb, cb)
        dxbc_ref[:, 1152:1280] = dC + _dot(dGb, bb)
        da = _tri_dot(q["mask_t"], dA + dAT_s[d].T) + datot
        ddt = ddt + ddtT_s[d].T + da * q["aneg"]
        ddtr = jnp.where(q["hmask"], ddt * _sigmoid(q["pre"]), 0.0)
        ddtr_ref[...] = ddtr
        acc_ref[d, 0:1, :] += _rowsum(ddtr)
        acc_ref[d, 1:2, :] += _rowsum(da * dt) * q["aneg"]
        dh_s[d] = dhprev
        dh0_ref[d] = dhprev

    def back(s):
        return nc - 1 - s

    return pl.pallas_call(
        body,
        name=name,
        grid=(nc,),
        in_specs=[
            pl.BlockSpec((Q, D), lambda s: (back(s), 0)),
            pl.BlockSpec((Q, D), lambda s: (s, 0)),
            pl.BlockSpec((Q, DXBC), lambda s: (back(s), 0)),
            pl.BlockSpec((Q, DXBC), lambda s: (s, 0)),
            pl.BlockSpec((Q, 128), lambda s: (back(s), 0)),
            pl.BlockSpec((Q, 128), lambda s: (s, 1)),
            pl.BlockSpec((1, D, NS), lambda s: (back(s), 0, 0)),
            pl.BlockSpec((1, D, NS), lambda s: (s, 0, 0)),
            _resident((2, D, NS)),
            _resident((8, 128)),
            _resident((256, D)),
            _resident((D, 256)),
            _resident((1, D)),
        ],
        out_specs=[
            pl.BlockSpec((Q, DXBC), lambda s: (back(s), 0)),
            pl.BlockSpec((Q, DXBC), lambda s: (s, 0)),
            pl.BlockSpec((Q, 128), lambda s: (back(s), 0)),
            pl.BlockSpec((Q, 128), lambda s: (s, 0)),
            _const_out((2, 8, 128)),
            _const_out((2, D, NS)),
        ],
        out_shape=(
            jax.ShapeDtypeStruct((L, DXBC), F32),
            jax.ShapeDtypeStruct((L, DXBC), F32),
            jax.ShapeDtypeStruct((L, 128), F32),
            jax.ShapeDtypeStruct((L, 128), F32),
            jax.ShapeDtypeStruct((2, 8, 128), F32),
            jax.ShapeDtypeStruct((2, D, NS), F32),
        ),
        scratch_shapes=[pltpu.VMEM((2, D, NS), F32)] + [pltpu.VMEM((2, 128, Q), F32)] * 4,
        compiler_params=_cp(),
    )(dy, dy, xbc, xbc, dtr, dtr, hprev_f, hprev_b, dh_init, par, e_mat, et_mat, dskip)


def _conv_bwd_call(dxf, dxb, xbc_raw, cw, acc_init, name, scattered=()):
    L = xbc_raw.shape[0]
    nt = L // TM
    ns = len(scattered)
    prev, nxt = _halo_specs(TM, DXBC, L)

    def body(*refs):
        dxf_ref, dxb_ref, cur_ref, prev_ref, next_ref, cw_ref, init_ref = refs[:7]
        dpre_ref, acc_ref = refs[7 + ns:9 + ns]
        ext = refs[9 + 2 * ns]
        if ns:
            ex = _Exchange(refs[7:7 + ns], refs[9 + ns:9 + 2 * ns], *refs[10 + 2 * ns:], False)

        @pl.when(pl.program_id(0) == 0)
        def _():
            if ns:
                ex.start()
            acc_ref[...] = init_ref[...]

        _extended(ext, cur_ref, prev_ref, next_ref)
        taps = [_shifted(ext, k - 2, TM) for k in range(5)]
        pre = cw_ref[5:6, :] + cw_ref[0:1, :] * taps[0]
        for k in range(1, 5):
            pre = pre + cw_ref[k:k + 1, :] * taps[k]
        sg = _sigmoid(pre)
        dpre = (dxf_ref[...] + dxb_ref[...]) * (sg * (1.0 + pre * (1.0 - sg)))
        dpre_ref[...] = dpre
        for k in range(5):
            acc_ref[k:k + 1, :] += _rowsum(dpre * taps[k])
        acc_ref[5:6, :] += _rowsum(dpre)
        if ns:
            @pl.when(pl.program_id(0) == nt - 1)
            def _():
                ex.wait()

    any_spec = pl.BlockSpec(memory_space=pl.ANY)
    return pl.pallas_call(
        body,
        name=name,
        grid=(nt,),
        in_specs=[_tiles(TM, DXBC), _tiles(TM, DXBC), _tiles(TM, DXBC), prev, nxt,
                  _resident((8, DXBC)), _resident((8, DXBC))] + [any_spec] * ns,
        out_specs=[_tiles(TM, DXBC), _const_out((8, DXBC))] + [any_spec] * ns,
        out_shape=[jax.ShapeDtypeStruct((L, DXBC), F32), jax.ShapeDtypeStruct((8, DXBC), F32)]
        + _exchange_out_shapes(scattered, False),
        scratch_shapes=[pltpu.VMEM((TM + 16, DXBC), F32)] + (_exchange_sems(ns) if ns else []),
        compiler_params=_cp(),
    )(dxf, dxb, xbc_raw, xbc_raw, xbc_raw, cw, acc_init, *scattered)


def _inproj_bwd_call(dpre, cw, dz, ddt0, ddt1, dup, h1, dxe_part, x0, vec, w_in, dw_init, name):
    L = x0.shape[0]
    nt = L // TM
    prev, nxt = _halo_specs(TM, DXBC, L)

    def body(cur_ref, prev_ref, next_ref, cw_ref, dz_ref, ddt0_ref, ddt1_ref, dup_ref, h1_ref, dxe_ref, x0_ref,
             vec_ref, w_ref, dwi_ref, gx_ref, gacc_ref, dw_ref, dw_s, ext):
        i = pl.program_id(0)

        @pl.when(i == 0)
        def _():
            gacc_ref[...] = jnp.zeros_like(gacc_ref)
            pltpu.sync_copy(dwi_ref, dw_s)

        def vrow(r):
            return vec_ref[r:r + 1, :]

        _extended(ext, cur_ref, prev_ref, next_ref)
        dxr = cw_ref[0:1, :] * _shifted(ext, 2, TM)
        for k in range(1, 5):
            dxr = dxr + cw_ref[k:k + 1, :] * _shifted(ext, 2 - k, TM)
        dproj = jnp.concatenate([dz_ref[...], dxr, ddt0_ref[...], ddt1_ref[...], dup_ref[...]], axis=1).astype(BF16)
        dh1 = _dot_nt(dproj, w_ref[...])
        _acc_tn(dw_s, h1_ref[...], dproj)
        xe, n0, rstd0 = _ln(x0_ref[...], vrow(V_EMBG), vrow(V_EMBB))
        dxe = dxe_ref[...] + dh1 * (1.0 + vrow(V_SC1))
        gacc_ref[V_SC1:V_SC1 + 1, :] += _rowsum(dh1 * xe)
        gacc_ref[V_SH1:V_SH1 + 1, :] += _rowsum(dh1)
        gacc_ref[V_EMBG:V_EMBG + 1, :] += _rowsum(dxe * n0)
        gacc_ref[V_EMBB:V_EMBB + 1, :] += _rowsum(dxe)
        gx_ref[...] = _ln_bwd(dxe, n0, rstd0, vrow(V_EMBG))

        @pl.when(i == nt - 1)
        def _():
            pltpu.sync_copy(dw_s, dw_ref)

    any_spec = pl.BlockSpec(memory_space=pl.ANY)
    return pl.pallas_call(
        body,
        name=name,
        grid=(nt,),
        in_specs=[_tiles(TM, DXBC), prev, nxt, _resident((8, DXBC)), _tiles(TM, D), _tiles(TM, 128), _tiles(TM, 128),
                  _tiles(TM, D), _tiles(TM, D), _tiles(TM, D), _tiles(TM, D), _resident((NV, D)), _resident((D, WIN)),
                  any_spec],
        out_specs=[_tiles(TM, D), _const_out((NV, D)), any_spec],
        out_shape=(
            jax.ShapeDtypeStruct((L, D), F32),
            jax.ShapeDtypeStruct((NV, D), F32),
            jax.ShapeDtypeStruct((D, WIN), F32),
        ),
        scratch_shapes=[pltpu.VMEM((D, WIN), F32), pltpu.VMEM((TM + 16, DXBC), F32)],
        compiler_params=_cp(),
    )(dpre, dpre, dpre, cw, dz, ddt0, ddt1, dup, h1, dxe_part, x0, vec, w_in, dw_init)


def _adamw(w, g, m, v):
    m = ADAM_B1 * m + (1.0 - ADAM_B1) * g
    v = ADAM_B2 * v + (1.0 - ADAM_B2) * (g * g)
    m_hat = m / (1.0 - ADAM_B1 ** ADAM_STEP)
    v_hat = v / (1.0 - ADAM_B2 ** ADAM_STEP)
    delta = -ADAM_LR * (m_hat / (jnp.sqrt(v_hat) + ADAM_EPS) + ADAM_WD * w)
    return delta, m, v


def _adamw_shard_call(gslots, w, m, v, tr, name):
    R, C = w.shape

    def body(gs_ref, w_ref, m_ref, v_ref, g_ref, d_ref, mo_ref, vo_ref):
        g = gs_ref[0].astype(F32)
        for i in range(1, NDEV):
            g = g + gs_ref[i].astype(F32)
        delta, mn, vn = _adamw(w_ref[...], g, m_ref[...], v_ref[...])
        g_ref[...] = g
        d_ref[...] = delta
        mo_ref[...] = mn
        vo_ref[...] = vn

    t = _tiles(tr, C)
    return pl.pallas_call(
        body,
        name=name,
        grid=(R // tr,),
        in_specs=[pl.BlockSpec((NDEV, tr, C), lambda i: (0, i, 0)), t, t, t],
        out_specs=[t, t, t, t],
        out_shape=tuple(jax.ShapeDtypeStruct((R, C), F32) for _ in range(4)),
        compiler_params=_cp(),
    )(gslots, w, m, v)


def _wada_call(dm_ex, dm_ctx, silu_all, w, m, v):
    ncol = w.shape[1]

    def body(dme_ref, dmc_ref, s_ref, w_ref, m_ref, v_ref, g_ref, d_ref, mo_ref, vo_ref, ds_ref):
        dmc = _rowsum(dmc_ref[...])
        rows = lax.broadcasted_iota(jnp.int32, (8, 1), 0)
        low = jnp.where(rows == 0, dmc, 0.0)
        dm = jnp.concatenate([dme_ref[...], low], axis=0).astype(BF16)
        wv = w_ref[...]
        g = _dot_tn(s_ref[...].astype(BF16), dm)
        delta, mn, vn = _adamw(wv, g, m_ref[...], v_ref[...])
        g_ref[...] = g
        d_ref[...] = delta
        mo_ref[...] = mn
        vo_ref[...] = vn
        ds_ref[...] = _dot_nt(low.astype(BF16), wv.astype(BF16))

    return pl.pallas_call(
        body,
        name="wada_update",
        out_shape=tuple(jax.ShapeDtypeStruct((D, ncol), F32) for _ in range(4)) + (jax.ShapeDtypeStruct((8, D), F32),),
        compiler_params=pltpu.CompilerParams(vmem_limit_bytes=VMEM_LIMIT),
    )(dm_ex, dm_ctx, silu_all, w, m, v)


P_DMOD, P_DMODC, P_EMBG, P_EMBB, P_LN1G, P_LN1B, P_LN2G, P_LN2B = 0, 6, 8, 9, 10, 11, 12, 13
P_SSDG, P_PSC, P_DSK, P_CONVB, P_DTB, P_ALOG, P_LOSS, NP = 14, 15, 16, 17, 19, 20, 21, 24
S_CCTX, S_EMBG, S_EMBB, S_BADA, S_CONVB, S_DTB, S_ALOG, S_DSK = 0, 1, 2, 3, 9, 11, 12, 13
S_SSDG, S_PSC, S_LN1G, S_LN1B, S_LN2G, S_LN2B, NSM = 14, 15, 16, 17, 18, 19, 24


def _small_update_call(pall, dsil, cctx, w, m, v, et_mat):
    def body(p_ref, ds_ref, c_ref, w_ref, m_ref, v_ref, et_ref, g_ref, d_ref, mo_ref, vo_ref, loss_ref,
             tot, dsum, dsk8):
        tot[...] = p_ref[0]
        dsum[...] = ds_ref[0]
        for i in range(1, NDEV):
            tot[...] += p_ref[i]
            dsum[...] += ds_ref[i]
        cv = c_ref[...]
        sc = _sigmoid(cv)
        g_ref[...] = jnp.zeros_like(g_ref)
        g_ref[S_CCTX:S_CCTX + 1, :] = dsum[0:1, :] * (sc * (1.0 + cv * (1.0 - sc)))
        g_ref[S_EMBG:S_EMBG + 1, :] = tot[P_EMBG:P_EMBG + 1, :]
        g_ref[S_EMBB:S_EMBB + 1, :] = tot[P_EMBB:P_EMBB + 1, :]
        g_ref[S_BADA:S_BADA + 2, :] = tot[P_DMOD:P_DMOD + 2, :] + tot[P_DMODC:P_DMODC + 2, :]
        g_ref[S_BADA + 2:S_BADA + 6, :] = tot[P_DMOD + 2:P_DMOD + 6, :]
        g_ref[S_CONVB:S_CONVB + 2, :] = tot[P_CONVB:P_CONVB + 2, :]
        g_ref[S_DTB:S_DTB + 1, :] = tot[P_DTB:P_DTB + 1, :]
        g_ref[S_ALOG:S_ALOG + 1, :] = tot[P_ALOG:P_ALOG + 1, :]
        dsk8[...] = _dot(jnp.broadcast_to(tot[P_DSK:P_DSK + 1, :], (8, D)), et_ref[:, 0:128].astype(F32), HI)
        g_ref[S_DSK:S_DSK + 1, 0:128] = dsk8[0:1, :]
        g_ref[S_SSDG:S_SSDG + 1, :] = tot[P_SSDG:P_SSDG + 1, :]
        g_ref[S_PSC:S_PSC + 1, :] = tot[P_PSC:P_PSC + 1, :]
        g_ref[S_LN1G:S_LN1G + 1, :] = tot[P_LN1G:P_LN1G + 1, :]
        g_ref[S_LN1B:S_LN1B + 1, :] = tot[P_LN1B:P_LN1B + 1, :]
        g_ref[S_LN2G:S_LN2G + 1, :] = tot[P_LN2G:P_LN2G + 1, :]
        g_ref[S_LN2B:S_LN2B + 1, :] = tot[P_LN2B:P_LN2B + 1, :]
        delta, mn, vn = _adamw(w_ref[...], g_ref[...], m_ref[...], v_ref[...])
        d_ref[...] = delta
        mo_ref[...] = mn
        vo_ref[...] = vn
        loss_ref[...] = jnp.broadcast_to(tot[P_LOSS:P_LOSS + 1, 0:128], (8, 128))

    return pl.pallas_call(
        body,
        name="small_update",
        out_shape=tuple(jax.ShapeDtypeStruct((NSM, D), F32) for _ in range(4)) + (jax.ShapeDtypeStruct((8, 128), F32),),
        scratch_shapes=[pltpu.VMEM((NP, D), F32), pltpu.VMEM((8, D), F32), pltpu.VMEM((8, 128), F32)],
        compiler_params=pltpu.CompilerParams(vmem_limit_bytes=VMEM_LIMIT),
    )(pall, dsil, cctx, w, m, v, et_mat)


def _pad_rows(flat, mult=16):
    n = flat.shape[0]
    rows = -(-n // D)
    rows = -(-rows // mult) * mult
    return jnp.pad(flat, (0, rows * D - n)).reshape(rows, D)


def _by_cols(dw):
    r = dw.shape[0]
    return jnp.transpose(dw.reshape(r, NDEV, -1), (1, 0, 2))


def _from_cols(g):
    return jnp.transpose(g, (1, 0, 2)).reshape(g.shape[1], -1)


def _pool_constants(L):
    rows = L // GW
    t_r = jnp.arange(PT) // GW
    t_c = jnp.arange(PT) % GW
    fw, bw, ic = [], [], []
    pos_r = jnp.arange(L) // GW
    pos_c = jnp.arange(L) % GW
    for g, w in enumerate(WINDOWS):
        lo, hi = -(w // 2), w - w // 2 - 1
        n_ext = PT + 2 * _halo_tokens(g)
        e_r = jnp.arange(n_ext) // GW - w // 2
        e_c = jnp.arange(n_ext) % GW
        dr = e_r[None, :] - t_r[:, None]
        dc = e_c[None, :] - t_c[:, None]
        fw.append(((dr >= lo) & (dr <= hi) & (dc >= lo) & (dc <= hi)).astype(BF16))
        bw.append(((-dr >= lo) & (-dr <= hi) & (-dc >= lo) & (-dc <= hi)).astype(BF16))
        cr = jnp.minimum(pos_r + hi, rows - 1) - jnp.maximum(pos_r + lo, 0) + 1
        cc = jnp.minimum(pos_c + hi, GW - 1) - jnp.maximum(pos_c + lo, 0) + 1
        ic.append(1.0 / (cr * cc).astype(F32))
    icnt = jnp.pad(jnp.stack(ic, axis=1), ((0, 0), (0, 124)))
    return fw, bw, icnt


def _head_matrices():
    hp = jnp.arange(D) // HP
    e = (jnp.arange(128)[:, None] == hp[None, :]).astype(BF16)
    return jnp.concatenate([e, e], axis=0), jnp.concatenate([e.T, e.T], axis=1)


def _aligned_in_proj(w):
    zpad = jnp.zeros((D, 128 - NH), w.dtype)
    return jnp.concatenate([w[:, 0:2304], w[:, 2304:2320], zpad, w[:, 2320:2336], zpad, w[:, 2336:3360]], axis=1)


def _unaligned_in_proj(dw):
    return jnp.concatenate([dw[:, 0:2304], dw[:, 2304:2320], dw[:, 2432:2448], dw[:, 2560:3584]], axis=1)


def _row(v):
    return v.reshape(1, -1).astype(F32)


def _pad_lanes(v, width=D):
    v = v.reshape(1, -1)
    return jnp.pad(v, ((0, 0), (0, width - v.shape[1])))


def kernel(x, c, ctx, c_ctx, emb_ln_g, emb_ln_b, w_ada, b_ada, in_proj, conv_w, conv_b, dt_bias, a_log, d_skip, ssd_norm_g, pool_w, pool_scale, w_out, ln1_g, ln1_b, w_gate, w_up, w_down, ln2_g, ln2_b, loss_target, m_c_ctx, m_emb_ln_g, m_emb_ln_b, m_w_ada, m_b_ada, m_in_proj, m_conv_w, m_conv_b, m_dt_bias, m_a_log, m_d_skip, m_ssd_norm_g, m_pool_w, m_pool_scale, m_w_out, m_ln1_g, m_ln1_b, m_w_gate, m_w_up, m_w_down, m_ln2_g, m_ln2_b, v_c_ctx, v_emb_ln_g, v_emb_ln_b, v_w_ada, v_b_ada, v_in_proj, v_conv_w, v_conv_b, v_dt_bias, v_a_log, v_d_skip, v_ssd_norm_g, v_pool_w, v_pool_scale, v_w_out, v_ln1_g, v_ln1_b, v_w_gate, v_w_up, v_w_down, v_ln2_g, v_ln2_b):
    me = 4 * lax.axis_index("x") + 2 * lax.axis_index("y") + lax.axis_index("c")
    x0 = x[0]
    ctx0 = ctx[0]
    tgt = loss_target[0]
    L = x0.shape[0]
    LC = ctx0.shape[0]
    ncol_ada = w_ada.shape[2]

    small_in = jnp.concatenate([c.reshape(-1), conv_w.reshape(-1)])
    gb = jnp.concatenate([_row(emb_ln_g), _row(emb_ln_b), jnp.zeros((6, D), F32)], axis=0)
    xe, small_all, g_inp = _emb_ln_call(x0, gb, "emb_ln", [_pad_rows(small_in, 8), in_proj[0].astype(BF16)])
    (xe_c,) = _emb_ln_call(ctx0, gb, "emb_ln_ctx")
    c_all = small_all[:, 0, :]
    convw_all = small_all.reshape(NDEV, -1)[:, D:D + 5 * (DXBC // NDEV)].reshape(NDEV, 5, DXBC // NDEV)
    conv_w_full = _from_cols(convw_all)
    w_in = _aligned_in_proj(_from_cols(g_inp))
    late_shards = [pool_w[0].astype(BF16), w_out[0].astype(BF16), w_gate[0].astype(BF16), w_up[0].astype(BF16),
                   w_down[0].astype(BF16)]

    c_in = jnp.concatenate([c_all, c_ctx.reshape(1, D), jnp.zeros((7, D), F32)], axis=0)
    b_mine = lax.dynamic_slice(b_ada, (0, me * ncol_ada), (1, ncol_ada))
    silu_all, mod_mine = _mod_call(c_in, w_ada[0], b_mine)
    (mod_all,) = _exchange([mod_mine], "gather_mod", True)
    mod_all = _from_cols(mod_all)
    mod_me = lax.dynamic_slice(mod_all, (me, 0), (1, 6 * D)).reshape(6, D)
    mod_ctx = mod_all[8].reshape(6, D)

    tail = jnp.concatenate([
        _row(emb_ln_g), _row(emb_ln_b), _row(ln1_g), _row(ln1_b), _row(ln2_g), _row(ln2_b),
        _row(ssd_norm_g), _row(pool_scale), _row(jnp.repeat(d_skip.reshape(-1), HP)), jnp.zeros((1, D), F32)], axis=0)
    vec = jnp.concatenate([mod_me, tail], axis=0)
    vec_ctx = jnp.concatenate([mod_ctx, tail], axis=0)

    cw = jnp.concatenate([conv_w_full, conv_b.reshape(1, DXBC), jnp.zeros((2, DXBC), F32)], axis=0)
    par = jnp.concatenate([_pad_lanes(dt_bias[0, 0], 128), _pad_lanes(dt_bias[0, 1], 128),
                           _pad_lanes(a_log[0, 0], 128), _pad_lanes(a_log[0, 1], 128),
                           jnp.zeros((4, 128), F32)], axis=0)
    e_mat, et_mat = _head_matrices()
    pmat, pmat_t, icnt = _pool_constants(L)
    dskip_row = vec[V_DSK:V_DSK + 1]

    h1_c, _, xbcr_c, dtr_c, _ = _f1_call(xe_c, vec_ctx, w_in, "inproj_fwd_ctx")
    xbc_c = _f2_call(xbcr_c, cw, "conv_fwd_ctx")
    hzero = jnp.zeros((2, D, NS), F32)
    _, _, hpf_c, hpb_c, hfin_c = _ssd_fwd_call(xbc_c, dtr_c, hzero, par, e_mat, et_mat, "ssd_fwd_ctx")

    h1, z, xbcr, dtr, up = _f1_call(xe, vec, w_in, "inproj_fwd")
    xbc = _f2_call(xbcr, cw, "conv_fwd")
    yf, yb, hpf, hpb, _, g_pw, g_wo, g_wg, g_wu, g_wd = _ssd_fwd_call(xbc, dtr, hfin_c, par, e_mat, et_mat, "ssd_fwd",
                                                                       late_shards)
    pool_w_full = jnp.transpose(g_pw, (1, 0, 2, 3)).reshape(4, PG, PG)
    w_out_full = g_wo.reshape(2 * D, D)
    w_gate_full = _from_cols(g_wg)
    w_up_full = _from_cols(g_wu)
    w_down_full = g_wd.reshape(DFF, D)
    dpool, pun = _pool_fwd_call(up, pmat, icnt, pool_w_full)
    x1, mix, cat = _merge_call(yf, yb, xbc, z, pun, xe, vec, w_out_full)
    dpre2, gacc_f, gt_b, up_b = _ffn_fwd_call(x1, tgt, vec, w_gate_full, w_up_full, w_down_full)

    dh2p, dwg2, dwu2, dwd2 = _ffn_bwd_call(x1, dpre2, gt_b, up_b, vec, w_gate_full, w_up_full, w_down_full)
    nq = FFC // (DFF // NDEV)
    ffn_parts = [
        jnp.transpose(dwg2.reshape(-1, D, nq, DFF // NDEV), (0, 2, 1, 3)).reshape(NDEV, D, DFF // NDEV),
        jnp.transpose(dwu2.reshape(-1, D, nq, DFF // NDEV), (0, 2, 1, 3)).reshape(NDEV, D, DFF // NDEV),
        dwd2.reshape(NDEV, DFF // NDEV, D)]
    dxe_part, dy, dz, dd, dpw, gacc_m, dwo, gs_wg, gs_wu, gs_wd = _merge_bwd_call(
        dh2p, dpre2, xe, mix, cat, yf, yb, xbc, z, dpool, pun, vec, w_out_full, pool_w_full, ffn_parts)
    dup = _pool_bwd_call(dd, pmat_t, icnt)
    dxf, dxb, ddt0, ddt1, sacc, dh0 = _ssd_bwd_call(dy, xbc, dtr, hpf, hpb, hzero, par, e_mat, et_mat, dskip_row,
                                                     "ssd_bwd")
    zeros_c = jnp.zeros((LC, D), F32)
    dxf_c, dxb_c, ddt0_c, ddt1_c, sacc_c, _ = _ssd_bwd_call(zeros_c, xbc_c, dtr_c, hpf_c, hpb_c, dh0, par, e_mat, et_mat,
                                                            jnp.zeros((1, D), F32), "ssd_bwd_ctx")
    dprec_c, cacc_c = _conv_bwd_call(dxf_c, dxb_c, xbcr_c, cw, jnp.zeros((8, DXBC), F32), "conv_bwd_ctx")
    _, gacc_c, dwin_c = _inproj_bwd_call(dprec_c, cw, zeros_c, ddt0_c, ddt1_c, zeros_c, h1_c, zeros_c, ctx0, vec_ctx,
                                         w_in, jnp.zeros((D, WIN), F32), "inproj_bwd_ctx")
    mix_parts = [dwo.reshape(NDEV, 2 * D // NDEV, D),
                 jnp.transpose(dpw.reshape(4, NDEV, PG // NDEV, PG), (1, 0, 2, 3)).reshape(NDEV, 4 * PG // NDEV, PG)]
    dprec, cacc, gs_wo, gs_pw = _conv_bwd_call(dxf, dxb, xbcr, cw, cacc_c, "conv_bwd", mix_parts)
    grad_x, gacc_i, dwin = _inproj_bwd_call(dprec, cw, dz, ddt0, ddt1, dup, h1, dxe_part, x0, vec, w_in, dwin_c,
                                            "inproj_bwd")

    gsum = gacc_f + gacc_m + gacc_i
    sa = sacc + sacc_c
    dtb_row = _pad_lanes(jnp.concatenate([sa[0, 0, 0:NH], sa[1, 0, 0:NH]]))
    alog_row = _pad_lanes(jnp.concatenate([sa[0, 1, 0:NH], sa[1, 1, 0:NH]]))
    convb_rows = jnp.pad(cacc[5], (0, 2 * D - DXBC)).reshape(2, D)
    pack = jnp.concatenate([
        gsum[V_SH1:V_G2 + 1],
        gacc_c[V_SH1:V_SC1 + 1],
        gsum[V_EMBG:V_EMBB + 1] + gacc_c[V_EMBG:V_EMBB + 1],
        gsum[V_LN1G:V_LN2B + 1],
        gsum[V_SSDG:V_DSK + 1],
        convb_rows, dtb_row, alog_row,
        gsum[V_LOSS:V_LOSS + 1],
        jnp.zeros((NP - 22, D), F32)], axis=0)
    (pall,) = _exchange([pack], "gather_small_grads", True)

    dm_flat = pall[:, 0:8, :].reshape(NDEV, 8 * D)
    dm_ex = lax.dynamic_slice(dm_flat, (0, me * ncol_ada), (NDEV, ncol_ada))
    dmc_full = jnp.concatenate([dm_flat[:, 6 * D:8 * D], jnp.zeros((NDEV, 4 * D), F32)], axis=1)
    dm_ctx = lax.dynamic_slice(dmc_full, (0, me * ncol_ada), (NDEV, ncol_ada))
    g_wada, d_wada, nm_wada, nv_wada, dsil = _wada_call(dm_ex, dm_ctx, silu_all, w_ada[0], m_w_ada[0], v_w_ada[0])
    (dsil_all,) = _exchange([dsil], "gather_dsilu", True)

    def small_pack(cc, eg, eb, ba, cb_, dtb, al, dsk, sg, ps, l1g, l1b, l2g, l2b):
        return jnp.concatenate([
            _row(cc), _row(eg), _row(eb), ba.reshape(6, D), jnp.pad(cb_.reshape(-1), (0, 2 * D - DXBC)).reshape(2, D),
            _pad_lanes(dtb.reshape(-1)), _pad_lanes(al.reshape(-1)), _pad_lanes(dsk.reshape(-1)),
            _row(sg), _row(ps), _row(l1g), _row(l1b), _row(l2g), _row(l2b), jnp.zeros((NSM - 20, D), F32)], axis=0)

    sw = small_pack(c_ctx, emb_ln_g, emb_ln_b, b_ada, conv_b, dt_bias, a_log, d_skip, ssd_norm_g, pool_scale,
                    ln1_g, ln1_b, ln2_g, ln2_b)
    sm = small_pack(m_c_ctx, m_emb_ln_g, m_emb_ln_b, m_b_ada, m_conv_b, m_dt_bias, m_a_log, m_d_skip, m_ssd_norm_g,
                    m_pool_scale, m_ln1_g, m_ln1_b, m_ln2_g, m_ln2_b)
    sv = small_pack(v_c_ctx, v_emb_ln_g, v_emb_ln_b, v_b_ada, v_conv_b, v_dt_bias, v_a_log, v_d_skip, v_ssd_norm_g,
                    v_pool_scale, v_ln1_g, v_ln1_b, v_ln2_g, v_ln2_b)
    s_g, s_d, s_m, s_v, loss8 = _small_update_call(pall, dsil_all, _row(c_ctx), sw, sm, sv, et_mat)

    def small_unpack(t):
        return (t[S_CCTX], t[S_EMBG], t[S_EMBB], t[S_BADA:S_BADA + 6].reshape(1, 6 * D),
                t[S_CONVB:S_CONVB + 2].reshape(-1)[:DXBC].reshape(1, DXBC),
                t[S_DTB, 0:2 * NH].reshape(1, 2, NH), t[S_ALOG, 0:2 * NH].reshape(1, 2, NH), t[S_DSK, 0:NH].reshape(1, NH),
                t[S_SSDG].reshape(1, D), t[S_PSC].reshape(1, D), t[S_LN1G].reshape(1, D), t[S_LN1B].reshape(1, D),
                t[S_LN2G].reshape(1, D), t[S_LN2B].reshape(1, D))

    gs_inp, gs_cw = _exchange([_by_cols(_unaligned_in_proj(dwin)).astype(BF16), _by_cols(cacc[0:5])],
                              "exchange_last_grads", False)

    pshape = (4 * PG // NDEV, PG)
    u_inp = _adamw_shard_call(gs_inp, in_proj[0], m_in_proj[0], v_in_proj[0], 256, "adamw_in_proj")
    u_cw = _adamw_shard_call(gs_cw, conv_w[0], m_conv_w[0], v_conv_w[0], 5, "adamw_conv_w")
    u_pw = _adamw_shard_call(gs_pw, pool_w[0].reshape(pshape), m_pool_w[0].reshape(pshape), v_pool_w[0].reshape(pshape),
                             pshape[0], "adamw_pool_w")
    u_wo = _adamw_shard_call(gs_wo, w_out[0], m_w_out[0], v_w_out[0], 64, "adamw_w_out")
    u_wg = _adamw_shard_call(gs_wg, w_gate[0], m_w_gate[0], v_w_gate[0], 256, "adamw_w_gate")
    u_wu = _adamw_shard_call(gs_wu, w_up[0], m_w_up[0], v_w_up[0], 256, "adamw_w_up")
    u_wd = _adamw_shard_call(gs_wd, w_down[0], m_w_down[0], v_w_down[0], 88, "adamw_w_down")

    def assemble(k, small, wada):
        (cc, eg, eb, ba, cb_, dtb, al, dsk, sg, ps, l1g, l1b, l2g, l2b) = small_unpack(small)
        pw = u_pw[k].reshape(1, 4, PG // NDEV, PG)
        return (cc, eg, eb, wada[None], ba, u_inp[k][None], u_cw[k][None], cb_, dtb, al, dsk, sg, pw, ps,
                u_wo[k][None], l1g, l1b, u_wg[k][None], u_wu[k][None], u_wd[k][None], l2g, l2b)

    loss = loss8[0, 0]
    return (loss, grad_x[None], *assemble(0, s_g, g_wada), *assemble(1, s_d, d_wada),
            *assemble(2, s_m, nm_wada), *assemble(3, s_v, nv_wada))
```

```python
import functools
import math

import jax
import jax.numpy as jnp
from jax import lax
from jax.experimental import pallas as pl
from jax.experimental.pallas import tpu as pltpu

F32 = jnp.float32
BF16 = jnp.bfloat16
HI = lax.Precision.HIGHEST

NDEV = 8
D = 1024
NH = 16
HP = 64
NS = 128
Q = 128
DXBC = 1280
DFF = 2816
FFC = 1408
GW = 64
PR = 8
PT = PR * GW
WINDOWS = (2, 4, 8, 16)
PG = 256
DIN = 3360
WIN = 3584
ALPHA = 2.0 ** 0.25
LN_EPS = 1e-5
TM = 256

ADAM_LR = 0.001
ADAM_B1 = 0.9
ADAM_B2 = 0.999
ADAM_EPS = 1e-08
ADAM_WD = 0.01
ADAM_STEP = 10

V_SH1, V_SC1, V_G1, V_SH2, V_SC2, V_G2 = 0, 1, 2, 3, 4, 5
V_EMBG, V_EMBB, V_LN1G, V_LN1B, V_LN2G, V_LN2B = 6, 7, 8, 9, 10, 11
V_SSDG, V_PSC, V_DSK, V_LOSS = 12, 13, 14, 15
NV = 16

VMEM_LIMIT = 60 * 1024 * 1024


def _cp(ndim=1):
    return pltpu.CompilerParams(dimension_semantics=("arbitrary",) * ndim, vmem_limit_bytes=VMEM_LIMIT)


def _dot(a, b, precision=None):
    return jnp.dot(a, b, preferred_element_type=F32, precision=precision)


def _dot_nt(a, b):
    return lax.dot_general(a, b, (((1,), (1,)), ((), ())), preferred_element_type=F32)


def _dot_tn(a, b, precision=None):
    return lax.dot_general(a, b, (((0,), (0,)), ((), ())), preferred_element_type=F32, precision=precision)


def _split2(x):
    hi = x.astype(BF16)
    return hi, (x - hi.astype(F32)).astype(BF16)


def _split_dot(m, x):
    hi, lo = _split2(x)
    return _dot(m, hi) + _dot(m, lo)


def _dot_split(x, m):
    hi, lo = _split2(x)
    return _dot(hi, m) + _dot(lo, m)


def _sigmoid(x):
    return 1.0 / (1.0 + jnp.exp(-x))


def _softplus(x):
    return jnp.maximum(x, 0.0) + jnp.log(1.0 + jnp.exp(-jnp.abs(x)))


def _ln(x, g, b):
    mu = jnp.mean(x, axis=-1, keepdims=True)
    xc = x - mu
    var = jnp.mean(xc * xc, axis=-1, keepdims=True)
    rstd = lax.rsqrt(var + LN_EPS)
    n = xc * rstd
    return n * g + b, n, rstd


def _ln_bwd(dy, n, rstd, g):
    dn = dy * g
    return rstd * (dn - jnp.mean(dn, axis=-1, keepdims=True) - n * jnp.mean(dn * n, axis=-1, keepdims=True))


def _rowsum(x):
    return jnp.sum(x, axis=0, keepdims=True)


def _resident(shape):
    nd = len(shape)
    return pl.BlockSpec(shape, lambda *_: (0,) * nd, pipeline_mode=pl.Buffered(1))


def _const_out(shape):
    nd = len(shape)
    return pl.BlockSpec(shape, lambda *_: (0,) * nd)


def _tiles(tm, width):
    return pl.BlockSpec((tm, width), lambda i: (i, 0))


def _halo_specs(tm, width, n_rows):
    r = tm // 8
    last = n_rows // 8 - 1
    prev = pl.BlockSpec((8, width), lambda i: (jnp.maximum(i * r - 1, 0), 0))
    nxt = pl.BlockSpec((8, width), lambda i: (jnp.minimum((i + 1) * r, last), 0))
    return prev, nxt


def _acc_tn(acc_ref, a, b, chunk=512):
    n = b.shape[1]
    for c0 in range(0, n, chunk):
        c1 = min(c0 + chunk, n)
        acc_ref[:, c0:c1] += _dot_tn(a, b[:, c0:c1])


def _my_coords():
    return lax.axis_index("x"), lax.axis_index("y"), lax.axis_index("c")


def _peer(k, mx, my, mc):
    kx, ky, kc = (k >> 2) & 1, (k >> 1) & 1, k & 1
    px = 1 - mx if kx else mx
    py = 1 - my if ky else my
    pc = 1 - mc if kc else mc
    return px, py, pc


class _Exchange:
    def __init__(self, srcs, dsts, send_sems, recv_sems, local_sems, gather):
        self.srcs, self.dsts, self.gather = srcs, dsts, gather
        self.send_sems, self.recv_sems, self.local_sems = send_sems, recv_sems, local_sems

    def _copies(self, outgoing):
        mx, my, mc = _my_coords()
        me = 4 * mx + 2 * my + mc
        local, remote = [], []
        for t, (src, dst) in enumerate(zip(self.srcs, self.dsts)):
            local.append(pltpu.make_async_copy(src if self.gather else src.at[me], dst.at[me], self.local_sems.at[t]))
            for k in range(1, NDEV):
                px, py, pc = _peer(k, mx, my, mc)
                pid = 4 * px + 2 * py + pc
                remote.append(pltpu.make_async_remote_copy(
                    src_ref=src if self.gather else src.at[pid],
                    dst_ref=dst.at[me] if outgoing else dst.at[pid],
                    send_sem=self.send_sems.at[t, k - 1],
                    recv_sem=self.recv_sems.at[t, k - 1],
                    device_id=(px, py, pc),
                    device_id_type=pl.DeviceIdType.MESH,
                ))
        return local, remote

    def start(self):
        local, remote = self._copies(True)
        for cp in local + remote:
            cp.start()

    def wait(self):
        local, sends = self._copies(True)
        _, recvs = self._copies(False)
        for cp in recvs:
            cp.wait_recv()
        for cp in sends:
            cp.wait_send()
        for cp in local:
            cp.wait()


class _ChipGather:
    def __init__(self, srcs, dsts, send_sems, recv_sems, local_sems):
        self.srcs, self.dsts = srcs, dsts
        self.send_sems, self.recv_sems, self.local_sems = send_sems, recv_sems, local_sems

    def _places(self):
        x, y, c = _my_coords()
        return (x, y, c), (x, y, 1 - c), [(1 - x, y), (x, 1 - y), (1 - x, 1 - y)]

    def _copy(self, t, k, block, to, own=False):
        slot = self.dsts[t].at[4 * block[0] + 2 * block[1] + block[2]]
        return pltpu.make_async_remote_copy(
            src_ref=self.srcs[t] if own else slot, dst_ref=slot,
            send_sem=self.send_sems.at[t, k], recv_sem=self.recv_sems.at[t, k],
            device_id=to, device_id_type=pl.DeviceIdType.MESH)

    def _local(self, t, me):
        return pltpu.make_async_copy(self.srcs[t], self.dsts[t].at[4 * me[0] + 2 * me[1] + me[2]], self.local_sems.at[t])

    def start(self):
        me, sib, chips = self._places()
        for t in range(len(self.srcs)):
            self._local(t, me).start()
            self._copy(t, 0, me, sib, own=True).start()
            for j, chip in enumerate(chips):
                self._copy(t, 1 + j, me, (*chip, me[2]), own=True).start()

    def wait(self):
        me, sib, chips = self._places()
        n = len(self.srcs)
        for t in range(n):
            for j, chip in enumerate(chips):
                self._copy(t, 1 + j, (*chip, me[2]), me).wait_recv()
                self._copy(t, 4 + j, (*chip, me[2]), sib).start()
        for t in range(n):
            self._copy(t, 0, sib, me).wait_recv()
            for j, chip in enumerate(chips):
                self._copy(t, 4 + j, (*chip, sib[2]), me).wait_recv()
            self._copy(t, 0, me, sib, own=True).wait_send()
            for j, chip in enumerate(chips):
                self._copy(t, 1 + j, me, (*chip, me[2]), own=True).wait_send()
                self._copy(t, 4 + j, (*chip, me[2]), sib).wait_send()
            self._local(t, me).wait()


def _exchange_sems(n):
    return [pltpu.SemaphoreType.DMA((n, NDEV - 1)), pltpu.SemaphoreType.DMA((n, NDEV - 1)), pltpu.SemaphoreType.DMA((n,))]


def _exchange_out_shapes(xs, gather):
    return [jax.ShapeDtypeStruct(x.shape if not gather else (NDEV,) + x.shape, x.dtype) for x in xs]


def _exchange(xs, name, gather):
    n = len(xs)

    def body(*refs):
        ex = _Exchange(refs[:n], refs[n:2 * n], *refs[2 * n:], gather)
        ex.start()
        ex.wait()

    any_spec = pl.BlockSpec(memory_space=pl.ANY)
    return pl.pallas_call(
        body,
        name=name,
        out_shape=_exchange_out_shapes(xs, gather),
        in_specs=[any_spec] * n,
        out_specs=[any_spec] * n,
        scratch_shapes=_exchange_sems(n),
    )(*xs)


def _mod_call(c_all, w_ada, b_ada):
    ncol = w_ada.shape[1]

    def body(c_ref, w_ref, b_ref, silu_ref, mod_ref):
        cv = c_ref[...]
        s = cv * _sigmoid(cv)
        silu_ref[...] = s
        mod_ref[...] = _dot(s.astype(BF16), w_ref[...].astype(BF16)) + b_ref[...]

    return pl.pallas_call(
        body,
        name="mod_fwd",
        out_shape=(jax.ShapeDtypeStruct((16, D), F32), jax.ShapeDtypeStruct((16, ncol), F32)),
    )(c_all, w_ada, b_ada)


def _emb_ln_call(x0, gb, name, gathered=()):
    L = x0.shape[0]
    nt = L // TM
    ng = len(gathered)

    def body(*refs):
        x_ref, gb_ref = refs[:2]
        xe_ref = refs[2 + ng]
        i = pl.program_id(0)
        if ng:
            ex = _ChipGather(refs[2:2 + ng], refs[3 + ng:3 + 2 * ng], *refs[3 + 2 * ng:])

            @pl.when(i == 0)
            def _():
                ex.start()

        xe_ref[...] = _ln(x_ref[...], gb_ref[0:1, :], gb_ref[1:2, :])[0]
        if ng:
            @pl.when(i == nt - 1)
            def _():
                ex.wait()

    any_spec = pl.BlockSpec(memory_space=pl.ANY)
    return pl.pallas_call(
        body,
        name=name,
        grid=(nt,),
        in_specs=[_tiles(TM, D), _resident((8, D))] + [any_spec] * ng,
        out_specs=[_tiles(TM, D)] + [any_spec] * ng,
        out_shape=[jax.ShapeDtypeStruct((L, D), F32)] + _exchange_out_shapes(gathered, True),
        scratch_shapes=_exchange_sems(ng) if ng else [],
        compiler_params=_cp(),
    )(x0, gb, *gathered)


def _f1_call(xe, vec, w_in, cw, name):
    L = xe.shape[0]
    prev, nxt = _halo_specs(TM, D, L)

    def body(xe_ref, prev_ref, next_ref, vec_ref, w_ref, cw_ref, h1_ref, z_ref, xbcr_ref, xbc_ref, dt_ref, up_ref,
             ext):
        i = pl.program_id(0)
        n = pl.num_programs(0)
        def modulated(v):
            return (v * (1.0 + vec_ref[V_SC1:V_SC1 + 1, :]) + vec_ref[V_SH1:V_SH1 + 1, :]).astype(BF16)

        rows = jnp.concatenate([prev_ref[...], xe_ref[...], next_ref[...]], axis=0)
        proj = _dot(modulated(rows), w_ref[...])
        h1_ref[...] = modulated(xe_ref[...])
        z_ref[...] = proj[8:8 + TM, 0:1024]
        xbcr_ref[...] = proj[8:8 + TM, 1024:2304]
        dt_ref[...] = proj[8:8 + TM, 2304:2560]
        up_ref[...] = proj[8:8 + TM, 2560:3584]
        ext[...] = proj[:, 1024:2304]
        ext[0:8, :] = jnp.where(i > 0, ext[0:8, :], 0.0)
        ext[8 + TM:16 + TM, :] = jnp.where(i < n - 1, ext[8 + TM:16 + TM, :], 0.0)
        pre = _conv_pre(ext, cw_ref, TM)
        xbc_ref[...] = pre * _sigmoid(pre)

    return pl.pallas_call(
        body,
        name=name,
        grid=(L // TM,),
        in_specs=[_tiles(TM, D), prev, nxt, _resident((NV, D)), _resident((D, WIN)), _resident((8, DXBC))],
        out_specs=[_tiles(TM, D), _tiles(TM, D), _tiles(TM, DXBC), _tiles(TM, DXBC), _tiles(TM, 256), _tiles(TM, D)],
        out_shape=(
            jax.ShapeDtypeStruct((L, D), BF16),
            jax.ShapeDtypeStruct((L, D), F32),
            jax.ShapeDtypeStruct((L, DXBC), F32),
            jax.ShapeDtypeStruct((L, DXBC), F32),
            jax.ShapeDtypeStruct((L, 256), F32),
            jax.ShapeDtypeStruct((L, D), F32),
        ),
        scratch_shapes=[pltpu.VMEM((TM + 16, DXBC), F32)],
        compiler_params=_cp(),
    )(xe, xe, xe, vec, w_in, cw)


def _extended(ext, cur_ref, prev_ref, next_ref):
    i = pl.program_id(0)
    n = pl.num_programs(0)
    tm = cur_ref.shape[0]
    ext[0:8, :] = jnp.where(i > 0, prev_ref[...], 0.0)
    ext[8:8 + tm, :] = cur_ref[...]
    ext[8 + tm:16 + tm, :] = jnp.where(i < n - 1, next_ref[...], 0.0)
    return ext


def _shifted(ext, offset, tm):
    return ext[8 + offset:8 + offset + tm, :]


def _conv_pre(ext, cw_ref, tm):
    acc = cw_ref[5:6, :] + cw_ref[0:1, :] * _shifted(ext, -2, tm)
    for k in range(1, 5):
        acc = acc + cw_ref[k:k + 1, :] * _shifted(ext, k - 2, tm)
    return acc


def _ssd_common(d, dtr, par_ref):
    lane = lax.broadcasted_iota(jnp.int32, (1, 128), 1)
    hmask = lane < NH
    bias = par_ref[d:d + 1, :]
    alog = par_ref[2 + d:3 + d, :]
    aneg = jnp.where(hmask, -jnp.exp(alog), 0.0)
    pre = dtr + bias
    dt = jnp.where(hmask, _softplus(pre), 0.0)
    a = dt * aneg
    row = lax.broadcasted_iota(jnp.int32, (Q, Q), 0)
    col = lax.broadcasted_iota(jnp.int32, (Q, Q), 1)
    maskf = ((row >= col) if d == 0 else (row <= col)).astype(F32)
    mask_t = ((row <= col) if d == 0 else (row >= col)).astype(F32)
    A = _tri_dot(maskf, a)
    atot = _rowsum(a)
    return dict(hmask=hmask, aneg=aneg, pre=pre, dt=dt, a=a, maskf=maskf, mask_t=mask_t, A=A, AT=A.T, dtT=dt.T,
                atot=atot, lane=lane)


def _tri_dot(mask, x):
    t1 = x.astype(BF16)
    r1 = x - t1.astype(F32)
    t2 = r1.astype(BF16)
    t3 = (r1 - t2.astype(F32)).astype(BF16)
    mb = mask.astype(BF16)
    return _dot(jnp.concatenate([mb, mb, mb], axis=1), jnp.concatenate([t1, t2, t3], axis=0))


def _column(v, lane, h):
    return jnp.sum(jnp.where(lane == h, v, 0.0), axis=1, keepdims=True)


def _head_expand(e_ref, v):
    hi, lo = _split2(v)
    return _dot(jnp.concatenate([hi, lo], axis=1), e_ref[...])


def _head_sum(et_ref, v):
    return _dot(v.astype(BF16), et_ref[:, 0:128])


def _state_decay(atot, lane):
    ea = jnp.exp(atot)
    return jnp.concatenate([jnp.broadcast_to(_column(ea, lane, h), (HP, NS)) for h in range(NH)], axis=0)


def _ssd_fwd_call(xbc, dtr, h0, par, e_mat, et_mat, name, gathered=()):
    L = xbc.shape[0]
    nc = L // Q
    ng = len(gathered)

    def body(*refs):
        xbc_refs, dtr_refs = refs[0:2], refs[2:4]
        h0_ref, par_ref, e_ref, et_ref = refs[4:8]
        y_refs, hp_refs = refs[8 + ng:10 + ng], refs[10 + ng:12 + ng]
        hf_ref = refs[12 + ng]
        hs, AT, dtT = refs[13 + 2 * ng:16 + 2 * ng]
        s = pl.program_id(0)
        if ng:
            ex = _Exchange(refs[8:8 + ng], refs[13 + ng:13 + 2 * ng], *refs[16 + 2 * ng:], True)

        @pl.when(s == 0)
        def _():
            if ng:
                ex.start()
            hs[...] = h0_ref[...]

        for d in range(2):
            xbc_ref, y_ref = xbc_refs[d], y_refs[d]
            q = _ssd_common(d, dtr_refs[d][...], par_ref)
            A, maskf, lane = q["A"], q["maskf"], q["lane"]
            AT[d] = q["AT"]
            dtT[d] = q["dtT"]
            hprev = hs[d]
            hp_refs[d][0] = hprev
            bb = xbc_ref[:, 1024:1152].astype(BF16)
            cb = xbc_ref[:, 1152:1280].astype(BF16)
            g = _dot_nt(cb, bb)
            yoff = _dot_nt(cb, hprev.astype(BF16)) * _head_expand(e_ref, jnp.exp(A))
            for k in range(NH // 2):
                ks = slice(128 * k, 128 * k + 128)
                xp = xbc_ref[:, ks]
                scs, xhs = [], []
                for half in range(2):
                    h = 2 * k + half
                    seg = _column(A, lane, h) - AT[d, h:h + 1, :]
                    lm = jnp.exp(jnp.minimum(seg, 0.0)) * maskf
                    scs.append((g * lm * dtT[d, h:h + 1, :]).astype(BF16))
                    inhead = (lane >= HP) if half else (lane < HP)
                    xhs.append(jnp.where(inhead, xp, 0.0).astype(BF16))
                y_ref[:, ks] = yoff[:, ks] + _dot(jnp.concatenate(scs, axis=1), jnp.concatenate(xhs, axis=0))
            wend = jnp.exp(q["atot"] - A) * q["dt"]
            xw = (xbc_ref[:, 0:1024] * _head_expand(e_ref, wend)).astype(BF16)
            hnew = hprev * _state_decay(q["atot"], lane) + _dot_tn(xw, bb)
            hs[d] = hnew
            hf_ref[d] = hnew
        if ng:
            @pl.when(s == nc - 1)
            def _():
                ex.wait()

    any_spec = pl.BlockSpec(memory_space=pl.ANY)
    return pl.pallas_call(
        body,
        name=name,
        grid=(nc,),
        in_specs=[
            pl.BlockSpec((Q, DXBC), lambda s: (s, 0)),
            pl.BlockSpec((Q, DXBC), lambda s: (nc - 1 - s, 0)),
            pl.BlockSpec((Q, 128), lambda s: (s, 0)),
            pl.BlockSpec((Q, 128), lambda s: (nc - 1 - s, 1)),
            _resident((2, D, NS)),
            _resident((8, 128)),
            _resident((256, D)),
            _resident((D, 256)),
        ] + [any_spec] * ng,
        out_specs=[
            pl.BlockSpec((Q, D), lambda s: (s, 0)),
            pl.BlockSpec((Q, D), lambda s: (nc - 1 - s, 0)),
            pl.BlockSpec((1, D, NS), lambda s: (s, 0, 0)),
            pl.BlockSpec((1, D, NS), lambda s: (nc - 1 - s, 0, 0)),
            _const_out((2, D, NS)),
        ] + [any_spec] * ng,
        out_shape=[
            jax.ShapeDtypeStruct((L, D), F32),
            jax.ShapeDtypeStruct((L, D), F32),
            jax.ShapeDtypeStruct((nc, D, NS), F32),
            jax.ShapeDtypeStruct((nc, D, NS), F32),
            jax.ShapeDtypeStruct((2, D, NS), F32),
        ] + _exchange_out_shapes(gathered, True),
        scratch_shapes=[pltpu.VMEM((2, D, NS), F32), pltpu.VMEM((2, 128, Q), F32), pltpu.VMEM((2, 128, Q), F32)]
        + (_exchange_sems(ng) if ng else []),
        compiler_params=_cp(),
    )(xbc, xbc, dtr, dtr, h0, par, e_mat, et_mat, *gathered)


def _halo_tokens(g):
    return (WINDOWS[g] // 2) * GW


def _pool_specs(n_tiles):
    cur = pl.BlockSpec((PT, D), lambda i: (i, 0))
    prev = pl.BlockSpec((PT, D), lambda i: (jnp.maximum(i - 1, 0), 0))
    nxt = pl.BlockSpec((PT, D), lambda i: (jnp.minimum(i + 1, n_tiles - 1), 0))
    return cur, prev, nxt


def _pool_fwd_call(up, pmat, icnt, pool_w):
    L = up.shape[0]
    nt = L // PT
    cur, prev, nxt = _pool_specs(nt)

    def body(cur_ref, prev_ref, next_ref, m0_ref, m1_ref, m2_ref, m3_ref, ic_ref, pw_ref, d_ref, pun_ref):
        i = pl.program_id(0)
        n = pl.num_programs(0)
        lane = lax.broadcasted_iota(jnp.int32, (1, 128), 1)
        icv = ic_ref[...]
        for g, m_ref in enumerate((m0_ref, m1_ref, m2_ref, m3_ref)):
            gs = slice(PG * g, PG * g + PG)
            halo = _halo_tokens(g)
            top = jnp.where(i > 0, prev_ref[PT - halo:PT, gs], 0.0)
            bot = jnp.where(i < n - 1, next_ref[0:halo, gs], 0.0)
            mid = cur_ref[:, gs]
            box = _split_dot(m_ref[...], jnp.concatenate([top, mid, bot], axis=0))
            dg = (box * _column(icv, lane, g) - mid).astype(BF16)
            d_ref[:, gs] = dg
            pun_ref[:, gs] = _dot(dg, pw_ref[g])

    return pl.pallas_call(
        body,
        name="pool_fwd",
        grid=(nt,),
        in_specs=[cur, prev, nxt] + [_resident(m.shape) for m in pmat] + [_tiles(PT, 128), _resident((4, PG, PG))],
        out_specs=[_tiles(PT, D), _tiles(PT, D)],
        out_shape=(jax.ShapeDtypeStruct((L, D), BF16), jax.ShapeDtypeStruct((L, D), F32)),
        compiler_params=_cp(),
    )(up, up, up, *pmat, icnt, pool_w)


def _gated(yf_ref, yb_ref, xs, z, vec_ref):
    ym = yf_ref[...] + yb_ref[...] + vec_ref[V_DSK:V_DSK + 1, :] * xs
    sz = _sigmoid(z)
    gated = ym * (z * sz)
    r = lax.rsqrt(jnp.mean(gated * gated, axis=-1, keepdims=True) + LN_EPS)
    return ym, sz, gated, r


def _merge_call(yf, yb, xbc, z, pun, xe, vec, w_out):
    L = z.shape[0]

    def body(yf_ref, yb_ref, xs_ref, z_ref, pun_ref, xe_ref, vec_ref, w_ref, x1_ref, mix_ref, cat_ref):
        _, _, gated, r = _gated(yf_ref, yb_ref, xs_ref[...], z_ref[...], vec_ref)
        yn = gated * r * vec_ref[V_SSDG:V_SSDG + 1, :]
        p = pun_ref[...] * vec_ref[V_PSC:V_PSC + 1, :]
        cat = jnp.concatenate([yn, p], axis=1).astype(BF16)
        mix = _dot(cat, w_ref[...])
        pre1 = ALPHA * xe_ref[...] + vec_ref[V_G1:V_G1 + 1, :] * mix
        x1, _, _ = _ln(pre1, vec_ref[V_LN1G:V_LN1G + 1, :], vec_ref[V_LN1B:V_LN1B + 1, :])
        x1_ref[...] = x1
        mix_ref[...] = mix
        cat_ref[...] = cat

    return pl.pallas_call(
        body,
        name="merge_fwd",
        grid=(L // TM,),
        in_specs=[
            _tiles(TM, D), _tiles(TM, D), _tiles(TM, D), _tiles(TM, D), _tiles(TM, D), _tiles(TM, D),
            _resident((NV, D)), _resident((2 * D, D)),
        ],
        out_specs=[_tiles(TM, D), _tiles(TM, D), _tiles(TM, 2 * D)],
        out_shape=(
            jax.ShapeDtypeStruct((L, D), F32),
            jax.ShapeDtypeStruct((L, D), F32),
            jax.ShapeDtypeStruct((L, 2 * D), BF16),
        ),
        compiler_params=_cp(),
    )(yf, yb, xbc, z, pun, xe, vec, w_out)


def _ffn_fwd_call(x1, tgt, vec, w_gate, w_up, w_down):
    L = x1.shape[0]

    def body(x1_ref, tgt_ref, vec_ref, wg_ref, wu_ref, wd_ref, dpre_ref, gacc_ref, gt_ref, up_ref):
        @pl.when(pl.program_id(0) == 0)
        def _():
            gacc_ref[...] = jnp.zeros_like(gacc_ref)

        x1 = x1_ref[...]
        h2 = (x1 * (1.0 + vec_ref[V_SC2:V_SC2 + 1, :]) + vec_ref[V_SH2:V_SH2 + 1, :]).astype(BF16)
        gt = _dot(h2, wg_ref[...])
        up = _dot(h2, wu_ref[...])
        gt_ref[...] = gt.astype(BF16)
        up_ref[...] = up.astype(BF16)
        f = (gt * _sigmoid(gt) * up).astype(BF16)
        ffn = _dot(f, wd_ref[...])
        g2 = vec_ref[V_G2:V_G2 + 1, :]
        lng = vec_ref[V_LN2G:V_LN2G + 1, :]
        x2, n2, rstd2 = _ln(ALPHA * x1 + g2 * ffn, lng, vec_ref[V_LN2B:V_LN2B + 1, :])
        diff = x2 - tgt_ref[...]
        dx2 = diff * (1.0 / D)
        dpre2 = _ln_bwd(dx2, n2, rstd2, lng)
        dpre_ref[...] = dpre2
        gacc_ref[V_LN2G:V_LN2G + 1, :] += _rowsum(dx2 * n2)
        gacc_ref[V_LN2B:V_LN2B + 1, :] += _rowsum(dx2)
        gacc_ref[V_G2:V_G2 + 1, :] += _rowsum(dpre2 * ffn)
        gacc_ref[V_LOSS:V_LOSS + 1, :] += jnp.sum(diff * diff) * (0.5 / D)

    return pl.pallas_call(
        body,
        name="ffn_fwd",
        grid=(L // TM,),
        in_specs=[_tiles(TM, D), _tiles(TM, D), _resident((NV, D)),
                  _resident((D, DFF)), _resident((D, DFF)), _resident((DFF, D))],
        out_specs=[_tiles(TM, D), _const_out((NV, D)), _tiles(TM, DFF), _tiles(TM, DFF)],
        out_shape=(jax.ShapeDtypeStruct((L, D), F32), jax.ShapeDtypeStruct((NV, D), F32),
                   jax.ShapeDtypeStruct((L, DFF), BF16), jax.ShapeDtypeStruct((L, DFF), BF16)),
        compiler_params=_cp(),
    )(x1, tgt, vec, w_gate, w_up, w_down)


def _ffn_bwd_call(x1, dpre2, gt_b, up_b, vec, w_gate, w_up, w_down):
    L = x1.shape[0]
    nt = L // TM
    nj = DFF // FFC

    def body(x1_ref, dpre_ref, gt_ref, up_ref, vec_ref, wg_ref, wu_ref, wd_ref, dh2_ref, dwg_ref, dwu_ref, dwd_ref,
             ag, au, ad):
        j = pl.program_id(0)
        i = pl.program_id(1)

        @pl.when(i == 0)
        def _():
            ag[...] = jnp.zeros_like(ag)
            au[...] = jnp.zeros_like(au)
            ad[...] = jnp.zeros_like(ad)

        h2 = (x1_ref[...] * (1.0 + vec_ref[V_SC2:V_SC2 + 1, :]) + vec_ref[V_SH2:V_SH2 + 1, :]).astype(BF16)
        gt = gt_ref[...].astype(F32)
        up = up_ref[...].astype(F32)
        sg = _sigmoid(gt)
        sl = gt * sg
        f = (sl * up).astype(BF16)
        dffn = (vec_ref[V_G2:V_G2 + 1, :] * dpre_ref[...]).astype(BF16)
        df = _dot_nt(dffn, wd_ref[...])
        dgt = (df * up * (sg * (1.0 + gt * (1.0 - sg)))).astype(BF16)
        dup = (df * sl).astype(BF16)
        dh2_ref[0] = _dot_nt(dgt, wg_ref[...]) + _dot_nt(dup, wu_ref[...])
        _acc_tn(ag, h2, dgt)
        _acc_tn(au, h2, dup)
        _acc_tn(ad, f, dffn)

        @pl.when(i == nt - 1)
        def _():
            pltpu.sync_copy(ag, dwg_ref.at[j])
            pltpu.sync_copy(au, dwu_ref.at[j])
            pltpu.sync_copy(ad, dwd_ref.at[j])

    any_spec = pl.BlockSpec(memory_space=pl.ANY)
    return pl.pallas_call(
        body,
        name="ffn_bwd",
        grid=(nj, nt),
        in_specs=[
            pl.BlockSpec((TM, D), lambda j, i: (i, 0)),
            pl.BlockSpec((TM, D), lambda j, i: (i, 0)),
            pl.BlockSpec((TM, FFC), lambda j, i: (i, j)),
            pl.BlockSpec((TM, FFC), lambda j, i: (i, j)),
            _resident((NV, D)),
            pl.BlockSpec((D, FFC), lambda j, i: (0, j)),
            pl.BlockSpec((D, FFC), lambda j, i: (0, j)),
            pl.BlockSpec((FFC, D), lambda j, i: (j, 0)),
        ],
        out_specs=[pl.BlockSpec((1, TM, D), lambda j, i: (j, i, 0)), any_spec, any_spec, any_spec],
        out_shape=(
            jax.ShapeDtypeStruct((nj, L, D), F32),
            jax.ShapeDtypeStruct((nj, D, FFC), F32),
            jax.ShapeDtypeStruct((nj, D, FFC), F32),
            jax.ShapeDtypeStruct((nj, FFC, D), F32),
        ),
        scratch_shapes=[pltpu.VMEM((D, FFC), F32), pltpu.VMEM((D, FFC), F32), pltpu.VMEM((FFC, D), F32)],
        compiler_params=_cp(2),
    )(x1, dpre2, gt_b, up_b, vec, w_gate, w_up, w_down)


def _merge_bwd_call(dh2p, dpre2, xe, mix, cat, yf, yb, xbc, z, dpool, pun, vec, w_out, pool_w, scattered):
    L = z.shape[0]
    nt = L // TM
    ns = len(scattered)

    def body(*refs):
        (dh2_ref, dpre2_ref, xe_ref, mix_ref, cat_ref, yf_ref, yb_ref, xs_ref, z_ref, dpool_ref, pun_ref,
         vec_ref, w_ref, pw_ref) = refs[:14]
        dxe_ref, dy_ref, dz_ref, dd_ref, dpw_ref, gacc_ref, dwo_ref = refs[14 + ns:21 + ns]
        dwo_s = refs[21 + 2 * ns]
        ex = _Exchange(refs[14:14 + ns], refs[21 + ns:21 + 2 * ns], *refs[22 + 2 * ns:], False)
        i = pl.program_id(0)

        @pl.when(i == 0)
        def _():
            ex.start()
            gacc_ref[...] = jnp.zeros_like(gacc_ref)
            dpw_ref[...] = jnp.zeros_like(dpw_ref)
            dwo_s[...] = jnp.zeros_like(dwo_s)

        def vrow(r):
            return vec_ref[r:r + 1, :]

        def gadd(r, val):
            gacc_ref[r:r + 1, :] += _rowsum(val)

        dh2 = dh2_ref[0] + dh2_ref[1]
        dx1 = ALPHA * dpre2_ref[...] + dh2 * (1.0 + vrow(V_SC2))
        mix = mix_ref[...]
        x1, n1, rstd1 = _ln(ALPHA * xe_ref[...] + vrow(V_G1) * mix, vrow(V_LN1G), vrow(V_LN1B))
        gadd(V_SC2, dh2 * x1)
        gadd(V_SH2, dh2)
        gadd(V_LN1G, dx1 * n1)
        gadd(V_LN1B, dx1)
        dpre1 = _ln_bwd(dx1, n1, rstd1, vrow(V_LN1G))
        dxe_ref[...] = ALPHA * dpre1
        gadd(V_G1, dpre1 * mix)
        dmix = (vrow(V_G1) * dpre1).astype(BF16)
        dcat = _dot_nt(dmix, w_ref[...])
        cat = cat_ref[...]
        for c0 in range(0, 2 * D, 512):
            dwo_s[c0:c0 + 512, :] += _dot_tn(cat[:, c0:c0 + 512], dmix)
        dyn = dcat[:, 0:D]
        dp = dcat[:, D:2 * D]
        xs = xs_ref[...]
        z = z_ref[...]
        ym, sz, gated, r = _gated(yf_ref, yb_ref, xs, z, vec_ref)
        gadd(V_SSDG, dyn * gated * r)
        a = dyn * vrow(V_SSDG)
        dgated = r * a - gated * (r * r * r * jnp.mean(a * gated, axis=-1, keepdims=True))
        dym = dgated * (z * sz)
        dy_ref[...] = dym
        dz_ref[...] = dgated * ym * (sz * (1.0 + z * (1.0 - sz)))
        gadd(V_DSK, dym * xs)
        gadd(V_PSC, dp * pun_ref[...])
        dps = (dp * vrow(V_PSC)).astype(BF16)
        dpool = dpool_ref[...]
        for g in range(4):
            gs = slice(PG * g, PG * g + PG)
            dd_ref[:, gs] = _dot_nt(dps[:, gs], pw_ref[g])
            dpw_ref[g] += _dot_tn(dpool[:, gs], dps[:, gs])

        @pl.when(i == nt - 1)
        def _():
            pltpu.sync_copy(dwo_s, dwo_ref)
            ex.wait()

    any_spec = pl.BlockSpec(memory_space=pl.ANY)
    return pl.pallas_call(
        body,
        name="merge_bwd",
        grid=(nt,),
        in_specs=[
            pl.BlockSpec((2, TM, D), lambda i: (0, i, 0)),
            _tiles(TM, D), _tiles(TM, D), _tiles(TM, D), _tiles(TM, 2 * D),
            _tiles(TM, D), _tiles(TM, D),
            _tiles(TM, D), _tiles(TM, D), _tiles(TM, D), _tiles(TM, D),
            _resident((NV, D)), _resident((2 * D, D)), _resident((4, PG, PG)),
        ] + [any_spec] * ns,
        out_specs=[_tiles(TM, D), _tiles(TM, D), _tiles(TM, D), _tiles(TM, D),
                   _const_out((4, PG, PG)), _const_out((NV, D)), any_spec] + [any_spec] * ns,
        out_shape=[
            jax.ShapeDtypeStruct((L, D), F32),
            jax.ShapeDtypeStruct((L, D), F32),
            jax.ShapeDtypeStruct((L, D), F32),
            jax.ShapeDtypeStruct((L, D), F32),
            jax.ShapeDtypeStruct((4, PG, PG), F32),
            jax.ShapeDtypeStruct((NV, D), F32),
            jax.ShapeDtypeStruct((2 * D, D), F32),
        ] + _exchange_out_shapes(scattered, False),
        scratch_shapes=[pltpu.VMEM((2 * D, D), F32)] + _exchange_sems(ns),
        compiler_params=_cp(),
    )(dh2p, dpre2, xe, mix, cat, yf, yb, xbc, z, dpool, pun, vec, w_out, pool_w, *scattered)


def _pool_bwd_call(dd, pmat_t, icnt):
    L = dd.shape[0]
    nt = L // PT
    cur, prev, nxt = _pool_specs(nt)
    icur = pl.BlockSpec((PT, 128), lambda i: (i, 0))
    iprev = pl.BlockSpec((PT, 128), lambda i: (jnp.maximum(i - 1, 0), 0))
    inxt = pl.BlockSpec((PT, 128), lambda i: (jnp.minimum(i + 1, nt - 1), 0))

    def body(cur_ref, prev_ref, next_ref, ic_ref, icp_ref, icn_ref, m0_ref, m1_ref, m2_ref, m3_ref, du_ref):
        i = pl.program_id(0)
        n = pl.num_programs(0)
        lane = lax.broadcasted_iota(jnp.int32, (1, 128), 1)
        icv = ic_ref[...]
        for g, m_ref in enumerate((m0_ref, m1_ref, m2_ref, m3_ref)):
            gs = slice(PG * g, PG * g + PG)
            halo = _halo_tokens(g)
            icp = _column(icp_ref[PT - halo:PT, :], lane, g)
            icn = _column(icn_ref[0:halo, :], lane, g)
            top = jnp.where(i > 0, prev_ref[PT - halo:PT, gs] * icp, 0.0)
            bot = jnp.where(i < n - 1, next_ref[0:halo, gs] * icn, 0.0)
            mid = cur_ref[:, gs]
            ext = jnp.concatenate([top, mid * _column(icv, lane, g), bot], axis=0)
            du_ref[:, gs] = _split_dot(m_ref[...], ext) - mid

    return pl.pallas_call(
        body,
        name="pool_bwd",
        grid=(nt,),
        in_specs=[cur, prev, nxt, icur, iprev, inxt] + [_resident(m.shape) for m in pmat_t],
        out_specs=_tiles(PT, D),
        out_shape=jax.ShapeDtypeStruct((L, D), F32),
        compiler_params=_cp(),
    )(dd, dd, dd, icnt, icnt, icnt, *pmat_t)


def _ssd_bwd_call(dy, xbc, dtr, hprev_f, hprev_b, dh_init, par, e_mat, et_mat, dskip, name):
    L = xbc.shape[0]
    nc = L // Q

    def body(dy0_ref, dy1_ref, xbc0_ref, xbc1_ref, dtr0_ref, dtr1_ref, hp0_ref, hp1_ref, dhi_ref, par_ref, e_ref,
             et_ref, dsk_ref, dx0_ref, dx1_ref, ddt0_ref, ddt1_ref, acc_ref, dh0_ref, dh_s, AT, dtT, ddtT_s, dAT_s):
        s = pl.program_id(0)

        @pl.when(s == 0)
        def _():
            dh_s[...] = dhi_ref[...]
            acc_ref[...] = jnp.zeros_like(acc_ref)
            ddtT_s[...] = jnp.zeros_like(ddtT_s)
            dAT_s[...] = jnp.zeros_like(dAT_s)

        one_direction(0, dy0_ref, xbc0_ref, dtr0_ref, hp0_ref, par_ref, e_ref, et_ref, dsk_ref, dx0_ref, ddt0_ref,
                      acc_ref, dh0_ref, dh_s, AT, dtT, ddtT_s, dAT_s)
        one_direction(1, dy1_ref, xbc1_ref, dtr1_ref, hp1_ref, par_ref, e_ref, et_ref, dsk_ref, dx1_ref, ddt1_ref,
                      acc_ref, dh0_ref, dh_s, AT, dtT, ddtT_s, dAT_s)

    def one_direction(d, dy_ref, xbc_ref, dtr_ref, hp_ref, par_ref, e_ref, et_ref, dsk_ref, dxbc_ref, ddtr_ref,
                      acc_ref, dh0_ref, dh_s, AT_s, dtT_s, ddtT_s, dAT_s):
        q = _ssd_common(d, dtr_ref[...], par_ref)
        A, maskf, lane, dt, atot = q["A"], q["maskf"], q["lane"], q["dt"], q["atot"]
        AT_s[d] = q["AT"]
        dtT_s[d] = q["dtT"]
        AT, dtT = AT_s.at[d], dtT_s.at[d]
        hprev = hp_ref[0]
        hpb = hprev.astype(BF16)
        dh = dh_s[d]
        dhb = dh.astype(BF16)
        xs = xbc_ref[:, 0:1024]
        bb = xbc_ref[:, 1024:1152].astype(BF16)
        cb = xbc_ref[:, 1152:1280].astype(BF16)
        dy = dy_ref[...]
        ea_f = _head_expand(e_ref, jnp.exp(A))
        ch = _dot_nt(cb, hpb)
        dch = (dy * ea_f).astype(BF16)
        dC = _dot(dch, hpb)
        dhprev = _dot_tn(dch, cb)
        dA = _head_sum(et_ref, dy * ch * ea_f)
        dec = _state_decay(atot, lane)
        dhprev = dhprev + dh * dec
        dhh = dh * hprev * dec
        datot = jnp.zeros((1, 128), F32)
        for h in range(NH):
            tot_h = jnp.sum(_rowsum(dhh[HP * h:HP * h + HP, :]), axis=1, keepdims=True)
            datot = datot + jnp.where(lane == h, tot_h, 0.0)
        ear = jnp.exp(atot - A)
        wend = ear * dt
        wf = _head_expand(e_ref, wend)
        xw = (xs * wf).astype(BF16)
        dxw = _dot_nt(bb, dhb)
        dB = _dot(xw, dhb)
        dxs = dxw * wf
        dwend = _head_sum(et_ref, dxw * xs)
        ddt = dwend * ear
        de = dwend * wend
        datot = datot + _rowsum(de)
        dA = dA - de
        g = _dot_nt(cb, bb)
        dG = jnp.zeros((Q, Q), F32)
        for k in range(NH // 2):
            ks = slice(128 * k, 128 * k + 128)
            xp = xs[:, ks]
            dyp = dy[:, ks]
            accdx = dxs[:, ks]
            if d == 0:
                accdx = accdx + dyp * dsk_ref[:, ks]
            scts, dyhs = [], []
            for half in range(2):
                h = 2 * k + half
                inhead = (lane >= HP) if half else (lane < HP)
                seg = _column(A, lane, h) - AT[h:h + 1, :]
                lm = jnp.exp(jnp.minimum(seg, 0.0)) * maskf
                dtrow = dtT[h:h + 1, :]
                gl = g * lm
                sc = gl * dtrow
                dyh = jnp.where(inhead, dyp, 0.0).astype(BF16)
                xh = jnp.where(inhead, xp, 0.0).astype(BF16)
                dS = _dot_nt(dyh, xh)
                scts.append(sc.T.astype(BF16))
                dyhs.append(dyh)
                nn = dS * gl
                cn = _rowsum(nn)
                rm = jnp.sum(nn * dtrow, axis=1, keepdims=True)
                dG = dG + dS * (lm * dtrow)
                ddtT_s[d, h:h + 1, :] = cn
                dAT_s[d, h:h + 1, :] = -(cn * dtrow)
                dA = dA + rm * (lane == h).astype(F32)
            dxbc_ref[:, ks] = accdx + _dot(jnp.concatenate(scts, axis=1), jnp.concatenate(dyhs, axis=0))
        dGb = dG.astype(BF16)
        dxbc_ref[:, 1024:1152] = dB + _dot_tn(dGb, cb)
        dxbc_ref[:, 1152:1280] = dC + _dot(dGb, bb)
        da = _tri_dot(q["mask_t"], dA + dAT_s[d].T) + datot
        ddt = ddt + ddtT_s[d].T + da * q["aneg"]
        ddtr = jnp.where(q["hmask"], ddt * _sigmoid(q["pre"]), 0.0)
        ddtr_ref[...] = ddtr
        acc_ref[d, 0:1, :] += _rowsum(ddtr)
        acc_ref[d, 1:2, :] += _rowsum(da * dt) * q["aneg"]
        dh_s[d] = dhprev
        dh0_ref[d] = dhprev

    def back(s):
        return nc - 1 - s

    return pl.pallas_call(
        body,
        name=name,
        grid=(nc,),
        in_specs=[
            pl.BlockSpec((Q, D), lambda s: (back(s), 0)),
            pl.BlockSpec((Q, D), lambda s: (s, 0)),
            pl.BlockSpec((Q, DXBC), lambda s: (back(s), 0)),
            pl.BlockSpec((Q, DXBC), lambda s: (s, 0)),
            pl.BlockSpec((Q, 128), lambda s: (back(s), 0)),
            pl.BlockSpec((Q, 128), lambda s: (s, 1)),
            pl.BlockSpec((1, D, NS), lambda s: (back(s), 0, 0)),
            pl.BlockSpec((1, D, NS), lambda s: (s, 0, 0)),
            _resident((2, D, NS)),
            _resident((8, 128)),
            _resident((256, D)),
            _resident((D, 256)),
            _resident((1, D)),
        ],
        out_specs=[
            pl.BlockSpec((Q, DXBC), lambda s: (back(s), 0)),
            pl.BlockSpec((Q, DXBC), lambda s: (s, 0)),
            pl.BlockSpec((Q, 128), lambda s: (back(s), 0)),
            pl.BlockSpec((Q, 128), lambda s: (s, 0)),
            _const_out((2, 8, 128)),
            _const_out((2, D, NS)),
        ],
        out_shape=(
            jax.ShapeDtypeStruct((L, DXBC), F32),
            jax.ShapeDtypeStruct((L, DXBC), F32),
            jax.ShapeDtypeStruct((L, 128), F32),
            jax.ShapeDtypeStruct((L, 128), F32),
            jax.ShapeDtypeStruct((2, 8, 128), F32),
            jax.ShapeDtypeStruct((2, D, NS), F32),
        ),
        scratch_shapes=[pltpu.VMEM((2, D, NS), F32)] + [pltpu.VMEM((2, 128, Q), F32)] * 4,
        compiler_params=_cp(),
    )(dy, dy, xbc, xbc, dtr, dtr, hprev_f, hprev_b, dh_init, par, e_mat, et_mat, dskip)


def _conv_bwd_call(dxf, dxb, xbc_raw, cw, acc_init, name, scattered=()):
    L = xbc_raw.shape[0]
    nt = L // TM
    ns = len(scattered)
    prev, nxt = _halo_specs(TM, DXBC, L)

    def body(*refs):
        dxf_ref, dxb_ref, cur_ref, prev_ref, next_ref, cw_ref, init_ref = refs[:7]
        dpre_ref, acc_ref = refs[7 + ns:9 + ns]
        ext = refs[9 + 2 * ns]
        if ns:
            ex = _Exchange(refs[7:7 + ns], refs[9 + ns:9 + 2 * ns], *refs[10 + 2 * ns:], False)

        @pl.when(pl.program_id(0) == 0)
        def _():
            if ns:
                ex.start()
            acc_ref[...] = init_ref[...]

        _extended(ext, cur_ref, prev_ref, next_ref)
        taps = [_shifted(ext, k - 2, TM) for k in range(5)]
        pre = cw_ref[5:6, :] + cw_ref[0:1, :] * taps[0]
        for k in range(1, 5):
            pre = pre + cw_ref[k:k + 1, :] * taps[k]
        sg = _sigmoid(pre)
        dpre = (dxf_ref[...] + dxb_ref[...]) * (sg * (1.0 + pre * (1.0 - sg)))
        dpre_ref[...] = dpre
        for k in range(5):
            acc_ref[k:k + 1, :] += _rowsum(dpre * taps[k])
        acc_ref[5:6, :] += _rowsum(dpre)
        if ns:
            @pl.when(pl.program_id(0) == nt - 1)
            def _():
                ex.wait()

    any_spec = pl.BlockSpec(memory_space=pl.ANY)
    return pl.pallas_call(
        body,
        name=name,
        grid=(nt,),
        in_specs=[_tiles(TM, DXBC), _tiles(TM, DXBC), _tiles(TM, DXBC), prev, nxt,
                  _resident((8, DXBC)), _resident((8, DXBC))] + [any_spec] * ns,
        out_specs=[_tiles(TM, DXBC), _const_out((8, DXBC))] + [any_spec] * ns,
        out_shape=[jax.ShapeDtypeStruct((L, DXBC), F32), jax.ShapeDtypeStruct((8, DXBC), F32)]
        + _exchange_out_shapes(scattered, False),
        scratch_shapes=[pltpu.VMEM((TM + 16, DXBC), F32)] + (_exchange_sems(ns) if ns else []),
        compiler_params=_cp(),
    )(dxf, dxb, xbc_raw, xbc_raw, xbc_raw, cw, acc_init, *scattered)


def _inproj_bwd_call(dpre, cw, dz, ddt0, ddt1, dup, h1, dxe_part, x0, vec, w_in, dw_init, name):
    L = x0.shape[0]
    nt = L // TM
    prev, nxt = _halo_specs(TM, DXBC, L)

    def body(cur_ref, prev_ref, next_ref, cw_ref, dz_ref, ddt0_ref, ddt1_ref, dup_ref, h1_ref, dxe_ref, x0_ref,
             vec_ref, w_ref, dwi_ref, gx_ref, gacc_ref, dw_ref, dw_s, ext):
        i = pl.program_id(0)

        @pl.when(i == 0)
        def _():
            gacc_ref[...] = jnp.zeros_like(gacc_ref)
            pltpu.sync_copy(dwi_ref, dw_s)

        def vrow(r):
            return vec_ref[r:r + 1, :]

        _extended(ext, cur_ref, prev_ref, next_ref)
        dxr = cw_ref[0:1, :] * _shifted(ext, 2, TM)
        for k in range(1, 5):
            dxr = dxr + cw_ref[k:k + 1, :] * _shifted(ext, 2 - k, TM)
        dproj = jnp.concatenate([dz_ref[...], dxr, ddt0_ref[...], ddt1_ref[...], dup_ref[...]], axis=1).astype(BF16)
        dh1 = _dot_nt(dproj, w_ref[...])
        _acc_tn(dw_s, h1_ref[...], dproj)
        xe, n0, rstd0 = _ln(x0_ref[...], vrow(V_EMBG), vrow(V_EMBB))
        dxe = dxe_ref[...] + dh1 * (1.0 + vrow(V_SC1))
        gacc_ref[V_SC1:V_SC1 + 1, :] += _rowsum(dh1 * xe)
        gacc_ref[V_SH1:V_SH1 + 1, :] += _rowsum(dh1)
        gacc_ref[V_EMBG:V_EMBG + 1, :] += _rowsum(dxe * n0)
        gacc_ref[V_EMBB:V_EMBB + 1, :] += _rowsum(dxe)
        gx_ref[...] = _ln_bwd(dxe, n0, rstd0, vrow(V_EMBG))

        @pl.when(i == nt - 1)
        def _():
            pltpu.sync_copy(dw_s, dw_ref)

    any_spec = pl.BlockSpec(memory_space=pl.ANY)
    return pl.pallas_call(
        body,
        name=name,
        grid=(nt,),
        in_specs=[_tiles(TM, DXBC), prev, nxt, _resident((8, DXBC)), _tiles(TM, D), _tiles(TM, 128), _tiles(TM, 128),
                  _tiles(TM, D), _tiles(TM, D), _tiles(TM, D), _tiles(TM, D), _resident((NV, D)), _resident((D, WIN)),
                  any_spec],
        out_specs=[_tiles(TM, D), _const_out((NV, D)), any_spec],
        out_shape=(
            jax.ShapeDtypeStruct((L, D), F32),
            jax.ShapeDtypeStruct((NV, D), F32),
            jax.ShapeDtypeStruct((D, WIN), F32),
        ),
        scratch_shapes=[pltpu.VMEM((D, WIN), F32), pltpu.VMEM((TM + 16, DXBC), F32)],
        compiler_params=_cp(),
    )(dpre, dpre, dpre, cw, dz, ddt0, ddt1, dup, h1, dxe_part, x0, vec, w_in, dw_init)


def _adamw(w, g, m, v):
    m = ADAM_B1 * m + (1.0 - ADAM_B1) * g
    v = ADAM_B2 * v + (1.0 - ADAM_B2) * (g * g)
    m_hat = m / (1.0 - ADAM_B1 ** ADAM_STEP)
    v_hat = v / (1.0 - ADAM_B2 ** ADAM_STEP)
    delta = -ADAM_LR * (m_hat / (jnp.sqrt(v_hat) + ADAM_EPS) + ADAM_WD * w)
    return delta, m, v


def _adamw_shard_call(gslots, w, m, v, tr, name):
    R, C = w.shape

    def body(gs_ref, w_ref, m_ref, v_ref, g_ref, d_ref, mo_ref, vo_ref):
        g = gs_ref[0].astype(F32)
        for i in range(1, NDEV):
            g = g + gs_ref[i].astype(F32)
        delta, mn, vn = _adamw(w_ref[...], g, m_ref[...], v_ref[...])
        g_ref[...] = g
        d_ref[...] = delta
        mo_ref[...] = mn
        vo_ref[...] = vn

    t = _tiles(tr, C)
    return pl.pallas_call(
        body,
        name=name,
        grid=(R // tr,),
        in_specs=[pl.BlockSpec((NDEV, tr, C), lambda i: (0, i, 0)), t, t, t],
        out_specs=[t, t, t, t],
        out_shape=tuple(jax.ShapeDtypeStruct((R, C), F32) for _ in range(4)),
        compiler_params=_cp(),
    )(gslots, w, m, v)


def _wada_call(dm_ex, dm_ctx, silu_all, w, m, v):
    ncol = w.shape[1]

    def body(dme_ref, dmc_ref, s_ref, w_ref, m_ref, v_ref, g_ref, d_ref, mo_ref, vo_ref, ds_ref):
        dmc = _rowsum(dmc_ref[...])
        rows = lax.broadcasted_iota(jnp.int32, (8, 1), 0)
        low = jnp.where(rows == 0, dmc, 0.0)
        dm = jnp.concatenate([dme_ref[...], low], axis=0).astype(BF16)
        wv = w_ref[...]
        g = _dot_tn(s_ref[...].astype(BF16), dm)
        delta, mn, vn = _adamw(wv, g, m_ref[...], v_ref[...])
        g_ref[...] = g
        d_ref[...] = delta
        mo_ref[...] = mn
        vo_ref[...] = vn
        ds_ref[...] = _dot_nt(low.astype(BF16), wv.astype(BF16))

    return pl.pallas_call(
        body,
        name="wada_update",
        out_shape=tuple(jax.ShapeDtypeStruct((D, ncol), F32) for _ in range(4)) + (jax.ShapeDtypeStruct((8, D), F32),),
        compiler_params=pltpu.CompilerParams(vmem_limit_bytes=VMEM_LIMIT),
    )(dm_ex, dm_ctx, silu_all, w, m, v)


P_DMOD, P_DMODC, P_EMBG, P_EMBB, P_LN1G, P_LN1B, P_LN2G, P_LN2B = 0, 6, 8, 9, 10, 11, 12, 13
P_SSDG, P_PSC, P_DSK, P_CONVB, P_DTB, P_ALOG, P_LOSS, NP = 14, 15, 16, 17, 19, 20, 21, 24
S_CCTX, S_EMBG, S_EMBB, S_BADA, S_CONVB, S_DTB, S_ALOG, S_DSK = 0, 1, 2, 3, 9, 11, 12, 13
S_SSDG, S_PSC, S_LN1G, S_LN1B, S_LN2G, S_LN2B, NSM = 14, 15, 16, 17, 18, 19, 24


def _small_update_call(pall, dsil, cctx, w, m, v, et_mat):
    def body(p_ref, ds_ref, c_ref, w_ref, m_ref, v_ref, et_ref, g_ref, d_ref, mo_ref, vo_ref, loss_ref,
             tot, dsum, dsk8):
        tot[...] = p_ref[0]
        dsum[...] = ds_ref[0]
        for i in range(1, NDEV):
            tot[...] += p_ref[i]
            dsum[...] += ds_ref[i]
        cv = c_ref[...]
        sc = _sigmoid(cv)
        g_ref[...] = jnp.zeros_like(g_ref)
        g_ref[S_CCTX:S_CCTX + 1, :] = dsum[0:1, :] * (sc * (1.0 + cv * (1.0 - sc)))
        g_ref[S_EMBG:S_EMBG + 1, :] = tot[P_EMBG:P_EMBG + 1, :]
        g_ref[S_EMBB:S_EMBB + 1, :] = tot[P_EMBB:P_EMBB + 1, :]
        g_ref[S_BADA:S_BADA + 2, :] = tot[P_DMOD:P_DMOD + 2, :] + tot[P_DMODC:P_DMODC + 2, :]
        g_ref[S_BADA + 2:S_BADA + 6, :] = tot[P_DMOD + 2:P_DMOD + 6, :]
        g_ref[S_CONVB:S_CONVB + 2, :] = tot[P_CONVB:P_CONVB + 2, :]
        g_ref[S_DTB:S_DTB + 1, :] = tot[P_DTB:P_DTB + 1, :]
        g_ref[S_ALOG:S_ALOG + 1, :] = tot[P_ALOG:P_ALOG + 1, :]
        dsk8[...] = _dot(jnp.broadcast_to(tot[P_DSK:P_DSK + 1, :], (8, D)), et_ref[:, 0:128].astype(F32), HI)
        g_ref[S_DSK:S_DSK + 1, 0:128] = dsk8[0:1, :]
        g_ref[S_SSDG:S_SSDG + 1, :] = tot[P_SSDG:P_SSDG + 1, :]
        g_ref[S_PSC:S_PSC + 1, :] = tot[P_PSC:P_PSC + 1, :]
        g_ref[S_LN1G:S_LN1G + 1, :] = tot[P_LN1G:P_LN1G + 1, :]
        g_ref[S_LN1B:S_LN1B + 1, :] = tot[P_LN1B:P_LN1B + 1, :]
        g_ref[S_LN2G:S_LN2G + 1, :] = tot[P_LN2G:P_LN2G + 1, :]
        g_ref[S_LN2B:S_LN2B + 1, :] = tot[P_LN2B:P_LN2B + 1, :]
        delta, mn, vn = _adamw(w_ref[...], g_ref[...], m_ref[...], v_ref[...])
        d_ref[...] = delta
        mo_ref[...] = mn
        vo_ref[...] = vn
        loss_ref[...] = jnp.broadcast_to(tot[P_LOSS:P_LOSS + 1, 0:128], (8, 128))

    return pl.pallas_call(
        body,
        name="small_update",
        out_shape=tuple(jax.ShapeDtypeStruct((NSM, D), F32) for _ in range(4)) + (jax.ShapeDtypeStruct((8, 128), F32),),
        scratch_shapes=[pltpu.VMEM((NP, D), F32), pltpu.VMEM((8, D), F32), pltpu.VMEM((8, 128), F32)],
        compiler_params=pltpu.CompilerParams(vmem_limit_bytes=VMEM_LIMIT),
    )(pall, dsil, cctx, w, m, v, et_mat)


def _pad_rows(flat, mult=16):
    n = flat.shape[0]
    rows = -(-n // D)
    rows = -(-rows // mult) * mult
    return jnp.pad(flat, (0, rows * D - n)).reshape(rows, D)


def _by_cols(dw):
    r = dw.shape[0]
    return jnp.transpose(dw.reshape(r, NDEV, -1), (1, 0, 2))


def _from_cols(g):
    return jnp.transpose(g, (1, 0, 2)).reshape(g.shape[1], -1)


def _pool_constants(L):
    rows = L // GW
    t_r = jnp.arange(PT) // GW
    t_c = jnp.arange(PT) % GW
    fw, bw, ic = [], [], []
    pos_r = jnp.arange(L) // GW
    pos_c = jnp.arange(L) % GW
    for g, w in enumerate(WINDOWS):
        lo, hi = -(w // 2), w - w // 2 - 1
        n_ext = PT + 2 * _halo_tokens(g)
        e_r = jnp.arange(n_ext) // GW - w // 2
        e_c = jnp.arange(n_ext) % GW
        dr = e_r[None, :] - t_r[:, None]
        dc = e_c[None, :] - t_c[:, None]
        fw.append(((dr >= lo) & (dr <= hi) & (dc >= lo) & (dc <= hi)).astype(BF16))
        bw.append(((-dr >= lo) & (-dr <= hi) & (-dc >= lo) & (-dc <= hi)).astype(BF16))
        cr = jnp.minimum(pos_r + hi, rows - 1) - jnp.maximum(pos_r + lo, 0) + 1
        cc = jnp.minimum(pos_c + hi, GW - 1) - jnp.maximum(pos_c + lo, 0) + 1
        ic.append(1.0 / (cr * cc).astype(F32))
    icnt = jnp.pad(jnp.stack(ic, axis=1), ((0, 0), (0, 124)))
    return fw, bw, icnt


def _head_matrices():
    hp = jnp.arange(D) // HP
    e = (jnp.arange(128)[:, None] == hp[None, :]).astype(BF16)
    return jnp.concatenate([e, e], axis=0), jnp.concatenate([e.T, e.T], axis=1)


def _aligned_in_proj(w):
    zpad = jnp.zeros((D, 128 - NH), w.dtype)
    return jnp.concatenate([w[:, 0:2304], w[:, 2304:2320], zpad, w[:, 2320:2336], zpad, w[:, 2336:3360]], axis=1)


def _unaligned_in_proj(dw):
    return jnp.concatenate([dw[:, 0:2304], dw[:, 2304:2320], dw[:, 2432:2448], dw[:, 2560:3584]], axis=1)


def _row(v):
    return v.reshape(1, -1).astype(F32)


def _pad_lanes(v, width=D):
    v = v.reshape(1, -1)
    return jnp.pad(v, ((0, 0), (0, width - v.shape[1])))


def kernel(x, c, ctx, c_ctx, emb_ln_g, emb_ln_b, w_ada, b_ada, in_proj, conv_w, conv_b, dt_bias, a_log, d_skip, ssd_norm_g, pool_w, pool_scale, w_out, ln1_g, ln1_b, w_gate, w_up, w_down, ln2_g, ln2_b, loss_target, m_c_ctx, m_emb_ln_g, m_emb_ln_b, m_w_ada, m_b_ada, m_in_proj, m_conv_w, m_conv_b, m_dt_bias, m_a_log, m_d_skip, m_ssd_norm_g, m_pool_w, m_pool_scale, m_w_out, m_ln1_g, m_ln1_b, m_w_gate, m_w_up, m_w_down, m_ln2_g, m_ln2_b, v_c_ctx, v_emb_ln_g, v_emb_ln_b, v_w_ada, v_b_ada, v_in_proj, v_conv_w, v_conv_b, v_dt_bias, v_a_log, v_d_skip, v_ssd_norm_g, v_pool_w, v_pool_scale, v_w_out, v_ln1_g, v_ln1_b, v_w_gate, v_w_up, v_w_down, v_ln2_g, v_ln2_b):
    me = 4 * lax.axis_index("x") + 2 * lax.axis_index("y") + lax.axis_index("c")
    x0 = x[0]
    ctx0 = ctx[0]
    tgt = loss_target[0]
    L = x0.shape[0]
    LC = ctx0.shape[0]
    ncol_ada = w_ada.shape[2]

    small_in = jnp.concatenate([c.reshape(-1), conv_w.reshape(-1)])
    gb = jnp.concatenate([_row(emb_ln_g), _row(emb_ln_b), jnp.zeros((6, D), F32)], axis=0)
    xe, small_all, g_inp = _emb_ln_call(x0, gb, "emb_ln", [_pad_rows(small_in, 8), in_proj[0].astype(BF16)])
    (xe_c,) = _emb_ln_call(ctx0, gb, "emb_ln_ctx")
    c_all = small_all[:, 0, :]
    convw_all = small_all.reshape(NDEV, -1)[:, D:D + 5 * (DXBC // NDEV)].reshape(NDEV, 5, DXBC // NDEV)
    conv_w_full = _from_cols(convw_all)
    w_in = _aligned_in_proj(_from_cols(g_inp))
    late_shards = [pool_w[0].astype(BF16), w_out[0].astype(BF16), w_gate[0].astype(BF16), w_up[0].astype(BF16),
                   w_down[0].astype(BF16)]

    c_in = jnp.concatenate([c_all, c_ctx.reshape(1, D), jnp.zeros((7, D), F32)], axis=0)
    b_mine = lax.dynamic_slice(b_ada, (0, me * ncol_ada), (1, ncol_ada))
    silu_all, mod_mine = _mod_call(c_in, w_ada[0], b_mine)
    (mod_all,) = _exchange([mod_mine], "gather_mod", True)
    mod_all = _from_cols(mod_all)
    mod_me = lax.dynamic_slice(mod_all, (me, 0), (1, 6 * D)).reshape(6, D)
    mod_ctx = mod_all[8].reshape(6, D)

    tail = jnp.concatenate([
        _row(emb_ln_g), _row(emb_ln_b), _row(ln1_g), _row(ln1_b), _row(ln2_g), _row(ln2_b),
        _row(ssd_norm_g), _row(pool_scale), _row(jnp.repeat(d_skip.reshape(-1), HP)), jnp.zeros((1, D), F32)], axis=0)
    vec = jnp.concatenate([mod_me, tail], axis=0)
    vec_ctx = jnp.concatenate([mod_ctx, tail], axis=0)

    cw = jnp.concatenate([conv_w_full, conv_b.reshape(1, DXBC), jnp.zeros((2, DXBC), F32)], axis=0)
    par = jnp.concatenate([_pad_lanes(dt_bias[0, 0], 128), _pad_lanes(dt_bias[0, 1], 128),
                           _pad_lanes(a_log[0, 0], 128), _pad_lanes(a_log[0, 1], 128),
                           jnp.zeros((4, 128), F32)], axis=0)
    e_mat, et_mat = _head_matrices()
    pmat, pmat_t, icnt = _pool_constants(L)
    dskip_row = vec[V_DSK:V_DSK + 1]

    h1_c, _, xbcr_c, xbc_c, dtr_c, _ = _f1_call(xe_c, vec_ctx, w_in, cw, "inproj_fwd_ctx")
    hzero = jnp.zeros((2, D, NS), F32)
    _, _, hpf_c, hpb_c, hfin_c = _ssd_fwd_call(xbc_c, dtr_c, hzero, par, e_mat, et_mat, "ssd_fwd_ctx")

    h1, z, xbcr, xbc, dtr, up = _f1_call(xe, vec, w_in, cw, "inproj_fwd")
    yf, yb, hpf, hpb, _, g_pw, g_wo, g_wg, g_wu, g_wd = _ssd_fwd_call(xbc, dtr, hfin_c, par, e_mat, et_mat, "ssd_fwd",
                                                                       late_shards)
    pool_w_full = jnp.transpose(g_pw, (1, 0, 2, 3)).reshape(4, PG, PG)
    w_out_full = g_wo.reshape(2 * D, D)
    w_gate_full = _from_cols(g_wg)
    w_up_full = _from_cols(g_wu)
    w_down_full = g_wd.reshape(DFF, D)
    dpool, pun = _pool_fwd_call(up, pmat, icnt, pool_w_full)
    x1, mix, cat = _merge_call(yf, yb, xbc, z, pun, xe, vec, w_out_full)
    dpre2, gacc_f, gt_b, up_b = _ffn_fwd_call(x1, tgt, vec, w_gate_full, w_up_full, w_down_full)

    dh2p, dwg2, dwu2, dwd2 = _ffn_bwd_call(x1, dpre2, gt_b, up_b, vec, w_gate_full, w_up_full, w_down_full)
    nq = FFC // (DFF // NDEV)
    ffn_parts = [
        jnp.transpose(dwg2.reshape(-1, D, nq, DFF // NDEV), (0, 2, 1, 3)).reshape(NDEV, D, DFF // NDEV),
        jnp.transpose(dwu2.reshape(-1, D, nq, DFF // NDEV), (0, 2, 1, 3)).reshape(NDEV, D, DFF // NDEV),
        dwd2.reshape(NDEV, DFF // NDEV, D)]
    dxe_part, dy, dz, dd, dpw, gacc_m, dwo, gs_wg, gs_wu, gs_wd = _merge_bwd_call(
        dh2p, dpre2, xe, mix, cat, yf, yb, xbc, z, dpool, pun, vec, w_out_full, pool_w_full, ffn_parts)
    dup = _pool_bwd_call(dd, pmat_t, icnt)
    dxf, dxb, ddt0, ddt1, sacc, dh0 = _ssd_bwd_call(dy, xbc, dtr, hpf, hpb, hzero, par, e_mat, et_mat, dskip_row,
                                                     "ssd_bwd")
    zeros_c = jnp.zeros((LC, D), F32)
    dxf_c, dxb_c, ddt0_c, ddt1_c, sacc_c, _ = _ssd_bwd_call(zeros_c, xbc_c, dtr_c, hpf_c, hpb_c, dh0, par, e_mat, et_mat,
                                                            jnp.zeros((1, D), F32), "ssd_bwd_ctx")
    dprec_c, cacc_c = _conv_bwd_call(dxf_c, dxb_c, xbcr_c, cw, jnp.zeros((8, DXBC), F32), "conv_bwd_ctx")
    _, gacc_c, dwin_c = _inproj_bwd_call(dprec_c, cw, zeros_c, ddt0_c, ddt1_c, zeros_c, h1_c, zeros_c, ctx0, vec_ctx,
                                         w_in, jnp.zeros((D, WIN), F32), "inproj_bwd_ctx")
    mix_parts = [dwo.reshape(NDEV, 2 * D // NDEV, D),
                 jnp.transpose(dpw.reshape(4, NDEV, PG // NDEV, PG), (1, 0, 2, 3)).reshape(NDEV, 4 * PG // NDEV, PG)]
    dprec, cacc, gs_wo, gs_pw = _conv_bwd_call(dxf, dxb, xbcr, cw, cacc_c, "conv_bwd", mix_parts)
    grad_x, gacc_i, dwin = _inproj_bwd_call(dprec, cw, dz, ddt0, ddt1, dup, h1, dxe_part, x0, vec, w_in, dwin_c,
                                            "inproj_bwd")

    gsum = gacc_f + gacc_m + gacc_i
    sa = sacc + sacc_c
    dtb_row = _pad_lanes(jnp.concatenate([sa[0, 0, 0:NH], sa[1, 0, 0:NH]]))
    alog_row = _pad_lanes(jnp.concatenate([sa[0, 1, 0:NH], sa[1, 1, 0:NH]]))
    convb_rows = jnp.pad(cacc[5], (0, 2 * D - DXBC)).reshape(2, D)
    pack = jnp.concatenate([
        gsum[V_SH1:V_G2 + 1],
        gacc_c[V_SH1:V_SC1 + 1],
        gsum[V_EMBG:V_EMBB + 1] + gacc_c[V_EMBG:V_EMBB + 1],
        gsum[V_LN1G:V_LN2B + 1],
        gsum[V_SSDG:V_DSK + 1],
        convb_rows, dtb_row, alog_row,
        gsum[V_LOSS:V_LOSS + 1],
        jnp.zeros((NP - 22, D), F32)], axis=0)
    (pall,) = _exchange([pack], "gather_small_grads", True)

    dm_flat = pall[:, 0:8, :].reshape(NDEV, 8 * D)
    dm_ex = lax.dynamic_slice(dm_flat, (0, me * ncol_ada), (NDEV, ncol_ada))
    dmc_full = jnp.concatenate([dm_flat[:, 6 * D:8 * D], jnp.zeros((NDEV, 4 * D), F32)], axis=1)
    dm_ctx = lax.dynamic_slice(dmc_full, (0, me * ncol_ada), (NDEV, ncol_ada))
    g_wada, d_wada, nm_wada, nv_wada, dsil = _wada_call(dm_ex, dm_ctx, silu_all, w_ada[0], m_w_ada[0], v_w_ada[0])
    (dsil_all,) = _exchange([dsil], "gather_dsilu", True)

    def small_pack(cc, eg, eb, ba, cb_, dtb, al, dsk, sg, ps, l1g, l1b, l2g, l2b):
        return jnp.concatenate([
            _row(cc), _row(eg), _row(eb), ba.reshape(6, D), jnp.pad(cb_.reshape(-1), (0, 2 * D - DXBC)).reshape(2, D),
            _pad_lanes(dtb.reshape(-1)), _pad_lanes(al.reshape(-1)), _pad_lanes(dsk.reshape(-1)),
            _row(sg), _row(ps), _row(l1g), _row(l1b), _row(l2g), _row(l2b), jnp.zeros((NSM - 20, D), F32)], axis=0)

    sw = small_pack(c_ctx, emb_ln_g, emb_ln_b, b_ada, conv_b, dt_bias, a_log, d_skip, ssd_norm_g, pool_scale,
                    ln1_g, ln1_b, ln2_g, ln2_b)
    sm = small_pack(m_c_ctx, m_emb_ln_g, m_emb_ln_b, m_b_ada, m_conv_b, m_dt_bias, m_a_log, m_d_skip, m_ssd_norm_g,
                    m_pool_scale, m_ln1_g, m_ln1_b, m_ln2_g, m_ln2_b)
    sv = small_pack(v_c_ctx, v_emb_ln_g, v_emb_ln_b, v_b_ada, v_conv_b, v_dt_bias, v_a_log, v_d_skip, v_ssd_norm_g,
                    v_pool_scale, v_ln1_g, v_ln1_b, v_ln2_g, v_ln2_b)
    s_g, s_d, s_m, s_v, loss8 = _small_update_call(pall, dsil_all, _row(c_ctx), sw, sm, sv, et_mat)

    def small_unpack(t):
        return (t[S_CCTX], t[S_EMBG], t[S_EMBB], t[S_BADA:S_BADA + 6].reshape(1, 6 * D),
                t[S_CONVB:S_CONVB + 2].reshape(-1)[:DXBC].reshape(1, DXBC),
                t[S_DTB, 0:2 * NH].reshape(1, 2, NH), t[S_ALOG, 0:2 * NH].reshape(1, 2, NH), t[S_DSK, 0:NH].reshape(1, NH),
                t[S_SSDG].reshape(1, D), t[S_PSC].reshape(1, D), t[S_LN1G].reshape(1, D), t[S_LN1B].reshape(1, D),
                t[S_LN2G].reshape(1, D), t[S_LN2B].reshape(1, D))

    gs_inp, gs_cw = _exchange([_by_cols(_unaligned_in_proj(dwin)).astype(BF16), _by_cols(cacc[0:5])],
                              "exchange_last_grads", False)

    pshape = (4 * PG // NDEV, PG)
    u_inp = _adamw_shard_call(gs_inp, in_proj[0], m_in_proj[0], v_in_proj[0], 256, "adamw_in_proj")
    u_cw = _adamw_shard_call(gs_cw, conv_w[0], m_conv_w[0], v_conv_w[0], 5, "adamw_conv_w")
    u_pw = _adamw_shard_call(gs_pw, pool_w[0].reshape(pshape), m_pool_w[0].reshape(pshape), v_pool_w[0].reshape(pshape),
                             pshape[0], "adamw_pool_w")
    u_wo = _adamw_shard_call(gs_wo, w_out[0], m_w_out[0], v_w_out[0], 64, "adamw_w_out")
    u_wg = _adamw_shard_call(gs_wg, w_gate[0], m_w_gate[0], v_w_gate[0], 256, "adamw_w_gate")
    u_wu = _adamw_shard_call(gs_wu, w_up[0], m_w_up[0], v_w_up[0], 256, "adamw_w_up")
    u_wd = _adamw_shard_call(gs_wd, w_down[0], m_w_down[0], v_w_down[0], 88, "adamw_w_down")

    def assemble(k, small, wada):
        (cc, eg, eb, ba, cb_, dtb, al, dsk, sg, ps, l1g, l1b, l2g, l2b) = small_unpack(small)
        pw = u_pw[k].reshape(1, 4, PG // NDEV, PG)
        return (cc, eg, eb, wada[None], ba, u_inp[k][None], u_cw[k][None], cb_, dtb, al, dsk, sg, pw, ps,
                u_wo[k][None], l1g, l1b, u_wg[k][None], u_wu[k][None], u_wd[k][None], l2g, l2b)

    loss = loss8[0, 0]
    return (loss, grad_x[None], *assemble(0, s_g, g_wada), *assemble(1, s_d, d_wada),
            *assemble(2, s_m, nm_wada), *assemble(3, s_v, nv_wada))
```

```python
import functools
import math

import jax
import jax.numpy as jnp
from jax import lax
from jax.experimental import pallas as pl
from jax.experimental.pallas import tpu as pltpu

F32 = jnp.float32
BF16 = jnp.bfloat16
HI = lax.Precision.HIGHEST

NDEV = 8
D = 1024
NH = 16
HP = 64
NS = 128
Q = 128
DXBC = 1280
DFF = 2816
FFC = 1408
GW = 64
PR = 8
PT = PR * GW
WINDOWS = (2, 4, 8, 16)
PG = 256
DIN = 3360
WIN = 3584
ALPHA = 2.0 ** 0.25
LN_EPS = 1e-5
TM = 256

ADAM_LR = 0.001
ADAM_B1 = 0.9
ADAM_B2 = 0.999
ADAM_EPS = 1e-08
ADAM_WD = 0.01
ADAM_STEP = 10

V_SH1, V_SC1, V_G1, V_SH2, V_SC2, V_G2 = 0, 1, 2, 3, 4, 5
V_EMBG, V_EMBB, V_LN1G, V_LN1B, V_LN2G, V_LN2B = 6, 7, 8, 9, 10, 11
V_SSDG, V_PSC, V_DSK, V_LOSS = 12, 13, 14, 15
NV = 16

VMEM_LIMIT = 60 * 1024 * 1024


def _cp(ndim=1):
    return pltpu.CompilerParams(dimension_semantics=("arbitrary",) * ndim, vmem_limit_bytes=VMEM_LIMIT)


def _dot(a, b, precision=None):
    return jnp.dot(a, b, preferred_element_type=F32, precision=precision)


def _dot_nt(a, b):
    return lax.dot_general(a, b, (((1,), (1,)), ((), ())), preferred_element_type=F32)


def _dot_tn(a, b, precision=None):
    return lax.dot_general(a, b, (((0,), (0,)), ((), ())), preferred_element_type=F32, precision=precision)


def _split2(x):
    hi = x.astype(BF16)
    return hi, (x - hi.astype(F32)).astype(BF16)


def _split_dot(m, x):
    hi, lo = _split2(x)
    return _dot(m, hi) + _dot(m, lo)


def _dot_split(x, m):
    hi, lo = _split2(x)
    return _dot(hi, m) + _dot(lo, m)


def _sigmoid(x):
    return 1.0 / (1.0 + jnp.exp(-x))


def _softplus(x):
    return jnp.maximum(x, 0.0) + jnp.log(1.0 + jnp.exp(-jnp.abs(x)))


def _ln(x, g, b):
    mu = jnp.mean(x, axis=-1, keepdims=True)
    xc = x - mu
    var = jnp.mean(xc * xc, axis=-1, keepdims=True)
    rstd = lax.rsqrt(var + LN_EPS)
    n = xc * rstd
    return n * g + b, n, rstd


def _ln_bwd(dy, n, rstd, g):
    dn = dy * g
    return rstd * (dn - jnp.mean(dn, axis=-1, keepdims=True) - n * jnp.mean(dn * n, axis=-1, keepdims=True))


def _rowsum(x):
    return jnp.sum(x, axis=0, keepdims=True)


def _resident(shape):
    nd = len(shape)
    return pl.BlockSpec(shape, lambda *_: (0,) * nd, pipeline_mode=pl.Buffered(1))


def _const_out(shape):
    nd = len(shape)
    return pl.BlockSpec(shape, lambda *_: (0,) * nd)


def _tiles(tm, width):
    return pl.BlockSpec((tm, width), lambda i: (i, 0))


def _halo_specs(tm, width, n_rows):
    r = tm // 8
    last = n_rows // 8 - 1
    prev = pl.BlockSpec((8, width), lambda i: (jnp.maximum(i * r - 1, 0), 0))
    nxt = pl.BlockSpec((8, width), lambda i: (jnp.minimum((i + 1) * r, last), 0))
    return prev, nxt


def _acc_tn(acc_ref, a, b, chunk=512):
    n = b.shape[1]
    for c0 in range(0, n, chunk):
        c1 = min(c0 + chunk, n)
        acc_ref[:, c0:c1] += _dot_tn(a, b[:, c0:c1])


def _my_coords():
    return lax.axis_index("x"), lax.axis_index("y"), lax.axis_index("c")


def _peer(k, mx, my, mc):
    kx, ky, kc = (k >> 2) & 1, (k >> 1) & 1, k & 1
    px = 1 - mx if kx else mx
    py = 1 - my if ky else my
    pc = 1 - mc if kc else mc
    return px, py, pc


class _Exchange:
    def __init__(self, srcs, dsts, send_sems, recv_sems, local_sems, gather):
        self.srcs, self.dsts, self.gather = srcs, dsts, gather
        self.send_sems, self.recv_sems, self.local_sems = send_sems, recv_sems, local_sems

    def _copies(self, outgoing):
        mx, my, mc = _my_coords()
        me = 4 * mx + 2 * my + mc
        local, remote = [], []
        for t, (src, dst) in enumerate(zip(self.srcs, self.dsts)):
            local.append(pltpu.make_async_copy(src if self.gather else src.at[me], dst.at[me], self.local_sems.at[t]))
            for k in range(1, NDEV):
                px, py, pc = _peer(k, mx, my, mc)
                pid = 4 * px + 2 * py + pc
                remote.append(pltpu.make_async_remote_copy(
                    src_ref=src if self.gather else src.at[pid],
                    dst_ref=dst.at[me] if outgoing else dst.at[pid],
                    send_sem=self.send_sems.at[t, k - 1],
                    recv_sem=self.recv_sems.at[t, k - 1],
                    device_id=(px, py, pc),
                    device_id_type=pl.DeviceIdType.MESH,
                ))
        return local, remote

    def start(self):
        local, remote = self._copies(True)
        for cp in local + remote:
            cp.start()

    def wait(self):
        local, sends = self._copies(True)
        _, recvs = self._copies(False)
        for cp in recvs:
            cp.wait_recv()
        for cp in sends:
            cp.wait_send()
        for cp in local:
            cp.wait()


class _ChipGather:
    def __init__(self, srcs, dsts, send_sems, recv_sems, local_sems):
        self.srcs, self.dsts = srcs, dsts
        self.send_sems, self.recv_sems, self.local_sems = send_sems, recv_sems, local_sems

    def _places(self):
        x, y, c = _my_coords()
        return (x, y, c), (x, y, 1 - c), [(1 - x, y), (x, 1 - y), (1 - x, 1 - y)]

    def _copy(self, t, k, block, to, own=False):
        slot = self.dsts[t].at[4 * block[0] + 2 * block[1] + block[2]]
        return pltpu.make_async_remote_copy(
            src_ref=self.srcs[t] if own else slot, dst_ref=slot,
            send_sem=self.send_sems.at[t, k], recv_sem=self.recv_sems.at[t, k],
            device_id=to, device_id_type=pl.DeviceIdType.MESH)

    def _local(self, t, me):
        return pltpu.make_async_copy(self.srcs[t], self.dsts[t].at[4 * me[0] + 2 * me[1] + me[2]], self.local_sems.at[t])

    def start(self):
        me, sib, chips = self._places()
        for t in range(len(self.srcs)):
            self._local(t, me).start()
            self._copy(t, 0, me, sib, own=True).start()
            for j, chip in enumerate(chips):
                self._copy(t, 1 + j, me, (*chip, me[2]), own=True).start()

    def wait(self):
        me, sib, chips = self._places()
        n = len(self.srcs)
        for t in range(n):
            for j, chip in enumerate(chips):
                self._copy(t, 1 + j, (*chip, me[2]), me).wait_recv()
                self._copy(t, 4 + j, (*chip, me[2]), sib).start()
        for t in range(n):
            self._copy(t, 0, sib, me).wait_recv()
            for j, chip in enumerate(chips):
                self._copy(t, 4 + j, (*chip, sib[2]), me).wait_recv()
            self._copy(t, 0, me, sib, own=True).wait_send()
            for j, chip in enumerate(chips):
                self._copy(t, 1 + j, me, (*chip, me[2]), own=True).wait_send()
                self._copy(t, 4 + j, (*chip, me[2]), sib).wait_send()
            self._local(t, me).wait()


def _exchange_sems(n):
    return [pltpu.SemaphoreType.DMA((n, NDEV - 1)), pltpu.SemaphoreType.DMA((n, NDEV - 1)), pltpu.SemaphoreType.DMA((n,))]


def _exchange_out_shapes(xs, gather):
    return [jax.ShapeDtypeStruct(x.shape if not gather else (NDEV,) + x.shape, x.dtype) for x in xs]


def _exchange(xs, name, gather):
    n = len(xs)

    def body(*refs):
        ex = _Exchange(refs[:n], refs[n:2 * n], *refs[2 * n:], gather)
        ex.start()
        ex.wait()

    any_spec = pl.BlockSpec(memory_space=pl.ANY)
    return pl.pallas_call(
        body,
        name=name,
        out_shape=_exchange_out_shapes(xs, gather),
        in_specs=[any_spec] * n,
        out_specs=[any_spec] * n,
        scratch_shapes=_exchange_sems(n),
    )(*xs)


def _mod_call(c_all, w_ada, b_ada):
    ncol = w_ada.shape[1]

    def body(c_ref, w_ref, b_ref, silu_ref, mod_ref):
        cv = c_ref[...]
        s = cv * _sigmoid(cv)
        silu_ref[...] = s
        mod_ref[...] = _dot(s.astype(BF16), w_ref[...].astype(BF16)) + b_ref[...]

    return pl.pallas_call(
        body,
        name="mod_fwd",
        out_shape=(jax.ShapeDtypeStruct((16, D), F32), jax.ShapeDtypeStruct((16, ncol), F32)),
    )(c_all, w_ada, b_ada)


def _emb_ln_call(x0, gb, name, gathered=()):
    L = x0.shape[0]
    nt = L // TM
    ng = len(gathered)

    def body(*refs):
        x_ref, gb_ref = refs[:2]
        xe_ref = refs[2 + ng]
        i = pl.program_id(0)
        if ng:
            ex = _ChipGather(refs[2:2 + ng], refs[3 + ng:3 + 2 * ng], *refs[3 + 2 * ng:])

            @pl.when(i == 0)
            def _():
                ex.start()

        xe_ref[...] = _ln(x_ref[...], gb_ref[0:1, :], gb_ref[1:2, :])[0]
        if ng:
            @pl.when(i == nt - 1)
            def _():
                ex.wait()

    any_spec = pl.BlockSpec(memory_space=pl.ANY)
    return pl.pallas_call(
        body,
        name=name,
        grid=(nt,),
        in_specs=[_tiles(TM, D), _resident((8, D))] + [any_spec] * ng,
        out_specs=[_tiles(TM, D)] + [any_spec] * ng,
        out_shape=[jax.ShapeDtypeStruct((L, D), F32)] + _exchange_out_shapes(gathered, True),
        scratch_shapes=_exchange_sems(ng) if ng else [],
        compiler_params=_cp(),
    )(x0, gb, *gathered)


def _f1_call(xe, vec, w_in, cw, name):
    L = xe.shape[0]
    prev, nxt = _halo_specs(TM, D, L)

    def body(xe_ref, prev_ref, next_ref, vec_ref, w_ref, cw_ref, h1_ref, z_ref, xbcr_ref, pre_ref, xbc_ref, dt_ref,
             up_ref, ext):
        i = pl.program_id(0)
        n = pl.num_programs(0)
        def modulated(v):
            return (v * (1.0 + vec_ref[V_SC1:V_SC1 + 1, :]) + vec_ref[V_SH1:V_SH1 + 1, :]).astype(BF16)

        rows = jnp.concatenate([prev_ref[...], xe_ref[...], next_ref[...]], axis=0)
        proj = _dot(modulated(rows), w_ref[...])
        h1_ref[...] = modulated(xe_ref[...])
        z_ref[...] = proj[8:8 + TM, 0:1024]
        xbcr_ref[...] = proj[8:8 + TM, 1024:2304]
        dt_ref[...] = proj[8:8 + TM, 2304:2560]
        up_ref[...] = proj[8:8 + TM, 2560:3584]
        ext[...] = proj[:, 1024:2304]
        ext[0:8, :] = jnp.where(i > 0, ext[0:8, :], 0.0)
        ext[8 + TM:16 + TM, :] = jnp.where(i < n - 1, ext[8 + TM:16 + TM, :], 0.0)
        pre = _conv_pre(ext, cw_ref, TM)
        pre_ref[...] = pre
        xbc_ref[...] = pre * _sigmoid(pre)

    return pl.pallas_call(
        body,
        name=name,
        grid=(L // TM,),
        in_specs=[_tiles(TM, D), prev, nxt, _resident((NV, D)), _resident((D, WIN)), _resident((8, DXBC))],
        out_specs=[_tiles(TM, D), _tiles(TM, D), _tiles(TM, DXBC), _tiles(TM, DXBC), _tiles(TM, DXBC), _tiles(TM, 256),
                   _tiles(TM, D)],
        out_shape=(
            jax.ShapeDtypeStruct((L, D), BF16),
            jax.ShapeDtypeStruct((L, D), F32),
            jax.ShapeDtypeStruct((L, DXBC), F32),
            jax.ShapeDtypeStruct((L, DXBC), F32),
            jax.ShapeDtypeStruct((L, DXBC), F32),
            jax.ShapeDtypeStruct((L, 256), F32),
            jax.ShapeDtypeStruct((L, D), F32),
        ),
        scratch_shapes=[pltpu.VMEM((TM + 16, DXBC), F32)],
        compiler_params=_cp(),
    )(xe, xe, xe, vec, w_in, cw)


def _extended(ext, cur_ref, prev_ref, next_ref):
    i = pl.program_id(0)
    n = pl.num_programs(0)
    tm = cur_ref.shape[0]
    ext[0:8, :] = jnp.where(i > 0, prev_ref[...], 0.0)
    ext[8:8 + tm, :] = cur_ref[...]
    ext[8 + tm:16 + tm, :] = jnp.where(i < n - 1, next_ref[...], 0.0)
    return ext


def _shifted(ext, offset, tm):
    return ext[8 + offset:8 + offset + tm, :]


def _conv_pre(ext, cw_ref, tm):
    acc = cw_ref[5:6, :] + cw_ref[0:1, :] * _shifted(ext, -2, tm)
    for k in range(1, 5):
        acc = acc + cw_ref[k:k + 1, :] * _shifted(ext, k - 2, tm)
    return acc


def _ssd_common(d, dtr, par_ref):
    lane = lax.broadcasted_iota(jnp.int32, (1, 128), 1)
    hmask = lane < NH
    bias = par_ref[d:d + 1, :]
    alog = par_ref[2 + d:3 + d, :]
    aneg = jnp.where(hmask, -jnp.exp(alog), 0.0)
    pre = dtr + bias
    dt = jnp.where(hmask, _softplus(pre), 0.0)
    a = dt * aneg
    row = lax.broadcasted_iota(jnp.int32, (Q, Q), 0)
    col = lax.broadcasted_iota(jnp.int32, (Q, Q), 1)
    maskf = ((row >= col) if d == 0 else (row <= col)).astype(F32)
    mask_t = ((row <= col) if d == 0 else (row >= col)).astype(F32)
    A = _tri_dot(maskf, a)
    atot = _rowsum(a)
    return dict(hmask=hmask, aneg=aneg, pre=pre, dt=dt, a=a, maskf=maskf, mask_t=mask_t, A=A, AT=A.T, dtT=dt.T,
                atot=atot, lane=lane)


def _tri_dot(mask, x):
    t1 = x.astype(BF16)
    r1 = x - t1.astype(F32)
    t2 = r1.astype(BF16)
    t3 = (r1 - t2.astype(F32)).astype(BF16)
    mb = mask.astype(BF16)
    return _dot(jnp.concatenate([mb, mb, mb], axis=1), jnp.concatenate([t1, t2, t3], axis=0))


def _column(v, lane, h):
    return jnp.sum(jnp.where(lane == h, v, 0.0), axis=1, keepdims=True)


def _head_expand(e_ref, v):
    hi, lo = _split2(v)
    return _dot(jnp.concatenate([hi, lo], axis=1), e_ref[...])


def _head_sum(et_ref, v):
    return _dot(v.astype(BF16), et_ref[:, 0:128])


def _state_decay(atot, lane):
    ea = jnp.exp(atot)
    return jnp.concatenate([jnp.broadcast_to(_column(ea, lane, h), (HP, NS)) for h in range(NH)], axis=0)


def _ssd_fwd_call(xbc, dtr, h0, par, e_mat, et_mat, name, gathered=()):
    L = xbc.shape[0]
    nc = L // Q
    ng = len(gathered)

    def body(*refs):
        xbc_refs, dtr_refs = refs[0:2], refs[2:4]
        h0_ref, par_ref, e_ref, et_ref = refs[4:8]
        y_refs, hp_refs = refs[8 + ng:10 + ng], refs[10 + ng:12 + ng]
        hf_ref = refs[12 + ng]
        hs, AT, dtT = refs[13 + 2 * ng:16 + 2 * ng]
        s = pl.program_id(0)
        if ng:
            ex = _Exchange(refs[8:8 + ng], refs[13 + ng:13 + 2 * ng], *refs[16 + 2 * ng:], True)

        @pl.when(s == 0)
        def _():
            if ng:
                ex.start()
            hs[...] = h0_ref[...]

        for d in range(2):
            xbc_ref, y_ref = xbc_refs[d], y_refs[d]
            q = _ssd_common(d, dtr_refs[d][...], par_ref)
            A, maskf, lane = q["A"], q["maskf"], q["lane"]
            AT[d] = q["AT"]
            dtT[d] = q["dtT"]
            hprev = hs[d]
            hp_refs[d][0] = hprev
            bb = xbc_ref[:, 1024:1152].astype(BF16)
            cb = xbc_ref[:, 1152:1280].astype(BF16)
            g = _dot_nt(cb, bb)
            yoff = _dot_nt(cb, hprev.astype(BF16)) * _head_expand(e_ref, jnp.exp(A))
            for k in range(NH // 2):
                ks = slice(128 * k, 128 * k + 128)
                xp = xbc_ref[:, ks]
                scs, xhs = [], []
                for half in range(2):
                    h = 2 * k + half
                    seg = _column(A, lane, h) - AT[d, h:h + 1, :]
                    lm = jnp.exp(jnp.minimum(seg, 0.0)) * maskf
                    scs.append((g * lm * dtT[d, h:h + 1, :]).astype(BF16))
                    inhead = (lane >= HP) if half else (lane < HP)
                    xhs.append(jnp.where(inhead, xp, 0.0).astype(BF16))
                y_ref[:, ks] = yoff[:, ks] + _dot(jnp.concatenate(scs, axis=1), jnp.concatenate(xhs, axis=0))
            wend = jnp.exp(q["atot"] - A) * q["dt"]
            xw = (xbc_ref[:, 0:1024] * _head_expand(e_ref, wend)).astype(BF16)
            hnew = hprev * _state_decay(q["atot"], lane) + _dot_tn(xw, bb)
            hs[d] = hnew
            hf_ref[d] = hnew
        if ng:
            @pl.when(s == nc - 1)
            def _():
                ex.wait()

    any_spec = pl.BlockSpec(memory_space=pl.ANY)
    return pl.pallas_call(
        body,
        name=name,
        grid=(nc,),
        in_specs=[
            pl.BlockSpec((Q, DXBC), lambda s: (s, 0)),
            pl.BlockSpec((Q, DXBC), lambda s: (nc - 1 - s, 0)),
            pl.BlockSpec((Q, 128), lambda s: (s, 0)),
            pl.BlockSpec((Q, 128), lambda s: (nc - 1 - s, 1)),
            _resident((2, D, NS)),
            _resident((8, 128)),
            _resident((256, D)),
            _resident((D, 256)),
        ] + [any_spec] * ng,
        out_specs=[
            pl.BlockSpec((Q, D), lambda s: (s, 0)),
            pl.BlockSpec((Q, D), lambda s: (nc - 1 - s, 0)),
            pl.BlockSpec((1, D, NS), lambda s: (s, 0, 0)),
            pl.BlockSpec((1, D, NS), lambda s: (nc - 1 - s, 0, 0)),
            _const_out((2, D, NS)),
        ] + [any_spec] * ng,
        out_shape=[
            jax.ShapeDtypeStruct((L, D), F32),
            jax.ShapeDtypeStruct((L, D), F32),
            jax.ShapeDtypeStruct((nc, D, NS), F32),
            jax.ShapeDtypeStruct((nc, D, NS), F32),
            jax.ShapeDtypeStruct((2, D, NS), F32),
        ] + _exchange_out_shapes(gathered, True),
        scratch_shapes=[pltpu.VMEM((2, D, NS), F32), pltpu.VMEM((2, 128, Q), F32), pltpu.VMEM((2, 128, Q), F32)]
        + (_exchange_sems(ng) if ng else []),
        compiler_params=_cp(),
    )(xbc, xbc, dtr, dtr, h0, par, e_mat, et_mat, *gathered)


def _halo_tokens(g):
    return (WINDOWS[g] // 2) * GW


def _pool_specs(n_tiles):
    cur = pl.BlockSpec((PT, D), lambda i: (i, 0))
    prev = pl.BlockSpec((PT, D), lambda i: (jnp.maximum(i - 1, 0), 0))
    nxt = pl.BlockSpec((PT, D), lambda i: (jnp.minimum(i + 1, n_tiles - 1), 0))
    return cur, prev, nxt


def _pool_fwd_call(up, pmat, icnt, pool_w):
    L = up.shape[0]
    nt = L // PT
    cur, prev, nxt = _pool_specs(nt)

    def body(cur_ref, prev_ref, next_ref, m0_ref, m1_ref, m2_ref, m3_ref, ic_ref, pw_ref, d_ref, pun_ref):
        i = pl.program_id(0)
        n = pl.num_programs(0)
        lane = lax.broadcasted_iota(jnp.int32, (1, 128), 1)
        icv = ic_ref[...]
        for g, m_ref in enumerate((m0_ref, m1_ref, m2_ref, m3_ref)):
            gs = slice(PG * g, PG * g + PG)
            halo = _halo_tokens(g)
            top = jnp.where(i > 0, prev_ref[PT - halo:PT, gs], 0.0)
            bot = jnp.where(i < n - 1, next_ref[0:halo, gs], 0.0)
            mid = cur_ref[:, gs]
            box = _split_dot(m_ref[...], jnp.concatenate([top, mid, bot], axis=0))
            dg = (box * _column(icv, lane, g) - mid).astype(BF16)
            d_ref[:, gs] = dg
            pun_ref[:, gs] = _dot(dg, pw_ref[g])

    return pl.pallas_call(
        body,
        name="pool_fwd",
        grid=(nt,),
        in_specs=[cur, prev, nxt] + [_resident(m.shape) for m in pmat] + [_tiles(PT, 128), _resident((4, PG, PG))],
        out_specs=[_tiles(PT, D), _tiles(PT, D)],
        out_shape=(jax.ShapeDtypeStruct((L, D), BF16), jax.ShapeDtypeStruct((L, D), F32)),
        compiler_params=_cp(),
    )(up, up, up, *pmat, icnt, pool_w)


def _gated(yf_ref, yb_ref, xs, z, vec_ref):
    ym = yf_ref[...] + yb_ref[...] + vec_ref[V_DSK:V_DSK + 1, :] * xs
    sz = _sigmoid(z)
    gated = ym * (z * sz)
    r = lax.rsqrt(jnp.mean(gated * gated, axis=-1, keepdims=True) + LN_EPS)
    return ym, sz, gated, r


def _merge_call(yf, yb, xbc, z, pun, xe, vec, w_out):
    L = z.shape[0]

    def body(yf_ref, yb_ref, xs_ref, z_ref, pun_ref, xe_ref, vec_ref, w_ref, x1_ref, mix_ref, cat_ref):
        _, _, gated, r = _gated(yf_ref, yb_ref, xs_ref[...], z_ref[...], vec_ref)
        yn = gated * r * vec_ref[V_SSDG:V_SSDG + 1, :]
        p = pun_ref[...] * vec_ref[V_PSC:V_PSC + 1, :]
        cat = jnp.concatenate([yn, p], axis=1).astype(BF16)
        mix = _dot(cat, w_ref[...])
        pre1 = ALPHA * xe_ref[...] + vec_ref[V_G1:V_G1 + 1, :] * mix
        x1, _, _ = _ln(pre1, vec_ref[V_LN1G:V_LN1G + 1, :], vec_ref[V_LN1B:V_LN1B + 1, :])
        x1_ref[...] = x1
        mix_ref[...] = mix
        cat_ref[...] = cat

    return pl.pallas_call(
        body,
        name="merge_fwd",
        grid=(L // TM,),
        in_specs=[
            _tiles(TM, D), _tiles(TM, D), _tiles(TM, D), _tiles(TM, D), _tiles(TM, D), _tiles(TM, D),
            _resident((NV, D)), _resident((2 * D, D)),
        ],
        out_specs=[_tiles(TM, D), _tiles(TM, D), _tiles(TM, 2 * D)],
        out_shape=(
            jax.ShapeDtypeStruct((L, D), F32),
            jax.ShapeDtypeStruct((L, D), F32),
            jax.ShapeDtypeStruct((L, 2 * D), BF16),
        ),
        compiler_params=_cp(),
    )(yf, yb, xbc, z, pun, xe, vec, w_out)


def _ffn_fwd_call(x1, tgt, vec, w_gate, w_up, w_down):
    L = x1.shape[0]

    def body(x1_ref, tgt_ref, vec_ref, wg_ref, wu_ref, wd_ref, dpre_ref, gacc_ref, gt_ref, up_ref):
        @pl.when(pl.program_id(0) == 0)
        def _():
            gacc_ref[...] = jnp.zeros_like(gacc_ref)

        x1 = x1_ref[...]
        h2 = (x1 * (1.0 + vec_ref[V_SC2:V_SC2 + 1, :]) + vec_ref[V_SH2:V_SH2 + 1, :]).astype(BF16)
        gt = _dot(h2, wg_ref[...])
        up = _dot(h2, wu_ref[...])
        gt_ref[...] = gt.astype(BF16)
        up_ref[...] = up.astype(BF16)
        f = (gt * _sigmoid(gt) * up).astype(BF16)
        ffn = _dot(f, wd_ref[...])
        g2 = vec_ref[V_G2:V_G2 + 1, :]
        lng = vec_ref[V_LN2G:V_LN2G + 1, :]
        x2, n2, rstd2 = _ln(ALPHA * x1 + g2 * ffn, lng, vec_ref[V_LN2B:V_LN2B + 1, :])
        diff = x2 - tgt_ref[...]
        dx2 = diff * (1.0 / D)
        dpre2 = _ln_bwd(dx2, n2, rstd2, lng)
        dpre_ref[...] = dpre2
        gacc_ref[V_LN2G:V_LN2G + 1, :] += _rowsum(dx2 * n2)
        gacc_ref[V_LN2B:V_LN2B + 1, :] += _rowsum(dx2)
        gacc_ref[V_G2:V_G2 + 1, :] += _rowsum(dpre2 * ffn)
        gacc_ref[V_LOSS:V_LOSS + 1, :] += jnp.sum(diff * diff) * (0.5 / D)

    return pl.pallas_call(
        body,
        name="ffn_fwd",
        grid=(L // TM,),
        in_specs=[_tiles(TM, D), _tiles(TM, D), _resident((NV, D)),
                  _resident((D, DFF)), _resident((D, DFF)), _resident((DFF, D))],
        out_specs=[_tiles(TM, D), _const_out((NV, D)), _tiles(TM, DFF), _tiles(TM, DFF)],
        out_shape=(jax.ShapeDtypeStruct((L, D), F32), jax.ShapeDtypeStruct((NV, D), F32),
                   jax.ShapeDtypeStruct((L, DFF), BF16), jax.ShapeDtypeStruct((L, DFF), BF16)),
        compiler_params=_cp(),
    )(x1, tgt, vec, w_gate, w_up, w_down)


def _ffn_bwd_call(x1, dpre2, gt_b, up_b, vec, w_gate, w_up, w_down):
    L = x1.shape[0]
    nt = L // TM
    nj = DFF // FFC

    def body(x1_ref, dpre_ref, gt_ref, up_ref, vec_ref, wg_ref, wu_ref, wd_ref, dh2_ref, dwg_ref, dwu_ref, dwd_ref,
             ag, au, ad):
        j = pl.program_id(0)
        i = pl.program_id(1)

        @pl.when(i == 0)
        def _():
            ag[...] = jnp.zeros_like(ag)
            au[...] = jnp.zeros_like(au)
            ad[...] = jnp.zeros_like(ad)

        h2 = (x1_ref[...] * (1.0 + vec_ref[V_SC2:V_SC2 + 1, :]) + vec_ref[V_SH2:V_SH2 + 1, :]).astype(BF16)
        gt = gt_ref[...].astype(F32)
        up = up_ref[...].astype(F32)
        sg = _sigmoid(gt)
        sl = gt * sg
        f = (sl * up).astype(BF16)
        dffn = (vec_ref[V_G2:V_G2 + 1, :] * dpre_ref[...]).astype(BF16)
        df = _dot_nt(dffn, wd_ref[...])
        dgt = (df * up * (sg * (1.0 + gt * (1.0 - sg)))).astype(BF16)
        dup = (df * sl).astype(BF16)
        dh2_ref[0] = _dot_nt(dgt, wg_ref[...]) + _dot_nt(dup, wu_ref[...])
        _acc_tn(ag, h2, dgt)
        _acc_tn(au, h2, dup)
        _acc_tn(ad, f, dffn)

        @pl.when(i == nt - 1)
        def _():
            pltpu.sync_copy(ag, dwg_ref.at[j])
            pltpu.sync_copy(au, dwu_ref.at[j])
            pltpu.sync_copy(ad, dwd_ref.at[j])

    any_spec = pl.BlockSpec(memory_space=pl.ANY)
    return pl.pallas_call(
        body,
        name="ffn_bwd",
        grid=(nj, nt),
        in_specs=[
            pl.BlockSpec((TM, D), lambda j, i: (i, 0)),
            pl.BlockSpec((TM, D), lambda j, i: (i, 0)),
            pl.BlockSpec((TM, FFC), lambda j, i: (i, j)),
            pl.BlockSpec((TM, FFC), lambda j, i: (i, j)),
            _resident((NV, D)),
            pl.BlockSpec((D, FFC), lambda j, i: (0, j)),
            pl.BlockSpec((D, FFC), lambda j, i: (0, j)),
            pl.BlockSpec((FFC, D), lambda j, i: (j, 0)),
        ],
        out_specs=[pl.BlockSpec((1, TM, D), lambda j, i: (j, i, 0)), any_spec, any_spec, any_spec],
        out_shape=(
            jax.ShapeDtypeStruct((nj, L, D), F32),
            jax.ShapeDtypeStruct((nj, D, FFC), F32),
            jax.ShapeDtypeStruct((nj, D, FFC), F32),
            jax.ShapeDtypeStruct((nj, FFC, D), F32),
        ),
        scratch_shapes=[pltpu.VMEM((D, FFC), F32), pltpu.VMEM((D, FFC), F32), pltpu.VMEM((FFC, D), F32)],
        compiler_params=_cp(2),
    )(x1, dpre2, gt_b, up_b, vec, w_gate, w_up, w_down)


def _merge_bwd_call(dh2p, dpre2, xe, mix, cat, yf, yb, xbc, z, dpool, pun, vec, w_out, pool_w, scattered):
    L = z.shape[0]
    nt = L // TM
    ns = len(scattered)

    def body(*refs):
        (dh2_ref, dpre2_ref, xe_ref, mix_ref, cat_ref, yf_ref, yb_ref, xs_ref, z_ref, dpool_ref, pun_ref,
         vec_ref, w_ref, pw_ref) = refs[:14]
        dxe_ref, dy_ref, dz_ref, dd_ref, dpw_ref, gacc_ref, dwo_ref = refs[14 + ns:21 + ns]
        dwo_s = refs[21 + 2 * ns]
        ex = _Exchange(refs[14:14 + ns], refs[21 + ns:21 + 2 * ns], *refs[22 + 2 * ns:], False)
        i = pl.program_id(0)

        @pl.when(i == 0)
        def _():
            ex.start()
            gacc_ref[...] = jnp.zeros_like(gacc_ref)
            dpw_ref[...] = jnp.zeros_like(dpw_ref)
            dwo_s[...] = jnp.zeros_like(dwo_s)

        def vrow(r):
            return vec_ref[r:r + 1, :]

        def gadd(r, val):
            gacc_ref[r:r + 1, :] += _rowsum(val)

        dh2 = dh2_ref[0] + dh2_ref[1]
        dx1 = ALPHA * dpre2_ref[...] + dh2 * (1.0 + vrow(V_SC2))
        mix = mix_ref[...]
        x1, n1, rstd1 = _ln(ALPHA * xe_ref[...] + vrow(V_G1) * mix, vrow(V_LN1G), vrow(V_LN1B))
        gadd(V_SC2, dh2 * x1)
        gadd(V_SH2, dh2)
        gadd(V_LN1G, dx1 * n1)
        gadd(V_LN1B, dx1)
        dpre1 = _ln_bwd(dx1, n1, rstd1, vrow(V_LN1G))
        dxe_ref[...] = ALPHA * dpre1
        gadd(V_G1, dpre1 * mix)
        dmix = (vrow(V_G1) * dpre1).astype(BF16)
        dcat = _dot_nt(dmix, w_ref[...])
        cat = cat_ref[...]
        for c0 in range(0, 2 * D, 512):
            dwo_s[c0:c0 + 512, :] += _dot_tn(cat[:, c0:c0 + 512], dmix)
        dyn = dcat[:, 0:D]
        dp = dcat[:, D:2 * D]
        xs = xs_ref[...]
        z = z_ref[...]
        ym, sz, gated, r = _gated(yf_ref, yb_ref, xs, z, vec_ref)
        gadd(V_SSDG, dyn * gated * r)
        a = dyn * vrow(V_SSDG)
        dgated = r * a - gated * (r * r * r * jnp.mean(a * gated, axis=-1, keepdims=True))
        dym = dgated * (z * sz)
        dy_ref[...] = dym
        dz_ref[...] = dgated * ym * (sz * (1.0 + z * (1.0 - sz)))
        gadd(V_DSK, dym * xs)
        gadd(V_PSC, dp * pun_ref[...])
        dps = (dp * vrow(V_PSC)).astype(BF16)
        dpool = dpool_ref[...]
        for g in range(4):
            gs = slice(PG * g, PG * g + PG)
            dd_ref[:, gs] = _dot_nt(dps[:, gs], pw_ref[g])
            dpw_ref[g] += _dot_tn(dpool[:, gs], dps[:, gs])

        @pl.when(i == nt - 1)
        def _():
            pltpu.sync_copy(dwo_s, dwo_ref)
            ex.wait()

    any_spec = pl.BlockSpec(memory_space=pl.ANY)
    return pl.pallas_call(
        body,
        name="merge_bwd",
        grid=(nt,),
        in_specs=[
            pl.BlockSpec((2, TM, D), lambda i: (0, i, 0)),
            _tiles(TM, D), _tiles(TM, D), _tiles(TM, D), _tiles(TM, 2 * D),
            _tiles(TM, D), _tiles(TM, D),
            _tiles(TM, D), _tiles(TM, D), _tiles(TM, D), _tiles(TM, D),
            _resident((NV, D)), _resident((2 * D, D)), _resident((4, PG, PG)),
        ] + [any_spec] * ns,
        out_specs=[_tiles(TM, D), _tiles(TM, D), _tiles(TM, D), _tiles(TM, D),
                   _const_out((4, PG, PG)), _const_out((NV, D)), any_spec] + [any_spec] * ns,
        out_shape=[
            jax.ShapeDtypeStruct((L, D), F32),
            jax.ShapeDtypeStruct((L, D), F32),
            jax.ShapeDtypeStruct((L, D), F32),
            jax.ShapeDtypeStruct((L, D), F32),
            jax.ShapeDtypeStruct((4, PG, PG), F32),
            jax.ShapeDtypeStruct((NV, D), F32),
            jax.ShapeDtypeStruct((2 * D, D), F32),
        ] + _exchange_out_shapes(scattered, False),
        scratch_shapes=[pltpu.VMEM((2 * D, D), F32)] + _exchange_sems(ns),
        compiler_params=_cp(),
    )(dh2p, dpre2, xe, mix, cat, yf, yb, xbc, z, dpool, pun, vec, w_out, pool_w, *scattered)


def _pool_bwd_call(dd, pmat_t, icnt):
    L = dd.shape[0]
    nt = L // PT
    cur, prev, nxt = _pool_specs(nt)
    icur = pl.BlockSpec((PT, 128), lambda i: (i, 0))
    iprev = pl.BlockSpec((PT, 128), lambda i: (jnp.maximum(i - 1, 0), 0))
    inxt = pl.BlockSpec((PT, 128), lambda i: (jnp.minimum(i + 1, nt - 1), 0))

    def body(cur_ref, prev_ref, next_ref, ic_ref, icp_ref, icn_ref, m0_ref, m1_ref, m2_ref, m3_ref, du_ref):
        i = pl.program_id(0)
        n = pl.num_programs(0)
        lane = lax.broadcasted_iota(jnp.int32, (1, 128), 1)
        icv = ic_ref[...]
        for g, m_ref in enumerate((m0_ref, m1_ref, m2_ref, m3_ref)):
            gs = slice(PG * g, PG * g + PG)
            halo = _halo_tokens(g)
            icp = _column(icp_ref[PT - halo:PT, :], lane, g)
            icn = _column(icn_ref[0:halo, :], lane, g)
            top = jnp.where(i > 0, prev_ref[PT - halo:PT, gs] * icp, 0.0)
            bot = jnp.where(i < n - 1, next_ref[0:halo, gs] * icn, 0.0)
            mid = cur_ref[:, gs]
            ext = jnp.concatenate([top, mid * _column(icv, lane, g), bot], axis=0)
            du_ref[:, gs] = _split_dot(m_ref[...], ext) - mid

    return pl.pallas_call(
        body,
        name="pool_bwd",
        grid=(nt,),
        in_specs=[cur, prev, nxt, icur, iprev, inxt] + [_resident(m.shape) for m in pmat_t],
        out_specs=_tiles(PT, D),
        out_shape=jax.ShapeDtypeStruct((L, D), F32),
        compiler_params=_cp(),
    )(dd, dd, dd, icnt, icnt, icnt, *pmat_t)


def _ssd_bwd_call(dy, xbc, dtr, hprev_f, hprev_b, dh_init, par, e_mat, et_mat, dskip, name):
    L = xbc.shape[0]
    nc = L // Q

    def body(dy0_ref, dy1_ref, xbc0_ref, xbc1_ref, dtr0_ref, dtr1_ref, hp0_ref, hp1_ref, dhi_ref, par_ref, e_ref,
             et_ref, dsk_ref, dx0_ref, dx1_ref, ddt0_ref, ddt1_ref, acc_ref, dh0_ref, dh_s, AT, dtT, ddtT_s, dAT_s):
        s = pl.program_id(0)

        @pl.when(s == 0)
        def _():
            dh_s[...] = dhi_ref[...]
            acc_ref[...] = jnp.zeros_like(acc_ref)
            ddtT_s[...] = jnp.zeros_like(ddtT_s)
            dAT_s[...] = jnp.zeros_like(dAT_s)

        one_direction(0, dy0_ref, xbc0_ref, dtr0_ref, hp0_ref, par_ref, e_ref, et_ref, dsk_ref, dx0_ref, ddt0_ref,
                      acc_ref, dh0_ref, dh_s, AT, dtT, ddtT_s, dAT_s)
        one_direction(1, dy1_ref, xbc1_ref, dtr1_ref, hp1_ref, par_ref, e_ref, et_ref, dsk_ref, dx1_ref, ddt1_ref,
                      acc_ref, dh0_ref, dh_s, AT, dtT, ddtT_s, dAT_s)

    def one_direction(d, dy_ref, xbc_ref, dtr_ref, hp_ref, par_ref, e_ref, et_ref, dsk_ref, dxbc_ref, ddtr_ref,
                      acc_ref, dh0_ref, dh_s, AT_s, dtT_s, ddtT_s, dAT_s):
        q = _ssd_common(d, dtr_ref[...], par_ref)
        A, maskf, lane, dt, atot = q["A"], q["maskf"], q["lane"], q["dt"], q["atot"]
        AT_s[d] = q["AT"]
        dtT_s[d] = q["dtT"]
        AT, dtT = AT_s.at[d], dtT_s.at[d]
        hprev = hp_ref[0]
        hpb = hprev.astype(BF16)
        dh = dh_s[d]
        dhb = dh.astype(BF16)
        xs = xbc_ref[:, 0:1024]
        bb = xbc_ref[:, 1024:1152].astype(BF16)
        cb = xbc_ref[:, 1152:1280].astype(BF16)
        dy = dy_ref[...]
        ea_f = _head_expand(e_ref, jnp.exp(A))
        ch = _dot_nt(cb, hpb)
        dch = (dy * ea_f).astype(BF16)
        dC = _dot(dch, hpb)
        dhprev = _dot_tn(dch, cb)
        dA = _head_sum(et_ref, dy * ch * ea_f)
        dec = _state_decay(atot, lane)
        dhprev = dhprev + dh * dec
        dhh = dh * hprev * dec
        datot = jnp.zeros((1, 128), F32)
        for h in range(NH):
            tot_h = jnp.sum(_rowsum(dhh[HP * h:HP * h + HP, :]), axis=1, keepdims=True)
            datot = datot + jnp.where(lane == h, tot_h, 0.0)
        ear = jnp.exp(atot - A)
        wend = ear * dt
        wf = _head_expand(e_ref, wend)
        xw = (xs * wf).astype(BF16)
        dxw = _dot_nt(bb, dhb)
        dB = _dot(xw, dhb)
        dxs = dxw * wf
        dwend = _head_sum(et_ref, dxw * xs)
        ddt = dwend * ear
        de = dwend * wend
        datot = datot + _rowsum(de)
        dA = dA - de
        g = _dot_nt(cb, bb)
        dG = jnp.zeros((Q, Q), F32)
        for k in range(NH // 2):
            ks = slice(128 * k, 128 * k + 128)
            xp = xs[:, ks]
            dyp = dy[:, ks]
            accdx = dxs[:, ks]
            if d == 0:
                accdx = accdx + dyp * dsk_ref[:, ks]
            scts, dyhs = [], []
            for half in range(2):
                h = 2 * k + half
                inhead = (lane >= HP) if half else (lane < HP)
                seg = _column(A, lane, h) - AT[h:h + 1, :]
                lm = jnp.exp(jnp.minimum(seg, 0.0)) * maskf
                dtrow = dtT[h:h + 1, :]
                gl = g * lm
                sc = gl * dtrow
                dyh = jnp.where(inhead, dyp, 0.0).astype(BF16)
                xh = jnp.where(inhead, xp, 0.0).astype(BF16)
                dS = _dot_nt(dyh, xh)
                scts.append(sc.T.astype(BF16))
                dyhs.append(dyh)
                nn = dS * gl
                cn = _rowsum(nn)
                rm = jnp.sum(nn * dtrow, axis=1, keepdims=True)
                dG = dG + dS * (lm * dtrow)
                ddtT_s[d, h:h + 1, :] = cn
                dAT_s[d, h:h + 1, :] = -(cn * dtrow)
                dA = dA + rm * (lane == h).astype(F32)
            dxbc_ref[:, ks] = accdx + _dot(jnp.concatenate(scts, axis=1), jnp.concatenate(dyhs, axis=0))
        dGb = dG.astype(BF16)
        dxbc_ref[:, 1024:1152] = dB + _dot_tn(dGb, cb)
        dxbc_ref[:, 1152:1280] = dC + _dot(dGb, bb)
        da = _tri_dot(q["mask_t"], dA + dAT_s[d].T) + datot
        ddt = ddt + ddtT_s[d].T + da * q["aneg"]
        ddtr = jnp.where(q["hmask"], ddt * _sigmoid(q["pre"]), 0.0)
        ddtr_ref[...] = ddtr
        acc_ref[d, 0:1, :] += _rowsum(ddtr)
        acc_ref[d, 1:2, :] += _rowsum(da * dt) * q["aneg"]
        dh_s[d] = dhprev
        dh0_ref[d] = dhprev

    def back(s):
        return nc - 1 - s

    return pl.pallas_call(
        body,
        name=name,
        grid=(nc,),
        in_specs=[
            pl.BlockSpec((Q, D), lambda s: (back(s), 0)),
            pl.BlockSpec((Q, D), lambda s: (s, 0)),
            pl.BlockSpec((Q, DXBC), lambda s: (back(s), 0)),
            pl.BlockSpec((Q, DXBC), lambda s: (s, 0)),
            pl.BlockSpec((Q, 128), lambda s: (back(s), 0)),
            pl.BlockSpec((Q, 128), lambda s: (s, 1)),
            pl.BlockSpec((1, D, NS), lambda s: (back(s), 0, 0)),
            pl.BlockSpec((1, D, NS), lambda s: (s, 0, 0)),
            _resident((2, D, NS)),
            _resident((8, 128)),
            _resident((256, D)),
            _resident((D, 256)),
            _resident((1, D)),
        ],
        out_specs=[
            pl.BlockSpec((Q, DXBC), lambda s: (back(s), 0)),
            pl.BlockSpec((Q, DXBC), lambda s: (s, 0)),
            pl.BlockSpec((Q, 128), lambda s: (back(s), 0)),
            pl.BlockSpec((Q, 128), lambda s: (s, 0)),
            _const_out((2, 8, 128)),
            _const_out((2, D, NS)),
        ],
        out_shape=(
            jax.ShapeDtypeStruct((L, DXBC), F32),
            jax.ShapeDtypeStruct((L, DXBC), F32),
            jax.ShapeDtypeStruct((L, 128), F32),
            jax.ShapeDtypeStruct((L, 128), F32),
            jax.ShapeDtypeStruct((2, 8, 128), F32),
            jax.ShapeDtypeStruct((2, D, NS), F32),
        ),
        scratch_shapes=[pltpu.VMEM((2, D, NS), F32)] + [pltpu.VMEM((2, 128, Q), F32)] * 4,
        compiler_params=_cp(),
    )(dy, dy, xbc, xbc, dtr, dtr, hprev_f, hprev_b, dh_init, par, e_mat, et_mat, dskip)


def _conv_bwd_call(dxf, dxb, pre, xbc_raw, acc_init, name, scattered=()):
    L = xbc_raw.shape[0]
    nt = L // TM
    ns = len(scattered)
    prev, nxt = _halo_specs(TM, DXBC, L)

    def body(*refs):
        dxf_ref, dxb_ref, pre_ref, cur_ref, prev_ref, next_ref, init_ref = refs[:7]
        dpre_ref, acc_ref = refs[7 + ns:9 + ns]
        ext = refs[9 + 2 * ns]
        if ns:
            ex = _Exchange(refs[7:7 + ns], refs[9 + ns:9 + 2 * ns], *refs[10 + 2 * ns:], False)

        @pl.when(pl.program_id(0) == 0)
        def _():
            if ns:
                ex.start()
            acc_ref[...] = init_ref[...]

        _extended(ext, cur_ref, prev_ref, next_ref)
        pre = pre_ref[...]
        sg = _sigmoid(pre)
        dpre = (dxf_ref[...] + dxb_ref[...]) * (sg * (1.0 + pre * (1.0 - sg)))
        dpre_ref[...] = dpre
        for k in range(5):
            acc_ref[k:k + 1, :] += _rowsum(dpre * _shifted(ext, k - 2, TM))
        acc_ref[5:6, :] += _rowsum(dpre)
        if ns:
            @pl.when(pl.program_id(0) == nt - 1)
            def _():
                ex.wait()

    any_spec = pl.BlockSpec(memory_space=pl.ANY)
    return pl.pallas_call(
        body,
        name=name,
        grid=(nt,),
        in_specs=[_tiles(TM, DXBC), _tiles(TM, DXBC), _tiles(TM, DXBC), _tiles(TM, DXBC), prev, nxt,
                  _resident((8, DXBC))] + [any_spec] * ns,
        out_specs=[_tiles(TM, DXBC), _const_out((8, DXBC))] + [any_spec] * ns,
        out_shape=[jax.ShapeDtypeStruct((L, DXBC), F32), jax.ShapeDtypeStruct((8, DXBC), F32)]
        + _exchange_out_shapes(scattered, False),
        scratch_shapes=[pltpu.VMEM((TM + 16, DXBC), F32)] + (_exchange_sems(ns) if ns else []),
        compiler_params=_cp(),
    )(dxf, dxb, pre, xbc_raw, xbc_raw, xbc_raw, acc_init, *scattered)


def _inproj_bwd_call(dpre, cw, dz, ddt0, ddt1, dup, h1, dxe_part, x0, vec, w_in, dw_init, name):
    L = x0.shape[0]
    nt = L // TM
    prev, nxt = _halo_specs(TM, DXBC, L)

    def body(cur_ref, prev_ref, next_ref, cw_ref, dz_ref, ddt0_ref, ddt1_ref, dup_ref, h1_ref, dxe_ref, x0_ref,
             vec_ref, w_ref, dwi_ref, gx_ref, gacc_ref, dw_ref, dw_s, ext):
        i = pl.program_id(0)

        @pl.when(i == 0)
        def _():
            gacc_ref[...] = jnp.zeros_like(gacc_ref)
            pltpu.sync_copy(dwi_ref, dw_s)

        def vrow(r):
            return vec_ref[r:r + 1, :]

        _extended(ext, cur_ref, prev_ref, next_ref)
        dxr = cw_ref[0:1, :] * _shifted(ext, 2, TM)
        for k in range(1, 5):
            dxr = dxr + cw_ref[k:k + 1, :] * _shifted(ext, 2 - k, TM)
        dproj = jnp.concatenate([dz_ref[...], dxr, ddt0_ref[...], ddt1_ref[...], dup_ref[...]], axis=1).astype(BF16)
        dh1 = _dot_nt(dproj, w_ref[...])
        _acc_tn(dw_s, h1_ref[...], dproj)
        xe, n0, rstd0 = _ln(x0_ref[...], vrow(V_EMBG), vrow(V_EMBB))
        dxe = dxe_ref[...] + dh1 * (1.0 + vrow(V_SC1))
        gacc_ref[V_SC1:V_SC1 + 1, :] += _rowsum(dh1 * xe)
        gacc_ref[V_SH1:V_SH1 + 1, :] += _rowsum(dh1)
        gacc_ref[V_EMBG:V_EMBG + 1, :] += _rowsum(dxe * n0)
        gacc_ref[V_EMBB:V_EMBB + 1, :] += _rowsum(dxe)
        gx_ref[...] = _ln_bwd(dxe, n0, rstd0, vrow(V_EMBG))

        @pl.when(i == nt - 1)
        def _():
            pltpu.sync_copy(dw_s, dw_ref)

    any_spec = pl.BlockSpec(memory_space=pl.ANY)
    return pl.pallas_call(
        body,
        name=name,
        grid=(nt,),
        in_specs=[_tiles(TM, DXBC), prev, nxt, _resident((8, DXBC)), _tiles(TM, D), _tiles(TM, 128), _tiles(TM, 128),
                  _tiles(TM, D), _tiles(TM, D), _tiles(TM, D), _tiles(TM, D), _resident((NV, D)), _resident((D, WIN)),
                  any_spec],
        out_specs=[_tiles(TM, D), _const_out((NV, D)), any_spec],
        out_shape=(
            jax.ShapeDtypeStruct((L, D), F32),
            jax.ShapeDtypeStruct((NV, D), F32),
            jax.ShapeDtypeStruct((D, WIN), F32),
        ),
        scratch_shapes=[pltpu.VMEM((D, WIN), F32), pltpu.VMEM((TM + 16, DXBC), F32)],
        compiler_params=_cp(),
    )(dpre, dpre, dpre, cw, dz, ddt0, ddt1, dup, h1, dxe_part, x0, vec, w_in, dw_init)


def _adamw(w, g, m, v):
    m = ADAM_B1 * m + (1.0 - ADAM_B1) * g
    v = ADAM_B2 * v + (1.0 - ADAM_B2) * (g * g)
    m_hat = m / (1.0 - ADAM_B1 ** ADAM_STEP)
    v_hat = v / (1.0 - ADAM_B2 ** ADAM_STEP)
    delta = -ADAM_LR * (m_hat / (jnp.sqrt(v_hat) + ADAM_EPS) + ADAM_WD * w)
    return delta, m, v


def _adamw_shard_call(gslots, w, m, v, tr, name):
    R, C = w.shape

    def body(gs_ref, w_ref, m_ref, v_ref, g_ref, d_ref, mo_ref, vo_ref):
        g = gs_ref[0].astype(F32)
        for i in range(1, NDEV):
            g = g + gs_ref[i].astype(F32)
        delta, mn, vn = _adamw(w_ref[...], g, m_ref[...], v_ref[...])
        g_ref[...] = g
        d_ref[...] = delta
        mo_ref[...] = mn
        vo_ref[...] = vn

    t = _tiles(tr, C)
    return pl.pallas_call(
        body,
        name=name,
        grid=(R // tr,),
        in_specs=[pl.BlockSpec((NDEV, tr, C), lambda i: (0, i, 0)), t, t, t],
        out_specs=[t, t, t, t],
        out_shape=tuple(jax.ShapeDtypeStruct((R, C), F32) for _ in range(4)),
        compiler_params=_cp(),
    )(gslots, w, m, v)


def _wada_call(dm_ex, dm_ctx, silu_all, w, m, v):
    ncol = w.shape[1]

    def body(dme_ref, dmc_ref, s_ref, w_ref, m_ref, v_ref, g_ref, d_ref, mo_ref, vo_ref, ds_ref):
        dmc = _rowsum(dmc_ref[...])
        rows = lax.broadcasted_iota(jnp.int32, (8, 1), 0)
        low = jnp.where(rows == 0, dmc, 0.0)
        dm = jnp.concatenate([dme_ref[...], low], axis=0).astype(BF16)
        wv = w_ref[...]
        g = _dot_tn(s_ref[...].astype(BF16), dm)
        delta, mn, vn = _adamw(wv, g, m_ref[...], v_ref[...])
        g_ref[...] = g
        d_ref[...] = delta
        mo_ref[...] = mn
        vo_ref[...] = vn
        ds_ref[...] = _dot_nt(low.astype(BF16), wv.astype(BF16))

    return pl.pallas_call(
        body,
        name="wada_update",
        out_shape=tuple(jax.ShapeDtypeStruct((D, ncol), F32) for _ in range(4)) + (jax.ShapeDtypeStruct((8, D), F32),),
        compiler_params=pltpu.CompilerParams(vmem_limit_bytes=VMEM_LIMIT),
    )(dm_ex, dm_ctx, silu_all, w, m, v)


P_DMOD, P_DMODC, P_EMBG, P_EMBB, P_LN1G, P_LN1B, P_LN2G, P_LN2B = 0, 6, 8, 9, 10, 11, 12, 13
P_SSDG, P_PSC, P_DSK, P_CONVB, P_DTB, P_ALOG, P_LOSS, NP = 14, 15, 16, 17, 19, 20, 21, 24
S_CCTX, S_EMBG, S_EMBB, S_BADA, S_CONVB, S_DTB, S_ALOG, S_DSK = 0, 1, 2, 3, 9, 11, 12, 13
S_SSDG, S_PSC, S_LN1G, S_LN1B, S_LN2G, S_LN2B, NSM = 14, 15, 16, 17, 18, 19, 24


def _small_update_call(pall, dsil, cctx, w, m, v, et_mat):
    def body(p_ref, ds_ref, c_ref, w_ref, m_ref, v_ref, et_ref, g_ref, d_ref, mo_ref, vo_ref, loss_ref,
             tot, dsum, dsk8):
        tot[...] = p_ref[0]
        dsum[...] = ds_ref[0]
        for i in range(1, NDEV):
            tot[...] += p_ref[i]
            dsum[...] += ds_ref[i]
        cv = c_ref[...]
        sc = _sigmoid(cv)
        g_ref[...] = jnp.zeros_like(g_ref)
        g_ref[S_CCTX:S_CCTX + 1, :] = dsum[0:1, :] * (sc * (1.0 + cv * (1.0 - sc)))
        g_ref[S_EMBG:S_EMBG + 1, :] = tot[P_EMBG:P_EMBG + 1, :]
        g_ref[S_EMBB:S_EMBB + 1, :] = tot[P_EMBB:P_EMBB + 1, :]
        g_ref[S_BADA:S_BADA + 2, :] = tot[P_DMOD:P_DMOD + 2, :] + tot[P_DMODC:P_DMODC + 2, :]
        g_ref[S_BADA + 2:S_BADA + 6, :] = tot[P_DMOD + 2:P_DMOD + 6, :]
        g_ref[S_CONVB:S_CONVB + 2, :] = tot[P_CONVB:P_CONVB + 2, :]
        g_ref[S_DTB:S_DTB + 1, :] = tot[P_DTB:P_DTB + 1, :]
        g_ref[S_ALOG:S_ALOG + 1, :] = tot[P_ALOG:P_ALOG + 1, :]
        dsk8[...] = _dot(jnp.broadcast_to(tot[P_DSK:P_DSK + 1, :], (8, D)), et_ref[:, 0:128].astype(F32), HI)
        g_ref[S_DSK:S_DSK + 1, 0:128] = dsk8[0:1, :]
        g_ref[S_SSDG:S_SSDG + 1, :] = tot[P_SSDG:P_SSDG + 1, :]
        g_ref[S_PSC:S_PSC + 1, :] = tot[P_PSC:P_PSC + 1, :]
        g_ref[S_LN1G:S_LN1G + 1, :] = tot[P_LN1G:P_LN1G + 1, :]
        g_ref[S_LN1B:S_LN1B + 1, :] = tot[P_LN1B:P_LN1B + 1, :]
        g_ref[S_LN2G:S_LN2G + 1, :] = tot[P_LN2G:P_LN2G + 1, :]
        g_ref[S_LN2B:S_LN2B + 1, :] = tot[P_LN2B:P_LN2B + 1, :]
        delta, mn, vn = _adamw(w_ref[...], g_ref[...], m_ref[...], v_ref[...])
        d_ref[...] = delta
        mo_ref[...] = mn
        vo_ref[...] = vn
        loss_ref[...] = jnp.broadcast_to(tot[P_LOSS:P_LOSS + 1, 0:128], (8, 128))

    return pl.pallas_call(
        body,
        name="small_update",
        out_shape=tuple(jax.ShapeDtypeStruct((NSM, D), F32) for _ in range(4)) + (jax.ShapeDtypeStruct((8, 128), F32),),
        scratch_shapes=[pltpu.VMEM((NP, D), F32), pltpu.VMEM((8, D), F32), pltpu.VMEM((8, 128), F32)],
        compiler_params=pltpu.CompilerParams(vmem_limit_bytes=VMEM_LIMIT),
    )(pall, dsil, cctx, w, m, v, et_mat)


def _pad_rows(flat, mult=16):
    n = flat.shape[0]
    rows = -(-n // D)
    rows = -(-rows // mult) * mult
    return jnp.pad(flat, (0, rows * D - n)).reshape(rows, D)


def _by_cols(dw):
    r = dw.shape[0]
    return jnp.transpose(dw.reshape(r, NDEV, -1), (1, 0, 2))


def _from_cols(g):
    return jnp.transpose(g, (1, 0, 2)).reshape(g.shape[1], -1)


def _pool_constants(L):
    rows = L // GW
    t_r = jnp.arange(PT) // GW
    t_c = jnp.arange(PT) % GW
    fw, bw, ic = [], [], []
    pos_r = jnp.arange(L) // GW
    pos_c = jnp.arange(L) % GW
    for g, w in enumerate(WINDOWS):
        lo, hi = -(w // 2), w - w // 2 - 1
        n_ext = PT + 2 * _halo_tokens(g)
        e_r = jnp.arange(n_ext) // GW - w // 2
        e_c = jnp.arange(n_ext) % GW
        dr = e_r[None, :] - t_r[:, None]
        dc = e_c[None, :] - t_c[:, None]
        fw.append(((dr >= lo) & (dr <= hi) & (dc >= lo) & (dc <= hi)).astype(BF16))
        bw.append(((-dr >= lo) & (-dr <= hi) & (-dc >= lo) & (-dc <= hi)).astype(BF16))
        cr = jnp.minimum(pos_r + hi, rows - 1) - jnp.maximum(pos_r + lo, 0) + 1
        cc = jnp.minimum(pos_c + hi, GW - 1) - jnp.maximum(pos_c + lo, 0) + 1
        ic.append(1.0 / (cr * cc).astype(F32))
    icnt = jnp.pad(jnp.stack(ic, axis=1), ((0, 0), (0, 124)))
    return fw, bw, icnt


def _head_matrices():
    hp = jnp.arange(D) // HP
    e = (jnp.arange(128)[:, None] == hp[None, :]).astype(BF16)
    return jnp.concatenate([e, e], axis=0), jnp.concatenate([e.T, e.T], axis=1)


def _aligned_in_proj(w):
    zpad = jnp.zeros((D, 128 - NH), w.dtype)
    return jnp.concatenate([w[:, 0:2304], w[:, 2304:2320], zpad, w[:, 2320:2336], zpad, w[:, 2336:3360]], axis=1)


def _unaligned_in_proj(dw):
    return jnp.concatenate([dw[:, 0:2304], dw[:, 2304:2320], dw[:, 2432:2448], dw[:, 2560:3584]], axis=1)


def _row(v):
    return v.reshape(1, -1).astype(F32)


def _pad_lanes(v, width=D):
    v = v.reshape(1, -1)
    return jnp.pad(v, ((0, 0), (0, width - v.shape[1])))


def kernel(x, c, ctx, c_ctx, emb_ln_g, emb_ln_b, w_ada, b_ada, in_proj, conv_w, conv_b, dt_bias, a_log, d_skip, ssd_norm_g, pool_w, pool_scale, w_out, ln1_g, ln1_b, w_gate, w_up, w_down, ln2_g, ln2_b, loss_target, m_c_ctx, m_emb_ln_g, m_emb_ln_b, m_w_ada, m_b_ada, m_in_proj, m_conv_w, m_conv_b, m_dt_bias, m_a_log, m_d_skip, m_ssd_norm_g, m_pool_w, m_pool_scale, m_w_out, m_ln1_g, m_ln1_b, m_w_gate, m_w_up, m_w_down, m_ln2_g, m_ln2_b, v_c_ctx, v_emb_ln_g, v_emb_ln_b, v_w_ada, v_b_ada, v_in_proj, v_conv_w, v_conv_b, v_dt_bias, v_a_log, v_d_skip, v_ssd_norm_g, v_pool_w, v_pool_scale, v_w_out, v_ln1_g, v_ln1_b, v_w_gate, v_w_up, v_w_down, v_ln2_g, v_ln2_b):
    me = 4 * lax.axis_index("x") + 2 * lax.axis_index("y") + lax.axis_index("c")
    x0 = x[0]
    ctx0 = ctx[0]
    tgt = loss_target[0]
    L = x0.shape[0]
    LC = ctx0.shape[0]
    ncol_ada = w_ada.shape[2]

    small_in = jnp.concatenate([c.reshape(-1), conv_w.reshape(-1)])
    gb = jnp.concatenate([_row(emb_ln_g), _row(emb_ln_b), jnp.zeros((6, D), F32)], axis=0)
    xe, small_all, g_inp = _emb_ln_call(x0, gb, "emb_ln", [_pad_rows(small_in, 8), in_proj[0].astype(BF16)])
    (xe_c,) = _emb_ln_call(ctx0, gb, "emb_ln_ctx")
    c_all = small_all[:, 0, :]
    convw_all = small_all.reshape(NDEV, -1)[:, D:D + 5 * (DXBC // NDEV)].reshape(NDEV, 5, DXBC // NDEV)
    conv_w_full = _from_cols(convw_all)
    w_in = _aligned_in_proj(_from_cols(g_inp))
    late_shards = [pool_w[0].astype(BF16), w_out[0].astype(BF16), w_gate[0].astype(BF16), w_up[0].astype(BF16),
                   w_down[0].astype(BF16)]

    c_in = jnp.concatenate([c_all, c_ctx.reshape(1, D), jnp.zeros((7, D), F32)], axis=0)
    b_mine = lax.dynamic_slice(b_ada, (0, me * ncol_ada), (1, ncol_ada))
    silu_all, mod_mine = _mod_call(c_in, w_ada[0], b_mine)
    (mod_all,) = _exchange([mod_mine], "gather_mod", True)
    mod_all = _from_cols(mod_all)
    mod_me = lax.dynamic_slice(mod_all, (me, 0), (1, 6 * D)).reshape(6, D)
    mod_ctx = mod_all[8].reshape(6, D)

    tail = jnp.concatenate([
        _row(emb_ln_g), _row(emb_ln_b), _row(ln1_g), _row(ln1_b), _row(ln2_g), _row(ln2_b),
        _row(ssd_norm_g), _row(pool_scale), _row(jnp.repeat(d_skip.reshape(-1), HP)), jnp.zeros((1, D), F32)], axis=0)
    vec = jnp.concatenate([mod_me, tail], axis=0)
    vec_ctx = jnp.concatenate([mod_ctx, tail], axis=0)

    cw = jnp.concatenate([conv_w_full, conv_b.reshape(1, DXBC), jnp.zeros((2, DXBC), F32)], axis=0)
    par = jnp.concatenate([_pad_lanes(dt_bias[0, 0], 128), _pad_lanes(dt_bias[0, 1], 128),
                           _pad_lanes(a_log[0, 0], 128), _pad_lanes(a_log[0, 1], 128),
                           jnp.zeros((4, 128), F32)], axis=0)
    e_mat, et_mat = _head_matrices()
    pmat, pmat_t, icnt = _pool_constants(L)
    dskip_row = vec[V_DSK:V_DSK + 1]

    h1_c, _, xbcr_c, pre_c, xbc_c, dtr_c, _ = _f1_call(xe_c, vec_ctx, w_in, cw, "inproj_fwd_ctx")
    hzero = jnp.zeros((2, D, NS), F32)
    _, _, hpf_c, hpb_c, hfin_c = _ssd_fwd_call(xbc_c, dtr_c, hzero, par, e_mat, et_mat, "ssd_fwd_ctx")

    h1, z, xbcr, pre, xbc, dtr, up = _f1_call(xe, vec, w_in, cw, "inproj_fwd")
    yf, yb, hpf, hpb, _, g_pw, g_wo, g_wg, g_wu, g_wd = _ssd_fwd_call(xbc, dtr, hfin_c, par, e_mat, et_mat, "ssd_fwd",
                                                                       late_shards)
    pool_w_full = jnp.transpose(g_pw, (1, 0, 2, 3)).reshape(4, PG, PG)
    w_out_full = g_wo.reshape(2 * D, D)
    w_gate_full = _from_cols(g_wg)
    w_up_full = _from_cols(g_wu)
    w_down_full = g_wd.reshape(DFF, D)
    dpool, pun = _pool_fwd_call(up, pmat, icnt, pool_w_full)
    x1, mix, cat = _merge_call(yf, yb, xbc, z, pun, xe, vec, w_out_full)
    dpre2, gacc_f, gt_b, up_b = _ffn_fwd_call(x1, tgt, vec, w_gate_full, w_up_full, w_down_full)

    dh2p, dwg2, dwu2, dwd2 = _ffn_bwd_call(x1, dpre2, gt_b, up_b, vec, w_gate_full, w_up_full, w_down_full)
    nq = FFC // (DFF // NDEV)
    ffn_parts = [
        jnp.transpose(dwg2.reshape(-1, D, nq, DFF // NDEV), (0, 2, 1, 3)).reshape(NDEV, D, DFF // NDEV),
        jnp.transpose(dwu2.reshape(-1, D, nq, DFF // NDEV), (0, 2, 1, 3)).reshape(NDEV, D, DFF // NDEV),
        dwd2.reshape(NDEV, DFF // NDEV, D)]
    dxe_part, dy, dz, dd, dpw, gacc_m, dwo, gs_wg, gs_wu, gs_wd = _merge_bwd_call(
        dh2p, dpre2, xe, mix, cat, yf, yb, xbc, z, dpool, pun, vec, w_out_full, pool_w_full, ffn_parts)
    dup = _pool_bwd_call(dd, pmat_t, icnt)
    dxf, dxb, ddt0, ddt1, sacc, dh0 = _ssd_bwd_call(dy, xbc, dtr, hpf, hpb, hzero, par, e_mat, et_mat, dskip_row,
                                                     "ssd_bwd")
    zeros_c = jnp.zeros((LC, D), F32)
    dxf_c, dxb_c, ddt0_c, ddt1_c, sacc_c, _ = _ssd_bwd_call(zeros_c, xbc_c, dtr_c, hpf_c, hpb_c, dh0, par, e_mat, et_mat,
                                                            jnp.zeros((1, D), F32), "ssd_bwd_ctx")
    dprec_c, cacc_c = _conv_bwd_call(dxf_c, dxb_c, pre_c, xbcr_c, jnp.zeros((8, DXBC), F32), "conv_bwd_ctx")
    _, gacc_c, dwin_c = _inproj_bwd_call(dprec_c, cw, zeros_c, ddt0_c, ddt1_c, zeros_c, h1_c, zeros_c, ctx0, vec_ctx,
                                         w_in, jnp.zeros((D, WIN), F32), "inproj_bwd_ctx")
    mix_parts = [dwo.reshape(NDEV, 2 * D // NDEV, D),
                 jnp.transpose(dpw.reshape(4, NDEV, PG // NDEV, PG), (1, 0, 2, 3)).reshape(NDEV, 4 * PG // NDEV, PG)]
    dprec, cacc, gs_wo, gs_pw = _conv_bwd_call(dxf, dxb, pre, xbcr, cacc_c, "conv_bwd", mix_parts)
    grad_x, gacc_i, dwin = _inproj_bwd_call(dprec, cw, dz, ddt0, ddt1, dup, h1, dxe_part, x0, vec, w_in, dwin_c,
                                            "inproj_bwd")

    gsum = gacc_f + gacc_m + gacc_i
    sa = sacc + sacc_c
    dtb_row = _pad_lanes(jnp.concatenate([sa[0, 0, 0:NH], sa[1, 0, 0:NH]]))
    alog_row = _pad_lanes(jnp.concatenate([sa[0, 1, 0:NH], sa[1, 1, 0:NH]]))
    convb_rows = jnp.pad(cacc[5], (0, 2 * D - DXBC)).reshape(2, D)
    pack = jnp.concatenate([
        gsum[V_SH1:V_G2 + 1],
        gacc_c[V_SH1:V_SC1 + 1],
        gsum[V_EMBG:V_EMBB + 1] + gacc_c[V_EMBG:V_EMBB + 1],
        gsum[V_LN1G:V_LN2B + 1],
        gsum[V_SSDG:V_DSK + 1],
        convb_rows, dtb_row, alog_row,
        gsum[V_LOSS:V_LOSS + 1],
        jnp.zeros((NP - 22, D), F32)], axis=0)
    (pall,) = _exchange([pack], "gather_small_grads", True)

    dm_flat = pall[:, 0:8, :].reshape(NDEV, 8 * D)
    dm_ex = lax.dynamic_slice(dm_flat, (0, me * ncol_ada), (NDEV, ncol_ada))
    dmc_full = jnp.concatenate([dm_flat[:, 6 * D:8 * D], jnp.zeros((NDEV, 4 * D), F32)], axis=1)
    dm_ctx = lax.dynamic_slice(dmc_full, (0, me * ncol_ada), (NDEV, ncol_ada))
    g_wada, d_wada, nm_wada, nv_wada, dsil = _wada_call(dm_ex, dm_ctx, silu_all, w_ada[0], m_w_ada[0], v_w_ada[0])
    (dsil_all,) = _exchange([dsil], "gather_dsilu", True)

    def small_pack(cc, eg, eb, ba, cb_, dtb, al, dsk, sg, ps, l1g, l1b, l2g, l2b):
        return jnp.concatenate([
            _row(cc), _row(eg), _row(eb), ba.reshape(6, D), jnp.pad(cb_.reshape(-1), (0, 2 * D - DXBC)).reshape(2, D),
            _pad_lanes(dtb.reshape(-1)), _pad_lanes(al.reshape(-1)), _pad_lanes(dsk.reshape(-1)),
            _row(sg), _row(ps), _row(l1g), _row(l1b), _row(l2g), _row(l2b), jnp.zeros((NSM - 20, D), F32)], axis=0)

    sw = small_pack(c_ctx, emb_ln_g, emb_ln_b, b_ada, conv_b, dt_bias, a_log, d_skip, ssd_norm_g, pool_scale,
                    ln1_g, ln1_b, ln2_g, ln2_b)
    sm = small_pack(m_c_ctx, m_emb_ln_g, m_emb_ln_b, m_b_ada, m_conv_b, m_dt_bias, m_a_log, m_d_skip, m_ssd_norm_g,
                    m_pool_scale, m_ln1_g, m_ln1_b, m_ln2_g, m_ln2_b)
    sv = small_pack(v_c_ctx, v_emb_ln_g, v_emb_ln_b, v_b_ada, v_conv_b, v_dt_bias, v_a_log, v_d_skip, v_ssd_norm_g,
                    v_pool_scale, v_ln1_g, v_ln1_b, v_ln2_g, v_ln2_b)
    s_g, s_d, s_m, s_v, loss8 = _small_update_call(pall, dsil_all, _row(c_ctx), sw, sm, sv, et_mat)

    def small_unpack(t):
        return (t[S_CCTX], t[S_EMBG], t[S_EMBB], t[S_BADA:S_BADA + 6].reshape(1, 6 * D),
                t[S_CONVB:S_CONVB + 2].reshape(-1)[:DXBC].reshape(1, DXBC),
                t[S_DTB, 0:2 * NH].reshape(1, 2, NH), t[S_ALOG, 0:2 * NH].reshape(1, 2, NH), t[S_DSK, 0:NH].reshape(1, NH),
                t[S_SSDG].reshape(1, D), t[S_PSC].reshape(1, D), t[S_LN1G].reshape(1, D), t[S_LN1B].reshape(1, D),
                t[S_LN2G].reshape(1, D), t[S_LN2B].reshape(1, D))

    gs_inp, gs_cw = _exchange([_by_cols(_unaligned_in_proj(dwin)).astype(BF16), _by_cols(cacc[0:5])],
                              "exchange_last_grads", False)

    pshape = (4 * PG // NDEV, PG)
    u_inp = _adamw_shard_call(gs_inp, in_proj[0], m_in_proj[0], v_in_proj[0], 256, "adamw_in_proj")
    u_cw = _adamw_shard_call(gs_cw, conv_w[0], m_conv_w[0], v_conv_w[0], 5, "adamw_conv_w")
    u_pw = _adamw_shard_call(gs_pw, pool_w[0].reshape(pshape), m_pool_w[0].reshape(pshape), v_pool_w[0].reshape(pshape),
                             pshape[0], "adamw_pool_w")
    u_wo = _adamw_shard_call(gs_wo, w_out[0], m_w_out[0], v_w_out[0], 64, "adamw_w_out")
    u_wg = _adamw_shard_call(gs_wg, w_gate[0], m_w_gate[0], v_w_gate[0], 256, "adamw_w_gate")
    u_wu = _adamw_shard_call(gs_wu, w_up[0], m_w_up[0], v_w_up[0], 256, "adamw_w_up")
    u_wd = _adamw_shard_call(gs_wd, w_down[0], m_w_down[0], v_w_down[0], 88, "adamw_w_down")

    def assemble(k, small, wada):
        (cc, eg, eb, ba, cb_, dtb, al, dsk, sg, ps, l1g, l1b, l2g, l2b) = small_unpack(small)
        pw = u_pw[k].reshape(1, 4, PG // NDEV, PG)
        return (cc, eg, eb, wada[None], ba, u_inp[k][None], u_cw[k][None], cb_, dtb, al, dsk, sg, pw, ps,
                u_wo[k][None], l1g, l1b, u_wg[k][None], u_wu[k][None], u_wd[k][None], l2g, l2b)

    loss = loss8[0, 0]
    return (loss, grad_x[None], *assemble(0, s_g, g_wada), *assemble(1, s_d, d_wada),
            *assemble(2, s_m, nm_wada), *assemble(3, s_v, nv_wada))
```

```python
import functools
import math

import jax
import jax.numpy as jnp
from jax import lax
from jax.experimental import pallas as pl
from jax.experimental.pallas import tpu as pltpu

F32 = jnp.float32
BF16 = jnp.bfloat16
HI = lax.Precision.HIGHEST

NDEV = 8
D = 1024
NH = 16
HP = 64
NS = 128
Q = 128
DXBC = 1280
DFF = 2816
FFC = 1408
GW = 64
PR = 8
PT = PR * GW
WINDOWS = (2, 4, 8, 16)
PG = 256
ICW = 8
DIN = 3360
WIN = 3584
ALPHA = 2.0 ** 0.25
LN_EPS = 1e-5
TM = 256

ADAM_LR = 0.001
ADAM_B1 = 0.9
ADAM_B2 = 0.999
ADAM_EPS = 1e-08
ADAM_WD = 0.01
ADAM_STEP = 10

V_SH1, V_SC1, V_G1, V_SH2, V_SC2, V_G2 = 0, 1, 2, 3, 4, 5
V_EMBG, V_EMBB, V_LN1G, V_LN1B, V_LN2G, V_LN2B = 6, 7, 8, 9, 10, 11
V_SSDG, V_PSC, V_DSK, V_LOSS = 12, 13, 14, 15
NV = 16

VMEM_LIMIT = 60 * 1024 * 1024


def _cp(ndim=1):
    return pltpu.CompilerParams(dimension_semantics=("arbitrary",) * ndim, vmem_limit_bytes=VMEM_LIMIT)


def _dot(a, b, precision=None):
    return jnp.dot(a, b, preferred_element_type=F32, precision=precision)


def _dot_nt(a, b):
    return lax.dot_general(a, b, (((1,), (1,)), ((), ())), preferred_element_type=F32)


def _dot_tn(a, b, precision=None):
    return lax.dot_general(a, b, (((0,), (0,)), ((), ())), preferred_element_type=F32, precision=precision)


def _split2(x):
    hi = x.astype(BF16)
    return hi, (x - hi.astype(F32)).astype(BF16)


def _split_dot(m, x):
    hi, lo = _split2(x)
    return _dot(m, hi) + _dot(m, lo)


def _dot_split(x, m):
    hi, lo = _split2(x)
    return _dot(hi, m) + _dot(lo, m)


def _sigmoid(x):
    return 1.0 / (1.0 + jnp.exp(-x))


def _softplus(x):
    return jnp.maximum(x, 0.0) + jnp.log(1.0 + jnp.exp(-jnp.abs(x)))


def _ln(x, g, b):
    mu = jnp.mean(x, axis=-1, keepdims=True)
    xc = x - mu
    var = jnp.mean(xc * xc, axis=-1, keepdims=True)
    rstd = lax.rsqrt(var + LN_EPS)
    n = xc * rstd
    return n * g + b, n, rstd


def _ln_bwd(dy, n, rstd, g):
    dn = dy * g
    return rstd * (dn - jnp.mean(dn, axis=-1, keepdims=True) - n * jnp.mean(dn * n, axis=-1, keepdims=True))


def _rowsum(x):
    return jnp.sum(x, axis=0, keepdims=True)


def _resident(shape):
    nd = len(shape)
    return pl.BlockSpec(shape, lambda *_: (0,) * nd, pipeline_mode=pl.Buffered(1))


def _const_out(shape):
    nd = len(shape)
    return pl.BlockSpec(shape, lambda *_: (0,) * nd)


def _tiles(tm, width):
    return pl.BlockSpec((tm, width), lambda i: (i, 0))


def _halo_specs(tm, width, n_rows):
    r = tm // 8
    last = n_rows // 8 - 1
    prev = pl.BlockSpec((8, width), lambda i: (jnp.maximum(i * r - 1, 0), 0))
    nxt = pl.BlockSpec((8, width), lambda i: (jnp.minimum((i + 1) * r, last), 0))
    return prev, nxt


def _acc_tn(acc_ref, a, b, chunk=512):
    n = b.shape[1]
    for c0 in range(0, n, chunk):
        c1 = min(c0 + chunk, n)
        acc_ref[:, c0:c1] += _dot_tn(a, b[:, c0:c1])


def _my_coords():
    return lax.axis_index("x"), lax.axis_index("y"), lax.axis_index("c")


def _peer(k, mx, my, mc):
    kx, ky, kc = (k >> 2) & 1, (k >> 1) & 1, k & 1
    px = 1 - mx if kx else mx
    py = 1 - my if ky else my
    pc = 1 - mc if kc else mc
    return px, py, pc


class _Exchange:
    def __init__(self, srcs, dsts, send_sems, recv_sems, local_sems, gather):
        self.srcs, self.dsts, self.gather = srcs, dsts, gather
        self.send_sems, self.recv_sems, self.local_sems = send_sems, recv_sems, local_sems

    def _copies(self, outgoing):
        mx, my, mc = _my_coords()
        me = 4 * mx + 2 * my + mc
        local, remote = [], []
        for t, (src, dst) in enumerate(zip(self.srcs, self.dsts)):
            local.append(pltpu.make_async_copy(src if self.gather else src.at[me], dst.at[me], self.local_sems.at[t]))
            for k in range(1, NDEV):
                px, py, pc = _peer(k, mx, my, mc)
                pid = 4 * px + 2 * py + pc
                remote.append(pltpu.make_async_remote_copy(
                    src_ref=src if self.gather else src.at[pid],
                    dst_ref=dst.at[me] if outgoing else dst.at[pid],
                    send_sem=self.send_sems.at[t, k - 1],
                    recv_sem=self.recv_sems.at[t, k - 1],
                    device_id=(px, py, pc),
                    device_id_type=pl.DeviceIdType.MESH,
                ))
        return local, remote

    def start(self):
        local, remote = self._copies(True)
        for cp in local + remote:
            cp.start()

    def wait(self):
        local, sends = self._copies(True)
        _, recvs = self._copies(False)
        for cp in recvs:
            cp.wait_recv()
        for cp in sends:
            cp.wait_send()
        for cp in local:
            cp.wait()


class _ChipGather:
    def __init__(self, srcs, dsts, send_sems, recv_sems, local_sems):
        self.srcs, self.dsts = srcs, dsts
        self.send_sems, self.recv_sems, self.local_sems = send_sems, recv_sems, local_sems

    def _places(self):
        x, y, c = _my_coords()
        return (x, y, c), (x, y, 1 - c), [(1 - x, y), (x, 1 - y), (1 - x, 1 - y)]

    def _copy(self, t, k, block, to, own=False):
        slot = self.dsts[t].at[4 * block[0] + 2 * block[1] + block[2]]
        return pltpu.make_async_remote_copy(
            src_ref=self.srcs[t] if own else slot, dst_ref=slot,
            send_sem=self.send_sems.at[t, k], recv_sem=self.recv_sems.at[t, k],
            device_id=to, device_id_type=pl.DeviceIdType.MESH)

    def _local(self, t, me):
        return pltpu.make_async_copy(self.srcs[t], self.dsts[t].at[4 * me[0] + 2 * me[1] + me[2]], self.local_sems.at[t])

    def start(self):
        me, sib, chips = self._places()
        for t in range(len(self.srcs)):
            self._local(t, me).start()
            self._copy(t, 0, me, sib, own=True).start()
            for j, chip in enumerate(chips):
                self._copy(t, 1 + j, me, (*chip, me[2]), own=True).start()

    def wait(self):
        me, sib, chips = self._places()
        n = len(self.srcs)
        for t in range(n):
            for j, chip in enumerate(chips):
                self._copy(t, 1 + j, (*chip, me[2]), me).wait_recv()
                self._copy(t, 4 + j, (*chip, me[2]), sib).start()
        for t in range(n):
            self._copy(t, 0, sib, me).wait_recv()
            for j, chip in enumerate(chips):
                self._copy(t, 4 + j, (*chip, sib[2]), me).wait_recv()
            self._copy(t, 0, me, sib, own=True).wait_send()
            for j, chip in enumerate(chips):
                self._copy(t, 1 + j, me, (*chip, me[2]), own=True).wait_send()
                self._copy(t, 4 + j, (*chip, me[2]), sib).wait_send()
            self._local(t, me).wait()


def _exchange_sems(n):
    return [pltpu.SemaphoreType.DMA((n, NDEV - 1)), pltpu.SemaphoreType.DMA((n, NDEV - 1)), pltpu.SemaphoreType.DMA((n,))]


def _exchange_out_shapes(xs, gather):
    return [jax.ShapeDtypeStruct(x.shape if not gather else (NDEV,) + x.shape, x.dtype) for x in xs]


def _exchange(xs, name, gather):
    n = len(xs)

    def body(*refs):
        ex = _Exchange(refs[:n], refs[n:2 * n], *refs[2 * n:], gather)
        ex.start()
        ex.wait()

    any_spec = pl.BlockSpec(memory_space=pl.ANY)
    return pl.pallas_call(
        body,
        name=name,
        out_shape=_exchange_out_shapes(xs, gather),
        in_specs=[any_spec] * n,
        out_specs=[any_spec] * n,
        scratch_shapes=_exchange_sems(n),
    )(*xs)


def _mod_call(c_all, w_ada, b_ada):
    ncol = w_ada.shape[1]

    def body(c_ref, w_ref, b_ref, silu_ref, mod_ref):
        cv = c_ref[...]
        s = cv * _sigmoid(cv)
        silu_ref[...] = s
        mod_ref[...] = _dot(s.astype(BF16), w_ref[...].astype(BF16)) + b_ref[...]

    return pl.pallas_call(
        body,
        name="mod_fwd",
        out_shape=(jax.ShapeDtypeStruct((16, D), F32), jax.ShapeDtypeStruct((16, ncol), F32)),
    )(c_all, w_ada, b_ada)


def _emb_ln_call(x0, gb, name, gathered=()):
    L = x0.shape[0]
    nt = L // TM
    ng = len(gathered)

    def body(*refs):
        x_ref, gb_ref = refs[:2]
        xe_ref = refs[2 + ng]
        i = pl.program_id(0)
        if ng:
            ex = _ChipGather(refs[2:2 + ng], refs[3 + ng:3 + 2 * ng], *refs[3 + 2 * ng:])

            @pl.when(i == 0)
            def _():
                ex.start()

        xe_ref[...] = _ln(x_ref[...], gb_ref[0:1, :], gb_ref[1:2, :])[0]
        if ng:
            @pl.when(i == nt - 1)
            def _():
                ex.wait()

    any_spec = pl.BlockSpec(memory_space=pl.ANY)
    return pl.pallas_call(
        body,
        name=name,
        grid=(nt,),
        in_specs=[_tiles(TM, D), _resident((8, D))] + [any_spec] * ng,
        out_specs=[_tiles(TM, D)] + [any_spec] * ng,
        out_shape=[jax.ShapeDtypeStruct((L, D), F32)] + _exchange_out_shapes(gathered, True),
        scratch_shapes=_exchange_sems(ng) if ng else [],
        compiler_params=_cp(),
    )(x0, gb, *gathered)


def _f1_call(xe, vec, w_in, cw, name):
    L = xe.shape[0]
    prev, nxt = _halo_specs(TM, D, L)

    def body(xe_ref, prev_ref, next_ref, vec_ref, w_ref, cw_ref, h1_ref, z_ref, xbcr_ref, pre_ref, xbc_ref, dt_ref,
             up_ref, ext):
        i = pl.program_id(0)
        n = pl.num_programs(0)
        def modulated(v):
            return (v * (1.0 + vec_ref[V_SC1:V_SC1 + 1, :]) + vec_ref[V_SH1:V_SH1 + 1, :]).astype(BF16)

        rows = jnp.concatenate([prev_ref[...], xe_ref[...], next_ref[...]], axis=0)
        proj = _dot(modulated(rows), w_ref[...])
        h1_ref[...] = modulated(xe_ref[...])
        z_ref[...] = proj[8:8 + TM, 0:1024]
        xbcr_ref[...] = proj[8:8 + TM, 1024:2304]
        dt_ref[...] = proj[8:8 + TM, 2304:2560]
        up_ref[...] = proj[8:8 + TM, 2560:3584]
        ext[...] = proj[:, 1024:2304]
        ext[0:8, :] = jnp.where(i > 0, ext[0:8, :], 0.0)
        ext[8 + TM:16 + TM, :] = jnp.where(i < n - 1, ext[8 + TM:16 + TM, :], 0.0)
        pre = _conv_pre(ext, cw_ref, TM)
        pre_ref[...] = pre
        xbc_ref[...] = pre * _sigmoid(pre)

    return pl.pallas_call(
        body,
        name=name,
        grid=(L // TM,),
        in_specs=[_tiles(TM, D), prev, nxt, _resident((NV, D)), _resident((D, WIN)), _resident((8, DXBC))],
        out_specs=[_tiles(TM, D), _tiles(TM, D), _tiles(TM, DXBC), _tiles(TM, DXBC), _tiles(TM, DXBC), _tiles(TM, 256),
                   _tiles(TM, D)],
        out_shape=(
            jax.ShapeDtypeStruct((L, D), BF16),
            jax.ShapeDtypeStruct((L, D), F32),
            jax.ShapeDtypeStruct((L, DXBC), F32),
            jax.ShapeDtypeStruct((L, DXBC), F32),
            jax.ShapeDtypeStruct((L, DXBC), F32),
            jax.ShapeDtypeStruct((L, 256), F32),
            jax.ShapeDtypeStruct((L, D), F32),
        ),
        scratch_shapes=[pltpu.VMEM((TM + 16, DXBC), F32)],
        compiler_params=_cp(),
    )(xe, xe, xe, vec, w_in, cw)


def _extended(ext, cur_ref, prev_ref, next_ref):
    i = pl.program_id(0)
    n = pl.num_programs(0)
    tm = cur_ref.shape[0]
    ext[0:8, :] = jnp.where(i > 0, prev_ref[...], 0.0)
    ext[8:8 + tm, :] = cur_ref[...]
    ext[8 + tm:16 + tm, :] = jnp.where(i < n - 1, next_ref[...], 0.0)
    return ext


def _shifted(ext, offset, tm):
    return ext[8 + offset:8 + offset + tm, :]


def _conv_pre(ext, cw_ref, tm):
    acc = cw_ref[5:6, :] + cw_ref[0:1, :] * _shifted(ext, -2, tm)
    for k in range(1, 5):
        acc = acc + cw_ref[k:k + 1, :] * _shifted(ext, k - 2, tm)
    return acc


def _ssd_common(d, dtr, par_ref):
    lane = lax.broadcasted_iota(jnp.int32, (1, 128), 1)
    hmask = lane < NH
    bias = par_ref[d:d + 1, :]
    alog = par_ref[2 + d:3 + d, :]
    aneg = jnp.where(hmask, -jnp.exp(alog), 0.0)
    pre = dtr + bias
    dt = jnp.where(hmask, _softplus(pre), 0.0)
    a = dt * aneg
    row = lax.broadcasted_iota(jnp.int32, (Q, Q), 0)
    col = lax.broadcasted_iota(jnp.int32, (Q, Q), 1)
    maskf = ((row >= col) if d == 0 else (row <= col)).astype(F32)
    mask_t = ((row <= col) if d == 0 else (row >= col)).astype(F32)
    A = _tri_dot(maskf, a)
    atot = _rowsum(a)
    return dict(hmask=hmask, aneg=aneg, pre=pre, dt=dt, a=a, maskf=maskf, mask_t=mask_t, A=A, AT=A.T, dtT=dt.T,
                atot=atot, lane=lane)


def _tri_dot(mask, x):
    t1 = x.astype(BF16)
    r1 = x - t1.astype(F32)
    t2 = r1.astype(BF16)
    t3 = (r1 - t2.astype(F32)).astype(BF16)
    mb = mask.astype(BF16)
    return _dot(jnp.concatenate([mb, mb, mb], axis=1), jnp.concatenate([t1, t2, t3], axis=0))


def _column(v, lane, h):
    return jnp.sum(jnp.where(lane == h, v, 0.0), axis=1, keepdims=True)


def _head_expand(e_ref, v):
    hi, lo = _split2(v)
    return _dot(jnp.concatenate([hi, lo], axis=1), e_ref[...])


def _head_sum(et_ref, v):
    return _dot(v.astype(BF16), et_ref[:, 0:128])


def _state_decay(atot, lane):
    ea = jnp.exp(atot)
    return jnp.concatenate([jnp.broadcast_to(_column(ea, lane, h), (HP, NS)) for h in range(NH)], axis=0)


def _ssd_fwd_call(xbc, dtr, h0, par, e_mat, et_mat, name, gathered=()):
    L = xbc.shape[0]
    nc = L // Q
    ng = len(gathered)

    def body(*refs):
        xbc_refs, dtr_refs = refs[0:2], refs[2:4]
        h0_ref, par_ref, e_ref, et_ref = refs[4:8]
        y_refs, hp_refs = refs[8 + ng:10 + ng], refs[10 + ng:12 + ng]
        hf_ref = refs[12 + ng]
        hs, AT, dtT = refs[13 + 2 * ng:16 + 2 * ng]
        s = pl.program_id(0)
        if ng:
            ex = _Exchange(refs[8:8 + ng], refs[13 + ng:13 + 2 * ng], *refs[16 + 2 * ng:], True)

        @pl.when(s == 0)
        def _():
            if ng:
                ex.start()
            hs[...] = h0_ref[...]

        for d in range(2):
            xbc_ref, y_ref = xbc_refs[d], y_refs[d]
            q = _ssd_common(d, dtr_refs[d][...], par_ref)
            A, maskf, lane = q["A"], q["maskf"], q["lane"]
            AT[d] = q["AT"]
            dtT[d] = q["dtT"]
            hprev = hs[d]
            hp_refs[d][0] = hprev
            bb = xbc_ref[:, 1024:1152].astype(BF16)
            cb = xbc_ref[:, 1152:1280].astype(BF16)
            g = _dot_nt(cb, bb)
            yoff = _dot_nt(cb, hprev.astype(BF16)) * _head_expand(e_ref, jnp.exp(A))
            for k in range(NH // 2):
                ks = slice(128 * k, 128 * k + 128)
                xp = xbc_ref[:, ks]
                scs, xhs = [], []
                for half in range(2):
                    h = 2 * k + half
                    seg = _column(A, lane, h) - AT[d, h:h + 1, :]
                    lm = jnp.exp(jnp.minimum(seg, 0.0)) * maskf
                    scs.append((g * lm * dtT[d, h:h + 1, :]).astype(BF16))
                    inhead = (lane >= HP) if half else (lane < HP)
                    xhs.append(jnp.where(inhead, xp, 0.0).astype(BF16))
                y_ref[:, ks] = yoff[:, ks] + _dot(jnp.concatenate(scs, axis=1), jnp.concatenate(xhs, axis=0))
            wend = jnp.exp(q["atot"] - A) * q["dt"]
            xw = (xbc_ref[:, 0:1024] * _head_expand(e_ref, wend)).astype(BF16)
            hnew = hprev * _state_decay(q["atot"], lane) + _dot_tn(xw, bb)
            hs[d] = hnew
            hf_ref[d] = hnew
        if ng:
            @pl.when(s == nc - 1)
            def _():
                ex.wait()

    any_spec = pl.BlockSpec(memory_space=pl.ANY)
    return pl.pallas_call(
        body,
        name=name,
        grid=(nc,),
        in_specs=[
            pl.BlockSpec((Q, DXBC), lambda s: (s, 0)),
            pl.BlockSpec((Q, DXBC), lambda s: (nc - 1 - s, 0)),
            pl.BlockSpec((Q, 128), lambda s: (s, 0)),
            pl.BlockSpec((Q, 128), lambda s: (nc - 1 - s, 1)),
            _resident((2, D, NS)),
            _resident((8, 128)),
            _resident((256, D)),
            _resident((D, 256)),
        ] + [any_spec] * ng,
        out_specs=[
            pl.BlockSpec((Q, D), lambda s: (s, 0)),
            pl.BlockSpec((Q, D), lambda s: (nc - 1 - s, 0)),
            pl.BlockSpec((1, D, NS), lambda s: (s, 0, 0)),
            pl.BlockSpec((1, D, NS), lambda s: (nc - 1 - s, 0, 0)),
            _const_out((2, D, NS)),
        ] + [any_spec] * ng,
        out_shape=[
            jax.ShapeDtypeStruct((L, D), F32),
            jax.ShapeDtypeStruct((L, D), F32),
            jax.ShapeDtypeStruct((nc, D, NS), F32),
            jax.ShapeDtypeStruct((nc, D, NS), F32),
            jax.ShapeDtypeStruct((2, D, NS), F32),
        ] + _exchange_out_shapes(gathered, True),
        scratch_shapes=[pltpu.VMEM((2, D, NS), F32), pltpu.VMEM((2, 128, Q), F32), pltpu.VMEM((2, 128, Q), F32)]
        + (_exchange_sems(ng) if ng else []),
        compiler_params=_cp(),
    )(xbc, xbc, dtr, dtr, h0, par, e_mat, et_mat, *gathered)


def _halo_tokens(g):
    return (WINDOWS[g] // 2) * GW


def _pool_specs(n_tiles):
    cur = pl.BlockSpec((PT, D), lambda i: (i, 0))
    prev = pl.BlockSpec((PT, D), lambda i: (jnp.maximum(i - 1, 0), 0))
    nxt = pl.BlockSpec((PT, D), lambda i: (jnp.minimum(i + 1, n_tiles - 1), 0))
    return cur, prev, nxt


def _pool_fwd_call(up, pmat, icnt, pool_w):
    L = up.shape[0]
    nt = L // PT
    cur, prev, nxt = _pool_specs(nt)

    def body(cur_ref, prev_ref, next_ref, m0_ref, m1_ref, m2_ref, m3_ref, ic_ref, pw_ref, d_ref, pun_ref):
        i = pl.program_id(0)
        n = pl.num_programs(0)
        lane = lax.broadcasted_iota(jnp.int32, (1, ICW), 1)
        icv = ic_ref[...]
        for g, m_ref in enumerate((m0_ref, m1_ref, m2_ref, m3_ref)):
            gs = slice(PG * g, PG * g + PG)
            halo = _halo_tokens(g)
            top = jnp.where(i > 0, prev_ref[PT - halo:PT, gs], 0.0)
            bot = jnp.where(i < n - 1, next_ref[0:halo, gs], 0.0)
            mid = cur_ref[:, gs]
            box = _split_dot(m_ref[...], jnp.concatenate([top, mid, bot], axis=0))
            dg = (box * _column(icv, lane, g) - mid).astype(BF16)
            d_ref[:, gs] = dg
            pun_ref[:, gs] = _dot(dg, pw_ref[g])

    return pl.pallas_call(
        body,
        name="pool_fwd",
        grid=(nt,),
        in_specs=[cur, prev, nxt] + [_resident(m.shape) for m in pmat] + [_tiles(PT, ICW), _resident((4, PG, PG))],
        out_specs=[_tiles(PT, D), _tiles(PT, D)],
        out_shape=(jax.ShapeDtypeStruct((L, D), BF16), jax.ShapeDtypeStruct((L, D), F32)),
        compiler_params=_cp(),
    )(up, up, up, *pmat, icnt, pool_w)


def _gated(yf_ref, yb_ref, xs, z, vec_ref):
    ym = yf_ref[...] + yb_ref[...] + vec_ref[V_DSK:V_DSK + 1, :] * xs
    sz = _sigmoid(z)
    gated = ym * (z * sz)
    r = lax.rsqrt(jnp.mean(gated * gated, axis=-1, keepdims=True) + LN_EPS)
    return ym, sz, gated, r


def _merge_call(yf, yb, xbc, z, pun, xe, vec, w_out):
    L = z.shape[0]

    def body(yf_ref, yb_ref, xs_ref, z_ref, pun_ref, xe_ref, vec_ref, w_ref, x1_ref, mix_ref, cat_ref):
        _, _, gated, r = _gated(yf_ref, yb_ref, xs_ref[...], z_ref[...], vec_ref)
        yn = gated * r * vec_ref[V_SSDG:V_SSDG + 1, :]
        p = pun_ref[...] * vec_ref[V_PSC:V_PSC + 1, :]
        cat = jnp.concatenate([yn, p], axis=1).astype(BF16)
        mix = _dot(cat, w_ref[...])
        pre1 = ALPHA * xe_ref[...] + vec_ref[V_G1:V_G1 + 1, :] * mix
        x1, _, _ = _ln(pre1, vec_ref[V_LN1G:V_LN1G + 1, :], vec_ref[V_LN1B:V_LN1B + 1, :])
        x1_ref[...] = x1
        mix_ref[...] = mix
        cat_ref[...] = cat

    return pl.pallas_call(
        body,
        name="merge_fwd",
        grid=(L // TM,),
        in_specs=[
            _tiles(TM, D), _tiles(TM, D), _tiles(TM, D), _tiles(TM, D), _tiles(TM, D), _tiles(TM, D),
            _resident((NV, D)), _resident((2 * D, D)),
        ],
        out_specs=[_tiles(TM, D), _tiles(TM, D), _tiles(TM, 2 * D)],
        out_shape=(
            jax.ShapeDtypeStruct((L, D), F32),
            jax.ShapeDtypeStruct((L, D), F32),
            jax.ShapeDtypeStruct((L, 2 * D), BF16),
        ),
        compiler_params=_cp(),
    )(yf, yb, xbc, z, pun, xe, vec, w_out)


def _ffn_fwd_call(x1, tgt, vec, w_gate, w_up, w_down):
    L = x1.shape[0]

    def body(x1_ref, tgt_ref, vec_ref, wg_ref, wu_ref, wd_ref, dpre_ref, gacc_ref, gt_ref, up_ref):
        @pl.when(pl.program_id(0) == 0)
        def _():
            gacc_ref[...] = jnp.zeros_like(gacc_ref)

        x1 = x1_ref[...]
        h2 = (x1 * (1.0 + vec_ref[V_SC2:V_SC2 + 1, :]) + vec_ref[V_SH2:V_SH2 + 1, :]).astype(BF16)
        gt = _dot(h2, wg_ref[...])
        up = _dot(h2, wu_ref[...])
        gt_ref[...] = gt.astype(BF16)
        up_ref[...] = up.astype(BF16)
        f = (gt * _sigmoid(gt) * up).astype(BF16)
        ffn = _dot(f, wd_ref[...])
        g2 = vec_ref[V_G2:V_G2 + 1, :]
        lng = vec_ref[V_LN2G:V_LN2G + 1, :]
        x2, n2, rstd2 = _ln(ALPHA * x1 + g2 * ffn, lng, vec_ref[V_LN2B:V_LN2B + 1, :])
        diff = x2 - tgt_ref[...]
        dx2 = diff * (1.0 / D)
        dpre2 = _ln_bwd(dx2, n2, rstd2, lng)
        dpre_ref[...] = dpre2
        gacc_ref[V_LN2G:V_LN2G + 1, :] += _rowsum(dx2 * n2)
        gacc_ref[V_LN2B:V_LN2B + 1, :] += _rowsum(dx2)
        gacc_ref[V_G2:V_G2 + 1, :] += _rowsum(dpre2 * ffn)
        gacc_ref[V_LOSS:V_LOSS + 1, :] += jnp.sum(diff * diff) * (0.5 / D)

    return pl.pallas_call(
        body,
        name="ffn_fwd",
        grid=(L // TM,),
        in_specs=[_tiles(TM, D), _tiles(TM, D), _resident((NV, D)),
                  _resident((D, DFF)), _resident((D, DFF)), _resident((DFF, D))],
        out_specs=[_tiles(TM, D), _const_out((NV, D)), _tiles(TM, DFF), _tiles(TM, DFF)],
        out_shape=(jax.ShapeDtypeStruct((L, D), F32), jax.ShapeDtypeStruct((NV, D), F32),
                   jax.ShapeDtypeStruct((L, DFF), BF16), jax.ShapeDtypeStruct((L, DFF), BF16)),
        compiler_params=_cp(),
    )(x1, tgt, vec, w_gate, w_up, w_down)


def _ffn_bwd_call(x1, dpre2, gt_b, up_b, vec, w_gate, w_up, w_down):
    L = x1.shape[0]
    nt = L // TM
    nj = DFF // FFC

    def body(x1_ref, dpre_ref, gt_ref, up_ref, vec_ref, wg_ref, wu_ref, wd_ref, dh2_ref, dwg_ref, dwu_ref, dwd_ref,
             ag, au, ad):
        j = pl.program_id(0)
        i = pl.program_id(1)

        @pl.when(i == 0)
        def _():
            ag[...] = jnp.zeros_like(ag)
            au[...] = jnp.zeros_like(au)
            ad[...] = jnp.zeros_like(ad)

        h2 = (x1_ref[...] * (1.0 + vec_ref[V_SC2:V_SC2 + 1, :]) + vec_ref[V_SH2:V_SH2 + 1, :]).astype(BF16)
        gt = gt_ref[...].astype(F32)
        up = up_ref[...].astype(F32)
        sg = _sigmoid(gt)
        sl = gt * sg
        f = (sl * up).astype(BF16)
        dffn = (vec_ref[V_G2:V_G2 + 1, :] * dpre_ref[...]).astype(BF16)
        df = _dot_nt(dffn, wd_ref[...])
        dgt = (df * up * (sg * (1.0 + gt * (1.0 - sg)))).astype(BF16)
        dup = (df * sl).astype(BF16)
        dh2_ref[0] = _dot_nt(dgt, wg_ref[...]) + _dot_nt(dup, wu_ref[...])
        _acc_tn(ag, h2, dgt)
        _acc_tn(au, h2, dup)
        _acc_tn(ad, f, dffn)

        @pl.when(i == nt - 1)
        def _():
            pltpu.sync_copy(ag, dwg_ref.at[j])
            pltpu.sync_copy(au, dwu_ref.at[j])
            pltpu.sync_copy(ad, dwd_ref.at[j])

    any_spec = pl.BlockSpec(memory_space=pl.ANY)
    return pl.pallas_call(
        body,
        name="ffn_bwd",
        grid=(nj, nt),
        in_specs=[
            pl.BlockSpec((TM, D), lambda j, i: (i, 0)),
            pl.BlockSpec((TM, D), lambda j, i: (i, 0)),
            pl.BlockSpec((TM, FFC), lambda j, i: (i, j)),
            pl.BlockSpec((TM, FFC), lambda j, i: (i, j)),
            _resident((NV, D)),
            pl.BlockSpec((D, FFC), lambda j, i: (0, j)),
            pl.BlockSpec((D, FFC), lambda j, i: (0, j)),
            pl.BlockSpec((FFC, D), lambda j, i: (j, 0)),
        ],
        out_specs=[pl.BlockSpec((1, TM, D), lambda j, i: (j, i, 0)), any_spec, any_spec, any_spec],
        out_shape=(
            jax.ShapeDtypeStruct((nj, L, D), F32),
            jax.ShapeDtypeStruct((nj, D, FFC), F32),
            jax.ShapeDtypeStruct((nj, D, FFC), F32),
            jax.ShapeDtypeStruct((nj, FFC, D), F32),
        ),
        scratch_shapes=[pltpu.VMEM((D, FFC), F32), pltpu.VMEM((D, FFC), F32), pltpu.VMEM((FFC, D), F32)],
        compiler_params=_cp(2),
    )(x1, dpre2, gt_b, up_b, vec, w_gate, w_up, w_down)


def _merge_bwd_call(dh2p, dpre2, xe, mix, cat, yf, yb, xbc, z, dpool, pun, vec, w_out, pool_w, scattered):
    L = z.shape[0]
    nt = L // TM
    ns = len(scattered)

    def body(*refs):
        (dh2_ref, dpre2_ref, xe_ref, mix_ref, cat_ref, yf_ref, yb_ref, xs_ref, z_ref, dpool_ref, pun_ref,
         vec_ref, w_ref, pw_ref) = refs[:14]
        dxe_ref, dy_ref, dz_ref, dd_ref, dpw_ref, gacc_ref, dwo_ref = refs[14 + ns:21 + ns]
        dwo_s = refs[21 + 2 * ns]
        ex = _Exchange(refs[14:14 + ns], refs[21 + ns:21 + 2 * ns], *refs[22 + 2 * ns:], False)
        i = pl.program_id(0)

        @pl.when(i == 0)
        def _():
            ex.start()
            gacc_ref[...] = jnp.zeros_like(gacc_ref)
            dpw_ref[...] = jnp.zeros_like(dpw_ref)
            dwo_s[...] = jnp.zeros_like(dwo_s)

        def vrow(r):
            return vec_ref[r:r + 1, :]

        def gadd(r, val):
            gacc_ref[r:r + 1, :] += _rowsum(val)

        dh2 = dh2_ref[0] + dh2_ref[1]
        dx1 = ALPHA * dpre2_ref[...] + dh2 * (1.0 + vrow(V_SC2))
        mix = mix_ref[...]
        x1, n1, rstd1 = _ln(ALPHA * xe_ref[...] + vrow(V_G1) * mix, vrow(V_LN1G), vrow(V_LN1B))
        gadd(V_SC2, dh2 * x1)
        gadd(V_SH2, dh2)
        gadd(V_LN1G, dx1 * n1)
        gadd(V_LN1B, dx1)
        dpre1 = _ln_bwd(dx1, n1, rstd1, vrow(V_LN1G))
        dxe_ref[...] = ALPHA * dpre1
        gadd(V_G1, dpre1 * mix)
        dmix = (vrow(V_G1) * dpre1).astype(BF16)
        dcat = _dot_nt(dmix, w_ref[...])
        cat = cat_ref[...]
        for c0 in range(0, 2 * D, 512):
            dwo_s[c0:c0 + 512, :] += _dot_tn(cat[:, c0:c0 + 512], dmix)
        dyn = dcat[:, 0:D]
        dp = dcat[:, D:2 * D]
        xs = xs_ref[...]
        z = z_ref[...]
        ym, sz, gated, r = _gated(yf_ref, yb_ref, xs, z, vec_ref)
        gadd(V_SSDG, dyn * gated * r)
        a = dyn * vrow(V_SSDG)
        dgated = r * a - gated * (r * r * r * jnp.mean(a * gated, axis=-1, keepdims=True))
        dym = dgated * (z * sz)
        dy_ref[...] = dym
        dz_ref[...] = dgated * ym * (sz * (1.0 + z * (1.0 - sz)))
        gadd(V_DSK, dym * xs)
        gadd(V_PSC, dp * pun_ref[...])
        dps = (dp * vrow(V_PSC)).astype(BF16)
        dpool = dpool_ref[...]
        for g in range(4):
            gs = slice(PG * g, PG * g + PG)
            dd_ref[:, gs] = _dot_nt(dps[:, gs], pw_ref[g])
            dpw_ref[g] += _dot_tn(dpool[:, gs], dps[:, gs])

        @pl.when(i == nt - 1)
        def _():
            pltpu.sync_copy(dwo_s, dwo_ref)
            ex.wait()

    any_spec = pl.BlockSpec(memory_space=pl.ANY)
    return pl.pallas_call(
        body,
        name="merge_bwd",
        grid=(nt,),
        in_specs=[
            pl.BlockSpec((2, TM, D), lambda i: (0, i, 0)),
            _tiles(TM, D), _tiles(TM, D), _tiles(TM, D), _tiles(TM, 2 * D),
            _tiles(TM, D), _tiles(TM, D),
            _tiles(TM, D), _tiles(TM, D), _tiles(TM, D), _tiles(TM, D),
            _resident((NV, D)), _resident((2 * D, D)), _resident((4, PG, PG)),
        ] + [any_spec] * ns,
        out_specs=[_tiles(TM, D), _tiles(TM, D), _tiles(TM, D), _tiles(TM, D),
                   _const_out((4, PG, PG)), _const_out((NV, D)), any_spec] + [any_spec] * ns,
        out_shape=[
            jax.ShapeDtypeStruct((L, D), F32),
            jax.ShapeDtypeStruct((L, D), F32),
            jax.ShapeDtypeStruct((L, D), F32),
            jax.ShapeDtypeStruct((L, D), F32),
            jax.ShapeDtypeStruct((4, PG, PG), F32),
            jax.ShapeDtypeStruct((NV, D), F32),
            jax.ShapeDtypeStruct((2 * D, D), F32),
        ] + _exchange_out_shapes(scattered, False),
        scratch_shapes=[pltpu.VMEM((2 * D, D), F32)] + _exchange_sems(ns),
        compiler_params=_cp(),
    )(dh2p, dpre2, xe, mix, cat, yf, yb, xbc, z, dpool, pun, vec, w_out, pool_w, *scattered)


def _pool_bwd_call(dd, pmat_t, icnt):
    L = dd.shape[0]
    nt = L // PT
    cur, prev, nxt = _pool_specs(nt)
    icur = pl.BlockSpec((PT, ICW), lambda i: (i, 0))
    iprev = pl.BlockSpec((PT, ICW), lambda i: (jnp.maximum(i - 1, 0), 0))
    inxt = pl.BlockSpec((PT, ICW), lambda i: (jnp.minimum(i + 1, nt - 1), 0))

    def body(cur_ref, prev_ref, next_ref, ic_ref, icp_ref, icn_ref, m0_ref, m1_ref, m2_ref, m3_ref, du_ref):
        i = pl.program_id(0)
        n = pl.num_programs(0)
        lane = lax.broadcasted_iota(jnp.int32, (1, ICW), 1)
        icv = ic_ref[...]
        for g, m_ref in enumerate((m0_ref, m1_ref, m2_ref, m3_ref)):
            gs = slice(PG * g, PG * g + PG)
            halo = _halo_tokens(g)
            icp = _column(icp_ref[PT - halo:PT, :], lane, g)
            icn = _column(icn_ref[0:halo, :], lane, g)
            top = jnp.where(i > 0, prev_ref[PT - halo:PT, gs] * icp, 0.0)
            bot = jnp.where(i < n - 1, next_ref[0:halo, gs] * icn, 0.0)
            mid = cur_ref[:, gs]
            ext = jnp.concatenate([top, mid * _column(icv, lane, g), bot], axis=0)
            du_ref[:, gs] = _split_dot(m_ref[...], ext) - mid

    return pl.pallas_call(
        body,
        name="pool_bwd",
        grid=(nt,),
        in_specs=[cur, prev, nxt, icur, iprev, inxt] + [_resident(m.shape) for m in pmat_t],
        out_specs=_tiles(PT, D),
        out_shape=jax.ShapeDtypeStruct((L, D), F32),
        compiler_params=_cp(),
    )(dd, dd, dd, icnt, icnt, icnt, *pmat_t)


def _ssd_bwd_call(dy, xbc, dtr, hprev_f, hprev_b, dh_init, par, e_mat, et_mat, dskip, name):
    L = xbc.shape[0]
    nc = L // Q

    def body(dy0_ref, dy1_ref, xbc0_ref, xbc1_ref, dtr0_ref, dtr1_ref, hp0_ref, hp1_ref, dhi_ref, par_ref, e_ref,
             et_ref, dsk_ref, dx0_ref, dx1_ref, ddt0_ref, ddt1_ref, acc_ref, dh0_ref, dh_s, AT, dtT, ddtT_s, dAT_s):
        s = pl.program_id(0)

        @pl.when(s == 0)
        def _():
            dh_s[...] = dhi_ref[...]
            acc_ref[...] = jnp.zeros_like(acc_ref)
            ddtT_s[...] = jnp.zeros_like(ddtT_s)
            dAT_s[...] = jnp.zeros_like(dAT_s)

        one_direction(0, dy0_ref, xbc0_ref, dtr0_ref, hp0_ref, par_ref, e_ref, et_ref, dsk_ref, dx0_ref, ddt0_ref,
                      acc_ref, dh0_ref, dh_s, AT, dtT, ddtT_s, dAT_s)
        one_direction(1, dy1_ref, xbc1_ref, dtr1_ref, hp1_ref, par_ref, e_ref, et_ref, dsk_ref, dx1_ref, ddt1_ref,
                      acc_ref, dh0_ref, dh_s, AT, dtT, ddtT_s, dAT_s)

    def one_direction(d, dy_ref, xbc_ref, dtr_ref, hp_ref, par_ref, e_ref, et_ref, dsk_ref, dxbc_ref, ddtr_ref,
                      acc_ref, dh0_ref, dh_s, AT_s, dtT_s, ddtT_s, dAT_s):
        q = _ssd_common(d, dtr_ref[...], par_ref)
        A, maskf, lane, dt, atot = q["A"], q["maskf"], q["lane"], q["dt"], q["atot"]
        AT_s[d] = q["AT"]
        dtT_s[d] = q["dtT"]
        AT, dtT = AT_s.at[d], dtT_s.at[d]
        hprev = hp_ref[0]
        hpb = hprev.astype(BF16)
        dh = dh_s[d]
        dhb = dh.astype(BF16)
        xs = xbc_ref[:, 0:1024]
        bb = xbc_ref[:, 1024:1152].astype(BF16)
        cb = xbc_ref[:, 1152:1280].astype(BF16)
        dy = dy_ref[...]
        ea_f = _head_expand(e_ref, jnp.exp(A))
        ch = _dot_nt(cb, hpb)
        dch = (dy * ea_f).astype(BF16)
        dC = _dot(dch, hpb)
        dhprev = _dot_tn(dch, cb)
        dA = _head_sum(et_ref, dy * ch * ea_f)
        dec = _state_decay(atot, lane)
        dhprev = dhprev + dh * dec
        dhh = dh * hprev * dec
        datot = jnp.zeros((1, 128), F32)
        for h in range(NH):
            tot_h = jnp.sum(_rowsum(dhh[HP * h:HP * h + HP, :]), axis=1, keepdims=True)
            datot = datot + jnp.where(lane == h, tot_h, 0.0)
        ear = jnp.exp(atot - A)
        wend = ear * dt
        wf = _head_expand(e_ref, wend)
        xw = (xs * wf).astype(BF16)
        dxw = _dot_nt(bb, dhb)
        dB = _dot(xw, dhb)
        dxs = dxw * wf
        dwend = _head_sum(et_ref, dxw * xs)
        ddt = dwend * ear
        de = dwend * wend
        datot = datot + _rowsum(de)
        dA = dA - de
        g = _dot_nt(cb, bb)
        dG = jnp.zeros((Q, Q), F32)
        for k in range(NH // 2):
            ks = slice(128 * k, 128 * k + 128)
            xp = xs[:, ks]
            dyp = dy[:, ks]
            accdx = dxs[:, ks]
            if d == 0:
                accdx = accdx + dyp * dsk_ref[:, ks]
            scts, dyhs = [], []
            for half in range(2):
                h = 2 * k + half
                inhead = (lane >= HP) if half else (lane < HP)
                seg = _column(A, lane, h) - AT[h:h + 1, :]
                lm = jnp.exp(jnp.minimum(seg, 0.0)) * maskf
                dtrow = dtT[h:h + 1, :]
                gl = g * lm
                sc = gl * dtrow
                dyh = jnp.where(inhead, dyp, 0.0).astype(BF16)
                xh = jnp.where(inhead, xp, 0.0).astype(BF16)
                dS = _dot_nt(dyh, xh)
                scts.append(sc.T.astype(BF16))
                dyhs.append(dyh)
                nn = dS * gl
                cn = _rowsum(nn)
                rm = jnp.sum(nn * dtrow, axis=1, keepdims=True)
                dG = dG + dS * (lm * dtrow)
                ddtT_s[d, h:h + 1, :] = cn
                dAT_s[d, h:h + 1, :] = -(cn * dtrow)
                dA = dA + rm * (lane == h).astype(F32)
            dxbc_ref[:, ks] = accdx + _dot(jnp.concatenate(scts, axis=1), jnp.concatenate(dyhs, axis=0))
        dGb = dG.astype(BF16)
        dxbc_ref[:, 1024:1152] = dB + _dot_tn(dGb, cb)
        dxbc_ref[:, 1152:1280] = dC + _dot(dGb, bb)
        da = _tri_dot(q["mask_t"], dA + dAT_s[d].T) + datot
        ddt = ddt + ddtT_s[d].T + da * q["aneg"]
        ddtr = jnp.where(q["hmask"], ddt * _sigmoid(q["pre"]), 0.0)
        ddtr_ref[...] = ddtr
        acc_ref[d, 0:1, :] += _rowsum(ddtr)
        acc_ref[d, 1:2, :] += _rowsum(da * dt) * q["aneg"]
        dh_s[d] = dhprev
        dh0_ref[d] = dhprev

    def back(s):
        return nc - 1 - s

    return pl.pallas_call(
        body,
        name=name,
        grid=(nc,),
        in_specs=[
            pl.BlockSpec((Q, D), lambda s: (back(s), 0)),
            pl.BlockSpec((Q, D), lambda s: (s, 0)),
            pl.BlockSpec((Q, DXBC), lambda s: (back(s), 0)),
            pl.BlockSpec((Q, DXBC), lambda s: (s, 0)),
            pl.BlockSpec((Q, 128), lambda s: (back(s), 0)),
            pl.BlockSpec((Q, 128), lambda s: (s, 1)),
            pl.BlockSpec((1, D, NS), lambda s: (back(s), 0, 0)),
            pl.BlockSpec((1, D, NS), lambda s: (s, 0, 0)),
            _resident((2, D, NS)),
            _resident((8, 128)),
            _resident((256, D)),
            _resident((D, 256)),
            _resident((1, D)),
        ],
        out_specs=[
            pl.BlockSpec((Q, DXBC), lambda s: (back(s), 0)),
            pl.BlockSpec((Q, DXBC), lambda s: (s, 0)),
            pl.BlockSpec((Q, 128), lambda s: (back(s), 0)),
            pl.BlockSpec((Q, 128), lambda s: (s, 0)),
            _const_out((2, 8, 128)),
            _const_out((2, D, NS)),
        ],
        out_shape=(
            jax.ShapeDtypeStruct((L, DXBC), F32),
            jax.ShapeDtypeStruct((L, DXBC), F32),
            jax.ShapeDtypeStruct((L, 128), F32),
            jax.ShapeDtypeStruct((L, 128), F32),
            jax.ShapeDtypeStruct((2, 8, 128), F32),
            jax.ShapeDtypeStruct((2, D, NS), F32),
        ),
        scratch_shapes=[pltpu.VMEM((2, D, NS), F32)] + [pltpu.VMEM((2, 128, Q), F32)] * 4,
        compiler_params=_cp(),
    )(dy, dy, xbc, xbc, dtr, dtr, hprev_f, hprev_b, dh_init, par, e_mat, et_mat, dskip)


def _conv_bwd_call(dxf, dxb, pre, xbc_raw, acc_init, name, scattered=()):
    L = xbc_raw.shape[0]
    nt = L // TM
    ns = len(scattered)
    prev, nxt = _halo_specs(TM, DXBC, L)

    def body(*refs):
        dxf_ref, dxb_ref, pre_ref, cur_ref, prev_ref, next_ref, init_ref = refs[:7]
        dpre_ref, acc_ref = refs[7 + ns:9 + ns]
        ext = refs[9 + 2 * ns]
        if ns:
            ex = _Exchange(refs[7:7 + ns], refs[9 + ns:9 + 2 * ns], *refs[10 + 2 * ns:], False)

        @pl.when(pl.program_id(0) == 0)
        def _():
            if ns:
                ex.start()
            acc_ref[...] = init_ref[...]

        _extended(ext, cur_ref, prev_ref, next_ref)
        pre = pre_ref[...]
        sg = _sigmoid(pre)
        dpre = (dxf_ref[...] + dxb_ref[...]) * (sg * (1.0 + pre * (1.0 - sg)))
        dpre_ref[...] = dpre
        for k in range(5):
            acc_ref[k:k + 1, :] += _rowsum(dpre * _shifted(ext, k - 2, TM))
        acc_ref[5:6, :] += _rowsum(dpre)
        if ns:
            @pl.when(pl.program_id(0) == nt - 1)
            def _():
                ex.wait()

    any_spec = pl.BlockSpec(memory_space=pl.ANY)
    return pl.pallas_call(
        body,
        name=name,
        grid=(nt,),
        in_specs=[_tiles(TM, DXBC), _tiles(TM, DXBC), _tiles(TM, DXBC), _tiles(TM, DXBC), prev, nxt,
                  _resident((8, DXBC))] + [any_spec] * ns,
        out_specs=[_tiles(TM, DXBC), _const_out((8, DXBC))] + [any_spec] * ns,
        out_shape=[jax.ShapeDtypeStruct((L, DXBC), F32), jax.ShapeDtypeStruct((8, DXBC), F32)]
        + _exchange_out_shapes(scattered, False),
        scratch_shapes=[pltpu.VMEM((TM + 16, DXBC), F32)] + (_exchange_sems(ns) if ns else []),
        compiler_params=_cp(),
    )(dxf, dxb, pre, xbc_raw, xbc_raw, xbc_raw, acc_init, *scattered)


def _inproj_bwd_call(dpre, cw, dz, ddt0, ddt1, dup, h1, dxe_part, x0, vec, w_in, dw_init, name):
    L = x0.shape[0]
    nt = L // TM
    prev, nxt = _halo_specs(TM, DXBC, L)

    def body(cur_ref, prev_ref, next_ref, cw_ref, dz_ref, ddt0_ref, ddt1_ref, dup_ref, h1_ref, dxe_ref, x0_ref,
             vec_ref, w_ref, dwi_ref, gx_ref, gacc_ref, dw_ref, dw_s, ext):
        i = pl.program_id(0)

        @pl.when(i == 0)
        def _():
            gacc_ref[...] = jnp.zeros_like(gacc_ref)
            pltpu.sync_copy(dwi_ref, dw_s)

        def vrow(r):
            return vec_ref[r:r + 1, :]

        _extended(ext, cur_ref, prev_ref, next_ref)
        dxr = cw_ref[0:1, :] * _shifted(ext, 2, TM)
        for k in range(1, 5):
            dxr = dxr + cw_ref[k:k + 1, :] * _shifted(ext, 2 - k, TM)
        dproj = jnp.concatenate([dz_ref[...], dxr, ddt0_ref[...], ddt1_ref[...], dup_ref[...]], axis=1).astype(BF16)
        dh1 = _dot_nt(dproj, w_ref[...])
        _acc_tn(dw_s, h1_ref[...], dproj)
        xe, n0, rstd0 = _ln(x0_ref[...], vrow(V_EMBG), vrow(V_EMBB))
        dxe = dxe_ref[...] + dh1 * (1.0 + vrow(V_SC1))
        gacc_ref[V_SC1:V_SC1 + 1, :] += _rowsum(dh1 * xe)
        gacc_ref[V_SH1:V_SH1 + 1, :] += _rowsum(dh1)
        gacc_ref[V_EMBG:V_EMBG + 1, :] += _rowsum(dxe * n0)
        gacc_ref[V_EMBB:V_EMBB + 1, :] += _rowsum(dxe)
        gx_ref[...] = _ln_bwd(dxe, n0, rstd0, vrow(V_EMBG))

        @pl.when(i == nt - 1)
        def _():
            pltpu.sync_copy(dw_s, dw_ref)

    any_spec = pl.BlockSpec(memory_space=pl.ANY)
    return pl.pallas_call(
        body,
        name=name,
        grid=(nt,),
        in_specs=[_tiles(TM, DXBC), prev, nxt, _resident((8, DXBC)), _tiles(TM, D), _tiles(TM, 128), _tiles(TM, 128),
                  _tiles(TM, D), _tiles(TM, D), _tiles(TM, D), _tiles(TM, D), _resident((NV, D)), _resident((D, WIN)),
                  any_spec],
        out_specs=[_tiles(TM, D), _const_out((NV, D)), any_spec],
        out_shape=(
            jax.ShapeDtypeStruct((L, D), F32),
            jax.ShapeDtypeStruct((NV, D), F32),
            jax.ShapeDtypeStruct((D, WIN), F32),
        ),
        scratch_shapes=[pltpu.VMEM((D, WIN), F32), pltpu.VMEM((TM + 16, DXBC), F32)],
        compiler_params=_cp(),
    )(dpre, dpre, dpre, cw, dz, ddt0, ddt1, dup, h1, dxe_part, x0, vec, w_in, dw_init)


def _adamw(w, g, m, v):
    m = ADAM_B1 * m + (1.0 - ADAM_B1) * g
    v = ADAM_B2 * v + (1.0 - ADAM_B2) * (g * g)
    m_hat = m / (1.0 - ADAM_B1 ** ADAM_STEP)
    v_hat = v / (1.0 - ADAM_B2 ** ADAM_STEP)
    delta = -ADAM_LR * (m_hat / (jnp.sqrt(v_hat) + ADAM_EPS) + ADAM_WD * w)
    return delta, m, v


def _adamw_shard_call(gslots, w, m, v, tr, name):
    R, C = w.shape

    def body(gs_ref, w_ref, m_ref, v_ref, g_ref, d_ref, mo_ref, vo_ref):
        g = gs_ref[0].astype(F32)
        for i in range(1, NDEV):
            g = g + gs_ref[i].astype(F32)
        delta, mn, vn = _adamw(w_ref[...], g, m_ref[...], v_ref[...])
        g_ref[...] = g
        d_ref[...] = delta
        mo_ref[...] = mn
        vo_ref[...] = vn

    t = _tiles(tr, C)
    return pl.pallas_call(
        body,
        name=name,
        grid=(R // tr,),
        in_specs=[pl.BlockSpec((NDEV, tr, C), lambda i: (0, i, 0)), t, t, t],
        out_specs=[t, t, t, t],
        out_shape=tuple(jax.ShapeDtypeStruct((R, C), F32) for _ in range(4)),
        compiler_params=_cp(),
    )(gslots, w, m, v)


def _wada_call(dm_ex, dm_ctx, silu_all, w, m, v):
    ncol = w.shape[1]

    def body(dme_ref, dmc_ref, s_ref, w_ref, m_ref, v_ref, g_ref, d_ref, mo_ref, vo_ref, ds_ref):
        dmc = _rowsum(dmc_ref[...])
        rows = lax.broadcasted_iota(jnp.int32, (8, 1), 0)
        low = jnp.where(rows == 0, dmc, 0.0)
        dm = jnp.concatenate([dme_ref[...], low], axis=0).astype(BF16)
        wv = w_ref[...]
        g = _dot_tn(s_ref[...].astype(BF16), dm)
        delta, mn, vn = _adamw(wv, g, m_ref[...], v_ref[...])
        g_ref[...] = g
        d_ref[...] = delta
        mo_ref[...] = mn
        vo_ref[...] = vn
        ds_ref[...] = _dot_nt(low.astype(BF16), wv.astype(BF16))

    return pl.pallas_call(
        body,
        name="wada_update",
        out_shape=tuple(jax.ShapeDtypeStruct((D, ncol), F32) for _ in range(4)) + (jax.ShapeDtypeStruct((8, D), F32),),
        compiler_params=pltpu.CompilerParams(vmem_limit_bytes=VMEM_LIMIT),
    )(dm_ex, dm_ctx, silu_all, w, m, v)


P_DMOD, P_DMODC, P_EMBG, P_EMBB, P_LN1G, P_LN1B, P_LN2G, P_LN2B = 0, 6, 8, 9, 10, 11, 12, 13
P_SSDG, P_PSC, P_DSK, P_CONVB, P_DTB, P_ALOG, P_LOSS, NP = 14, 15, 16, 17, 19, 20, 21, 24
S_CCTX, S_EMBG, S_EMBB, S_BADA, S_CONVB, S_DTB, S_ALOG, S_DSK = 0, 1, 2, 3, 9, 11, 12, 13
S_SSDG, S_PSC, S_LN1G, S_LN1B, S_LN2G, S_LN2B, NSM = 14, 15, 16, 17, 18, 19, 24


def _small_update_call(pall, dsil, cctx, w, m, v, et_mat):
    def body(p_ref, ds_ref, c_ref, w_ref, m_ref, v_ref, et_ref, g_ref, d_ref, mo_ref, vo_ref, loss_ref,
             tot, dsum, dsk8):
        tot[...] = p_ref[0]
        dsum[...] = ds_ref[0]
        for i in range(1, NDEV):
            tot[...] += p_ref[i]
            dsum[...] += ds_ref[i]
        cv = c_ref[...]
        sc = _sigmoid(cv)
        g_ref[...] = jnp.zeros_like(g_ref)
        g_ref[S_CCTX:S_CCTX + 1, :] = dsum[0:1, :] * (sc * (1.0 + cv * (1.0 - sc)))
        g_ref[S_EMBG:S_EMBG + 1, :] = tot[P_EMBG:P_EMBG + 1, :]
        g_ref[S_EMBB:S_EMBB + 1, :] = tot[P_EMBB:P_EMBB + 1, :]
        g_ref[S_BADA:S_BADA + 2, :] = tot[P_DMOD:P_DMOD + 2, :] + tot[P_DMODC:P_DMODC + 2, :]
        g_ref[S_BADA + 2:S_BADA + 6, :] = tot[P_DMOD + 2:P_DMOD + 6, :]
        g_ref[S_CONVB:S_CONVB + 2, :] = tot[P_CONVB:P_CONVB + 2, :]
        g_ref[S_DTB:S_DTB + 1, :] = tot[P_DTB:P_DTB + 1, :]
        g_ref[S_ALOG:S_ALOG + 1, :] = tot[P_ALOG:P_ALOG + 1, :]
        dsk8[...] = _dot(jnp.broadcast_to(tot[P_DSK:P_DSK + 1, :], (8, D)), et_ref[:, 0:128].astype(F32), HI)
        g_ref[S_DSK:S_DSK + 1, 0:128] = dsk8[0:1, :]
        g_ref[S_SSDG:S_SSDG + 1, :] = tot[P_SSDG:P_SSDG + 1, :]
        g_ref[S_PSC:S_PSC + 1, :] = tot[P_PSC:P_PSC + 1, :]
        g_ref[S_LN1G:S_LN1G + 1, :] = tot[P_LN1G:P_LN1G + 1, :]
        g_ref[S_LN1B:S_LN1B + 1, :] = tot[P_LN1B:P_LN1B + 1, :]
        g_ref[S_LN2G:S_LN2G + 1, :] = tot[P_LN2G:P_LN2G + 1, :]
        g_ref[S_LN2B:S_LN2B + 1, :] = tot[P_LN2B:P_LN2B + 1, :]
        delta, mn, vn = _adamw(w_ref[...], g_ref[...], m_ref[...], v_ref[...])
        d_ref[...] = delta
        mo_ref[...] = mn
        vo_ref[...] = vn
        loss_ref[...] = jnp.broadcast_to(tot[P_LOSS:P_LOSS + 1, 0:128], (8, 128))

    return pl.pallas_call(
        body,
        name="small_update",
        out_shape=tuple(jax.ShapeDtypeStruct((NSM, D), F32) for _ in range(4)) + (jax.ShapeDtypeStruct((8, 128), F32),),
        scratch_shapes=[pltpu.VMEM((NP, D), F32), pltpu.VMEM((8, D), F32), pltpu.VMEM((8, 128), F32)],
        compiler_params=pltpu.CompilerParams(vmem_limit_bytes=VMEM_LIMIT),
    )(pall, dsil, cctx, w, m, v, et_mat)


def _pad_rows(flat, mult=16):
    n = flat.shape[0]
    rows = -(-n // D)
    rows = -(-rows // mult) * mult
    return jnp.pad(flat, (0, rows * D - n)).reshape(rows, D)


def _by_cols(dw):
    r = dw.shape[0]
    return jnp.transpose(dw.reshape(r, NDEV, -1), (1, 0, 2))


def _from_cols(g):
    return jnp.transpose(g, (1, 0, 2)).reshape(g.shape[1], -1)


def _pool_constants(L):
    rows = L // GW
    t_r = jnp.arange(PT) // GW
    t_c = jnp.arange(PT) % GW
    fw, bw, ic = [], [], []
    pos_r = jnp.arange(L) // GW
    pos_c = jnp.arange(L) % GW
    for g, w in enumerate(WINDOWS):
        lo, hi = -(w // 2), w - w // 2 - 1
        n_ext = PT + 2 * _halo_tokens(g)
        e_r = jnp.arange(n_ext) // GW - w // 2
        e_c = jnp.arange(n_ext) % GW
        dr = e_r[None, :] - t_r[:, None]
        dc = e_c[None, :] - t_c[:, None]
        fw.append(((dr >= lo) & (dr <= hi) & (dc >= lo) & (dc <= hi)).astype(BF16))
        bw.append(((-dr >= lo) & (-dr <= hi) & (-dc >= lo) & (-dc <= hi)).astype(BF16))
        cr = jnp.minimum(pos_r + hi, rows - 1) - jnp.maximum(pos_r + lo, 0) + 1
        cc = jnp.minimum(pos_c + hi, GW - 1) - jnp.maximum(pos_c + lo, 0) + 1
        ic.append(1.0 / (cr * cc).astype(F32))
    icnt = jnp.pad(jnp.stack(ic, axis=1), ((0, 0), (0, ICW - len(WINDOWS))))
    return fw, bw, icnt


def _head_matrices():
    hp = jnp.arange(D) // HP
    e = (jnp.arange(128)[:, None] == hp[None, :]).astype(BF16)
    return jnp.concatenate([e, e], axis=0), jnp.concatenate([e.T, e.T], axis=1)


def _aligned_in_proj(w):
    zpad = jnp.zeros((D, 128 - NH), w.dtype)
    return jnp.concatenate([w[:, 0:2304], w[:, 2304:2320], zpad, w[:, 2320:2336], zpad, w[:, 2336:3360]], axis=1)


def _unaligned_in_proj(dw):
    return jnp.concatenate([dw[:, 0:2304], dw[:, 2304:2320], dw[:, 2432:2448], dw[:, 2560:3584]], axis=1)


def _row(v):
    return v.reshape(1, -1).astype(F32)


def _pad_lanes(v, width=D):
    v = v.reshape(1, -1)
    return jnp.pad(v, ((0, 0), (0, width - v.shape[1])))


def kernel(x, c, ctx, c_ctx, emb_ln_g, emb_ln_b, w_ada, b_ada, in_proj, conv_w, conv_b, dt_bias, a_log, d_skip, ssd_norm_g, pool_w, pool_scale, w_out, ln1_g, ln1_b, w_gate, w_up, w_down, ln2_g, ln2_b, loss_target, m_c_ctx, m_emb_ln_g, m_emb_ln_b, m_w_ada, m_b_ada, m_in_proj, m_conv_w, m_conv_b, m_dt_bias, m_a_log, m_d_skip, m_ssd_norm_g, m_pool_w, m_pool_scale, m_w_out, m_ln1_g, m_ln1_b, m_w_gate, m_w_up, m_w_down, m_ln2_g, m_ln2_b, v_c_ctx, v_emb_ln_g, v_emb_ln_b, v_w_ada, v_b_ada, v_in_proj, v_conv_w, v_conv_b, v_dt_bias, v_a_log, v_d_skip, v_ssd_norm_g, v_pool_w, v_pool_scale, v_w_out, v_ln1_g, v_ln1_b, v_w_gate, v_w_up, v_w_down, v_ln2_g, v_ln2_b):
    me = 4 * lax.axis_index("x") + 2 * lax.axis_index("y") + lax.axis_index("c")
    x0 = x[0]
    ctx0 = ctx[0]
    tgt = loss_target[0]
    L = x0.shape[0]
    LC = ctx0.shape[0]
    ncol_ada = w_ada.shape[2]

    small_in = jnp.concatenate([c.reshape(-1), conv_w.reshape(-1)])
    gb = jnp.concatenate([_row(emb_ln_g), _row(emb_ln_b), jnp.zeros((6, D), F32)], axis=0)
    xe, small_all, g_inp = _emb_ln_call(x0, gb, "emb_ln", [_pad_rows(small_in, 8), in_proj[0].astype(BF16)])
    (xe_c,) = _emb_ln_call(ctx0, gb, "emb_ln_ctx")
    c_all = small_all[:, 0, :]
    convw_all = small_all.reshape(NDEV, -1)[:, D:D + 5 * (DXBC // NDEV)].reshape(NDEV, 5, DXBC // NDEV)
    conv_w_full = _from_cols(convw_all)
    w_in = _aligned_in_proj(_from_cols(g_inp))
    late_shards = [pool_w[0].astype(BF16), w_out[0].astype(BF16), w_gate[0].astype(BF16), w_up[0].astype(BF16),
                   w_down[0].astype(BF16)]

    c_in = jnp.concatenate([c_all, c_ctx.reshape(1, D), jnp.zeros((7, D), F32)], axis=0)
    b_mine = lax.dynamic_slice(b_ada, (0, me * ncol_ada), (1, ncol_ada))
    silu_all, mod_mine = _mod_call(c_in, w_ada[0], b_mine)
    (mod_all,) = _exchange([mod_mine], "gather_mod", True)
    mod_all = _from_cols(mod_all)
    mod_me = lax.dynamic_slice(mod_all, (me, 0), (1, 6 * D)).reshape(6, D)
    mod_ctx = mod_all[8].reshape(6, D)

    tail = jnp.concatenate([
        _row(emb_ln_g), _row(emb_ln_b), _row(ln1_g), _row(ln1_b), _row(ln2_g), _row(ln2_b),
        _row(ssd_norm_g), _row(pool_scale), _row(jnp.repeat(d_skip.reshape(-1), HP)), jnp.zeros((1, D), F32)], axis=0)
    vec = jnp.concatenate([mod_me, tail], axis=0)
    vec_ctx = jnp.concatenate([mod_ctx, tail], axis=0)

    cw = jnp.concatenate([conv_w_full, conv_b.reshape(1, DXBC), jnp.zeros((2, DXBC), F32)], axis=0)
    par = jnp.concatenate([_pad_lanes(dt_bias[0, 0], 128), _pad_lanes(dt_bias[0, 1], 128),
                           _pad_lanes(a_log[0, 0], 128), _pad_lanes(a_log[0, 1], 128),
                           jnp.zeros((4, 128), F32)], axis=0)
    e_mat, et_mat = _head_matrices()
    pmat, pmat_t, icnt = _pool_constants(L)
    dskip_row = vec[V_DSK:V_DSK + 1]

    h1_c, _, xbcr_c, pre_c, xbc_c, dtr_c, _ = _f1_call(xe_c, vec_ctx, w_in, cw, "inproj_fwd_ctx")
    hzero = jnp.zeros((2, D, NS), F32)
    _, _, hpf_c, hpb_c, hfin_c = _ssd_fwd_call(xbc_c, dtr_c, hzero, par, e_mat, et_mat, "ssd_fwd_ctx")

    h1, z, xbcr, pre, xbc, dtr, up = _f1_call(xe, vec, w_in, cw, "inproj_fwd")
    yf, yb, hpf, hpb, _, g_pw, g_wo, g_wg, g_wu, g_wd = _ssd_fwd_call(xbc, dtr, hfin_c, par, e_mat, et_mat, "ssd_fwd",
                                                                       late_shards)
    pool_w_full = jnp.transpose(g_pw, (1, 0, 2, 3)).reshape(4, PG, PG)
    w_out_full = g_wo.reshape(2 * D, D)
    w_gate_full = _from_cols(g_wg)
    w_up_full = _from_cols(g_wu)
    w_down_full = g_wd.reshape(DFF, D)
    dpool, pun = _pool_fwd_call(up, pmat, icnt, pool_w_full)
    x1, mix, cat = _merge_call(yf, yb, xbc, z, pun, xe, vec, w_out_full)
    dpre2, gacc_f, gt_b, up_b = _ffn_fwd_call(x1, tgt, vec, w_gate_full, w_up_full, w_down_full)

    dh2p, dwg2, dwu2, dwd2 = _ffn_bwd_call(x1, dpre2, gt_b, up_b, vec, w_gate_full, w_up_full, w_down_full)
    nq = FFC // (DFF // NDEV)
    ffn_parts = [
        jnp.transpose(dwg2.reshape(-1, D, nq, DFF // NDEV), (0, 2, 1, 3)).reshape(NDEV, D, DFF // NDEV),
        jnp.transpose(dwu2.reshape(-1, D, nq, DFF // NDEV), (0, 2, 1, 3)).reshape(NDEV, D, DFF // NDEV),
        dwd2.reshape(NDEV, DFF // NDEV, D)]
    dxe_part, dy, dz, dd, dpw, gacc_m, dwo, gs_wg, gs_wu, gs_wd = _merge_bwd_call(
        dh2p, dpre2, xe, mix, cat, yf, yb, xbc, z, dpool, pun, vec, w_out_full, pool_w_full, ffn_parts)
    dup = _pool_bwd_call(dd, pmat_t, icnt)
    dxf, dxb, ddt0, ddt1, sacc, dh0 = _ssd_bwd_call(dy, xbc, dtr, hpf, hpb, hzero, par, e_mat, et_mat, dskip_row,
                                                     "ssd_bwd")
    zeros_c = jnp.zeros((LC, D), F32)
    dxf_c, dxb_c, ddt0_c, ddt1_c, sacc_c, _ = _ssd_bwd_call(zeros_c, xbc_c, dtr_c, hpf_c, hpb_c, dh0, par, e_mat, et_mat,
                                                            jnp.zeros((1, D), F32), "ssd_bwd_ctx")
    dprec_c, cacc_c = _conv_bwd_call(dxf_c, dxb_c, pre_c, xbcr_c, jnp.zeros((8, DXBC), F32), "conv_bwd_ctx")
    _, gacc_c, dwin_c = _inproj_bwd_call(dprec_c, cw, zeros_c, ddt0_c, ddt1_c, zeros_c, h1_c, zeros_c, ctx0, vec_ctx,
                                         w_in, jnp.zeros((D, WIN), F32), "inproj_bwd_ctx")
    mix_parts = [dwo.reshape(NDEV, 2 * D // NDEV, D),
                 jnp.transpose(dpw.reshape(4, NDEV, PG // NDEV, PG), (1, 0, 2, 3)).reshape(NDEV, 4 * PG // NDEV, PG)]
    dprec, cacc, gs_wo, gs_pw = _conv_bwd_call(dxf, dxb, pre, xbcr, cacc_c, "conv_bwd", mix_parts)
    grad_x, gacc_i, dwin = _inproj_bwd_call(dprec, cw, dz, ddt0, ddt1, dup, h1, dxe_part, x0, vec, w_in, dwin_c,
                                            "inproj_bwd")

    gsum = gacc_f + gacc_m + gacc_i
    sa = sacc + sacc_c
    dtb_row = _pad_lanes(jnp.concatenate([sa[0, 0, 0:NH], sa[1, 0, 0:NH]]))
    alog_row = _pad_lanes(jnp.concatenate([sa[0, 1, 0:NH], sa[1, 1, 0:NH]]))
    convb_rows = jnp.pad(cacc[5], (0, 2 * D - DXBC)).reshape(2, D)
    pack = jnp.concatenate([
        gsum[V_SH1:V_G2 + 1],
        gacc_c[V_SH1:V_SC1 + 1],
        gsum[V_EMBG:V_EMBB + 1] + gacc_c[V_EMBG:V_EMBB + 1],
        gsum[V_LN1G:V_LN2B + 1],
        gsum[V_SSDG:V_DSK + 1],
        convb_rows, dtb_row, alog_row,
        gsum[V_LOSS:V_LOSS + 1],
        jnp.zeros((NP - 22, D), F32)], axis=0)
    (pall,) = _exchange([pack], "gather_small_grads", True)

    dm_flat = pall[:, 0:8, :].reshape(NDEV, 8 * D)
    dm_ex = lax.dynamic_slice(dm_flat, (0, me * ncol_ada), (NDEV, ncol_ada))
    dmc_full = jnp.concatenate([dm_flat[:, 6 * D:8 * D], jnp.zeros((NDEV, 4 * D), F32)], axis=1)
    dm_ctx = lax.dynamic_slice(dmc_full, (0, me * ncol_ada), (NDEV, ncol_ada))
    g_wada, d_wada, nm_wada, nv_wada, dsil = _wada_call(dm_ex, dm_ctx, silu_all, w_ada[0], m_w_ada[0], v_w_ada[0])
    (dsil_all,) = _exchange([dsil], "gather_dsilu", True)

    def small_pack(cc, eg, eb, ba, cb_, dtb, al, dsk, sg, ps, l1g, l1b, l2g, l2b):
        return jnp.concatenate([
            _row(cc), _row(eg), _row(eb), ba.reshape(6, D), jnp.pad(cb_.reshape(-1), (0, 2 * D - DXBC)).reshape(2, D),
            _pad_lanes(dtb.reshape(-1)), _pad_lanes(al.reshape(-1)), _pad_lanes(dsk.reshape(-1)),
            _row(sg), _row(ps), _row(l1g), _row(l1b), _row(l2g), _row(l2b), jnp.zeros((NSM - 20, D), F32)], axis=0)

    sw = small_pack(c_ctx, emb_ln_g, emb_ln_b, b_ada, conv_b, dt_bias, a_log, d_skip, ssd_norm_g, pool_scale,
                    ln1_g, ln1_b, ln2_g, ln2_b)
    sm = small_pack(m_c_ctx, m_emb_ln_g, m_emb_ln_b, m_b_ada, m_conv_b, m_dt_bias, m_a_log, m_d_skip, m_ssd_norm_g,
                    m_pool_scale, m_ln1_g, m_ln1_b, m_ln2_g, m_ln2_b)
    sv = small_pack(v_c_ctx, v_emb_ln_g, v_emb_ln_b, v_b_ada, v_conv_b, v_dt_bias, v_a_log, v_d_skip, v_ssd_norm_g,
                    v_pool_scale, v_ln1_g, v_ln1_b, v_ln2_g, v_ln2_b)
    s_g, s_d, s_m, s_v, loss8 = _small_update_call(pall, dsil_all, _row(c_ctx), sw, sm, sv, et_mat)

    def small_unpack(t):
        return (t[S_CCTX], t[S_EMBG], t[S_EMBB], t[S_BADA:S_BADA + 6].reshape(1, 6 * D),
                t[S_CONVB:S_CONVB + 2].reshape(-1)[:DXBC].reshape(1, DXBC),
                t[S_DTB, 0:2 * NH].reshape(1, 2, NH), t[S_ALOG, 0:2 * NH].reshape(1, 2, NH), t[S_DSK, 0:NH].reshape(1, NH),
                t[S_SSDG].reshape(1, D), t[S_PSC].reshape(1, D), t[S_LN1G].reshape(1, D), t[S_LN1B].reshape(1, D),
                t[S_LN2G].reshape(1, D), t[S_LN2B].reshape(1, D))

    gs_inp, gs_cw = _exchange([_by_cols(_unaligned_in_proj(dwin)).astype(BF16), _by_cols(cacc[0:5])],
                              "exchange_last_grads", False)

    pshape = (4 * PG // NDEV, PG)
    u_inp = _adamw_shard_call(gs_inp, in_proj[0], m_in_proj[0], v_in_proj[0], 256, "adamw_in_proj")
    u_cw = _adamw_shard_call(gs_cw, conv_w[0], m_conv_w[0], v_conv_w[0], 5, "adamw_conv_w")
    u_pw = _adamw_shard_call(gs_pw, pool_w[0].reshape(pshape), m_pool_w[0].reshape(pshape), v_pool_w[0].reshape(pshape),
                             pshape[0], "adamw_pool_w")
    u_wo = _adamw_shard_call(gs_wo, w_out[0], m_w_out[0], v_w_out[0], 64, "adamw_w_out")
    u_wg = _adamw_shard_call(gs_wg, w_gate[0], m_w_gate[0], v_w_gate[0], 256, "adamw_w_gate")
    u_wu = _adamw_shard_call(gs_wu, w_up[0], m_w_up[0], v_w_up[0], 256, "adamw_w_up")
    u_wd = _adamw_shard_call(gs_wd, w_down[0], m_w_down[0], v_w_down[0], 88, "adamw_w_down")

    def assemble(k, small, wada):
        (cc, eg, eb, ba, cb_, dtb, al, dsk, sg, ps, l1g, l1b, l2g, l2b) = small_unpack(small)
        pw = u_pw[k].reshape(1, 4, PG // NDEV, PG)
        return (cc, eg, eb, wada[None], ba, u_inp[k][None], u_cw[k][None], cb_, dtb, al, dsk, sg, pw, ps,
                u_wo[k][None], l1g, l1b, u_wg[k][None], u_wu[k][None], u_wd[k][None], l2g, l2b)

    loss = loss8[0, 0]
    return (loss, grad_x[None], *assemble(0, s_g, g_wada), *assemble(1, s_d, d_wada),
            *assemble(2, s_m, nm_wada), *assemble(3, s_v, nv_wada))
```

```python
import functools
import math

import jax
import jax.numpy as jnp
from jax import lax
from jax.experimental import pallas as pl
from jax.experimental.pallas import tpu as pltpu

F32 = jnp.float32
BF16 = jnp.bfloat16
HI = lax.Precision.HIGHEST

NDEV = 8
D = 1024
NH = 16
HP = 64
NS = 128
Q = 128
DXBC = 1280
DFF = 2816
FFC = 1408
GW = 64
PR = 8
PT = PR * GW
WINDOWS = (2, 4, 8, 16)
PG = 256
DIN = 3360
WIN = 3584
ALPHA = 2.0 ** 0.25
LN_EPS = 1e-5
TM = 256
TMH = 512

ADAM_LR = 0.001
ADAM_B1 = 0.9
ADAM_B2 = 0.999
ADAM_EPS = 1e-08
ADAM_WD = 0.01
ADAM_STEP = 10

V_SH1, V_SC1, V_G1, V_SH2, V_SC2, V_G2 = 0, 1, 2, 3, 4, 5
V_EMBG, V_EMBB, V_LN1G, V_LN1B, V_LN2G, V_LN2B = 6, 7, 8, 9, 10, 11
V_SSDG, V_PSC, V_DSK, V_LOSS = 12, 13, 14, 15
NV = 16

VMEM_LIMIT = 60 * 1024 * 1024


def _cp(ndim=1):
    return pltpu.CompilerParams(dimension_semantics=("arbitrary",) * ndim, vmem_limit_bytes=VMEM_LIMIT)


def _dot(a, b, precision=None):
    return jnp.dot(a, b, preferred_element_type=F32, precision=precision)


def _dot_nt(a, b):
    return lax.dot_general(a, b, (((1,), (1,)), ((), ())), preferred_element_type=F32)


def _dot_tn(a, b, precision=None):
    return lax.dot_general(a, b, (((0,), (0,)), ((), ())), preferred_element_type=F32, precision=precision)


def _split2(x):
    hi = x.astype(BF16)
    return hi, (x - hi.astype(F32)).astype(BF16)


def _split_dot(m, x):
    hi, lo = _split2(x)
    return _dot(m, hi) + _dot(m, lo)


def _dot_split(x, m):
    hi, lo = _split2(x)
    return _dot(hi, m) + _dot(lo, m)


def _sigmoid(x):
    return 1.0 / (1.0 + jnp.exp(-x))


def _softplus(x):
    return jnp.maximum(x, 0.0) + jnp.log(1.0 + jnp.exp(-jnp.abs(x)))


def _ln(x, g, b):
    mu = jnp.mean(x, axis=-1, keepdims=True)
    xc = x - mu
    var = jnp.mean(xc * xc, axis=-1, keepdims=True)
    rstd = lax.rsqrt(var + LN_EPS)
    n = xc * rstd
    return n * g + b, n, rstd


def _ln_bwd(dy, n, rstd, g):
    dn = dy * g
    return rstd * (dn - jnp.mean(dn, axis=-1, keepdims=True) - n * jnp.mean(dn * n, axis=-1, keepdims=True))


def _rowsum(x):
    return jnp.sum(x, axis=0, keepdims=True)


def _resident(shape):
    nd = len(shape)
    return pl.BlockSpec(shape, lambda *_: (0,) * nd, pipeline_mode=pl.Buffered(1))


def _const_out(shape):
    nd = len(shape)
    return pl.BlockSpec(shape, lambda *_: (0,) * nd)


def _tiles(tm, width):
    return pl.BlockSpec((tm, width), lambda i: (i, 0))


def _halo_specs(tm, width, n_rows):
    r = tm // 8
    last = n_rows // 8 - 1
    prev = pl.BlockSpec((8, width), lambda i: (jnp.maximum(i * r - 1, 0), 0))
    nxt = pl.BlockSpec((8, width), lambda i: (jnp.minimum((i + 1) * r, last), 0))
    return prev, nxt


def _acc_tn(acc_ref, a, b, chunk=512):
    n = b.shape[1]
    for c0 in range(0, n, chunk):
        c1 = min(c0 + chunk, n)
        acc_ref[:, c0:c1] += _dot_tn(a, b[:, c0:c1])


def _my_coords():
    return lax.axis_index("x"), lax.axis_index("y"), lax.axis_index("c")


def _peer(k, mx, my, mc):
    kx, ky, kc = (k >> 2) & 1, (k >> 1) & 1, k & 1
    px = 1 - mx if kx else mx
    py = 1 - my if ky else my
    pc = 1 - mc if kc else mc
    return px, py, pc


class _Exchange:
    def __init__(self, srcs, dsts, send_sems, recv_sems, local_sems, gather):
        self.srcs, self.dsts, self.gather = srcs, dsts, gather
        self.send_sems, self.recv_sems, self.local_sems = send_sems, recv_sems, local_sems

    def _copies(self, outgoing):
        mx, my, mc = _my_coords()
        me = 4 * mx + 2 * my + mc
        local, remote = [], []
        for t, (src, dst) in enumerate(zip(self.srcs, self.dsts)):
            local.append(pltpu.make_async_copy(src if self.gather else src.at[me], dst.at[me], self.local_sems.at[t]))
            for k in range(1, NDEV):
                px, py, pc = _peer(k, mx, my, mc)
                pid = 4 * px + 2 * py + pc
                remote.append(pltpu.make_async_remote_copy(
                    src_ref=src if self.gather else src.at[pid],
                    dst_ref=dst.at[me] if outgoing else dst.at[pid],
                    send_sem=self.send_sems.at[t, k - 1],
                    recv_sem=self.recv_sems.at[t, k - 1],
                    device_id=(px, py, pc),
                    device_id_type=pl.DeviceIdType.MESH,
                ))
        return local, remote

    def start(self):
        local, remote = self._copies(True)
        for cp in local + remote:
            cp.start()

    def wait(self):
        local, sends = self._copies(True)
        _, recvs = self._copies(False)
        for cp in recvs:
            cp.wait_recv()
        for cp in sends:
            cp.wait_send()
        for cp in local:
            cp.wait()


class _ChipGather:
    def __init__(self, srcs, dsts, send_sems, recv_sems, local_sems):
        self.srcs, self.dsts = srcs, dsts
        self.send_sems, self.recv_sems, self.local_sems = send_sems, recv_sems, local_sems

    def _places(self):
        x, y, c = _my_coords()
        return (x, y, c), (x, y, 1 - c), [(1 - x, y), (x, 1 - y), (1 - x, 1 - y)]

    def _copy(self, t, k, block, to, own=False):
        slot = self.dsts[t].at[4 * block[0] + 2 * block[1] + block[2]]
        return pltpu.make_async_remote_copy(
            src_ref=self.srcs[t] if own else slot, dst_ref=slot,
            send_sem=self.send_sems.at[t, k], recv_sem=self.recv_sems.at[t, k],
            device_id=to, device_id_type=pl.DeviceIdType.MESH)

    def _local(self, t, me):
        return pltpu.make_async_copy(self.srcs[t], self.dsts[t].at[4 * me[0] + 2 * me[1] + me[2]], self.local_sems.at[t])

    def start(self):
        me, sib, chips = self._places()
        for t in range(len(self.srcs)):
            self._local(t, me).start()
            self._copy(t, 0, me, sib, own=True).start()
            for j, chip in enumerate(chips):
                self._copy(t, 1 + j, me, (*chip, me[2]), own=True).start()

    def wait(self):
        me, sib, chips = self._places()
        n = len(self.srcs)
        for t in range(n):
            for j, chip in enumerate(chips):
                self._copy(t, 1 + j, (*chip, me[2]), me).wait_recv()
                self._copy(t, 4 + j, (*chip, me[2]), sib).start()
        for t in range(n):
            self._copy(t, 0, sib, me).wait_recv()
            for j, chip in enumerate(chips):
                self._copy(t, 4 + j, (*chip, sib[2]), me).wait_recv()
            self._copy(t, 0, me, sib, own=True).wait_send()
            for j, chip in enumerate(chips):
                self._copy(t, 1 + j, me, (*chip, me[2]), own=True).wait_send()
                self._copy(t, 4 + j, (*chip, me[2]), sib).wait_send()
            self._local(t, me).wait()


def _exchange_sems(n):
    return [pltpu.SemaphoreType.DMA((n, NDEV - 1)), pltpu.SemaphoreType.DMA((n, NDEV - 1)), pltpu.SemaphoreType.DMA((n,))]


def _exchange_out_shapes(xs, gather):
    return [jax.ShapeDtypeStruct(x.shape if not gather else (NDEV,) + x.shape, x.dtype) for x in xs]


def _exchange(xs, name, gather):
    n = len(xs)

    def body(*refs):
        ex = _Exchange(refs[:n], refs[n:2 * n], *refs[2 * n:], gather)
        ex.start()
        ex.wait()

    any_spec = pl.BlockSpec(memory_space=pl.ANY)
    return pl.pallas_call(
        body,
        name=name,
        out_shape=_exchange_out_shapes(xs, gather),
        in_specs=[any_spec] * n,
        out_specs=[any_spec] * n,
        scratch_shapes=_exchange_sems(n),
    )(*xs)


def _mod_call(c_all, w_ada, b_ada):
    ncol = w_ada.shape[1]

    def body(c_ref, w_ref, b_ref, silu_ref, mod_ref):
        cv = c_ref[...]
        s = cv * _sigmoid(cv)
        silu_ref[...] = s
        mod_ref[...] = _dot(s.astype(BF16), w_ref[...].astype(BF16)) + b_ref[...]

    return pl.pallas_call(
        body,
        name="mod_fwd",
        out_shape=(jax.ShapeDtypeStruct((16, D), F32), jax.ShapeDtypeStruct((16, ncol), F32)),
    )(c_all, w_ada, b_ada)


def _emb_ln_call(x0, gb, name, gathered=()):
    L = x0.shape[0]
    nt = L // TM
    ng = len(gathered)

    def body(*refs):
        x_ref, gb_ref = refs[:2]
        xe_ref = refs[2 + ng]
        i = pl.program_id(0)
        if ng:
            ex = _ChipGather(refs[2:2 + ng], refs[3 + ng:3 + 2 * ng], *refs[3 + 2 * ng:])

            @pl.when(i == 0)
            def _():
                ex.start()

        xe_ref[...] = _ln(x_ref[...], gb_ref[0:1, :], gb_ref[1:2, :])[0]
        if ng:
            @pl.when(i == nt - 1)
            def _():
                ex.wait()

    any_spec = pl.BlockSpec(memory_space=pl.ANY)
    return pl.pallas_call(
        body,
        name=name,
        grid=(nt,),
        in_specs=[_tiles(TM, D), _resident((8, D))] + [any_spec] * ng,
        out_specs=[_tiles(TM, D)] + [any_spec] * ng,
        out_shape=[jax.ShapeDtypeStruct((L, D), F32)] + _exchange_out_shapes(gathered, True),
        scratch_shapes=_exchange_sems(ng) if ng else [],
        compiler_params=_cp(),
    )(x0, gb, *gathered)


def _f1_call(xe, vec, w_in, cw, name):
    L = xe.shape[0]
    prev, nxt = _halo_specs(TM, D, L)

    def body(xe_ref, prev_ref, next_ref, vec_ref, w_ref, cw_ref, h1_ref, z_ref, xbcr_ref, pre_ref, xbc_ref, dt_ref,
             up_ref, ext):
        i = pl.program_id(0)
        n = pl.num_programs(0)
        def modulated(v):
            return (v * (1.0 + vec_ref[V_SC1:V_SC1 + 1, :]) + vec_ref[V_SH1:V_SH1 + 1, :]).astype(BF16)

        rows = jnp.concatenate([prev_ref[...], xe_ref[...], next_ref[...]], axis=0)
        proj = _dot(modulated(rows), w_ref[...])
        h1_ref[...] = modulated(xe_ref[...])
        z_ref[...] = proj[8:8 + TM, 0:1024]
        xbcr_ref[...] = proj[8:8 + TM, 1024:2304]
        dt_ref[...] = proj[8:8 + TM, 2304:2560]
        up_ref[...] = proj[8:8 + TM, 2560:3584]
        ext[...] = proj[:, 1024:2304]
        ext[0:8, :] = jnp.where(i > 0, ext[0:8, :], 0.0)
        ext[8 + TM:16 + TM, :] = jnp.where(i < n - 1, ext[8 + TM:16 + TM, :], 0.0)
        pre = _conv_pre(ext, cw_ref, TM)
        pre_ref[...] = pre
        xbc_ref[...] = pre * _sigmoid(pre)

    return pl.pallas_call(
        body,
        name=name,
        grid=(L // TM,),
        in_specs=[_tiles(TM, D), prev, nxt, _resident((NV, D)), _resident((D, WIN)), _resident((8, DXBC))],
        out_specs=[_tiles(TM, D), _tiles(TM, D), _tiles(TM, DXBC), _tiles(TM, DXBC), _tiles(TM, DXBC), _tiles(TM, 256),
                   _tiles(TM, D)],
        out_shape=(
            jax.ShapeDtypeStruct((L, D), BF16),
            jax.ShapeDtypeStruct((L, D), F32),
            jax.ShapeDtypeStruct((L, DXBC), F32),
            jax.ShapeDtypeStruct((L, DXBC), F32),
            jax.ShapeDtypeStruct((L, DXBC), F32),
            jax.ShapeDtypeStruct((L, 256), F32),
            jax.ShapeDtypeStruct((L, D), F32),
        ),
        scratch_shapes=[pltpu.VMEM((TM + 16, DXBC), F32)],
        compiler_params=_cp(),
    )(xe, xe, xe, vec, w_in, cw)


def _extended(ext, cur_ref, prev_ref, next_ref):
    i = pl.program_id(0)
    n = pl.num_programs(0)
    tm = cur_ref.shape[0]
    ext[0:8, :] = jnp.where(i > 0, prev_ref[...], 0.0)
    ext[8:8 + tm, :] = cur_ref[...]
    ext[8 + tm:16 + tm, :] = jnp.where(i < n - 1, next_ref[...], 0.0)
    return ext


def _shifted(ext, offset, tm):
    return ext[8 + offset:8 + offset + tm, :]


def _conv_pre(ext, cw_ref, tm):
    acc = cw_ref[5:6, :] + cw_ref[0:1, :] * _shifted(ext, -2, tm)
    for k in range(1, 5):
        acc = acc + cw_ref[k:k + 1, :] * _shifted(ext, k - 2, tm)
    return acc


def _ssd_common(d, dtr, par_ref):
    lane = lax.broadcasted_iota(jnp.int32, (1, 128), 1)
    hmask = lane < NH
    bias = par_ref[d:d + 1, :]
    alog = par_ref[2 + d:3 + d, :]
    aneg = jnp.where(hmask, -jnp.exp(alog), 0.0)
    pre = dtr + bias
    dt = jnp.where(hmask, _softplus(pre), 0.0)
    a = dt * aneg
    row = lax.broadcasted_iota(jnp.int32, (Q, Q), 0)
    col = lax.broadcasted_iota(jnp.int32, (Q, Q), 1)
    maskf = ((row >= col) if d == 0 else (row <= col)).astype(F32)
    mask_t = ((row <= col) if d == 0 else (row >= col)).astype(F32)
    A = _tri_dot(maskf, a)
    atot = _rowsum(a)
    return dict(hmask=hmask, aneg=aneg, pre=pre, dt=dt, a=a, maskf=maskf, mask_t=mask_t, A=A, AT=A.T, dtT=dt.T,
                atot=atot, lane=lane)


def _tri_dot(mask, x):
    t1 = x.astype(BF16)
    r1 = x - t1.astype(F32)
    t2 = r1.astype(BF16)
    t3 = (r1 - t2.astype(F32)).astype(BF16)
    mb = mask.astype(BF16)
    return _dot(jnp.concatenate([mb, mb, mb], axis=1), jnp.concatenate([t1, t2, t3], axis=0))


def _column(v, lane, h):
    return jnp.sum(jnp.where(lane == h, v, 0.0), axis=1, keepdims=True)


def _head_expand(e_ref, v):
    hi, lo = _split2(v)
    return _dot(jnp.concatenate([hi, lo], axis=1), e_ref[...])


def _head_sum(et_ref, v):
    return _dot(v.astype(BF16), et_ref[:, 0:128])


def _state_decay(atot, lane):
    ea = jnp.exp(atot)
    return jnp.concatenate([jnp.broadcast_to(_column(ea, lane, h), (HP, NS)) for h in range(NH)], axis=0)


def _ssd_fwd_call(xbc, dtr, h0, par, e_mat, et_mat, name, gathered=()):
    L = xbc.shape[0]
    nc = L // Q
    ng = len(gathered)

    def body(*refs):
        xbc_refs, dtr_refs = refs[0:2], refs[2:4]
        h0_ref, par_ref, e_ref, et_ref = refs[4:8]
        y_refs, hp_refs = refs[8 + ng:10 + ng], refs[10 + ng:12 + ng]
        hf_ref = refs[12 + ng]
        hs, AT, dtT = refs[13 + 2 * ng:16 + 2 * ng]
        s = pl.program_id(0)
        if ng:
            ex = _Exchange(refs[8:8 + ng], refs[13 + ng:13 + 2 * ng], *refs[16 + 2 * ng:], True)

        @pl.when(s == 0)
        def _():
            if ng:
                ex.start()
            hs[...] = h0_ref[...]

        for d in range(2):
            xbc_ref, y_ref = xbc_refs[d], y_refs[d]
            q = _ssd_common(d, dtr_refs[d][...], par_ref)
            A, maskf, lane = q["A"], q["maskf"], q["lane"]
            AT[d] = q["AT"]
            dtT[d] = q["dtT"]
            hprev = hs[d]
            hp_refs[d][0] = hprev
            bb = xbc_ref[:, 1024:1152].astype(BF16)
            cb = xbc_ref[:, 1152:1280].astype(BF16)
            g = _dot_nt(cb, bb)
            yoff = _dot_nt(cb, hprev.astype(BF16)) * _head_expand(e_ref, jnp.exp(A))
            for k in range(NH // 2):
                ks = slice(128 * k, 128 * k + 128)
                xp = xbc_ref[:, ks]
                scs, xhs = [], []
                for half in range(2):
                    h = 2 * k + half
                    seg = _column(A, lane, h) - AT[d, h:h + 1, :]
                    lm = jnp.exp(jnp.minimum(seg, 0.0)) * maskf
                    scs.append((g * lm * dtT[d, h:h + 1, :]).astype(BF16))
                    inhead = (lane >= HP) if half else (lane < HP)
                    xhs.append(jnp.where(inhead, xp, 0.0).astype(BF16))
                y_ref[:, ks] = yoff[:, ks] + _dot(jnp.concatenate(scs, axis=1), jnp.concatenate(xhs, axis=0))
            wend = jnp.exp(q["atot"] - A) * q["dt"]
            xw = (xbc_ref[:, 0:1024] * _head_expand(e_ref, wend)).astype(BF16)
            hnew = hprev * _state_decay(q["atot"], lane) + _dot_tn(xw, bb)
            hs[d] = hnew
            hf_ref[d] = hnew
        if ng:
            @pl.when(s == nc - 1)
            def _():
                ex.wait()

    any_spec = pl.BlockSpec(memory_space=pl.ANY)
    return pl.pallas_call(
        body,
        name=name,
        grid=(nc,),
        in_specs=[
            pl.BlockSpec((Q, DXBC), lambda s: (s, 0)),
            pl.BlockSpec((Q, DXBC), lambda s: (nc - 1 - s, 0)),
            pl.BlockSpec((Q, 128), lambda s: (s, 0)),
            pl.BlockSpec((Q, 128), lambda s: (nc - 1 - s, 1)),
            _resident((2, D, NS)),
            _resident((8, 128)),
            _resident((256, D)),
            _resident((D, 256)),
        ] + [any_spec] * ng,
        out_specs=[
            pl.BlockSpec((Q, D), lambda s: (s, 0)),
            pl.BlockSpec((Q, D), lambda s: (nc - 1 - s, 0)),
            pl.BlockSpec((1, D, NS), lambda s: (s, 0, 0)),
            pl.BlockSpec((1, D, NS), lambda s: (nc - 1 - s, 0, 0)),
            _const_out((2, D, NS)),
        ] + [any_spec] * ng,
        out_shape=[
            jax.ShapeDtypeStruct((L, D), F32),
            jax.ShapeDtypeStruct((L, D), F32),
            jax.ShapeDtypeStruct((nc, D, NS), F32),
            jax.ShapeDtypeStruct((nc, D, NS), F32),
            jax.ShapeDtypeStruct((2, D, NS), F32),
        ] + _exchange_out_shapes(gathered, True),
        scratch_shapes=[pltpu.VMEM((2, D, NS), F32), pltpu.VMEM((2, 128, Q), F32), pltpu.VMEM((2, 128, Q), F32)]
        + (_exchange_sems(ng) if ng else []),
        compiler_params=_cp(),
    )(xbc, xbc, dtr, dtr, h0, par, e_mat, et_mat, *gathered)


def _halo_tokens(g):
    return (WINDOWS[g] // 2) * GW


def _pool_specs(n_tiles):
    cur = pl.BlockSpec((PT, D), lambda i: (i, 0))
    prev = pl.BlockSpec((PT, D), lambda i: (jnp.maximum(i - 1, 0), 0))
    nxt = pl.BlockSpec((PT, D), lambda i: (jnp.minimum(i + 1, n_tiles - 1), 0))
    return cur, prev, nxt


def _pool_fwd_call(up, pmat, icnt, pool_w):
    L = up.shape[0]
    nt = L // PT
    cur, prev, nxt = _pool_specs(nt)

    def body(cur_ref, prev_ref, next_ref, m0_ref, m1_ref, m2_ref, m3_ref, ic_ref, pw_ref, d_ref, pun_ref):
        i = pl.program_id(0)
        n = pl.num_programs(0)
        lane = lax.broadcasted_iota(jnp.int32, (1, 128), 1)
        icv = ic_ref[...]
        for g, m_ref in enumerate((m0_ref, m1_ref, m2_ref, m3_ref)):
            gs = slice(PG * g, PG * g + PG)
            halo = _halo_tokens(g)
            top = jnp.where(i > 0, prev_ref[PT - halo:PT, gs], 0.0)
            bot = jnp.where(i < n - 1, next_ref[0:halo, gs], 0.0)
            mid = cur_ref[:, gs]
            box = _split_dot(m_ref[...], jnp.concatenate([top, mid, bot], axis=0))
            dg = (box * _column(icv, lane, g) - mid).astype(BF16)
            d_ref[:, gs] = dg
            pun_ref[:, gs] = _dot(dg, pw_ref[g])

    return pl.pallas_call(
        body,
        name="pool_fwd",
        grid=(nt,),
        in_specs=[cur, prev, nxt] + [_resident(m.shape) for m in pmat] + [_tiles(PT, 128), _resident((4, PG, PG))],
        out_specs=[_tiles(PT, D), _tiles(PT, D)],
        out_shape=(jax.ShapeDtypeStruct((L, D), BF16), jax.ShapeDtypeStruct((L, D), F32)),
        compiler_params=_cp(),
    )(up, up, up, *pmat, icnt, pool_w)


def _gated(yf_ref, yb_ref, xs, z, vec_ref):
    ym = yf_ref[...] + yb_ref[...] + vec_ref[V_DSK:V_DSK + 1, :] * xs
    sz = _sigmoid(z)
    gated = ym * (z * sz)
    r = lax.rsqrt(jnp.mean(gated * gated, axis=-1, keepdims=True) + LN_EPS)
    return ym, sz, gated, r


def _merge_call(yf, yb, xbc, z, pun, xe, vec, w_out):
    L = z.shape[0]

    def body(yf_ref, yb_ref, xs_ref, z_ref, pun_ref, xe_ref, vec_ref, w_ref, x1_ref, mix_ref, cat_ref):
        _, _, gated, r = _gated(yf_ref, yb_ref, xs_ref[...], z_ref[...], vec_ref)
        yn = gated * r * vec_ref[V_SSDG:V_SSDG + 1, :]
        p = pun_ref[...] * vec_ref[V_PSC:V_PSC + 1, :]
        cat = jnp.concatenate([yn, p], axis=1).astype(BF16)
        mix = _dot(cat, w_ref[...])
        pre1 = ALPHA * xe_ref[...] + vec_ref[V_G1:V_G1 + 1, :] * mix
        x1, _, _ = _ln(pre1, vec_ref[V_LN1G:V_LN1G + 1, :], vec_ref[V_LN1B:V_LN1B + 1, :])
        x1_ref[...] = x1
        mix_ref[...] = mix
        cat_ref[...] = cat

    return pl.pallas_call(
        body,
        name="merge_fwd",
        grid=(L // TMH,),
        in_specs=[
            _tiles(TMH, D), _tiles(TMH, D), _tiles(TMH, D), _tiles(TMH, D), _tiles(TMH, D), _tiles(TMH, D),
            _resident((NV, D)), _resident((2 * D, D)),
        ],
        out_specs=[_tiles(TMH, D), _tiles(TMH, D), _tiles(TMH, 2 * D)],
        out_shape=(
            jax.ShapeDtypeStruct((L, D), F32),
            jax.ShapeDtypeStruct((L, D), F32),
            jax.ShapeDtypeStruct((L, 2 * D), BF16),
        ),
        compiler_params=_cp(),
    )(yf, yb, xbc, z, pun, xe, vec, w_out)


def _ffn_fwd_call(x1, tgt, vec, w_gate, w_up, w_down):
    L = x1.shape[0]

    def body(x1_ref, tgt_ref, vec_ref, wg_ref, wu_ref, wd_ref, dpre_ref, gacc_ref, gt_ref, up_ref):
        @pl.when(pl.program_id(0) == 0)
        def _():
            gacc_ref[...] = jnp.zeros_like(gacc_ref)

        x1 = x1_ref[...]
        h2 = (x1 * (1.0 + vec_ref[V_SC2:V_SC2 + 1, :]) + vec_ref[V_SH2:V_SH2 + 1, :]).astype(BF16)
        gt = _dot(h2, wg_ref[...])
        up = _dot(h2, wu_ref[...])
        gt_ref[...] = gt.astype(BF16)
        up_ref[...] = up.astype(BF16)
        f = (gt * _sigmoid(gt) * up).astype(BF16)
        ffn = _dot(f, wd_ref[...])
        g2 = vec_ref[V_G2:V_G2 + 1, :]
        lng = vec_ref[V_LN2G:V_LN2G + 1, :]
        x2, n2, rstd2 = _ln(ALPHA * x1 + g2 * ffn, lng, vec_ref[V_LN2B:V_LN2B + 1, :])
        diff = x2 - tgt_ref[...]
        dx2 = diff * (1.0 / D)
        dpre2 = _ln_bwd(dx2, n2, rstd2, lng)
        dpre_ref[...] = dpre2
        gacc_ref[V_LN2G:V_LN2G + 1, :] += _rowsum(dx2 * n2)
        gacc_ref[V_LN2B:V_LN2B + 1, :] += _rowsum(dx2)
        gacc_ref[V_G2:V_G2 + 1, :] += _rowsum(dpre2 * ffn)
        gacc_ref[V_LOSS:V_LOSS + 1, :] += jnp.sum(diff * diff) * (0.5 / D)

    return pl.pallas_call(
        body,
        name="ffn_fwd",
        grid=(L // TM,),
        in_specs=[_tiles(TM, D), _tiles(TM, D), _resident((NV, D)),
                  _resident((D, DFF)), _resident((D, DFF)), _resident((DFF, D))],
        out_specs=[_tiles(TM, D), _const_out((NV, D)), _tiles(TM, DFF), _tiles(TM, DFF)],
        out_shape=(jax.ShapeDtypeStruct((L, D), F32), jax.ShapeDtypeStruct((NV, D), F32),
                   jax.ShapeDtypeStruct((L, DFF), BF16), jax.ShapeDtypeStruct((L, DFF), BF16)),
        compiler_params=_cp(),
    )(x1, tgt, vec, w_gate, w_up, w_down)


def _ffn_bwd_call(x1, dpre2, gt_b, up_b, vec, w_gate, w_up, w_down):
    L = x1.shape[0]
    nt = L // TM
    nj = DFF // FFC

    def body(x1_ref, dpre_ref, gt_ref, up_ref, vec_ref, wg_ref, wu_ref, wd_ref, dh2_ref, dwg_ref, dwu_ref, dwd_ref,
             ag, au, ad):
        j = pl.program_id(0)
        i = pl.program_id(1)

        @pl.when(i == 0)
        def _():
            ag[...] = jnp.zeros_like(ag)
            au[...] = jnp.zeros_like(au)
            ad[...] = jnp.zeros_like(ad)

        h2 = (x1_ref[...] * (1.0 + vec_ref[V_SC2:V_SC2 + 1, :]) + vec_ref[V_SH2:V_SH2 + 1, :]).astype(BF16)
        gt = gt_ref[...].astype(F32)
        up = up_ref[...].astype(F32)
        sg = _sigmoid(gt)
        sl = gt * sg
        f = (sl * up).astype(BF16)
        dffn = (vec_ref[V_G2:V_G2 + 1, :] * dpre_ref[...]).astype(BF16)
        df = _dot_nt(dffn, wd_ref[...])
        dgt = (df * up * (sg * (1.0 + gt * (1.0 - sg)))).astype(BF16)
        dup = (df * sl).astype(BF16)
        dh2_ref[0] = _dot_nt(dgt, wg_ref[...]) + _dot_nt(dup, wu_ref[...])
        _acc_tn(ag, h2, dgt)
        _acc_tn(au, h2, dup)
        _acc_tn(ad, f, dffn)

        @pl.when(i == nt - 1)
        def _():
            pltpu.sync_copy(ag, dwg_ref.at[j])
            pltpu.sync_copy(au, dwu_ref.at[j])
            pltpu.sync_copy(ad, dwd_ref.at[j])

    any_spec = pl.BlockSpec(memory_space=pl.ANY)
    return pl.pallas_call(
        body,
        name="ffn_bwd",
        grid=(nj, nt),
        in_specs=[
            pl.BlockSpec((TM, D), lambda j, i: (i, 0)),
            pl.BlockSpec((TM, D), lambda j, i: (i, 0)),
            pl.BlockSpec((TM, FFC), lambda j, i: (i, j)),
            pl.BlockSpec((TM, FFC), lambda j, i: (i, j)),
            _resident((NV, D)),
            pl.BlockSpec((D, FFC), lambda j, i: (0, j)),
            pl.BlockSpec((D, FFC), lambda j, i: (0, j)),
            pl.BlockSpec((FFC, D), lambda j, i: (j, 0)),
        ],
        out_specs=[pl.BlockSpec((1, TM, D), lambda j, i: (j, i, 0)), any_spec, any_spec, any_spec],
        out_shape=(
            jax.ShapeDtypeStruct((nj, L, D), F32),
            jax.ShapeDtypeStruct((nj, D, FFC), F32),
            jax.ShapeDtypeStruct((nj, D, FFC), F32),
            jax.ShapeDtypeStruct((nj, FFC, D), F32),
        ),
        scratch_shapes=[pltpu.VMEM((D, FFC), F32), pltpu.VMEM((D, FFC), F32), pltpu.VMEM((FFC, D), F32)],
        compiler_params=_cp(2),
    )(x1, dpre2, gt_b, up_b, vec, w_gate, w_up, w_down)


def _merge_bwd_call(dh2p, dpre2, xe, mix, cat, yf, yb, xbc, z, dpool, pun, vec, w_out, pool_w, scattered):
    L = z.shape[0]
    nt = L // TM
    ns = len(scattered)

    def body(*refs):
        (dh2_ref, dpre2_ref, xe_ref, mix_ref, cat_ref, yf_ref, yb_ref, xs_ref, z_ref, dpool_ref, pun_ref,
         vec_ref, w_ref, pw_ref) = refs[:14]
        dxe_ref, dy_ref, dz_ref, dd_ref, dpw_ref, gacc_ref, dwo_ref = refs[14 + ns:21 + ns]
        dwo_s = refs[21 + 2 * ns]
        ex = _Exchange(refs[14:14 + ns], refs[21 + ns:21 + 2 * ns], *refs[22 + 2 * ns:], False)
        i = pl.program_id(0)

        @pl.when(i == 0)
        def _():
            ex.start()
            gacc_ref[...] = jnp.zeros_like(gacc_ref)
            dpw_ref[...] = jnp.zeros_like(dpw_ref)
            dwo_s[...] = jnp.zeros_like(dwo_s)

        def vrow(r):
            return vec_ref[r:r + 1, :]

        def gadd(r, val):
            gacc_ref[r:r + 1, :] += _rowsum(val)

        dh2 = dh2_ref[0] + dh2_ref[1]
        dx1 = ALPHA * dpre2_ref[...] + dh2 * (1.0 + vrow(V_SC2))
        mix = mix_ref[...]
        x1, n1, rstd1 = _ln(ALPHA * xe_ref[...] + vrow(V_G1) * mix, vrow(V_LN1G), vrow(V_LN1B))
        gadd(V_SC2, dh2 * x1)
        gadd(V_SH2, dh2)
        gadd(V_LN1G, dx1 * n1)
        gadd(V_LN1B, dx1)
        dpre1 = _ln_bwd(dx1, n1, rstd1, vrow(V_LN1G))
        dxe_ref[...] = ALPHA * dpre1
        gadd(V_G1, dpre1 * mix)
        dmix = (vrow(V_G1) * dpre1).astype(BF16)
        dcat = _dot_nt(dmix, w_ref[...])
        cat = cat_ref[...]
        for c0 in range(0, 2 * D, 512):
            dwo_s[c0:c0 + 512, :] += _dot_tn(cat[:, c0:c0 + 512], dmix)
        dyn = dcat[:, 0:D]
        dp = dcat[:, D:2 * D]
        xs = xs_ref[...]
        z = z_ref[...]
        ym, sz, gated, r = _gated(yf_ref, yb_ref, xs, z, vec_ref)
        gadd(V_SSDG, dyn * gated * r)
        a = dyn * vrow(V_SSDG)
        dgated = r * a - gated * (r * r * r * jnp.mean(a * gated, axis=-1, keepdims=True))
        dym = dgated * (z * sz)
        dy_ref[...] = dym
        dz_ref[...] = dgated * ym * (sz * (1.0 + z * (1.0 - sz)))
        gadd(V_DSK, dym * xs)
        gadd(V_PSC, dp * pun_ref[...])
        dps = (dp * vrow(V_PSC)).astype(BF16)
        dpool = dpool_ref[...]
        for g in range(4):
            gs = slice(PG * g, PG * g + PG)
            dd_ref[:, gs] = _dot_nt(dps[:, gs], pw_ref[g])
            dpw_ref[g] += _dot_tn(dpool[:, gs], dps[:, gs])

        @pl.when(i == nt - 1)
        def _():
            pltpu.sync_copy(dwo_s, dwo_ref)
            ex.wait()

    any_spec = pl.BlockSpec(memory_space=pl.ANY)
    return pl.pallas_call(
        body,
        name="merge_bwd",
        grid=(nt,),
        in_specs=[
            pl.BlockSpec((2, TM, D), lambda i: (0, i, 0)),
            _tiles(TM, D), _tiles(TM, D), _tiles(TM, D), _tiles(TM, 2 * D),
            _tiles(TM, D), _tiles(TM, D),
            _tiles(TM, D), _tiles(TM, D), _tiles(TM, D), _tiles(TM, D),
            _resident((NV, D)), _resident((2 * D, D)), _resident((4, PG, PG)),
        ] + [any_spec] * ns,
        out_specs=[_tiles(TM, D), _tiles(TM, D), _tiles(TM, D), _tiles(TM, D),
                   _const_out((4, PG, PG)), _const_out((NV, D)), any_spec] + [any_spec] * ns,
        out_shape=[
            jax.ShapeDtypeStruct((L, D), F32),
            jax.ShapeDtypeStruct((L, D), F32),
            jax.ShapeDtypeStruct((L, D), F32),
            jax.ShapeDtypeStruct((L, D), F32),
            jax.ShapeDtypeStruct((4, PG, PG), F32),
            jax.ShapeDtypeStruct((NV, D), F32),
            jax.ShapeDtypeStruct((2 * D, D), F32),
        ] + _exchange_out_shapes(scattered, False),
        scratch_shapes=[pltpu.VMEM((2 * D, D), F32)] + _exchange_sems(ns),
        compiler_params=_cp(),
    )(dh2p, dpre2, xe, mix, cat, yf, yb, xbc, z, dpool, pun, vec, w_out, pool_w, *scattered)


def _pool_bwd_call(dd, pmat_t, icnt):
    L = dd.shape[0]
    nt = L // PT
    cur, prev, nxt = _pool_specs(nt)
    icur = pl.BlockSpec((PT, 128), lambda i: (i, 0))
    iprev = pl.BlockSpec((PT, 128), lambda i: (jnp.maximum(i - 1, 0), 0))
    inxt = pl.BlockSpec((PT, 128), lambda i: (jnp.minimum(i + 1, nt - 1), 0))

    def body(cur_ref, prev_ref, next_ref, ic_ref, icp_ref, icn_ref, m0_ref, m1_ref, m2_ref, m3_ref, du_ref):
        i = pl.program_id(0)
        n = pl.num_programs(0)
        lane = lax.broadcasted_iota(jnp.int32, (1, 128), 1)
        icv = ic_ref[...]
        for g, m_ref in enumerate((m0_ref, m1_ref, m2_ref, m3_ref)):
            gs = slice(PG * g, PG * g + PG)
            halo = _halo_tokens(g)
            icp = _column(icp_ref[PT - halo:PT, :], lane, g)
            icn = _column(icn_ref[0:halo, :], lane, g)
            top = jnp.where(i > 0, prev_ref[PT - halo:PT, gs] * icp, 0.0)
            bot = jnp.where(i < n - 1, next_ref[0:halo, gs] * icn, 0.0)
            mid = cur_ref[:, gs]
            ext = jnp.concatenate([top, mid * _column(icv, lane, g), bot], axis=0)
            du_ref[:, gs] = _split_dot(m_ref[...], ext) - mid

    return pl.pallas_call(
        body,
        name="pool_bwd",
        grid=(nt,),
        in_specs=[cur, prev, nxt, icur, iprev, inxt] + [_resident(m.shape) for m in pmat_t],
        out_specs=_tiles(PT, D),
        out_shape=jax.ShapeDtypeStruct((L, D), F32),
        compiler_params=_cp(),
    )(dd, dd, dd, icnt, icnt, icnt, *pmat_t)


def _ssd_bwd_call(dy, xbc, dtr, hprev_f, hprev_b, dh_init, par, e_mat, et_mat, dskip, name):
    L = xbc.shape[0]
    nc = L // Q

    def body(dy0_ref, dy1_ref, xbc0_ref, xbc1_ref, dtr0_ref, dtr1_ref, hp0_ref, hp1_ref, dhi_ref, par_ref, e_ref,
             et_ref, dsk_ref, dx0_ref, dx1_ref, ddt0_ref, ddt1_ref, acc_ref, dh0_ref, dh_s, AT, dtT, ddtT_s, dAT_s):
        s = pl.program_id(0)

        @pl.when(s == 0)
        def _():
            dh_s[...] = dhi_ref[...]
            acc_ref[...] = jnp.zeros_like(acc_ref)
            ddtT_s[...] = jnp.zeros_like(ddtT_s)
            dAT_s[...] = jnp.zeros_like(dAT_s)

        one_direction(0, dy0_ref, xbc0_ref, dtr0_ref, hp0_ref, par_ref, e_ref, et_ref, dsk_ref, dx0_ref, ddt0_ref,
                      acc_ref, dh0_ref, dh_s, AT, dtT, ddtT_s, dAT_s)
        one_direction(1, dy1_ref, xbc1_ref, dtr1_ref, hp1_ref, par_ref, e_ref, et_ref, dsk_ref, dx1_ref, ddt1_ref,
                      acc_ref, dh0_ref, dh_s, AT, dtT, ddtT_s, dAT_s)

    def one_direction(d, dy_ref, xbc_ref, dtr_ref, hp_ref, par_ref, e_ref, et_ref, dsk_ref, dxbc_ref, ddtr_ref,
                      acc_ref, dh0_ref, dh_s, AT_s, dtT_s, ddtT_s, dAT_s):
        q = _ssd_common(d, dtr_ref[...], par_ref)
        A, maskf, lane, dt, atot = q["A"], q["maskf"], q["lane"], q["dt"], q["atot"]
        AT_s[d] = q["AT"]
        dtT_s[d] = q["dtT"]
        AT, dtT = AT_s.at[d], dtT_s.at[d]
        hprev = hp_ref[0]
        hpb = hprev.astype(BF16)
        dh = dh_s[d]
        dhb = dh.astype(BF16)
        xs = xbc_ref[:, 0:1024]
        bb = xbc_ref[:, 1024:1152].astype(BF16)
        cb = xbc_ref[:, 1152:1280].astype(BF16)
        dy = dy_ref[...]
        ea_f = _head_expand(e_ref, jnp.exp(A))
        ch = _dot_nt(cb, hpb)
        dch = (dy * ea_f).astype(BF16)
        dC = _dot(dch, hpb)
        dhprev = _dot_tn(dch, cb)
        dA = _head_sum(et_ref, dy * ch * ea_f)
        dec = _state_decay(atot, lane)
        dhprev = dhprev + dh * dec
        dhh = dh * hprev * dec
        datot = jnp.zeros((1, 128), F32)
        for h in range(NH):
            tot_h = jnp.sum(_rowsum(dhh[HP * h:HP * h + HP, :]), axis=1, keepdims=True)
            datot = datot + jnp.where(lane == h, tot_h, 0.0)
        ear = jnp.exp(atot - A)
        wend = ear * dt
        wf = _head_expand(e_ref, wend)
        xw = (xs * wf).astype(BF16)
        dxw = _dot_nt(bb, dhb)
        dB = _dot(xw, dhb)
        dxs = dxw * wf
        dwend = _head_sum(et_ref, dxw * xs)
        ddt = dwend * ear
        de = dwend * wend
        datot = datot + _rowsum(de)
        dA = dA - de
        g = _dot_nt(cb, bb)
        dG = jnp.zeros((Q, Q), F32)
        for k in range(NH // 2):
            ks = slice(128 * k, 128 * k + 128)
            xp = xs[:, ks]
            dyp = dy[:, ks]
            accdx = dxs[:, ks]
            if d == 0:
                accdx = accdx + dyp * dsk_ref[:, ks]
            scts, dyhs = [], []
            for half in range(2):
                h = 2 * k + half
                inhead = (lane >= HP) if half else (lane < HP)
                seg = _column(A, lane, h) - AT[h:h + 1, :]
                lm = jnp.exp(jnp.minimum(seg, 0.0)) * maskf
                dtrow = dtT[h:h + 1, :]
                gl = g * lm
                sc = gl * dtrow
                dyh = jnp.where(inhead, dyp, 0.0).astype(BF16)
                xh = jnp.where(inhead, xp, 0.0).astype(BF16)
                dS = _dot_nt(dyh, xh)
                scts.append(sc.T.astype(BF16))
                dyhs.append(dyh)
                nn = dS * gl
                cn = _rowsum(nn)
                rm = jnp.sum(nn * dtrow, axis=1, keepdims=True)
                dG = dG + dS * (lm * dtrow)
                ddtT_s[d, h:h + 1, :] = cn
                dAT_s[d, h:h + 1, :] = -(cn * dtrow)
                dA = dA + rm * (lane == h).astype(F32)
            dxbc_ref[:, ks] = accdx + _dot(jnp.concatenate(scts, axis=1), jnp.concatenate(dyhs, axis=0))
        dGb = dG.astype(BF16)
        dxbc_ref[:, 1024:1152] = dB + _dot_tn(dGb, cb)
        dxbc_ref[:, 1152:1280] = dC + _dot(dGb, bb)
        da = _tri_dot(q["mask_t"], dA + dAT_s[d].T) + datot
        ddt = ddt + ddtT_s[d].T + da * q["aneg"]
        ddtr = jnp.where(q["hmask"], ddt * _sigmoid(q["pre"]), 0.0)
        ddtr_ref[...] = ddtr
        acc_ref[d, 0:1, :] += _rowsum(ddtr)
        acc_ref[d, 1:2, :] += _rowsum(da * dt) * q["aneg"]
        dh_s[d] = dhprev
        dh0_ref[d] = dhprev

    def back(s):
        return nc - 1 - s

    return pl.pallas_call(
        body,
        name=name,
        grid=(nc,),
        in_specs=[
            pl.BlockSpec((Q, D), lambda s: (back(s), 0)),
            pl.BlockSpec((Q, D), lambda s: (s, 0)),
            pl.BlockSpec((Q, DXBC), lambda s: (back(s), 0)),
            pl.BlockSpec((Q, DXBC), lambda s: (s, 0)),
            pl.BlockSpec((Q, 128), lambda s: (back(s), 0)),
            pl.BlockSpec((Q, 128), lambda s: (s, 1)),
            pl.BlockSpec((1, D, NS), lambda s: (back(s), 0, 0)),
            pl.BlockSpec((1, D, NS), lambda s: (s, 0, 0)),
            _resident((2, D, NS)),
            _resident((8, 128)),
            _resident((256, D)),
            _resident((D, 256)),
            _resident((1, D)),
        ],
        out_specs=[
            pl.BlockSpec((Q, DXBC), lambda s: (back(s), 0)),
            pl.BlockSpec((Q, DXBC), lambda s: (s, 0)),
            pl.BlockSpec((Q, 128), lambda s: (back(s), 0)),
            pl.BlockSpec((Q, 128), lambda s: (s, 0)),
            _const_out((2, 8, 128)),
            _const_out((2, D, NS)),
        ],
        out_shape=(
            jax.ShapeDtypeStruct((L, DXBC), F32),
            jax.ShapeDtypeStruct((L, DXBC), F32),
            jax.ShapeDtypeStruct((L, 128), F32),
            jax.ShapeDtypeStruct((L, 128), F32),
            jax.ShapeDtypeStruct((2, 8, 128), F32),
            jax.ShapeDtypeStruct((2, D, NS), F32),
        ),
        scratch_shapes=[pltpu.VMEM((2, D, NS), F32)] + [pltpu.VMEM((2, 128, Q), F32)] * 4,
        compiler_params=_cp(),
    )(dy, dy, xbc, xbc, dtr, dtr, hprev_f, hprev_b, dh_init, par, e_mat, et_mat, dskip)


def _conv_bwd_call(dxf, dxb, pre, xbc_raw, acc_init, name, scattered=()):
    L = xbc_raw.shape[0]
    nt = L // TM
    ns = len(scattered)
    prev, nxt = _halo_specs(TM, DXBC, L)

    def body(*refs):
        dxf_ref, dxb_ref, pre_ref, cur_ref, prev_ref, next_ref, init_ref = refs[:7]
        dpre_ref, acc_ref = refs[7 + ns:9 + ns]
        ext = refs[9 + 2 * ns]
        if ns:
            ex = _Exchange(refs[7:7 + ns], refs[9 + ns:9 + 2 * ns], *refs[10 + 2 * ns:], False)

        @pl.when(pl.program_id(0) == 0)
        def _():
            if ns:
                ex.start()
            acc_ref[...] = init_ref[...]

        _extended(ext, cur_ref, prev_ref, next_ref)
        pre = pre_ref[...]
        sg = _sigmoid(pre)
        dpre = (dxf_ref[...] + dxb_ref[...]) * (sg * (1.0 + pre * (1.0 - sg)))
        dpre_ref[...] = dpre
        for k in range(5):
            acc_ref[k:k + 1, :] += _rowsum(dpre * _shifted(ext, k - 2, TM))
        acc_ref[5:6, :] += _rowsum(dpre)
        if ns:
            @pl.when(pl.program_id(0) == nt - 1)
            def _():
                ex.wait()

    any_spec = pl.BlockSpec(memory_space=pl.ANY)
    return pl.pallas_call(
        body,
        name=name,
        grid=(nt,),
        in_specs=[_tiles(TM, DXBC), _tiles(TM, DXBC), _tiles(TM, DXBC), _tiles(TM, DXBC), prev, nxt,
                  _resident((8, DXBC))] + [any_spec] * ns,
        out_specs=[_tiles(TM, DXBC), _const_out((8, DXBC))] + [any_spec] * ns,
        out_shape=[jax.ShapeDtypeStruct((L, DXBC), F32), jax.ShapeDtypeStruct((8, DXBC), F32)]
        + _exchange_out_shapes(scattered, False),
        scratch_shapes=[pltpu.VMEM((TM + 16, DXBC), F32)] + (_exchange_sems(ns) if ns else []),
        compiler_params=_cp(),
    )(dxf, dxb, pre, xbc_raw, xbc_raw, xbc_raw, acc_init, *scattered)


def _inproj_bwd_call(dpre, cw, dz, ddt0, ddt1, dup, h1, dxe_part, x0, vec, w_in, dw_init, name):
    L = x0.shape[0]
    nt = L // TM
    prev, nxt = _halo_specs(TM, DXBC, L)

    def body(cur_ref, prev_ref, next_ref, cw_ref, dz_ref, ddt0_ref, ddt1_ref, dup_ref, h1_ref, dxe_ref, x0_ref,
             vec_ref, w_ref, dwi_ref, gx_ref, gacc_ref, dw_ref, dw_s, ext):
        i = pl.program_id(0)

        @pl.when(i == 0)
        def _():
            gacc_ref[...] = jnp.zeros_like(gacc_ref)
            pltpu.sync_copy(dwi_ref, dw_s)

        def vrow(r):
            return vec_ref[r:r + 1, :]

        _extended(ext, cur_ref, prev_ref, next_ref)
        dxr = cw_ref[0:1, :] * _shifted(ext, 2, TM)
        for k in range(1, 5):
            dxr = dxr + cw_ref[k:k + 1, :] * _shifted(ext, 2 - k, TM)
        dproj = jnp.concatenate([dz_ref[...], dxr, ddt0_ref[...], ddt1_ref[...], dup_ref[...]], axis=1).astype(BF16)
        dh1 = _dot_nt(dproj, w_ref[...])
        _acc_tn(dw_s, h1_ref[...], dproj)
        xe, n0, rstd0 = _ln(x0_ref[...], vrow(V_EMBG), vrow(V_EMBB))
        dxe = dxe_ref[...] + dh1 * (1.0 + vrow(V_SC1))
        gacc_ref[V_SC1:V_SC1 + 1, :] += _rowsum(dh1 * xe)
        gacc_ref[V_SH1:V_SH1 + 1, :] += _rowsum(dh1)
        gacc_ref[V_EMBG:V_EMBG + 1, :] += _rowsum(dxe * n0)
        gacc_ref[V_EMBB:V_EMBB + 1, :] += _rowsum(dxe)
        gx_ref[...] = _ln_bwd(dxe, n0, rstd0, vrow(V_EMBG))

        @pl.when(i == nt - 1)
        def _():
            pltpu.sync_copy(dw_s, dw_ref)

    any_spec = pl.BlockSpec(memory_space=pl.ANY)
    return pl.pallas_call(
        body,
        name=name,
        grid=(nt,),
        in_specs=[_tiles(TM, DXBC), prev, nxt, _resident((8, DXBC)), _tiles(TM, D), _tiles(TM, 128), _tiles(TM, 128),
                  _tiles(TM, D), _tiles(TM, D), _tiles(TM, D), _tiles(TM, D), _resident((NV, D)), _resident((D, WIN)),
                  any_spec],
        out_specs=[_tiles(TM, D), _const_out((NV, D)), any_spec],
        out_shape=(
            jax.ShapeDtypeStruct((L, D), F32),
            jax.ShapeDtypeStruct((NV, D), F32),
            jax.ShapeDtypeStruct((D, WIN), F32),
        ),
        scratch_shapes=[pltpu.VMEM((D, WIN), F32), pltpu.VMEM((TM + 16, DXBC), F32)],
        compiler_params=_cp(),
    )(dpre, dpre, dpre, cw, dz, ddt0, ddt1, dup, h1, dxe_part, x0, vec, w_in, dw_init)


def _adamw(w, g, m, v):
    m = ADAM_B1 * m + (1.0 - ADAM_B1) * g
    v = ADAM_B2 * v + (1.0 - ADAM_B2) * (g * g)
    m_hat = m / (1.0 - ADAM_B1 ** ADAM_STEP)
    v_hat = v / (1.0 - ADAM_B2 ** ADAM_STEP)
    delta = -ADAM_LR * (m_hat / (jnp.sqrt(v_hat) + ADAM_EPS) + ADAM_WD * w)
    return delta, m, v


def _adamw_shard_call(gslots, w, m, v, tr, name):
    R, C = w.shape

    def body(gs_ref, w_ref, m_ref, v_ref, g_ref, d_ref, mo_ref, vo_ref):
        g = gs_ref[0].astype(F32)
        for i in range(1, NDEV):
            g = g + gs_ref[i].astype(F32)
        delta, mn, vn = _adamw(w_ref[...], g, m_ref[...], v_ref[...])
        g_ref[...] = g
        d_ref[...] = delta
        mo_ref[...] = mn
        vo_ref[...] = vn

    t = _tiles(tr, C)
    return pl.pallas_call(
        body,
        name=name,
        grid=(R // tr,),
        in_specs=[pl.BlockSpec((NDEV, tr, C), lambda i: (0, i, 0)), t, t, t],
        out_specs=[t, t, t, t],
        out_shape=tuple(jax.ShapeDtypeStruct((R, C), F32) for _ in range(4)),
        compiler_params=_cp(),
    )(gslots, w, m, v)


def _wada_call(dm_ex, dm_ctx, silu_all, w, m, v):
    ncol = w.shape[1]

    def body(dme_ref, dmc_ref, s_ref, w_ref, m_ref, v_ref, g_ref, d_ref, mo_ref, vo_ref, ds_ref):
        dmc = _rowsum(dmc_ref[...])
        rows = lax.broadcasted_iota(jnp.int32, (8, 1), 0)
        low = jnp.where(rows == 0, dmc, 0.0)
        dm = jnp.concatenate([dme_ref[...], low], axis=0).astype(BF16)
        wv = w_ref[...]
        g = _dot_tn(s_ref[...].astype(BF16), dm)
        delta, mn, vn = _adamw(wv, g, m_ref[...], v_ref[...])
        g_ref[...] = g
        d_ref[...] = delta
        mo_ref[...] = mn
        vo_ref[...] = vn
        ds_ref[...] = _dot_nt(low.astype(BF16), wv.astype(BF16))

    return pl.pallas_call(
        body,
        name="wada_update",
        out_shape=tuple(jax.ShapeDtypeStruct((D, ncol), F32) for _ in range(4)) + (jax.ShapeDtypeStruct((8, D), F32),),
        compiler_params=pltpu.CompilerParams(vmem_limit_bytes=VMEM_LIMIT),
    )(dm_ex, dm_ctx, silu_all, w, m, v)


P_DMOD, P_DMODC, P_EMBG, P_EMBB, P_LN1G, P_LN1B, P_LN2G, P_LN2B = 0, 6, 8, 9, 10, 11, 12, 13
P_SSDG, P_PSC, P_DSK, P_CONVB, P_DTB, P_ALOG, P_LOSS, NP = 14, 15, 16, 17, 19, 20, 21, 24
S_CCTX, S_EMBG, S_EMBB, S_BADA, S_CONVB, S_DTB, S_ALOG, S_DSK = 0, 1, 2, 3, 9, 11, 12, 13
S_SSDG, S_PSC, S_LN1G, S_LN1B, S_LN2G, S_LN2B, NSM = 14, 15, 16, 17, 18, 19, 24


def _small_update_call(pall, dsil, cctx, w, m, v, et_mat):
    def body(p_ref, ds_ref, c_ref, w_ref, m_ref, v_ref, et_ref, g_ref, d_ref, mo_ref, vo_ref, loss_ref,
             tot, dsum, dsk8):
        tot[...] = p_ref[0]
        dsum[...] = ds_ref[0]
        for i in range(1, NDEV):
            tot[...] += p_ref[i]
            dsum[...] += ds_ref[i]
        cv = c_ref[...]
        sc = _sigmoid(cv)
        g_ref[...] = jnp.zeros_like(g_ref)
        g_ref[S_CCTX:S_CCTX + 1, :] = dsum[0:1, :] * (sc * (1.0 + cv * (1.0 - sc)))
        g_ref[S_EMBG:S_EMBG + 1, :] = tot[P_EMBG:P_EMBG + 1, :]
        g_ref[S_EMBB:S_EMBB + 1, :] = tot[P_EMBB:P_EMBB + 1, :]
        g_ref[S_BADA:S_BADA + 2, :] = tot[P_DMOD:P_DMOD + 2, :] + tot[P_DMODC:P_DMODC + 2, :]
        g_ref[S_BADA + 2:S_BADA + 6, :] = tot[P_DMOD + 2:P_DMOD + 6, :]
        g_ref[S_CONVB:S_CONVB + 2, :] = tot[P_CONVB:P_CONVB + 2, :]
        g_ref[S_DTB:S_DTB + 1, :] = tot[P_DTB:P_DTB + 1, :]
        g_ref[S_ALOG:S_ALOG + 1, :] = tot[P_ALOG:P_ALOG + 1, :]
        dsk8[...] = _dot(jnp.broadcast_to(tot[P_DSK:P_DSK + 1, :], (8, D)), et_ref[:, 0:128].astype(F32), HI)
        g_ref[S_DSK:S_DSK + 1, 0:128] = dsk8[0:1, :]
        g_ref[S_SSDG:S_SSDG + 1, :] = tot[P_SSDG:P_SSDG + 1, :]
        g_ref[S_PSC:S_PSC + 1, :] = tot[P_PSC:P_PSC + 1, :]
        g_ref[S_LN1G:S_LN1G + 1, :] = tot[P_LN1G:P_LN1G + 1, :]
        g_ref[S_LN1B:S_LN1B + 1, :] = tot[P_LN1B:P_LN1B + 1, :]
        g_ref[S_LN2G:S_LN2G + 1, :] = tot[P_LN2G:P_LN2G + 1, :]
        g_ref[S_LN2B:S_LN2B + 1, :] = tot[P_LN2B:P_LN2B + 1, :]
        delta, mn, vn = _adamw(w_ref[...], g_ref[...], m_ref[...], v_ref[...])
        d_ref[...] = delta
        mo_ref[...] = mn
        vo_ref[...] = vn
        loss_ref[...] = jnp.broadcast_to(tot[P_LOSS:P_LOSS + 1, 0:128], (8, 128))

    return pl.pallas_call(
        body,
        name="small_update",
        out_shape=tuple(jax.ShapeDtypeStruct((NSM, D), F32) for _ in range(4)) + (jax.ShapeDtypeStruct((8, 128), F32),),
        scratch_shapes=[pltpu.VMEM((NP, D), F32), pltpu.VMEM((8, D), F32), pltpu.VMEM((8, 128), F32)],
        compiler_params=pltpu.CompilerParams(vmem_limit_bytes=VMEM_LIMIT),
    )(pall, dsil, cctx, w, m, v, et_mat)


def _pad_rows(flat, mult=16):
    n = flat.shape[0]
    rows = -(-n // D)
    rows = -(-rows // mult) * mult
    return jnp.pad(flat, (0, rows * D - n)).reshape(rows, D)


def _by_cols(dw):
    r = dw.shape[0]
    return jnp.transpose(dw.reshape(r, NDEV, -1), (1, 0, 2))


def _from_cols(g):
    return jnp.transpose(g, (1, 0, 2)).reshape(g.shape[1], -1)


def _pool_constants(L):
    rows = L // GW
    t_r = jnp.arange(PT) // GW
    t_c = jnp.arange(PT) % GW
    fw, bw, ic = [], [], []
    pos_r = jnp.arange(L) // GW
    pos_c = jnp.arange(L) % GW
    for g, w in enumerate(WINDOWS):
        lo, hi = -(w // 2), w - w // 2 - 1
        n_ext = PT + 2 * _halo_tokens(g)
        e_r = jnp.arange(n_ext) // GW - w // 2
        e_c = jnp.arange(n_ext) % GW
        dr = e_r[None, :] - t_r[:, None]
        dc = e_c[None, :] - t_c[:, None]
        fw.append(((dr >= lo) & (dr <= hi) & (dc >= lo) & (dc <= hi)).astype(BF16))
        bw.append(((-dr >= lo) & (-dr <= hi) & (-dc >= lo) & (-dc <= hi)).astype(BF16))
        cr = jnp.minimum(pos_r + hi, rows - 1) - jnp.maximum(pos_r + lo, 0) + 1
        cc = jnp.minimum(pos_c + hi, GW - 1) - jnp.maximum(pos_c + lo, 0) + 1
        ic.append(1.0 / (cr * cc).astype(F32))
    icnt = jnp.pad(jnp.stack(ic, axis=1), ((0, 0), (0, 124)))
    return fw, bw, icnt


def _head_matrices():
    hp = jnp.arange(D) // HP
    e = (jnp.arange(128)[:, None] == hp[None, :]).astype(BF16)
    return jnp.concatenate([e, e], axis=0), jnp.concatenate([e.T, e.T], axis=1)


def _aligned_in_proj(w):
    zpad = jnp.zeros((D, 128 - NH), w.dtype)
    return jnp.concatenate([w[:, 0:2304], w[:, 2304:2320], zpad, w[:, 2320:2336], zpad, w[:, 2336:3360]], axis=1)


def _unaligned_in_proj(dw):
    return jnp.concatenate([dw[:, 0:2304], dw[:, 2304:2320], dw[:, 2432:2448], dw[:, 2560:3584]], axis=1)


def _row(v):
    return v.reshape(1, -1).astype(F32)


def _pad_lanes(v, width=D):
    v = v.reshape(1, -1)
    return jnp.pad(v, ((0, 0), (0, width - v.shape[1])))


def kernel(x, c, ctx, c_ctx, emb_ln_g, emb_ln_b, w_ada, b_ada, in_proj, conv_w, conv_b, dt_bias, a_log, d_skip, ssd_norm_g, pool_w, pool_scale, w_out, ln1_g, ln1_b, w_gate, w_up, w_down, ln2_g, ln2_b, loss_target, m_c_ctx, m_emb_ln_g, m_emb_ln_b, m_w_ada, m_b_ada, m_in_proj, m_conv_w, m_conv_b, m_dt_bias, m_a_log, m_d_skip, m_ssd_norm_g, m_pool_w, m_pool_scale, m_w_out, m_ln1_g, m_ln1_b, m_w_gate, m_w_up, m_w_down, m_ln2_g, m_ln2_b, v_c_ctx, v_emb_ln_g, v_emb_ln_b, v_w_ada, v_b_ada, v_in_proj, v_conv_w, v_conv_b, v_dt_bias, v_a_log, v_d_skip, v_ssd_norm_g, v_pool_w, v_pool_scale, v_w_out, v_ln1_g, v_ln1_b, v_w_gate, v_w_up, v_w_down, v_ln2_g, v_ln2_b):
    me = 4 * lax.axis_index("x") + 2 * lax.axis_index("y") + lax.axis_index("c")
    x0 = x[0]
    ctx0 = ctx[0]
    tgt = loss_target[0]
    L = x0.shape[0]
    LC = ctx0.shape[0]
    ncol_ada = w_ada.shape[2]

    small_in = jnp.concatenate([c.reshape(-1), conv_w.reshape(-1)])
    gb = jnp.concatenate([_row(emb_ln_g), _row(emb_ln_b), jnp.zeros((6, D), F32)], axis=0)
    xe, small_all, g_inp = _emb_ln_call(x0, gb, "emb_ln", [_pad_rows(small_in, 8), in_proj[0].astype(BF16)])
    (xe_c,) = _emb_ln_call(ctx0, gb, "emb_ln_ctx")
    c_all = small_all[:, 0, :]
    convw_all = small_all.reshape(NDEV, -1)[:, D:D + 5 * (DXBC // NDEV)].reshape(NDEV, 5, DXBC // NDEV)
    conv_w_full = _from_cols(convw_all)
    w_in = _aligned_in_proj(_from_cols(g_inp))
    late_shards = [pool_w[0].astype(BF16), w_out[0].astype(BF16), w_gate[0].astype(BF16), w_up[0].astype(BF16),
                   w_down[0].astype(BF16)]

    c_in = jnp.concatenate([c_all, c_ctx.reshape(1, D), jnp.zeros((7, D), F32)], axis=0)
    b_mine = lax.dynamic_slice(b_ada, (0, me * ncol_ada), (1, ncol_ada))
    silu_all, mod_mine = _mod_call(c_in, w_ada[0], b_mine)
    (mod_all,) = _exchange([mod_mine], "gather_mod", True)
    mod_all = _from_cols(mod_all)
    mod_me = lax.dynamic_slice(mod_all, (me, 0), (1, 6 * D)).reshape(6, D)
    mod_ctx = mod_all[8].reshape(6, D)

    tail = jnp.concatenate([
        _row(emb_ln_g), _row(emb_ln_b), _row(ln1_g), _row(ln1_b), _row(ln2_g), _row(ln2_b),
        _row(ssd_norm_g), _row(pool_scale), _row(jnp.repeat(d_skip.reshape(-1), HP)), jnp.zeros((1, D), F32)], axis=0)
    vec = jnp.concatenate([mod_me, tail], axis=0)
    vec_ctx = jnp.concatenate([mod_ctx, tail], axis=0)

    cw = jnp.concatenate([conv_w_full, conv_b.reshape(1, DXBC), jnp.zeros((2, DXBC), F32)], axis=0)
    par = jnp.concatenate([_pad_lanes(dt_bias[0, 0], 128), _pad_lanes(dt_bias[0, 1], 128),
                           _pad_lanes(a_log[0, 0], 128), _pad_lanes(a_log[0, 1], 128),
                           jnp.zeros((4, 128), F32)], axis=0)
    e_mat, et_mat = _head_matrices()
    pmat, pmat_t, icnt = _pool_constants(L)
    dskip_row = vec[V_DSK:V_DSK + 1]

    h1_c, _, xbcr_c, pre_c, xbc_c, dtr_c, _ = _f1_call(xe_c, vec_ctx, w_in, cw, "inproj_fwd_ctx")
    hzero = jnp.zeros((2, D, NS), F32)
    _, _, hpf_c, hpb_c, hfin_c = _ssd_fwd_call(xbc_c, dtr_c, hzero, par, e_mat, et_mat, "ssd_fwd_ctx")

    h1, z, xbcr, pre, xbc, dtr, up = _f1_call(xe, vec, w_in, cw, "inproj_fwd")
    yf, yb, hpf, hpb, _, g_pw, g_wo, g_wg, g_wu, g_wd = _ssd_fwd_call(xbc, dtr, hfin_c, par, e_mat, et_mat, "ssd_fwd",
                                                                       late_shards)
    pool_w_full = jnp.transpose(g_pw, (1, 0, 2, 3)).reshape(4, PG, PG)
    w_out_full = g_wo.reshape(2 * D, D)
    w_gate_full = _from_cols(g_wg)
    w_up_full = _from_cols(g_wu)
    w_down_full = g_wd.reshape(DFF, D)
    dpool, pun = _pool_fwd_call(up, pmat, icnt, pool_w_full)
    x1, mix, cat = _merge_call(yf, yb, xbc, z, pun, xe, vec, w_out_full)
    dpre2, gacc_f, gt_b, up_b = _ffn_fwd_call(x1, tgt, vec, w_gate_full, w_up_full, w_down_full)

    dh2p, dwg2, dwu2, dwd2 = _ffn_bwd_call(x1, dpre2, gt_b, up_b, vec, w_gate_full, w_up_full, w_down_full)
    nq = FFC // (DFF // NDEV)
    ffn_parts = [
        jnp.transpose(dwg2.reshape(-1, D, nq, DFF // NDEV), (0, 2, 1, 3)).reshape(NDEV, D, DFF // NDEV),
        jnp.transpose(dwu2.reshape(-1, D, nq, DFF // NDEV), (0, 2, 1, 3)).reshape(NDEV, D, DFF // NDEV),
        dwd2.reshape(NDEV, DFF // NDEV, D)]
    dxe_part, dy, dz, dd, dpw, gacc_m, dwo, gs_wg, gs_wu, gs_wd = _merge_bwd_call(
        dh2p, dpre2, xe, mix, cat, yf, yb, xbc, z, dpool, pun, vec, w_out_full, pool_w_full, ffn_parts)
    dup = _pool_bwd_call(dd, pmat_t, icnt)
    dxf, dxb, ddt0, ddt1, sacc, dh0 = _ssd_bwd_call(dy, xbc, dtr, hpf, hpb, hzero, par, e_mat, et_mat, dskip_row,
                                                     "ssd_bwd")
    zeros_c = jnp.zeros((LC, D), F32)
    dxf_c, dxb_c, ddt0_c, ddt1_c, sacc_c, _ = _ssd_bwd_call(zeros_c, xbc_c, dtr_c, hpf_c, hpb_c, dh0, par, e_mat, et_mat,
                                                            jnp.zeros((1, D), F32), "ssd_bwd_ctx")
    dprec_c, cacc_c = _conv_bwd_call(dxf_c, dxb_c, pre_c, xbcr_c, jnp.zeros((8, DXBC), F32), "conv_bwd_ctx")
    _, gacc_c, dwin_c = _inproj_bwd_call(dprec_c, cw, zeros_c, ddt0_c, ddt1_c, zeros_c, h1_c, zeros_c, ctx0, vec_ctx,
                                         w_in, jnp.zeros((D, WIN), F32), "inproj_bwd_ctx")
    mix_parts = [dwo.reshape(NDEV, 2 * D // NDEV, D),
                 jnp.transpose(dpw.reshape(4, NDEV, PG // NDEV, PG), (1, 0, 2, 3)).reshape(NDEV, 4 * PG // NDEV, PG)]
    dprec, cacc, gs_wo, gs_pw = _conv_bwd_call(dxf, dxb, pre, xbcr, cacc_c, "conv_bwd", mix_parts)
    grad_x, gacc_i, dwin = _inproj_bwd_call(dprec, cw, dz, ddt0, ddt1, dup, h1, dxe_part, x0, vec, w_in, dwin_c,
                                            "inproj_bwd")

    gsum = gacc_f + gacc_m + gacc_i
    sa = sacc + sacc_c
    dtb_row = _pad_lanes(jnp.concatenate([sa[0, 0, 0:NH], sa[1, 0, 0:NH]]))
    alog_row = _pad_lanes(jnp.concatenate([sa[0, 1, 0:NH], sa[1, 1, 0:NH]]))
    convb_rows = jnp.pad(cacc[5], (0, 2 * D - DXBC)).reshape(2, D)
    pack = jnp.concatenate([
        gsum[V_SH1:V_G2 + 1],
        gacc_c[V_SH1:V_SC1 + 1],
        gsum[V_EMBG:V_EMBB + 1] + gacc_c[V_EMBG:V_EMBB + 1],
        gsum[V_LN1G:V_LN2B + 1],
        gsum[V_SSDG:V_DSK + 1],
        convb_rows, dtb_row, alog_row,
        gsum[V_LOSS:V_LOSS + 1],
        jnp.zeros((NP - 22, D), F32)], axis=0)
    (pall,) = _exchange([pack], "gather_small_grads", True)

    dm_flat = pall[:, 0:8, :].reshape(NDEV, 8 * D)
    dm_ex = lax.dynamic_slice(dm_flat, (0, me * ncol_ada), (NDEV, ncol_ada))
    dmc_full = jnp.concatenate([dm_flat[:, 6 * D:8 * D], jnp.zeros((NDEV, 4 * D), F32)], axis=1)
    dm_ctx = lax.dynamic_slice(dmc_full, (0, me * ncol_ada), (NDEV, ncol_ada))
    g_wada, d_wada, nm_wada, nv_wada, dsil = _wada_call(dm_ex, dm_ctx, silu_all, w_ada[0], m_w_ada[0], v_w_ada[0])
    (dsil_all,) = _exchange([dsil], "gather_dsilu", True)

    def small_pack(cc, eg, eb, ba, cb_, dtb, al, dsk, sg, ps, l1g, l1b, l2g, l2b):
        return jnp.concatenate([
            _row(cc), _row(eg), _row(eb), ba.reshape(6, D), jnp.pad(cb_.reshape(-1), (0, 2 * D - DXBC)).reshape(2, D),
            _pad_lanes(dtb.reshape(-1)), _pad_lanes(al.reshape(-1)), _pad_lanes(dsk.reshape(-1)),
            _row(sg), _row(ps), _row(l1g), _row(l1b), _row(l2g), _row(l2b), jnp.zeros((NSM - 20, D), F32)], axis=0)

    sw = small_pack(c_ctx, emb_ln_g, emb_ln_b, b_ada, conv_b, dt_bias, a_log, d_skip, ssd_norm_g, pool_scale,
                    ln1_g, ln1_b, ln2_g, ln2_b)
    sm = small_pack(m_c_ctx, m_emb_ln_g, m_emb_ln_b, m_b_ada, m_conv_b, m_dt_bias, m_a_log, m_d_skip, m_ssd_norm_g,
                    m_pool_scale, m_ln1_g, m_ln1_b, m_ln2_g, m_ln2_b)
    sv = small_pack(v_c_ctx, v_emb_ln_g, v_emb_ln_b, v_b_ada, v_conv_b, v_dt_bias, v_a_log, v_d_skip, v_ssd_norm_g,
                    v_pool_scale, v_ln1_g, v_ln1_b, v_ln2_g, v_ln2_b)
    s_g, s_d, s_m, s_v, loss8 = _small_update_call(pall, dsil_all, _row(c_ctx), sw, sm, sv, et_mat)

    def small_unpack(t):
        return (t[S_CCTX], t[S_EMBG], t[S_EMBB], t[S_BADA:S_BADA + 6].reshape(1, 6 * D),
                t[S_CONVB:S_CONVB + 2].reshape(-1)[:DXBC].reshape(1, DXBC),
                t[S_DTB, 0:2 * NH].reshape(1, 2, NH), t[S_ALOG, 0:2 * NH].reshape(1, 2, NH), t[S_DSK, 0:NH].reshape(1, NH),
                t[S_SSDG].reshape(1, D), t[S_PSC].reshape(1, D), t[S_LN1G].reshape(1, D), t[S_LN1B].reshape(1, D),
                t[S_LN2G].reshape(1, D), t[S_LN2B].reshape(1, D))

    gs_inp, gs_cw = _exchange([_by_cols(_unaligned_in_proj(dwin)).astype(BF16), _by_cols(cacc[0:5])],
                              "exchange_last_grads", False)

    pshape = (4 * PG // NDEV, PG)
    u_inp = _adamw_shard_call(gs_inp, in_proj[0], m_in_proj[0], v_in_proj[0], 256, "adamw_in_proj")
    u_cw = _adamw_shard_call(gs_cw, conv_w[0], m_conv_w[0], v_conv_w[0], 5, "adamw_conv_w")
    u_pw = _adamw_shard_call(gs_pw, pool_w[0].reshape(pshape), m_pool_w[0].reshape(pshape), v_pool_w[0].reshape(pshape),
                             pshape[0], "adamw_pool_w")
    u_wo = _adamw_shard_call(gs_wo, w_out[0], m_w_out[0], v_w_out[0], 64, "adamw_w_out")
    u_wg = _adamw_shard_call(gs_wg, w_gate[0], m_w_gate[0], v_w_gate[0], 256, "adamw_w_gate")
    u_wu = _adamw_shard_call(gs_wu, w_up[0], m_w_up[0], v_w_up[0], 256, "adamw_w_up")
    u_wd = _adamw_shard_call(gs_wd, w_down[0], m_w_down[0], v_w_down[0], 88, "adamw_w_down")

    def assemble(k, small, wada):
        (cc, eg, eb, ba, cb_, dtb, al, dsk, sg, ps, l1g, l1b, l2g, l2b) = small_unpack(small)
        pw = u_pw[k].reshape(1, 4, PG // NDEV, PG)
        return (cc, eg, eb, wada[None], ba, u_inp[k][None], u_cw[k][None], cb_, dtb, al, dsk, sg, pw, ps,
                u_wo[k][None], l1g, l1b, u_wg[k][None], u_wu[k][None], u_wd[k][None], l2g, l2b)

    loss = loss8[0, 0]
    return (loss, grad_x[None], *assemble(0, s_g, g_wada), *assemble(1, s_d, d_wada),
            *assemble(2, s_m, nm_wada), *assemble(3, s_v, nv_wada))
```
